```python
import math
import jax, jax.numpy as jnp
from jax import lax
import numpy as np

D_MODEL = 4096
BATCH = 1
SEQ = 8192
DEPTH = 1

D_MIX = D_MODEL
A_HEADS = 16
A_HEAD_DIM = 128
A_WIDTH = A_HEADS * A_HEAD_DIM
IDX_HEADS = 32
IDX_DIM = 128
TOPK_MAX = 256
Q_BLOCK = 128
G_WIDTH = D_MIX - A_WIDTH
G_HEADS = 4
G_DV = G_WIDTH // G_HEADS
G_DK = G_DV // 2
G_KWIDTH = G_HEADS * G_DK
G_LOWRANK = 16
G_TAU = 16.0
G_CHUNK = 64
D_FF = 11008
CONV_W = 3
LN_EPS = 1e-5
RMS_EPS = 1e-6
DN_ALPHA = (2 * DEPTH) ** 0.25
DN_BETA = (8 * DEPTH) ** -0.25

SPLITS = (A_WIDTH, A_WIDTH, A_WIDTH, IDX_HEADS * IDX_DIM, IDX_DIM, IDX_HEADS,
          G_KWIDTH, G_KWIDTH, G_WIDTH, G_LOWRANK, G_WIDTH)
VALUE_SEGMENTS = (2, 8)
N_IN = sum(SPLITS)

kernel_name = "hymba_dsa_gla_convffn_deepnorm"


def layer_norm(x, g, b):
    xf = x.astype(jnp.float32)
    mu = xf.mean(-1, keepdims=True)
    var = jnp.square(xf - mu).mean(-1, keepdims=True)
    return ((xf - mu) * lax.rsqrt(var + LN_EPS)).astype(x.dtype) * g + b


def rms_norm(x, g):
    xf = x.astype(jnp.float32)
    ms = jnp.square(xf).mean(-1, keepdims=True)
    return (xf * lax.rsqrt(ms + RMS_EPS)).astype(x.dtype) * g


def split_columns(a, sizes):
    offs = np.cumsum(sizes)[:-1].tolist()
    return jnp.split(a, offs, axis=-1)


def dsa_attention(q, k, v, q_idx, k_idx, w_idx):
    B, S, H, Dh = q.shape
    topk = min(TOPK_MAX, S // 4)
    n_blk = S // Q_BLOCK
    key_pos = jnp.arange(S)

    def to_blocks(a):
        return a.reshape(B, n_blk, Q_BLOCK, *a.shape[2:]).swapaxes(0, 1)

    def block(args):
        blk, q_b, qi_b, wi_b = args
        q_pos = blk * Q_BLOCK + jnp.arange(Q_BLOCK)
        causal = key_pos[None, :] <= q_pos[:, None]
        dots = jnp.einsum('bthd,bsd->bths', qi_b, k_idx)
        score = jnp.einsum('bths,bth->bts', jax.nn.relu(dots), wi_b).astype(jnp.float32)
        score = jnp.where(causal[None], score, -jnp.inf)
        _, sel = lax.top_k(score, topk)
        valid = sel <= q_pos[None, :, None]
        k_sel = jax.vmap(lambda kk, ii: kk[ii])(k, sel)
        v_sel = jax.vmap(lambda vv, ii: vv[ii])(v, sel)
        logits = jnp.einsum('bthd,btkhd->bthk', q_b, k_sel).astype(jnp.float32) * (Dh ** -0.5)
        logits = jnp.where(valid[:, :, None, :], logits, -jnp.inf)
        p = jax.nn.softmax(logits, axis=-1).astype(v.dtype)
        return jnp.einsum('bthk,btkhd->bthd', p, v_sel)

    out = lax.map(block, (jnp.arange(n_blk), to_blocks(q), to_blocks(q_idx), to_blocks(w_idx)))
    return out.swapaxes(0, 1).reshape(B, S, H, Dh)


def gla_chunked(q, k, v, log_a):
    B, S, H, Dk = q.shape
    Dv = v.shape[-1]
    n_c = S // G_CHUNK

    def chunks(a):
        return a.reshape(B, n_c, G_CHUNK, H, a.shape[-1]).transpose(1, 0, 3, 2, 4)

    causal = jnp.tril(jnp.ones((G_CHUNK, G_CHUNK), dtype=bool))

    def step(state, inp):
        qc, kc, vc, gc = inp
        qf, kf, vf = qc.astype(jnp.float32), kc.astype(jnp.float32), vc.astype(jnp.float32)
        b = jnp.cumsum(gc.astype(jnp.float32), axis=2)
        diff = b[:, :, :, None, :] - b[:, :, None, :, :]
        decay = jnp.exp(jnp.where(causal[:, :, None], diff, -jnp.inf))
        attn = jnp.einsum('bhtd,bhsd,bhtsd->bhts', qf, kf, decay)
        o = (jnp.einsum('bhts,bhsv->bhtv', attn, vf)
             + jnp.einsum('bhtd,bhdv->bhtv', qf * jnp.exp(b), state))
        b_last = b[:, :, -1:, :]
        state = (state * jnp.exp(b_last[:, :, 0, :, None])
                 + jnp.einsum('bhsd,bhsv->bhdv', kf * jnp.exp(b_last - b), vf))
        return state, o

    state0 = jnp.zeros((B, H, Dk, Dv), jnp.float32)
    _, o = lax.scan(step, state0, (chunks(q), chunks(k), chunks(v), chunks(log_a)))
    return o.transpose(1, 0, 3, 2, 4).reshape(B, S, H, Dv)


def causal_dwconv(u, w, b):
    S = u.shape[1]
    up = jnp.pad(u, ((0, 0), (CONV_W - 1, 0), (0, 0)))
    return sum(w[j] * up[:, j:j + S] for j in range(CONV_W)) + b


def setup_inputs(seed: int = 0) -> dict:
    key = jax.random.key(seed)
    ks = jax.random.split(key, 16)
    f32 = jnp.float32
    col_scale = np.concatenate([
        np.full((n,), DN_BETA if i in VALUE_SEGMENTS else 1.0, np.float32)
        for i, n in enumerate(SPLITS)]) * np.float32(D_MODEL ** -0.5)
    x = jax.random.normal(ks[0], (BATCH, SEQ, D_MODEL), f32)
    w_in = jax.random.normal(ks[1], (DEPTH, D_MODEL, N_IN), f32) * jnp.asarray(col_scale)
    w_gk2 = jax.random.normal(ks[2], (DEPTH, G_LOWRANK, G_KWIDTH), f32) * G_LOWRANK ** -0.5
    b_gk = 0.1 * jax.random.normal(ks[3], (DEPTH, G_KWIDTH), f32)
    attn_out_g = 1.0 + 0.02 * jax.random.normal(ks[4], (DEPTH, A_WIDTH), f32)
    gla_norm_g = 1.0 + 0.02 * jax.random.normal(ks[5], (DEPTH, G_DV), f32)
    w_o = jax.random.normal(ks[6], (DEPTH, D_MIX, D_MODEL), f32) * (D_MIX ** -0.5 * DN_BETA)
    ln1_g = 1.0 + 0.02 * jax.random.normal(ks[7], (DEPTH, D_MODEL), f32)
    ln1_b = 0.02 * jax.random.normal(ks[8], (DEPTH, D_MODEL), f32)
    w_up = jax.random.normal(ks[9], (DEPTH, D_MODEL, 2 * D_FF), f32) * (D_MODEL ** -0.5 * DN_BETA)
    conv_w = jax.random.normal(ks[10], (DEPTH, CONV_W, 2 * D_FF), f32) * CONV_W ** -0.5
    conv_b = 0.02 * jax.random.normal(ks[11], (DEPTH, 2 * D_FF), f32)
    w_down = jax.random.normal(ks[12], (DEPTH, D_FF, D_MODEL), f32) * (D_FF ** -0.5 * DN_BETA)
    ln2_g = 1.0 + 0.02 * jax.random.normal(ks[13], (DEPTH, D_MODEL), f32)
    ln2_b = 0.02 * jax.random.normal(ks[14], (DEPTH, D_MODEL), f32)
    return {"x": x, "w_in": w_in, "w_gk2": w_gk2, "b_gk": b_gk, "attn_out_g": attn_out_g,
            "gla_norm_g": gla_norm_g, "w_o": w_o, "ln1_g": ln1_g, "ln1_b": ln1_b,
            "w_up": w_up, "conv_w": conv_w, "conv_b": conv_b, "w_down": w_down,
            "ln2_g": ln2_g, "ln2_b": ln2_b}


def reference(x, w_in, w_gk2, b_gk, attn_out_g, gla_norm_g, w_o, ln1_g, ln1_b,
              w_up, conv_w, conv_b, w_down, ln2_g, ln2_b):
    B, S, _ = x.shape
    idx_w_scale = IDX_HEADS ** -0.5 * IDX_DIM ** -0.5
    for l in range(DEPTH):
        proj = x @ w_in[l]
        qa, ka, va, qi, ki, wi, qg, kg, vg, glr, gg = split_columns(proj, SPLITS)
        o_a = dsa_attention(qa.reshape(B, S, A_HEADS, A_HEAD_DIM),
                            ka.reshape(B, S, A_HEADS, A_HEAD_DIM),
                            va.reshape(B, S, A_HEADS, A_HEAD_DIM),
                            qi.reshape(B, S, IDX_HEADS, IDX_DIM), ki, wi * idx_w_scale)
        o_a = rms_norm(o_a.reshape(B, S, A_WIDTH), attn_out_g[l])
        log_a = jax.nn.log_sigmoid((glr @ w_gk2[l] + b_gk[l]).astype(jnp.float32)) / G_TAU
        o_g = gla_chunked((qg * G_DK ** -0.5).reshape(B, S, G_HEADS, G_DK),
                          kg.reshape(B, S, G_HEADS, G_DK),
                          vg.reshape(B, S, G_HEADS, G_DV),
                          log_a.reshape(B, S, G_HEADS, G_DK)).astype(x.dtype)
        o_g = rms_norm(o_g, gla_norm_g[l]).reshape(B, S, G_WIDTH) * jax.nn.silu(gg)
        mix = jnp.concatenate([o_a, o_g], axis=-1) @ w_o[l]
        h = layer_norm(DN_ALPHA * x + mix, ln1_g[l], ln1_b[l])
        u = causal_dwconv(h @ w_up[l], conv_w[l], conv_b[l])
        gate, val = jnp.split(u, 2, axis=-1)
        f = (jax.nn.silu(gate) * val) @ w_down[l]
        x = layer_norm(DN_ALPHA * h + f, ln2_g[l], ln2_b[l])
    return x
```

```python
import functools

import numpy as np
import jax
import jax.numpy as jnp
from jax import lax
from jax.experimental import pallas as pl
from jax.experimental.pallas import tpu as pltpu

A_HEADS = 16
A_HEAD_DIM = 128
IDX_HEADS = 32
IDX_DIM = 128
TOPK_MAX = 256
G_HEADS = 4
G_LOWRANK = 16
G_TAU = 16.0
CONV_W = 3
LN_EPS = 1e-5
RMS_EPS = 1e-6
DEPTH = 1
DN_ALPHA = (2 * DEPTH) ** 0.25

LANES = 128
VMEM_LIMIT = 56 * 1024 * 1024

BF16 = jnp.bfloat16
F32 = jnp.float32
NEG_BIG = -1e30
INT_MIN = -2 ** 31


def _params(*sem):
    return pltpu.CompilerParams(dimension_semantics=sem, vmem_limit_bytes=VMEM_LIMIT)


def _tile(n, max_tile, quantum=LANES):
    best = None
    for t in range(quantum, min(n, max_tile) + 1, quantum):
        if n % t == 0:
            best = t
    assert best is not None, (n, max_tile, quantum)
    return best


def _dot(a, b):
    return jnp.dot(a, b, preferred_element_type=F32)


def _dot_nt(a, b):
    return lax.dot_general(a, b, (((1,), (1,)), ((), ())), preferred_element_type=F32)


def _dot_tn(a, b):
    return lax.dot_general(a, b, (((0,), (0,)), ((), ())), preferred_element_type=F32)


def _proj_kernel(x_ref, w_ref, o_ref):
    res = _dot(x_ref[...], w_ref[...])
    for c in range(o_ref.shape[0]):
        o_ref[c] = res[:, c * LANES:(c + 1) * LANES].astype(o_ref.dtype)


def _proj_blocks(xb, wb, *, tm, tn):
    s, k = xb.shape
    n = wb.shape[1]
    return pl.pallas_call(
        _proj_kernel,
        grid=(s // tm, n // tn),
        in_specs=[pl.BlockSpec((tm, k), lambda i, j: (i, 0)),
                  pl.BlockSpec((k, tn), lambda i, j: (0, j))],
        out_specs=pl.BlockSpec((tn // LANES, tm, LANES), lambda i, j: (j, i, 0)),
        out_shape=jax.ShapeDtypeStruct((n // LANES, s, LANES), BF16),
        compiler_params=_params("parallel", "arbitrary"),
        name="proj",
    )(xb, wb)


def _mm_kernel(a_ref, b_ref, *rest, alpha, has_res, nk):
    if has_res:
        r_ref, o_ref = rest[0], rest[1]
        rest = rest[2:]
    else:
        r_ref, o_ref = None, rest[0]
        rest = rest[1:]

    def finish(acc):
        if has_res:
            acc = acc + alpha * r_ref[...]
        o_ref[...] = acc.astype(o_ref.dtype)

    if nk == 1:
        finish(_dot(a_ref[...], b_ref[...]))
        return
    acc_ref = rest[0]
    kk = pl.program_id(2)

    @pl.when(kk == 0)
    def _():
        acc_ref[...] = _dot(a_ref[...], b_ref[...])

    @pl.when(kk > 0)
    def _():
        acc_ref[...] += _dot(a_ref[...], b_ref[...])

    @pl.when(kk == nk - 1)
    def _():
        finish(acc_ref[...])


def _matmul(a, b, *, tm, tn, tk, out_dtype, residual=None, alpha=1.0, name="matmul"):
    m, k = a.shape
    n = b.shape[1]
    nk = k // tk
    in_specs = [pl.BlockSpec((tm, tk), lambda i, j, kk: (i, kk)),
                pl.BlockSpec((tk, tn), lambda i, j, kk: (kk, j))]
    args = [a, b]
    if residual is not None:
        in_specs.append(pl.BlockSpec((tm, tn), lambda i, j, kk: (i, j)))
        args.append(residual)
    scratch = [pltpu.VMEM((tm, tn), F32)] if nk > 1 else []
    return pl.pallas_call(
        functools.partial(_mm_kernel, alpha=alpha, has_res=residual is not None, nk=nk),
        grid=(m // tm, n // tn, nk),
        in_specs=in_specs,
        out_specs=pl.BlockSpec((tm, tn), lambda i, j, kk: (i, j)),
        out_shape=jax.ShapeDtypeStruct((m, n), out_dtype),
        scratch_shapes=scratch,
        compiler_params=_params("parallel", "parallel", "arbitrary"),
        name=name,
    )(*args)


def _ln_kernel(y_ref, g_ref, b_ref, *o_refs):
    y = y_ref[...]
    mu = jnp.mean(y, axis=1, keepdims=True)
    yc = y - mu
    var = jnp.mean(yc * yc, axis=1, keepdims=True)
    out = yc * lax.rsqrt(var + LN_EPS) * g_ref[...] + b_ref[...]
    for o_ref in o_refs:
        o_ref[...] = out.astype(o_ref.dtype)


def _layer_norm(y, g, b, *, tr, also_bf16):
    s, d = y.shape
    out_shape = [jax.ShapeDtypeStruct((s, d), F32)]
    out_specs = [pl.BlockSpec((tr, d), lambda i: (i, 0))]
    if also_bf16:
        out_shape.append(jax.ShapeDtypeStruct((s, d), BF16))
        out_specs.append(pl.BlockSpec((tr, d), lambda i: (i, 0)))
    return pl.pallas_call(
        _ln_kernel,
        grid=(s // tr,),
        in_specs=[pl.BlockSpec((tr, d), lambda i: (i, 0)),
                  pl.BlockSpec((1, d), lambda i: (0, 0)),
                  pl.BlockSpec((1, d), lambda i: (0, 0))],
        out_specs=out_specs,
        out_shape=out_shape,
        compiler_params=_params("parallel"),
        name="layer_norm",
    )(y, g.reshape(1, d), b.reshape(1, d))


def _idx_kernel(qi_ref, ki_ref, sm_ref, bias_ref, key_scr, wb_scr, *, tq, tk, nkt, hi, group,
                topk, wscale):
    qb = pl.program_id(0)
    nk = ((qb + 1) * tq + tk - 1) // tk
    reps = tk // LANES

    for h in range(hi):
        wb_scr[h] = jnp.broadcast_to(sm_ref[:, h:h + 1] * wscale, (tq, LANES))

    row = qb * tq + lax.broadcasted_iota(jnp.int32, (tq, tk), 0)
    col = lax.broadcasted_iota(jnp.int32, (tq, tk), 1)

    def lanes(a):
        return jnp.concatenate([a] * reps, axis=1) if reps > 1 else a

    def score_tile(kb, carry):
        kt = ki_ref[0, pl.ds(pl.multiple_of(kb * tk, tk), tk), :]

        def grp(g, acc):
            qs = qi_ref[pl.ds(g * group, group)].reshape(group * tq, IDX_DIM)
            d = _dot_nt(qs, kt)
            for u in range(group):
                acc = acc + jnp.maximum(d[u * tq:(u + 1) * tq], 0.0) * lanes(wb_scr[g * group + u])
            return acc

        acc = lax.fori_loop(0, hi // group, grp, jnp.zeros((tq, tk), F32))
        sc = jnp.where(kb * tk + col <= row, acc + 0.0, -jnp.inf)
        bits = pltpu.bitcast(sc, jnp.int32)
        key_scr[kb] = bits ^ ((bits >> 31) & jnp.int32(0x7FFFFFFF))
        return carry

    lax.fori_loop(0, nk, score_tile, 0)

    def bit_step(it, v):
        cand = v + jnp.left_shift(jnp.int32(1), 31 - it)
        cand_t = lanes(cand)

        def cnt_step(kb, c):
            return c + jnp.where(key_scr[kb] >= cand_t, 1.0, 0.0)

        c = lax.fori_loop(0, nk, cnt_step, jnp.zeros((tq, tk), F32))
        cnt = jnp.sum(c, axis=1, keepdims=True)
        return jnp.where(cnt >= float(topk), cand, v)

    v = lax.fori_loop(0, 32, bit_step, jnp.full((tq, LANES), INT_MIN, jnp.int32))
    v_t = lanes(v)

    def emit(kb, carry):
        sel = (key_scr[kb] >= v_t) & (kb * tk + col <= row)
        bias_ref[0, kb] = jnp.where(sel, 0.0, NEG_BIG).astype(bias_ref.dtype)
        return carry

    lax.fori_loop(0, nk, emit, 0)

    def fill(kb, carry):
        bias_ref[0, kb] = jnp.full((tq, tk), NEG_BIG, bias_ref.dtype)
        return carry

    lax.fori_loop(nk, nkt, fill, 0)


def _indexer(main, small, lay, *, s, tq, tk, topk):
    nkt = s // tk
    hi = IDX_HEADS
    group = min(8, hi)
    qi0, ki0 = lay["qi"][0], lay["ki"][0]
    assert qi0 % hi == 0
    return pl.pallas_call(
        functools.partial(_idx_kernel, tq=tq, tk=tk, nkt=nkt, hi=hi, group=group, topk=topk,
                          wscale=IDX_HEADS ** -0.5 * IDX_DIM ** -0.5),
        grid=(s // tq,),
        in_specs=[pl.BlockSpec((hi, tq, LANES), lambda i: (qi0 // hi, i, 0)),
                  pl.BlockSpec((1, s, LANES), lambda i: (ki0, 0, 0)),
                  pl.BlockSpec((tq, LANES), lambda i: (i, 0))],
        out_specs=pl.BlockSpec((1, nkt, tq, tk), lambda i: (i, 0, 0, 0)),
        out_shape=jax.ShapeDtypeStruct((s // tq, nkt, tq, tk), BF16),
        scratch_shapes=[pltpu.VMEM((nkt, tq, tk), jnp.int32),
                        pltpu.VMEM((hi, tq, LANES), F32)],
        compiler_params=_params("parallel"),
        name="indexer",
    )(main, main, small)


def _attn_kernel(q_ref, k_ref, v_ref, b_ref, g_ref, o_ref, bias_scr, m_scr, l_scr, acc_scr, *,
                 nh, tq, tk, scale):
    qi = pl.program_id(0)
    kb = pl.program_id(1)

    @pl.when(kb == 0)
    def _():
        m_scr[...] = jnp.full(m_scr.shape, NEG_BIG, F32)
        l_scr[...] = jnp.zeros(l_scr.shape, F32)
        acc_scr[...] = jnp.zeros(acc_scr.shape, F32)

    @pl.when(kb * tk <= qi * tq + (tq - 1))
    def _():
        na, nb, sq, sk = b_ref.shape
        for a in range(na):
            for b in range(nb):
                bias_scr[a * sq:(a + 1) * sq, b * sk:(b + 1) * sk] = b_ref[a, b].astype(F32)

        def head(h, carry):
            s = _dot_nt(q_ref[h], k_ref[h]) * scale + bias_scr[...]
            m_prev = m_scr[h]
            m_new = jnp.maximum(m_prev, jnp.max(s, axis=1, keepdims=True))
            p = jnp.exp(s - m_new[:, :1])
            alpha = jnp.exp(m_prev - m_new)
            l_scr[h] = alpha * l_scr[h] + jnp.sum(p, axis=1, keepdims=True)
            m_scr[h] = m_new
            acc_scr[h] = acc_scr[h] * alpha + _dot(p.astype(BF16), v_ref[h])
            return carry

        lax.fori_loop(0, nh, head, 0)

    @pl.when(kb == pl.num_programs(1) - 1)
    def _():
        ss = jnp.zeros((tq, LANES), F32)
        for h in range(nh):
            o_h = acc_scr[h] / l_scr[h]
            acc_scr[h] = o_h
            ss = ss + o_h * o_h
        ms = jnp.sum(ss, axis=1, keepdims=True) * (1.0 / (nh * LANES))
        r = lax.rsqrt(ms + RMS_EPS)
        for h in range(nh):
            sl = slice(h * LANES, (h + 1) * LANES)
            o_ref[:, sl] = (acc_scr[h] * r * g_ref[:, sl]).astype(o_ref.dtype)


def _attention(main, bias, g, lay, *, s, tq, tk):
    nh = A_HEADS
    nqb, nkt, sq, sk = bias.shape
    q0, k0, v0 = lay["qa"][0], lay["ka"][0], lay["va"][0]
    assert q0 % nh == 0 and k0 % nh == 0 and v0 % nh == 0

    def last_kb(i):
        return (i * tq + tq - 1) // tk

    return pl.pallas_call(
        functools.partial(_attn_kernel, nh=nh, tq=tq, tk=tk, scale=A_HEAD_DIM ** -0.5),
        grid=(s // tq, s // tk),
        in_specs=[pl.BlockSpec((nh, tq, LANES), lambda i, j: (q0 // nh, i, 0)),
                  pl.BlockSpec((nh, tk, LANES), lambda i, j: (k0 // nh, jnp.minimum(j, last_kb(i)), 0)),
                  pl.BlockSpec((nh, tk, LANES), lambda i, j: (v0 // nh, jnp.minimum(j, last_kb(i)), 0)),
                  pl.BlockSpec((tq // sq, tk // sk, sq, sk),
                               lambda i, j: (i, jnp.minimum(j, last_kb(i)), 0, 0)),
                  pl.BlockSpec((1, nh * LANES), lambda i, j: (0, 0))],
        out_specs=pl.BlockSpec((tq, nh * LANES), lambda i, j: (i, 0)),
        out_shape=jax.ShapeDtypeStruct((s, nh * LANES), BF16),
        scratch_shapes=[pltpu.VMEM((tq, tk), F32),
                        pltpu.VMEM((nh, tq, LANES), F32),
                        pltpu.VMEM((nh, tq, LANES), F32),
                        pltpu.VMEM((nh, tq, LANES), F32)],
        compiler_params=_params("parallel", "arbitrary"),
        name="attention",
    )(main, main, main, bias, g.reshape(1, nh * LANES))


def _gla_kernel(q_ref, k_ref, v_ref, gg_ref, sm_ref, w2_ref, bgk_ref, gn_ref, o_ref, st_scr, *,
                rows, chunk, dk, dv):
    @pl.when(pl.program_id(1) == 0)
    def _():
        st_scr[...] = jnp.zeros(st_scr.shape, F32)

    r_i = lax.broadcasted_iota(jnp.int32, (chunk, chunk), 0)
    c_i = lax.broadcasted_iota(jnp.int32, (chunk, chunk), 1)
    lower = r_i >= c_i
    tri = jnp.where(lower, 1.0, 0.0).astype(BF16)
    mid = chunk // 2

    def wide(ref, r0):
        return jnp.concatenate([ref[j, pl.ds(r0, chunk), :] for j in range(ref.shape[0])], axis=1)

    def step(c, carry):
        r0 = pl.multiple_of(c * chunk, chunk)
        q = wide(q_ref, r0).astype(F32) * (dk ** -0.5)
        k = wide(k_ref, r0).astype(F32)
        v = wide(v_ref, r0)
        z = _dot(sm_ref[pl.ds(r0, chunk), :].astype(BF16), w2_ref[0]) + bgk_ref[0]
        g = (jnp.minimum(z, 0.0) - jnp.log1p(jnp.exp(-jnp.abs(z)))) * (1.0 / G_TAU)
        g_hi = g.astype(BF16)
        rem = g - g_hi.astype(F32)
        g_mid = rem.astype(BF16)
        g_lo = (rem - g_mid.astype(F32)).astype(BF16)
        b = _dot(tri, g_hi) + _dot(tri, g_mid) + _dot(tri, g_lo)
        b_mid = b[mid:mid + 1, :]
        b_last = b[chunk - 1:chunk, :]
        qe = (q * jnp.exp(b - b_mid)).astype(BF16)
        ke = (k * jnp.exp(b_mid - b)).astype(BF16)
        att = jnp.where(lower, _dot_nt(qe, ke), 0.0)
        st = st_scr[...]
        o = _dot(att.astype(BF16), v) + _dot_nt((q * jnp.exp(b)).astype(BF16), st.astype(BF16))
        kd = (k * jnp.exp(b_last - b)).astype(BF16)
        st_scr[...] = st * jnp.exp(b_last) + _dot_tn(v, kd)
        ms = jnp.mean(o * o, axis=1, keepdims=True)
        gate = wide(gg_ref, r0).astype(F32)
        out = (o * lax.rsqrt(ms + RMS_EPS)) * gn_ref[...] * (gate * jax.nn.sigmoid(gate))
        o_ref[pl.ds(r0, chunk), :] = out.astype(o_ref.dtype)
        return carry

    lax.fori_loop(0, rows // chunk, step, 0)


def _gla(main, small, w2, bgk, gn, lay, *, s, rows, chunk, dk, dv):
    nbk, nbv = dk // LANES, dv // LANES
    q0, k0, v0, g0 = lay["qg"][0], lay["kg"][0], lay["vg"][0], lay["gg"][0]
    assert q0 % nbk == 0 and k0 % nbk == 0 and v0 % nbv == 0 and g0 % nbv == 0
    return pl.pallas_call(
        functools.partial(_gla_kernel, rows=rows, chunk=chunk, dk=dk, dv=dv),
        grid=(G_HEADS, s // rows),
        in_specs=[pl.BlockSpec((nbk, rows, LANES), lambda h, r: (q0 // nbk + h, r, 0)),
                  pl.BlockSpec((nbk, rows, LANES), lambda h, r: (k0 // nbk + h, r, 0)),
                  pl.BlockSpec((nbv, rows, LANES), lambda h, r: (v0 // nbv + h, r, 0)),
                  pl.BlockSpec((nbv, rows, LANES), lambda h, r: (g0 // nbv + h, r, 0)),
                  pl.BlockSpec((rows, LANES), lambda h, r: (r, 0)),
                  pl.BlockSpec((1, LANES, dk), lambda h, r: (h, 0, 0)),
                  pl.BlockSpec((1, 1, dk), lambda h, r: (h, 0, 0)),
                  pl.BlockSpec((1, dv), lambda h, r: (0, 0))],
        out_specs=pl.BlockSpec((rows, dv), lambda h, r: (r, h)),
        out_shape=jax.ShapeDtypeStruct((s, G_HEADS * dv), BF16),
        scratch_shapes=[pltpu.VMEM((dv, dk), F32)],
        compiler_params=_params("parallel", "arbitrary"),
        name="gla",
    )(main, main, main, main, small, w2, bgk, gn.reshape(1, dv))


def _ffn_up_kernel(h_ref, wg_ref, wv_ref, cwg_ref, cwv_ref, cbg_ref, cbv_ref, o_ref, cg_scr, cv_scr,
                   *, tm):
    i = pl.program_id(1)

    @pl.when(i == 0)
    def _():
        cg_scr[...] = jnp.zeros(cg_scr.shape, F32)
        cv_scr[...] = jnp.zeros(cv_scr.shape, F32)

    hrow = h_ref[...]
    row = lax.broadcasted_iota(jnp.int32, (tm, 1), 0)

    def conv(w_ref, cw_ref, cb_ref, carry_scr):
        hw = _dot(hrow, w_ref[...])
        prev = carry_scr[...]
        p1 = prev[7:8, :]
        p2 = prev[6:7, :]
        r1 = jnp.where(row == 0, p1, pltpu.roll(hw, 1, 0))
        r2 = jnp.where(row == 0, p2, jnp.where(row == 1, p1, pltpu.roll(hw, 2, 0)))
        carry_scr[...] = hw[tm - 8:tm, :]
        return cw_ref[2:3, :] * hw + cw_ref[1:2, :] * r1 + cw_ref[0:1, :] * r2 + cb_ref[...]

    gate = conv(wg_ref, cwg_ref, cbg_ref, cg_scr)
    val = conv(wv_ref, cwv_ref, cbv_ref, cv_scr)
    o_ref[...] = (gate * jax.nn.sigmoid(gate) * val).astype(o_ref.dtype)


def _ffn_up(hb, wg, wv, cwg, cwv, cbg, cbv, *, tm, tn):
    s, d = hb.shape
    ffp = wg.shape[1]
    wspec = pl.BlockSpec((d, tn), lambda j, i: (0, j))
    cwspec = pl.BlockSpec((CONV_W, tn), lambda j, i: (0, j))
    cbspec = pl.BlockSpec((1, tn), lambda j, i: (0, j))
    return pl.pallas_call(
        functools.partial(_ffn_up_kernel, tm=tm),
        grid=(ffp // tn, s // tm),
        in_specs=[pl.BlockSpec((tm, d), lambda j, i: (i, 0)), wspec, wspec, cwspec, cwspec, cbspec, cbspec],
        out_specs=pl.BlockSpec((tm, tn), lambda j, i: (i, j)),
        out_shape=jax.ShapeDtypeStruct((s, ffp), BF16),
        scratch_shapes=[pltpu.VMEM((8, tn), F32), pltpu.VMEM((8, tn), F32)],
        compiler_params=_params("parallel", "arbitrary"),
        name="ffn_up",
    )(hb, wg, wv, cwg, cwv, cbg, cbv)


def _layout(d_model):
    a_width = A_HEADS * A_HEAD_DIM
    g_width = d_model - a_width
    g_kwidth = g_width // 2
    names = ("qa", "ka", "va", "qi", "ki", "wi", "qg", "kg", "vg", "glr", "gg")
    sizes = (a_width, a_width, a_width, IDX_HEADS * IDX_DIM, IDX_DIM, IDX_HEADS,
             g_kwidth, g_kwidth, g_width, G_LOWRANK, g_width)
    offs = np.concatenate([[0], np.cumsum(sizes)])
    src = {n: (int(offs[i]), int(offs[i + 1])) for i, n in enumerate(names)}
    order = ("qi", "qa", "ka", "va", "qg", "kg", "vg", "gg", "ki")
    lay, blk = {}, 0
    for n in order:
        width = src[n][1] - src[n][0]
        assert width % LANES == 0
        lay[n] = (blk, width // LANES)
        blk += width // LANES
    return src, order, lay, blk


def kernel(x, w_in, w_gk2, b_gk, attn_out_g, gla_norm_g, w_o, ln1_g, ln1_b, w_up, conv_w, conv_b,
           w_down, ln2_g, ln2_b):
    assert x.shape[0] == 1 and w_in.shape[0] == DEPTH == 1
    _, s, d = x.shape
    x2 = x[0]
    src, order, lay, nb_main = _layout(d)
    g_width = d - A_HEADS * A_HEAD_DIM
    dv = g_width // G_HEADS
    dk = dv // 2
    d_ff = w_down.shape[1]
    topk = min(TOPK_MAX, s // 4)

    w = w_in[0]
    w_main = jnp.concatenate([w[:, src[n][0]:src[n][1]] for n in order], axis=1).astype(BF16)
    n_small = IDX_HEADS + G_LOWRANK
    assert n_small <= LANES
    w_small = jnp.concatenate([w[:, src["wi"][0]:src["wi"][1]], w[:, src["glr"][0]:src["glr"][1]],
                               jnp.zeros((d, LANES - n_small), F32)], axis=1).astype(BF16)
    w2 = jnp.zeros((LANES, G_HEADS * dk), F32).at[IDX_HEADS:n_small].set(w_gk2[0])
    w2 = w2.reshape(LANES, G_HEADS, dk).transpose(1, 0, 2).astype(BF16)
    bgk = b_gk[0].reshape(G_HEADS, 1, dk)
    ffp = -(-d_ff // 1024) * 1024 if d_ff > 1024 else d_ff
    pad = ffp - d_ff
    wu = w_up[0]
    wg = jnp.pad(wu[:, :d_ff], ((0, 0), (0, pad))).astype(BF16)
    wv = jnp.pad(wu[:, d_ff:], ((0, 0), (0, pad))).astype(BF16)
    cwg = jnp.pad(conv_w[0][:, :d_ff], ((0, 0), (0, pad)))
    cwv = jnp.pad(conv_w[0][:, d_ff:], ((0, 0), (0, pad)))
    cbg = jnp.pad(conv_b[0][:d_ff], (0, pad)).reshape(1, ffp)
    cbv = jnp.pad(conv_b[0][d_ff:], (0, pad)).reshape(1, ffp)
    wd = jnp.pad(w_down[0], ((0, pad), (0, 0))).astype(BF16)
    wo = w_o[0].astype(BF16)
    xb = x2.astype(BF16)

    tm = _tile(s, 1024)
    main = _proj_blocks(xb, w_main, tm=tm, tn=_tile(nb_main * LANES, 640))
    small = _matmul(xb, w_small, tm=tm, tn=LANES, tk=d, out_dtype=F32, name="proj_small")
    bias = _indexer(main, small, lay, s=s, tq=128, tk=256, topk=topk)
    ta = _tile(s, 512)
    o_a = _attention(main, bias, attn_out_g[0], lay, s=s, tq=ta, tk=ta)
    o_g = _gla(main, small, w2, bgk, gla_norm_g[0], lay, s=s, rows=_tile(s, 512), chunk=64, dk=dk, dv=dv)
    mix_in = jnp.concatenate([o_a, o_g], axis=1)
    y1 = _matmul(mix_in, wo, tm=tm, tn=_tile(d, 512), tk=d, out_dtype=F32, residual=x2, alpha=DN_ALPHA,
                 name="w_o")
    h, hb = _layer_norm(y1, ln1_g[0], ln1_b[0], tr=_tile(s, 256, 8), also_bf16=True)

    act = _ffn_up(hb, wg, wv, cwg, cwv, cbg, cbv, tm=_tile(s, 512), tn=_tile(ffp, 512))
    y2 = _matmul(act, wd, tm=tm, tn=_tile(d, 1024), tk=_tile(ffp, 1024), out_dtype=F32, residual=h,
                 alpha=DN_ALPHA, name="w_down")
    (out,) = _layer_norm(y2, ln2_g[0], ln2_b[0], tr=_tile(s, 256, 8), also_bf16=False)
    return out[None]
```

```python
import functools

import numpy as np
import jax
import jax.numpy as jnp
from jax import lax
from jax.experimental import pallas as pl
from jax.experimental.pallas import tpu as pltpu

A_HEADS = 16
A_HEAD_DIM = 128
IDX_HEADS = 32
IDX_DIM = 128
TOPK_MAX = 256
G_HEADS = 4
G_LOWRANK = 16
G_TAU = 16.0
CONV_W = 3
LN_EPS = 1e-5
RMS_EPS = 1e-6
DEPTH = 1
DN_ALPHA = (2 * DEPTH) ** 0.25

LANES = 128
VMEM_LIMIT = 56 * 1024 * 1024

BF16 = jnp.bfloat16
F32 = jnp.float32
NEG_BIG = -1e30
LOG2E = 1.4426950408889634
INT_MIN = -2 ** 31


def _params(*sem):
    return pltpu.CompilerParams(dimension_semantics=sem, vmem_limit_bytes=VMEM_LIMIT)


def _tile(n, max_tile, quantum=LANES):
    best = None
    for t in range(quantum, min(n, max_tile) + 1, quantum):
        if n % t == 0:
            best = t
    assert best is not None, (n, max_tile, quantum)
    return best


def _dot(a, b):
    return jnp.dot(a, b, preferred_element_type=F32)


def _dot_nt(a, b):
    return lax.dot_general(a, b, (((1,), (1,)), ((), ())), preferred_element_type=F32)


def _dot_tn(a, b):
    return lax.dot_general(a, b, (((0,), (0,)), ((), ())), preferred_element_type=F32)


def _proj_kernel(x_ref, w_ref, o_ref):
    res = _dot(x_ref[...], w_ref[...])
    for c in range(o_ref.shape[0]):
        o_ref[c] = res[:, c * LANES:(c + 1) * LANES].astype(o_ref.dtype)


def _proj_blocks(xb, wb, *, tm, tn):
    s, k = xb.shape
    n = wb.shape[1]
    return pl.pallas_call(
        _proj_kernel,
        grid=(s // tm, n // tn),
        in_specs=[pl.BlockSpec((tm, k), lambda i, j: (i, 0)),
                  pl.BlockSpec((k, tn), lambda i, j: (0, j))],
        out_specs=pl.BlockSpec((tn // LANES, tm, LANES), lambda i, j: (j, i, 0)),
        out_shape=jax.ShapeDtypeStruct((n // LANES, s, LANES), BF16),
        compiler_params=_params("parallel", "arbitrary"),
        name="proj",
    )(xb, wb)


def _mm_kernel(a_ref, b_ref, *rest, alpha, has_res, nk):
    if has_res:
        r_ref, o_ref = rest[0], rest[1]
        rest = rest[2:]
    else:
        r_ref, o_ref = None, rest[0]
        rest = rest[1:]

    def finish(acc):
        if has_res:
            acc = acc + alpha * r_ref[...]
        o_ref[...] = acc.astype(o_ref.dtype)

    if nk == 1:
        finish(_dot(a_ref[...], b_ref[...]))
        return
    acc_ref = rest[0]
    kk = pl.program_id(2)

    @pl.when(kk == 0)
    def _():
        acc_ref[...] = _dot(a_ref[...], b_ref[...])

    @pl.when(kk > 0)
    def _():
        acc_ref[...] += _dot(a_ref[...], b_ref[...])

    @pl.when(kk == nk - 1)
    def _():
        finish(acc_ref[...])


def _matmul(a, b, *, tm, tn, tk, out_dtype, residual=None, alpha=1.0, name="matmul"):
    m, k = a.shape
    n = b.shape[1]
    nk = k // tk
    in_specs = [pl.BlockSpec((tm, tk), lambda i, j, kk: (i, kk)),
                pl.BlockSpec((tk, tn), lambda i, j, kk: (kk, j))]
    args = [a, b]
    if residual is not None:
        in_specs.append(pl.BlockSpec((tm, tn), lambda i, j, kk: (i, j)))
        args.append(residual)
    scratch = [pltpu.VMEM((tm, tn), F32)] if nk > 1 else []
    return pl.pallas_call(
        functools.partial(_mm_kernel, alpha=alpha, has_res=residual is not None, nk=nk),
        grid=(m // tm, n // tn, nk),
        in_specs=in_specs,
        out_specs=pl.BlockSpec((tm, tn), lambda i, j, kk: (i, j)),
        out_shape=jax.ShapeDtypeStruct((m, n), out_dtype),
        scratch_shapes=scratch,
        compiler_params=_params("parallel", "parallel", "arbitrary"),
        name=name,
    )(*args)


def _ln_kernel(y_ref, g_ref, b_ref, *o_refs):
    y = y_ref[...]
    mu = jnp.mean(y, axis=1, keepdims=True)
    yc = y - mu
    var = jnp.mean(yc * yc, axis=1, keepdims=True)
    out = yc * lax.rsqrt(var + LN_EPS) * g_ref[...] + b_ref[...]
    for o_ref in o_refs:
        o_ref[...] = out.astype(o_ref.dtype)


def _layer_norm(y, g, b, *, tr, also_bf16):
    s, d = y.shape
    out_shape = [jax.ShapeDtypeStruct((s, d), F32)]
    out_specs = [pl.BlockSpec((tr, d), lambda i: (i, 0))]
    if also_bf16:
        out_shape.append(jax.ShapeDtypeStruct((s, d), BF16))
        out_specs.append(pl.BlockSpec((tr, d), lambda i: (i, 0)))
    return pl.pallas_call(
        _ln_kernel,
        grid=(s // tr,),
        in_specs=[pl.BlockSpec((tr, d), lambda i: (i, 0)),
                  pl.BlockSpec((1, d), lambda i: (0, 0)),
                  pl.BlockSpec((1, d), lambda i: (0, 0))],
        out_specs=out_specs,
        out_shape=out_shape,
        compiler_params=_params("parallel"),
        name="layer_norm",
    )(y, g.reshape(1, d), b.reshape(1, d))


def _idx_kernel(qi_ref, ki_ref, sm_ref, bias_ref, hi_scr, lo_scr, wb_scr, *, tq, tk, nkt, hi, group,
                topk, wscale):
    qb = pl.program_id(0)
    nk = ((qb + 1) * tq + tk - 1) // tk
    reps = tk // LANES
    i16 = jnp.int16
    low16 = -2 ** 15

    for h in range(hi):
        wb_scr[h] = jnp.broadcast_to(sm_ref[:, h:h + 1] * wscale, (tq, LANES))

    row = qb * tq + lax.broadcasted_iota(jnp.int32, (tq, tk), 0)
    col = lax.broadcasted_iota(jnp.int32, (tq, tk), 1)

    def lanes(a):
        return jnp.concatenate([a] * reps, axis=1) if reps > 1 else a

    def fold(a):
        out = a[:, :LANES]
        for r in range(1, reps):
            out = out + a[:, r * LANES:(r + 1) * LANES]
        return out

    def score_tile(kb, carry):
        kt = ki_ref[0, pl.ds(pl.multiple_of(kb * tk, tk), tk), :]
        acc = jnp.zeros((tq, tk), F32)
        for g in range(hi // group):
            qs = qi_ref[g * group:(g + 1) * group].reshape(group * tq, IDX_DIM)
            d = _dot_nt(qs, kt)
            for u in range(group):
                acc = acc + jnp.maximum(d[u * tq:(u + 1) * tq], 0.0) * lanes(wb_scr[g * group + u])
        bits = pltpu.bitcast(acc + 0.0, jnp.int32)
        key = jnp.where(kb * tk + col <= row, bits ^ ((bits >> 31) & jnp.int32(0x7FFFFFFF)), INT_MIN)
        hi_scr[kb] = (key >> 16).astype(i16)
        lo_scr[kb] = ((key & 0xFFFF) + low16).astype(i16)
        return carry

    lax.fori_loop(0, nk, score_tile, 0)

    def count_ge(ref, cand):
        cand_t = lanes(cand.astype(i16))

        def step(kb, c):
            return c + fold(jnp.where(ref[kb] >= cand_t, i16(1), i16(0)))

        c = lax.fori_loop(0, nk, step, jnp.zeros((tq, LANES), i16))
        return jnp.sum(c.astype(F32), axis=1, keepdims=True)

    def search(ref, base_cnt):
        def bit_step(it, v):
            cand = v + jnp.left_shift(jnp.int32(1), 15 - it)
            return jnp.where(base_cnt + count_ge(ref, cand) >= float(topk), cand, v)

        return lax.fori_loop(0, 16, bit_step, jnp.full((tq, LANES), low16, jnp.int32))

    v_hi = search(hi_scr, 0.0)
    n_above = count_ge(hi_scr, v_hi + 1)
    v_hi_t = lanes(v_hi.astype(i16))

    def keep_equal(kb, carry):
        lo_scr[kb] = jnp.where(hi_scr[kb] == v_hi_t, lo_scr[kb], i16(low16))
        return carry

    lax.fori_loop(0, nk, keep_equal, 0)
    v_lo = search(lo_scr, n_above)
    v_lo = jnp.where(v_hi == low16, jnp.maximum(v_lo, low16 + 1), v_lo)
    v_lo_t = lanes(v_lo.astype(i16))
    zero = jnp.zeros((tq, tk), bias_ref.dtype)
    neg = jnp.full((tq, tk), NEG_BIG, bias_ref.dtype)

    def emit(kb, carry):
        h16 = hi_scr[kb]
        sel = (h16 > v_hi_t) | ((h16 == v_hi_t) & (lo_scr[kb] >= v_lo_t))
        bias_ref[0, kb] = jnp.where(sel, zero, neg)
        return carry

    lax.fori_loop(0, nk, emit, 0)

    def fill(kb, carry):
        bias_ref[0, kb] = neg
        return carry

    lax.fori_loop(nk, nkt, fill, 0)


def _indexer(main, small, lay, *, s, tq, tk, topk):
    nkt = s // tk
    hi = IDX_HEADS
    group = min(8, hi)
    qi0, ki0 = lay["qi"][0], lay["ki"][0]
    assert qi0 % hi == 0
    return pl.pallas_call(
        functools.partial(_idx_kernel, tq=tq, tk=tk, nkt=nkt, hi=hi, group=group, topk=topk,
                          wscale=IDX_HEADS ** -0.5 * IDX_DIM ** -0.5),
        grid=(s // tq,),
        in_specs=[pl.BlockSpec((hi, tq, LANES), lambda i: (qi0 // hi, i, 0)),
                  pl.BlockSpec((1, s, LANES), lambda i: (ki0, 0, 0)),
                  pl.BlockSpec((tq, LANES), lambda i: (i, 0))],
        out_specs=pl.BlockSpec((1, nkt, tq, tk), lambda i: (i, 0, 0, 0)),
        out_shape=jax.ShapeDtypeStruct((s // tq, nkt, tq, tk), BF16),
        scratch_shapes=[pltpu.VMEM((nkt, tq, tk), jnp.int16),
                        pltpu.VMEM((nkt, tq, tk), jnp.int16),
                        pltpu.VMEM((hi, tq, LANES), F32)],
        compiler_params=_params("parallel"),
        name="indexer",
    )(main, main, small)


def _attn_kernel(q_ref, k_ref, v_ref, b_ref, g_ref, o_ref, bias_scr, m_scr, l_scr, acc_scr, *pipe,
                 nh, tq, tk, scale):
    qi = pl.program_id(0)
    kb = pl.program_id(1)

    @pl.when(kb == 0)
    def _():
        m_scr[...] = jnp.full(m_scr.shape, NEG_BIG, F32)
        l_scr[...] = jnp.zeros(l_scr.shape, F32)
        acc_scr[...] = jnp.zeros(acc_scr.shape, F32)

    @pl.when(kb * tk <= qi * tq + (tq - 1))
    def _():
        na, nb, sq, sk = b_ref.shape
        for a in range(na):
            for b in range(nb):
                bias_scr[a * sq:(a + 1) * sq, b * sk:(b + 1) * sk] = b_ref[a, b].astype(F32)

        c2 = scale * LOG2E

        def qk(h, s_ref):
            s_ref[...] = _dot_nt(q_ref[h], k_ref[h])

        def soft(h, s_ref, p_ref, al_ref):
            s = s_ref[...] * c2 + bias_scr[...]
            m_prev = m_scr[h]
            m_new = jnp.maximum(m_prev, jnp.max(s, axis=1, keepdims=True))
            p = jnp.exp2(s - m_new[:, :1])
            alpha = jnp.exp2(m_prev - m_new)
            l_scr[h] = alpha * l_scr[h] + jnp.sum(p, axis=1, keepdims=True)
            m_scr[h] = m_new
            p_ref[...] = p.astype(BF16)
            al_ref[...] = alpha

        def pv(h, p_ref, al_ref):
            acc_scr[h] = acc_scr[h] * al_ref[...] + _dot(p_ref[...], v_ref[h])

        s_a, s_b, p_a, p_b, al_a, al_b = pipe
        qk(0, s_a)
        qk(1, s_b)
        soft(0, s_a, p_a, al_a)

        def pair(j, carry):
            qk(2 * j, s_a)
            soft(2 * j - 1, s_b, p_b, al_b)
            pv(2 * j - 2, p_a, al_a)
            qk(2 * j + 1, s_b)
            soft(2 * j, s_a, p_a, al_a)
            pv(2 * j - 1, p_b, al_b)
            return carry

        lax.fori_loop(1, nh // 2, pair, 0)
        soft(nh - 1, s_b, p_b, al_b)
        pv(nh - 2, p_a, al_a)
        pv(nh - 1, p_b, al_b)

    @pl.when(kb == pl.num_programs(1) - 1)
    def _():
        ss = jnp.zeros((tq, LANES), F32)
        for h in range(nh):
            o_h = acc_scr[h] / l_scr[h]
            acc_scr[h] = o_h
            ss = ss + o_h * o_h
        ms = jnp.sum(ss, axis=1, keepdims=True) * (1.0 / (nh * LANES))
        r = lax.rsqrt(ms + RMS_EPS)
        for h in range(nh):
            sl = slice(h * LANES, (h + 1) * LANES)
            o_ref[:, sl] = (acc_scr[h] * r * g_ref[:, sl]).astype(o_ref.dtype)


def _attention(main, bias, g, lay, *, s, tq, tk):
    nh = A_HEADS
    nqb, nkt, sq, sk = bias.shape
    q0, k0, v0 = lay["qa"][0], lay["ka"][0], lay["va"][0]
    assert q0 % nh == 0 and k0 % nh == 0 and v0 % nh == 0

    def last_kb(i):
        return (i * tq + tq - 1) // tk

    return pl.pallas_call(
        functools.partial(_attn_kernel, nh=nh, tq=tq, tk=tk, scale=A_HEAD_DIM ** -0.5),
        grid=(s // tq, s // tk),
        in_specs=[pl.BlockSpec((nh, tq, LANES), lambda i, j: (q0 // nh, i, 0)),
                  pl.BlockSpec((nh, tk, LANES), lambda i, j: (k0 // nh, jnp.minimum(j, last_kb(i)), 0)),
                  pl.BlockSpec((nh, tk, LANES), lambda i, j: (v0 // nh, jnp.minimum(j, last_kb(i)), 0)),
                  pl.BlockSpec((tq // sq, tk // sk, sq, sk),
                               lambda i, j: (i, jnp.minimum(j, last_kb(i)), 0, 0)),
                  pl.BlockSpec((1, nh * LANES), lambda i, j: (0, 0))],
        out_specs=pl.BlockSpec((tq, nh * LANES), lambda i, j: (i, 0)),
        out_shape=jax.ShapeDtypeStruct((s, nh * LANES), BF16),
        scratch_shapes=[pltpu.VMEM((tq, tk), F32),
                        pltpu.VMEM((nh, tq, LANES), F32),
                        pltpu.VMEM((nh, tq, LANES), F32),
                        pltpu.VMEM((nh, tq, LANES), F32),
                        pltpu.VMEM((tq, tk), F32), pltpu.VMEM((tq, tk), F32),
                        pltpu.VMEM((tq, tk), BF16), pltpu.VMEM((tq, tk), BF16),
                        pltpu.VMEM((tq, LANES), F32), pltpu.VMEM((tq, LANES), F32)],
        compiler_params=_params("parallel", "arbitrary"),
        name="attention",
    )(main, main, main, bias, g.reshape(1, nh * LANES))


def _gla_kernel(q_ref, k_ref, v_ref, gg_ref, sm_ref, w2_ref, bgk_ref, gn_ref, o_ref, st_scr, *,
                rows, chunk, dk, dv):
    @pl.when(pl.program_id(1) == 0)
    def _():
        st_scr[...] = jnp.zeros(st_scr.shape, F32)

    r_i = lax.broadcasted_iota(jnp.int32, (chunk, chunk), 0)
    c_i = lax.broadcasted_iota(jnp.int32, (chunk, chunk), 1)
    lower = r_i >= c_i
    tri = jnp.where(lower, 1.0, 0.0).astype(BF16)
    mid = chunk // 2

    def wide(ref, r0):
        return jnp.concatenate([ref[j, pl.ds(r0, chunk), :] for j in range(ref.shape[0])], axis=1)

    def step(c, carry):
        r0 = pl.multiple_of(c * chunk, chunk)
        q = wide(q_ref, r0).astype(F32) * (dk ** -0.5)
        k = wide(k_ref, r0).astype(F32)
        v = wide(v_ref, r0)
        z = _dot(sm_ref[pl.ds(r0, chunk), :].astype(BF16), w2_ref[0]) + bgk_ref[0]
        g = (jnp.minimum(z, 0.0) - jnp.log1p(jnp.exp(-jnp.abs(z)))) * (1.0 / G_TAU)
        g_hi = g.astype(BF16)
        rem = g - g_hi.astype(F32)
        g_mid = rem.astype(BF16)
        g_lo = (rem - g_mid.astype(F32)).astype(BF16)
        b = _dot(tri, g_hi) + _dot(tri, g_mid) + _dot(tri, g_lo)
        b_mid = b[mid:mid + 1, :]
        b_last = b[chunk - 1:chunk, :]
        qe = (q * jnp.exp(b - b_mid)).astype(BF16)
        ke = (k * jnp.exp(b_mid - b)).astype(BF16)
        att = jnp.where(lower, _dot_nt(qe, ke), 0.0)
        st = st_scr[...]
        o = _dot(att.astype(BF16), v) + _dot_nt((q * jnp.exp(b)).astype(BF16), st.astype(BF16))
        kd = (k * jnp.exp(b_last - b)).astype(BF16)
        st_scr[...] = st * jnp.exp(b_last) + _dot_tn(v, kd)
        ms = jnp.mean(o * o, axis=1, keepdims=True)
        gate = wide(gg_ref, r0).astype(F32)
        out = (o * lax.rsqrt(ms + RMS_EPS)) * gn_ref[...] * (gate * jax.nn.sigmoid(gate))
        o_ref[pl.ds(r0, chunk), :] = out.astype(o_ref.dtype)
        return carry

    lax.fori_loop(0, rows // chunk, step, 0)


def _gla(main, small, w2, bgk, gn, lay, *, s, rows, chunk, dk, dv):
    nbk, nbv = dk // LANES, dv // LANES
    q0, k0, v0, g0 = lay["qg"][0], lay["kg"][0], lay["vg"][0], lay["gg"][0]
    assert q0 % nbk == 0 and k0 % nbk == 0 and v0 % nbv == 0 and g0 % nbv == 0
    return pl.pallas_call(
        functools.partial(_gla_kernel, rows=rows, chunk=chunk, dk=dk, dv=dv),
        grid=(G_HEADS, s // rows),
        in_specs=[pl.BlockSpec((nbk, rows, LANES), lambda h, r: (q0 // nbk + h, r, 0)),
                  pl.BlockSpec((nbk, rows, LANES), lambda h, r: (k0 // nbk + h, r, 0)),
                  pl.BlockSpec((nbv, rows, LANES), lambda h, r: (v0 // nbv + h, r, 0)),
                  pl.BlockSpec((nbv, rows, LANES), lambda h, r: (g0 // nbv + h, r, 0)),
                  pl.BlockSpec((rows, LANES), lambda h, r: (r, 0)),
                  pl.BlockSpec((1, LANES, dk), lambda h, r: (h, 0, 0)),
                  pl.BlockSpec((1, 1, dk), lambda h, r: (h, 0, 0)),
                  pl.BlockSpec((1, dv), lambda h, r: (0, 0))],
        out_specs=pl.BlockSpec((rows, dv), lambda h, r: (r, h)),
        out_shape=jax.ShapeDtypeStruct((s, G_HEADS * dv), BF16),
        scratch_shapes=[pltpu.VMEM((dv, dk), F32)],
        compiler_params=_params("parallel", "arbitrary"),
        name="gla",
    )(main, main, main, main, small, w2, bgk, gn.reshape(1, dv))


def _ffn_up_kernel(h_ref, wg_ref, wv_ref, cwg_ref, cwv_ref, cbg_ref, cbv_ref, o_ref, cg_scr, cv_scr,
                   *, tm):
    i = pl.program_id(1)

    @pl.when(i == 0)
    def _():
        cg_scr[...] = jnp.zeros(cg_scr.shape, F32)
        cv_scr[...] = jnp.zeros(cv_scr.shape, F32)

    hrow = h_ref[...]
    row = lax.broadcasted_iota(jnp.int32, (tm, 1), 0)

    def conv(w_ref, cw_ref, cb_ref, carry_scr):
        hw = _dot(hrow, w_ref[...])
        prev = carry_scr[...]
        p1 = prev[7:8, :]
        p2 = prev[6:7, :]
        r1 = jnp.where(row == 0, p1, pltpu.roll(hw, 1, 0))
        r2 = jnp.where(row == 0, p2, jnp.where(row == 1, p1, pltpu.roll(hw, 2, 0)))
        carry_scr[...] = hw[tm - 8:tm, :]
        return cw_ref[2:3, :] * hw + cw_ref[1:2, :] * r1 + cw_ref[0:1, :] * r2 + cb_ref[...]

    gate = conv(wg_ref, cwg_ref, cbg_ref, cg_scr)
    val = conv(wv_ref, cwv_ref, cbv_ref, cv_scr)
    o_ref[...] = (gate * jax.nn.sigmoid(gate) * val).astype(o_ref.dtype)


def _ffn_up(hb, wg, wv, cwg, cwv, cbg, cbv, *, tm, tn):
    s, d = hb.shape
    ffp = wg.shape[1]
    wspec = pl.BlockSpec((d, tn), lambda j, i: (0, j))
    cwspec = pl.BlockSpec((CONV_W, tn), lambda j, i: (0, j))
    cbspec = pl.BlockSpec((1, tn), lambda j, i: (0, j))
    return pl.pallas_call(
        functools.partial(_ffn_up_kernel, tm=tm),
        grid=(ffp // tn, s // tm),
        in_specs=[pl.BlockSpec((tm, d), lambda j, i: (i, 0)), wspec, wspec, cwspec, cwspec, cbspec, cbspec],
        out_specs=pl.BlockSpec((tm, tn), lambda j, i: (i, j)),
        out_shape=jax.ShapeDtypeStruct((s, ffp), BF16),
        scratch_shapes=[pltpu.VMEM((8, tn), F32), pltpu.VMEM((8, tn), F32)],
        compiler_params=_params("parallel", "arbitrary"),
        name="ffn_up",
    )(hb, wg, wv, cwg, cwv, cbg, cbv)


def _layout(d_model):
    a_width = A_HEADS * A_HEAD_DIM
    g_width = d_model - a_width
    g_kwidth = g_width // 2
    names = ("qa", "ka", "va", "qi", "ki", "wi", "qg", "kg", "vg", "glr", "gg")
    sizes = (a_width, a_width, a_width, IDX_HEADS * IDX_DIM, IDX_DIM, IDX_HEADS,
             g_kwidth, g_kwidth, g_width, G_LOWRANK, g_width)
    offs = np.concatenate([[0], np.cumsum(sizes)])
    src = {n: (int(offs[i]), int(offs[i + 1])) for i, n in enumerate(names)}
    order = ("qi", "qa", "ka", "va", "qg", "kg", "vg", "gg", "ki")
    lay, blk = {}, 0
    for n in order:
        width = src[n][1] - src[n][0]
        assert width % LANES == 0
        lay[n] = (blk, width // LANES)
        blk += width // LANES
    return src, order, lay, blk


def kernel(x, w_in, w_gk2, b_gk, attn_out_g, gla_norm_g, w_o, ln1_g, ln1_b, w_up, conv_w, conv_b,
           w_down, ln2_g, ln2_b):
    assert x.shape[0] == 1 and w_in.shape[0] == DEPTH == 1
    _, s, d = x.shape
    x2 = x[0]
    src, order, lay, nb_main = _layout(d)
    g_width = d - A_HEADS * A_HEAD_DIM
    dv = g_width // G_HEADS
    dk = dv // 2
    d_ff = w_down.shape[1]
    topk = min(TOPK_MAX, s // 4)

    w = w_in[0]
    w_main = jnp.concatenate([w[:, src[n][0]:src[n][1]] for n in order], axis=1).astype(BF16)
    n_small = IDX_HEADS + G_LOWRANK
    assert n_small <= LANES
    w_small = jnp.concatenate([w[:, src["wi"][0]:src["wi"][1]], w[:, src["glr"][0]:src["glr"][1]],
                               jnp.zeros((d, LANES - n_small), F32)], axis=1).astype(BF16)
    w2 = jnp.zeros((LANES, G_HEADS * dk), F32).at[IDX_HEADS:n_small].set(w_gk2[0])
    w2 = w2.reshape(LANES, G_HEADS, dk).transpose(1, 0, 2).astype(BF16)
    bgk = b_gk[0].reshape(G_HEADS, 1, dk)
    ffp = -(-d_ff // 1024) * 1024 if d_ff > 1024 else d_ff
    pad = ffp - d_ff
    wu = w_up[0]
    wg = jnp.pad(wu[:, :d_ff], ((0, 0), (0, pad))).astype(BF16)
    wv = jnp.pad(wu[:, d_ff:], ((0, 0), (0, pad))).astype(BF16)
    cwg = jnp.pad(conv_w[0][:, :d_ff], ((0, 0), (0, pad)))
    cwv = jnp.pad(conv_w[0][:, d_ff:], ((0, 0), (0, pad)))
    cbg = jnp.pad(conv_b[0][:d_ff], (0, pad)).reshape(1, ffp)
    cbv = jnp.pad(conv_b[0][d_ff:], (0, pad)).reshape(1, ffp)
    wd = jnp.pad(w_down[0], ((0, pad), (0, 0))).astype(BF16)
    wo = w_o[0].astype(BF16)
    xb = x2.astype(BF16)

    tm = _tile(s, 1024)
    main = _proj_blocks(xb, w_main, tm=tm, tn=_tile(nb_main * LANES, 640))
    small = _matmul(xb, w_small, tm=tm, tn=LANES, tk=d, out_dtype=F32, name="proj_small")
    bias = _indexer(main, small, lay, s=s, tq=256, tk=256, topk=topk)
    ta = _tile(s, 512)
    o_a = _attention(main, bias, attn_out_g[0], lay, s=s, tq=ta, tk=ta)
    o_g = _gla(main, small, w2, bgk, gla_norm_g[0], lay, s=s, rows=_tile(s, 512), chunk=64, dk=dk, dv=dv)
    mix_in = jnp.concatenate([o_a, o_g], axis=1)
    y1 = _matmul(mix_in, wo, tm=tm, tn=_tile(d, 512), tk=d, out_dtype=F32, residual=x2, alpha=DN_ALPHA,
                 name="w_o")
    h, hb = _layer_norm(y1, ln1_g[0], ln1_b[0], tr=_tile(s, 256, 8), also_bf16=True)

    act = _ffn_up(hb, wg, wv, cwg, cwv, cbg, cbv, tm=_tile(s, 512), tn=_tile(ffp, 512))
    y2 = _matmul(act, wd, tm=tm, tn=_tile(d, 1024), tk=_tile(ffp, 1024), out_dtype=F32, residual=h,
                 alpha=DN_ALPHA, name="w_down")
    (out,) = _layer_norm(y2, ln2_g[0], ln2_b[0], tr=_tile(s, 256, 8), also_bf16=False)
    return out[None]
```

```python
import functools

import numpy as np
import jax
import jax.numpy as jnp
from jax import lax
from jax.experimental import pallas as pl
from jax.experimental.pallas import tpu as pltpu

A_HEADS = 16
A_HEAD_DIM = 128
IDX_HEADS = 32
IDX_DIM = 128
TOPK_MAX = 256
G_HEADS = 4
G_LOWRANK = 16
G_TAU = 16.0
CONV_W = 3
LN_EPS = 1e-5
RMS_EPS = 1e-6
DEPTH = 1
DN_ALPHA = (2 * DEPTH) ** 0.25

LANES = 128
VMEM_LIMIT = 56 * 1024 * 1024

BF16 = jnp.bfloat16
F32 = jnp.float32
NEG_BIG = -1e30
LOG2E = 1.4426950408889634
INT_MIN = -2 ** 31


def _params(*sem):
    return pltpu.CompilerParams(dimension_semantics=sem, vmem_limit_bytes=VMEM_LIMIT)


def _tile(n, max_tile, quantum=LANES):
    best = None
    for t in range(quantum, min(n, max_tile) + 1, quantum):
        if n % t == 0:
            best = t
    assert best is not None, (n, max_tile, quantum)
    return best


def _dot(a, b):
    return jnp.dot(a, b, preferred_element_type=F32)


def _dot_nt(a, b):
    return lax.dot_general(a, b, (((1,), (1,)), ((), ())), preferred_element_type=F32)


def _dot_tn(a, b):
    return lax.dot_general(a, b, (((0,), (0,)), ((), ())), preferred_element_type=F32)


def _proj_kernel(x_ref, w_ref, o_ref):
    res = _dot(x_ref[...], w_ref[...])
    for c in range(o_ref.shape[0]):
        o_ref[c] = res[:, c * LANES:(c + 1) * LANES].astype(o_ref.dtype)


def _proj_blocks(xb, wb, *, tm, tn):
    s, k = xb.shape
    n = wb.shape[1]
    return pl.pallas_call(
        _proj_kernel,
        grid=(s // tm, n // tn),
        in_specs=[pl.BlockSpec((tm, k), lambda i, j: (i, 0)),
                  pl.BlockSpec((k, tn), lambda i, j: (0, j))],
        out_specs=pl.BlockSpec((tn // LANES, tm, LANES), lambda i, j: (j, i, 0)),
        out_shape=jax.ShapeDtypeStruct((n // LANES, s, LANES), BF16),
        compiler_params=_params("parallel", "arbitrary"),
        name="proj",
    )(xb, wb)


def _mm_kernel(a_ref, b_ref, *rest, alpha, has_res, nk):
    if has_res:
        r_ref, o_ref = rest[0], rest[1]
        rest = rest[2:]
    else:
        r_ref, o_ref = None, rest[0]
        rest = rest[1:]

    def finish(acc):
        if has_res:
            acc = acc + alpha * r_ref[...]
        o_ref[...] = acc.astype(o_ref.dtype)

    if nk == 1:
        finish(_dot(a_ref[...], b_ref[...]))
        return
    acc_ref = rest[0]
    kk = pl.program_id(2)

    @pl.when(kk == 0)
    def _():
        acc_ref[...] = _dot(a_ref[...], b_ref[...])

    @pl.when(kk > 0)
    def _():
        acc_ref[...] += _dot(a_ref[...], b_ref[...])

    @pl.when(kk == nk - 1)
    def _():
        finish(acc_ref[...])


def _matmul(a, b, *, tm, tn, tk, out_dtype, residual=None, alpha=1.0, name="matmul"):
    m, k = a.shape
    n = b.shape[1]
    nk = k // tk
    in_specs = [pl.BlockSpec((tm, tk), lambda i, j, kk: (i, kk)),
                pl.BlockSpec((tk, tn), lambda i, j, kk: (kk, j))]
    args = [a, b]
    if residual is not None:
        in_specs.append(pl.BlockSpec((tm, tn), lambda i, j, kk: (i, j)))
        args.append(residual)
    scratch = [pltpu.VMEM((tm, tn), F32)] if nk > 1 else []
    return pl.pallas_call(
        functools.partial(_mm_kernel, alpha=alpha, has_res=residual is not None, nk=nk),
        grid=(m // tm, n // tn, nk),
        in_specs=in_specs,
        out_specs=pl.BlockSpec((tm, tn), lambda i, j, kk: (i, j)),
        out_shape=jax.ShapeDtypeStruct((m, n), out_dtype),
        scratch_shapes=scratch,
        compiler_params=_params("parallel", "parallel", "arbitrary"),
        name=name,
    )(*args)


def _ln_kernel(y_ref, g_ref, b_ref, *o_refs):
    y = y_ref[...]
    mu = jnp.mean(y, axis=1, keepdims=True)
    yc = y - mu
    var = jnp.mean(yc * yc, axis=1, keepdims=True)
    out = yc * lax.rsqrt(var + LN_EPS) * g_ref[...] + b_ref[...]
    for o_ref in o_refs:
        o_ref[...] = out.astype(o_ref.dtype)


def _layer_norm(y, g, b, *, tr, also_bf16):
    s, d = y.shape
    out_shape = [jax.ShapeDtypeStruct((s, d), F32)]
    out_specs = [pl.BlockSpec((tr, d), lambda i: (i, 0))]
    if also_bf16:
        out_shape.append(jax.ShapeDtypeStruct((s, d), BF16))
        out_specs.append(pl.BlockSpec((tr, d), lambda i: (i, 0)))
    return pl.pallas_call(
        _ln_kernel,
        grid=(s // tr,),
        in_specs=[pl.BlockSpec((tr, d), lambda i: (i, 0)),
                  pl.BlockSpec((1, d), lambda i: (0, 0)),
                  pl.BlockSpec((1, d), lambda i: (0, 0))],
        out_specs=out_specs,
        out_shape=out_shape,
        compiler_params=_params("parallel"),
        name="layer_norm",
    )(y, g.reshape(1, d), b.reshape(1, d))


def _idx_kernel(qi_ref, ki_ref, sm_ref, bias_ref, hi_scr, lo_scr, wb_scr, *, tq, tk, nkt, hi, group,
                topk, wscale):
    qb = pl.program_id(0)
    nk = ((qb + 1) * tq + tk - 1) // tk
    reps = tk // LANES
    i16 = jnp.int16
    low16 = -2 ** 15

    for h in range(hi):
        wb_scr[h] = jnp.broadcast_to(sm_ref[:, h:h + 1] * wscale, (tq, LANES))

    row = qb * tq + lax.broadcasted_iota(jnp.int32, (tq, tk), 0)
    col = lax.broadcasted_iota(jnp.int32, (tq, tk), 1)

    def lanes(a):
        return jnp.concatenate([a] * reps, axis=1) if reps > 1 else a

    def fold(a):
        out = a[:, :LANES]
        for r in range(1, reps):
            out = out + a[:, r * LANES:(r + 1) * LANES]
        return out

    def score_tile(kb, carry):
        kt = ki_ref[0, pl.ds(pl.multiple_of(kb * tk, tk), tk), :]
        acc = jnp.zeros((tq, tk), F32)
        for g in range(hi // group):
            qs = qi_ref[g * group:(g + 1) * group].reshape(group * tq, IDX_DIM)
            d = _dot_nt(qs, kt)
            for u in range(group):
                acc = acc + jnp.maximum(d[u * tq:(u + 1) * tq], 0.0) * lanes(wb_scr[g * group + u])
        bits = pltpu.bitcast(acc + 0.0, jnp.int32)
        key = jnp.where(kb * tk + col <= row, bits ^ ((bits >> 31) & jnp.int32(0x7FFFFFFF)), INT_MIN)
        hi_scr[kb] = (key >> 16).astype(i16)
        lo_scr[kb] = ((key & 0xFFFF) + low16).astype(i16)
        return carry

    lax.fori_loop(0, nk, score_tile, 0)

    def count_ge(ref, cand):
        cand_t = lanes(cand.astype(i16))

        def step(kb, c):
            return c + fold(jnp.where(ref[kb] >= cand_t, i16(1), i16(0)))

        c = lax.fori_loop(0, nk, step, jnp.zeros((tq, LANES), i16))
        return jnp.sum(c.astype(F32), axis=1, keepdims=True)

    def search(ref, base_cnt):
        def bit_step(it, v):
            cand = v + jnp.left_shift(jnp.int32(1), 15 - it)
            return jnp.where(base_cnt + count_ge(ref, cand) >= float(topk), cand, v)

        return lax.fori_loop(0, 16, bit_step, jnp.full((tq, LANES), low16, jnp.int32))

    v_hi = search(hi_scr, 0.0)
    n_above = count_ge(hi_scr, v_hi + 1)
    v_hi_t = lanes(v_hi.astype(i16))

    def keep_equal(kb, carry):
        lo_scr[kb] = jnp.where(hi_scr[kb] == v_hi_t, lo_scr[kb], i16(low16))
        return carry

    lax.fori_loop(0, nk, keep_equal, 0)
    v_lo = search(lo_scr, n_above)
    v_lo = jnp.where(v_hi == low16, jnp.maximum(v_lo, low16 + 1), v_lo)
    v_lo_t = lanes(v_lo.astype(i16))
    zero = jnp.zeros((tq, tk), bias_ref.dtype)
    neg = jnp.full((tq, tk), NEG_BIG, bias_ref.dtype)

    def emit(kb, carry):
        h16 = hi_scr[kb]
        sel = (h16 > v_hi_t) | ((h16 == v_hi_t) & (lo_scr[kb] >= v_lo_t))
        bias_ref[0, kb] = jnp.where(sel, zero, neg)
        return carry

    lax.fori_loop(0, nk, emit, 0)

    def fill(kb, carry):
        bias_ref[0, kb] = neg
        return carry

    lax.fori_loop(nk, nkt, fill, 0)


def _indexer(main, small, lay, *, s, tq, tk, topk):
    nkt = s // tk
    hi = IDX_HEADS
    group = min(8, hi)
    qi0, ki0 = lay["qi"][0], lay["ki"][0]
    assert qi0 % hi == 0
    return pl.pallas_call(
        functools.partial(_idx_kernel, tq=tq, tk=tk, nkt=nkt, hi=hi, group=group, topk=topk,
                          wscale=IDX_HEADS ** -0.5 * IDX_DIM ** -0.5),
        grid=(s // tq,),
        in_specs=[pl.BlockSpec((hi, tq, LANES), lambda i: (qi0 // hi, i, 0)),
                  pl.BlockSpec((1, s, LANES), lambda i: (ki0, 0, 0)),
                  pl.BlockSpec((tq, LANES), lambda i: (i, 0))],
        out_specs=pl.BlockSpec((1, nkt, tq, tk), lambda i: (i, 0, 0, 0)),
        out_shape=jax.ShapeDtypeStruct((s // tq, nkt, tq, tk), BF16),
        scratch_shapes=[pltpu.VMEM((nkt, tq, tk), jnp.int16),
                        pltpu.VMEM((nkt, tq, tk), jnp.int16),
                        pltpu.VMEM((hi, tq, LANES), F32)],
        compiler_params=_params("parallel"),
        name="indexer",
    )(main, main, small)


def _attn_kernel(q_ref, k_ref, v_ref, b_ref, g_ref, o_ref, bias_scr, m_scr, l_scr, acc_scr, *pipe,
                 nh, tq, tk, scale):
    qi = pl.program_id(0)
    kb = pl.program_id(1)

    @pl.when(kb == 0)
    def _():
        m_scr[...] = jnp.full(m_scr.shape, NEG_BIG, F32)
        l_scr[...] = jnp.zeros(l_scr.shape, F32)
        acc_scr[...] = jnp.zeros(acc_scr.shape, F32)

    @pl.when(kb * tk <= qi * tq + (tq - 1))
    def _():
        na, nb, sq, sk = b_ref.shape
        for a in range(na):
            for b in range(nb):
                bias_scr[a * sq:(a + 1) * sq, b * sk:(b + 1) * sk] = b_ref[a, b].astype(F32)

        c2 = scale * LOG2E

        def qk(h, s_ref):
            s_ref[...] = _dot_nt(q_ref[h], k_ref[h])

        def soft(h, s_ref, p_ref, al_ref):
            s = s_ref[...] * c2 + bias_scr[...]
            m_prev = m_scr[h]
            m_new = jnp.maximum(m_prev, jnp.max(s, axis=1, keepdims=True))
            p = jnp.exp2(s - m_new[:, :1])
            alpha = jnp.exp2(m_prev - m_new)
            l_scr[h] = alpha * l_scr[h] + jnp.sum(p, axis=1, keepdims=True)
            m_scr[h] = m_new
            p_ref[...] = p.astype(BF16)
            al_ref[...] = alpha

        def pv(h, p_ref, al_ref):
            acc_scr[h] = acc_scr[h] * al_ref[...] + _dot(p_ref[...], v_ref[h])

        s_a, s_b, p_a, p_b, al_a, al_b = pipe
        qk(0, s_a)
        qk(1, s_b)
        soft(0, s_a, p_a, al_a)

        def pair(j, carry):
            qk(2 * j, s_a)
            soft(2 * j - 1, s_b, p_b, al_b)
            pv(2 * j - 2, p_a, al_a)
            qk(2 * j + 1, s_b)
            soft(2 * j, s_a, p_a, al_a)
            pv(2 * j - 1, p_b, al_b)
            return carry

        lax.fori_loop(1, nh // 2, pair, 0)
        soft(nh - 1, s_b, p_b, al_b)
        pv(nh - 2, p_a, al_a)
        pv(nh - 1, p_b, al_b)

    @pl.when(kb == pl.num_programs(1) - 1)
    def _():
        ss = jnp.zeros((tq, LANES), F32)
        for h in range(nh):
            o_h = acc_scr[h] / l_scr[h]
            acc_scr[h] = o_h
            ss = ss + o_h * o_h
        ms = jnp.sum(ss, axis=1, keepdims=True) * (1.0 / (nh * LANES))
        r = lax.rsqrt(ms + RMS_EPS)
        for h in range(nh):
            sl = slice(h * LANES, (h + 1) * LANES)
            o_ref[:, sl] = (acc_scr[h] * r * g_ref[:, sl]).astype(o_ref.dtype)


def _attention(main, bias, g, lay, *, s, tq, tk):
    nh = A_HEADS
    nqb, nkt, sq, sk = bias.shape
    q0, k0, v0 = lay["qa"][0], lay["ka"][0], lay["va"][0]
    assert q0 % nh == 0 and k0 % nh == 0 and v0 % nh == 0

    def last_kb(i):
        return (i * tq + tq - 1) // tk

    return pl.pallas_call(
        functools.partial(_attn_kernel, nh=nh, tq=tq, tk=tk, scale=A_HEAD_DIM ** -0.5),
        grid=(s // tq, s // tk),
        in_specs=[pl.BlockSpec((nh, tq, LANES), lambda i, j: (q0 // nh, i, 0)),
                  pl.BlockSpec((nh, tk, LANES), lambda i, j: (k0 // nh, jnp.minimum(j, last_kb(i)), 0)),
                  pl.BlockSpec((nh, tk, LANES), lambda i, j: (v0 // nh, jnp.minimum(j, last_kb(i)), 0)),
                  pl.BlockSpec((tq // sq, tk // sk, sq, sk),
                               lambda i, j: (i, jnp.minimum(j, last_kb(i)), 0, 0)),
                  pl.BlockSpec((1, nh * LANES), lambda i, j: (0, 0))],
        out_specs=pl.BlockSpec((tq, nh * LANES), lambda i, j: (i, 0)),
        out_shape=jax.ShapeDtypeStruct((s, nh * LANES), BF16),
        scratch_shapes=[pltpu.VMEM((tq, tk), F32),
                        pltpu.VMEM((nh, tq, LANES), F32),
                        pltpu.VMEM((nh, tq, LANES), F32),
                        pltpu.VMEM((nh, tq, LANES), F32),
                        pltpu.VMEM((tq, tk), F32), pltpu.VMEM((tq, tk), F32),
                        pltpu.VMEM((tq, tk), BF16), pltpu.VMEM((tq, tk), BF16),
                        pltpu.VMEM((tq, LANES), F32), pltpu.VMEM((tq, LANES), F32)],
        compiler_params=_params("parallel", "arbitrary"),
        name="attention",
    )(main, main, main, bias, g.reshape(1, nh * LANES))


def _gla_kernel(q_ref, k_ref, v_ref, gg_ref, sm_ref, w2_ref, bgk_ref, gn_ref, o_ref, st_scr, *,
                rows, chunk, dk, dv):
    @pl.when(pl.program_id(1) == 0)
    def _():
        st_scr[...] = jnp.zeros(st_scr.shape, F32)

    r_i = lax.broadcasted_iota(jnp.int32, (chunk, chunk), 0)
    c_i = lax.broadcasted_iota(jnp.int32, (chunk, chunk), 1)
    lower = r_i >= c_i
    tri = jnp.where(lower, 1.0, 0.0).astype(BF16)
    mid = chunk // 2

    def wide(ref, r0):
        return jnp.concatenate([ref[j, pl.ds(r0, chunk), :] for j in range(ref.shape[0])], axis=1)

    def step(c, carry):
        r0 = pl.multiple_of(c * chunk, chunk)
        q = wide(q_ref, r0).astype(F32) * (dk ** -0.5)
        k = wide(k_ref, r0).astype(F32)
        v = wide(v_ref, r0)
        z = _dot(sm_ref[pl.ds(r0, chunk), :].astype(BF16), w2_ref[0]) + bgk_ref[0]
        g = (jnp.minimum(z, 0.0) - jnp.log1p(jnp.exp(-jnp.abs(z)))) * (1.0 / G_TAU)
        g_hi = g.astype(BF16)
        rem = g - g_hi.astype(F32)
        g_mid = rem.astype(BF16)
        g_lo = (rem - g_mid.astype(F32)).astype(BF16)
        b = _dot(tri, g_hi) + _dot(tri, g_mid) + _dot(tri, g_lo)
        b_mid = b[mid:mid + 1, :]
        b_last = b[chunk - 1:chunk, :]
        qe = (q * jnp.exp(b - b_mid)).astype(BF16)
        ke = (k * jnp.exp(b_mid - b)).astype(BF16)
        att = jnp.where(lower, _dot_nt(qe, ke), 0.0)
        st = st_scr[...]
        o = _dot(att.astype(BF16), v) + _dot_nt((q * jnp.exp(b)).astype(BF16), st.astype(BF16))
        kd = (k * jnp.exp(b_last - b)).astype(BF16)
        st_scr[...] = st * jnp.exp(b_last) + _dot_tn(v, kd)
        ms = jnp.mean(o * o, axis=1, keepdims=True)
        gate = wide(gg_ref, r0).astype(F32)
        out = (o * lax.rsqrt(ms + RMS_EPS)) * gn_ref[...] * (gate * jax.nn.sigmoid(gate))
        o_ref[pl.ds(r0, chunk), :] = out.astype(o_ref.dtype)
        return carry

    lax.fori_loop(0, rows // chunk, step, 0)


def _gla(main, small, w2, bgk, gn, lay, *, s, rows, chunk, dk, dv):
    nbk, nbv = dk // LANES, dv // LANES
    q0, k0, v0, g0 = lay["qg"][0], lay["kg"][0], lay["vg"][0], lay["gg"][0]
    assert q0 % nbk == 0 and k0 % nbk == 0 and v0 % nbv == 0 and g0 % nbv == 0
    return pl.pallas_call(
        functools.partial(_gla_kernel, rows=rows, chunk=chunk, dk=dk, dv=dv),
        grid=(G_HEADS, s // rows),
        in_specs=[pl.BlockSpec((nbk, rows, LANES), lambda h, r: (q0 // nbk + h, r, 0)),
                  pl.BlockSpec((nbk, rows, LANES), lambda h, r: (k0 // nbk + h, r, 0)),
                  pl.BlockSpec((nbv, rows, LANES), lambda h, r: (v0 // nbv + h, r, 0)),
                  pl.BlockSpec((nbv, rows, LANES), lambda h, r: (g0 // nbv + h, r, 0)),
                  pl.BlockSpec((rows, LANES), lambda h, r: (r, 0)),
                  pl.BlockSpec((1, LANES, dk), lambda h, r: (h, 0, 0)),
                  pl.BlockSpec((1, 1, dk), lambda h, r: (h, 0, 0)),
                  pl.BlockSpec((1, dv), lambda h, r: (0, 0))],
        out_specs=pl.BlockSpec((rows, dv), lambda h, r: (r, h)),
        out_shape=jax.ShapeDtypeStruct((s, G_HEADS * dv), BF16),
        scratch_shapes=[pltpu.VMEM((dv, dk), F32)],
        compiler_params=_params("parallel", "arbitrary"),
        name="gla",
    )(main, main, main, main, small, w2, bgk, gn.reshape(1, dv))


def _ffn_up_kernel(h_ref, wg_ref, wv_ref, cwg_ref, cwv_ref, cbg_ref, cbv_ref, o_ref, w_scr, tail_scr,
                   *, tm, tn, rc):
    @pl.when(pl.program_id(1) == 0)
    def _():
        w_scr[:, :tn] = wg_ref[0].astype(BF16)
        w_scr[:, tn:] = wv_ref[0].astype(BF16)
        tail_scr[...] = jnp.zeros(tail_scr.shape, F32)

    cw = jnp.concatenate([cwg_ref[0], cwv_ref[0]], axis=1)
    cb = jnp.concatenate([cbg_ref[...], cbv_ref[...]], axis=1)
    w0, w1, w2 = cw[0:1, :], cw[1:2, :], cw[2:3, :]

    def taps(a):
        return w2 * a + w1 * pltpu.roll(a, 1, 0) + w0 * pltpu.roll(a, 2, 0) + cb

    tail = tail_scr[...]
    for r in range(tm // rc):
        rows = slice(r * rc, (r + 1) * rc)
        hw = _dot(h_ref[rows, :], w_scr[...])
        top = taps(jnp.concatenate([tail, hw[0:8]], axis=0))[8:16]
        u = jnp.concatenate([top, taps(hw)[8:]], axis=0)
        gate, val = u[:, :tn], u[:, tn:]
        o_ref[rows, :] = (gate * jax.nn.sigmoid(gate) * val).astype(o_ref.dtype)
        tail = hw[rc - 8:rc, :]
    tail_scr[...] = tail


def _ffn_up(hb, w_up, conv_w, conv_b, *, d_ff, tm, tn):
    s, d = hb.shape
    nj = d_ff // tn
    return pl.pallas_call(
        functools.partial(_ffn_up_kernel, tm=tm, tn=tn, rc=min(tm, 512)),
        grid=(nj, s // tm),
        in_specs=[pl.BlockSpec((tm, d), lambda j, i: (i, 0)),
                  pl.BlockSpec((1, d, tn), lambda j, i: (0, 0, j)),
                  pl.BlockSpec((1, d, tn), lambda j, i: (0, 0, nj + j)),
                  pl.BlockSpec((1, CONV_W, tn), lambda j, i: (0, 0, j)),
                  pl.BlockSpec((1, CONV_W, tn), lambda j, i: (0, 0, nj + j)),
                  pl.BlockSpec((1, tn), lambda j, i: (0, j)),
                  pl.BlockSpec((1, tn), lambda j, i: (0, nj + j))],
        out_specs=pl.BlockSpec((tm, tn), lambda j, i: (i, j)),
        out_shape=jax.ShapeDtypeStruct((s, d_ff), BF16),
        scratch_shapes=[pltpu.VMEM((d, 2 * tn), BF16), pltpu.VMEM((8, 2 * tn), F32)],
        compiler_params=_params("parallel", "arbitrary"),
        name="ffn_up",
    )(hb, w_up, w_up, conv_w, conv_w, conv_b, conv_b)


def _layout(d_model):
    a_width = A_HEADS * A_HEAD_DIM
    g_width = d_model - a_width
    g_kwidth = g_width // 2
    names = ("qa", "ka", "va", "qi", "ki", "wi", "qg", "kg", "vg", "glr", "gg")
    sizes = (a_width, a_width, a_width, IDX_HEADS * IDX_DIM, IDX_DIM, IDX_HEADS,
             g_kwidth, g_kwidth, g_width, G_LOWRANK, g_width)
    offs = np.concatenate([[0], np.cumsum(sizes)])
    src = {n: (int(offs[i]), int(offs[i + 1])) for i, n in enumerate(names)}
    order = ("qi", "qa", "ka", "va", "qg", "kg", "vg", "gg", "ki")
    lay, blk = {}, 0
    for n in order:
        width = src[n][1] - src[n][0]
        assert width % LANES == 0
        lay[n] = (blk, width // LANES)
        blk += width // LANES
    return src, order, lay, blk


def kernel(x, w_in, w_gk2, b_gk, attn_out_g, gla_norm_g, w_o, ln1_g, ln1_b, w_up, conv_w, conv_b,
           w_down, ln2_g, ln2_b):
    assert x.shape[0] == 1 and w_in.shape[0] == DEPTH == 1
    _, s, d = x.shape
    x2 = x[0]
    src, order, lay, nb_main = _layout(d)
    g_width = d - A_HEADS * A_HEAD_DIM
    dv = g_width // G_HEADS
    dk = dv // 2
    d_ff = w_down.shape[1]
    topk = min(TOPK_MAX, s // 4)

    w = w_in[0]
    w_main = jnp.concatenate([w[:, src[n][0]:src[n][1]] for n in order] + [jnp.zeros((d, LANES), F32)],
                             axis=1).astype(BF16)
    n_small = IDX_HEADS + G_LOWRANK
    assert n_small <= LANES
    w_small = jnp.concatenate([w[:, src["wi"][0]:src["wi"][1]], w[:, src["glr"][0]:src["glr"][1]],
                               jnp.zeros((d, LANES - n_small), F32)], axis=1).astype(BF16)
    w2 = jnp.zeros((LANES, G_HEADS * dk), F32).at[IDX_HEADS:n_small].set(w_gk2[0])
    w2 = w2.reshape(LANES, G_HEADS, dk).transpose(1, 0, 2).astype(BF16)
    bgk = b_gk[0].reshape(G_HEADS, 1, dk)
    wd = w_down[0].astype(BF16)
    wo = w_o[0].astype(BF16)
    xb = x2.astype(BF16)

    tm = _tile(s, 1024)
    main = _proj_blocks(xb, w_main, tm=_tile(s, 512), tn=_tile((nb_main + 1) * LANES, 1280))
    small = _matmul(xb, w_small, tm=tm, tn=LANES, tk=d, out_dtype=F32, name="proj_small")
    bias = _indexer(main, small, lay, s=s, tq=256, tk=256, topk=topk)
    ta = _tile(s, 512)
    o_a = _attention(main, bias, attn_out_g[0], lay, s=s, tq=ta, tk=ta)
    o_g = _gla(main, small, w2, bgk, gla_norm_g[0], lay, s=s, rows=_tile(s, 512), chunk=64, dk=dk, dv=dv)
    mix_in = jnp.concatenate([o_a, o_g], axis=1)
    y1 = _matmul(mix_in, wo, tm=tm, tn=_tile(d, 512), tk=d, out_dtype=F32, residual=x2, alpha=DN_ALPHA,
                 name="w_o")
    h, hb = _layer_norm(y1, ln1_g[0], ln1_b[0], tr=_tile(s, 256, 8), also_bf16=True)

    act = _ffn_up(hb, w_up, conv_w, conv_b[0].reshape(1, 2 * d_ff), d_ff=d_ff, tm=tm, tn=_tile(d_ff, 256))
    y2 = _matmul(act, wd, tm=_tile(s, 512), tn=_tile(d, 512), tk=d_ff, out_dtype=F32, residual=h,
                 alpha=DN_ALPHA, name="w_down")
    (out,) = _layer_norm(y2, ln2_g[0], ln2_b[0], tr=_tile(s, 256, 8), also_bf16=False)
    return out[None]
```

```python
import functools

import numpy as np
import jax
import jax.numpy as jnp
from jax import lax
from jax.experimental import pallas as pl
from jax.experimental.pallas import tpu as pltpu

A_HEADS = 16
A_HEAD_DIM = 128
IDX_HEADS = 32
IDX_DIM = 128
TOPK_MAX = 256
G_HEADS = 4
G_LOWRANK = 16
G_TAU = 16.0
CONV_W = 3
LN_EPS = 1e-5
RMS_EPS = 1e-6
DEPTH = 1
DN_ALPHA = (2 * DEPTH) ** 0.25

LANES = 128
VMEM_LIMIT = 56 * 1024 * 1024

BF16 = jnp.bfloat16
F32 = jnp.float32
NEG_BIG = -1e30
LOG2E = 1.4426950408889634
INT_MIN = -2 ** 31


def _params(*sem):
    return pltpu.CompilerParams(dimension_semantics=sem, vmem_limit_bytes=VMEM_LIMIT)


def _tile(n, max_tile, quantum=LANES):
    best = None
    for t in range(quantum, min(n, max_tile) + 1, quantum):
        if n % t == 0:
            best = t
    assert best is not None, (n, max_tile, quantum)
    return best


def _dot(a, b):
    return jnp.dot(a, b, preferred_element_type=F32)


def _dot_nt(a, b):
    return lax.dot_general(a, b, (((1,), (1,)), ((), ())), preferred_element_type=F32)


def _dot_tn(a, b):
    return lax.dot_general(a, b, (((0,), (0,)), ((), ())), preferred_element_type=F32)


def _proj_kernel(x_ref, w_ref, ws_ref, sc_ref, o_ref, os_ref, xb_scr):
    @pl.when(pl.program_id(1) == 0)
    def _():
        xb_scr[...] = x_ref[...].astype(BF16)
        os_ref[...] = _dot(xb_scr[...], ws_ref[...])

    res = _dot(xb_scr[...], w_ref[...]) * sc_ref[...]
    for c in range(o_ref.shape[0]):
        o_ref[c] = res[:, c * LANES:(c + 1) * LANES].astype(o_ref.dtype)


def _proj_blocks(x, wb, wsb, col_scale, *, tm, tn):
    s, k = x.shape
    n = wb.shape[1]
    return pl.pallas_call(
        _proj_kernel,
        grid=(s // tm, n // tn),
        in_specs=[pl.BlockSpec((tm, k), lambda i, j: (i, 0)),
                  pl.BlockSpec((k, tn), lambda i, j: (0, j)),
                  pl.BlockSpec((k, LANES), lambda i, j: (0, 0)),
                  pl.BlockSpec((1, tn), lambda i, j: (0, j))],
        out_specs=[pl.BlockSpec((tn // LANES, tm, LANES), lambda i, j: (j, i, 0)),
                   pl.BlockSpec((tm, LANES), lambda i, j: (i, 0))],
        out_shape=[jax.ShapeDtypeStruct((n // LANES, s, LANES), BF16),
                   jax.ShapeDtypeStruct((s, LANES), F32)],
        scratch_shapes=[pltpu.VMEM((tm, k), BF16)],
        compiler_params=_params("parallel", "arbitrary"),
        name="proj",
    )(x, wb, wsb, col_scale)


def _mm_kernel(a_ref, b_ref, *rest, alpha, has_res, nk):
    if has_res:
        r_ref, o_ref = rest[0], rest[1]
        rest = rest[2:]
    else:
        r_ref, o_ref = None, rest[0]
        rest = rest[1:]

    def finish(acc):
        if has_res:
            acc = acc + alpha * r_ref[...]
        o_ref[...] = acc.astype(o_ref.dtype)

    if nk == 1:
        finish(_dot(a_ref[...], b_ref[...]))
        return
    acc_ref = rest[0]
    kk = pl.program_id(2)

    @pl.when(kk == 0)
    def _():
        acc_ref[...] = _dot(a_ref[...], b_ref[...])

    @pl.when(kk > 0)
    def _():
        acc_ref[...] += _dot(a_ref[...], b_ref[...])

    @pl.when(kk == nk - 1)
    def _():
        finish(acc_ref[...])


def _matmul(a, b, *, tm, tn, tk, out_dtype, residual=None, alpha=1.0, name="matmul"):
    m, k = a.shape
    n = b.shape[1]
    nk = k // tk
    in_specs = [pl.BlockSpec((tm, tk), lambda i, j, kk: (i, kk)),
                pl.BlockSpec((tk, tn), lambda i, j, kk: (kk, j))]
    args = [a, b]
    if residual is not None:
        in_specs.append(pl.BlockSpec((tm, tn), lambda i, j, kk: (i, j)))
        args.append(residual)
    scratch = [pltpu.VMEM((tm, tn), F32)] if nk > 1 else []
    return pl.pallas_call(
        functools.partial(_mm_kernel, alpha=alpha, has_res=residual is not None, nk=nk),
        grid=(m // tm, n // tn, nk),
        in_specs=in_specs,
        out_specs=pl.BlockSpec((tm, tn), lambda i, j, kk: (i, j)),
        out_shape=jax.ShapeDtypeStruct((m, n), out_dtype),
        scratch_shapes=scratch,
        compiler_params=_params("parallel", "parallel", "arbitrary"),
        name=name,
    )(*args)


def _wo_kernel(oa_ref, og_ref, wt_ref, wb_ref, x_ref, o_ref, w_scr, *, ka, alpha):
    @pl.when(pl.program_id(1) == 0)
    def _():
        w_scr[:ka, :] = wt_ref[0].astype(BF16)
        w_scr[ka:, :] = wb_ref[0].astype(BF16)

    acc = _dot(oa_ref[...], w_scr[:ka, :]) + _dot(og_ref[...], w_scr[ka:, :])
    o_ref[...] = acc + alpha * x_ref[...]


def _w_o(o_a, o_g, w_o, x, *, tm, tn, alpha):
    s, ka = o_a.shape
    kg = o_g.shape[1]
    d = w_o.shape[2]
    assert ka % kg == 0
    return pl.pallas_call(
        functools.partial(_wo_kernel, ka=ka, alpha=alpha),
        grid=(d // tn, s // tm),
        in_specs=[pl.BlockSpec((tm, ka), lambda j, i: (i, 0)),
                  pl.BlockSpec((tm, kg), lambda j, i: (i, 0)),
                  pl.BlockSpec((1, ka, tn), lambda j, i: (0, 0, j)),
                  pl.BlockSpec((1, kg, tn), lambda j, i: (0, ka // kg, j)),
                  pl.BlockSpec((tm, tn), lambda j, i: (i, j))],
        out_specs=pl.BlockSpec((tm, tn), lambda j, i: (i, j)),
        out_shape=jax.ShapeDtypeStruct((s, d), F32),
        scratch_shapes=[pltpu.VMEM((ka + kg, tn), BF16)],
        compiler_params=_params("parallel", "arbitrary"),
        name="w_o",
    )(o_a, o_g, w_o, w_o, x)


def _ln_kernel(y_ref, g_ref, b_ref, *o_refs):
    y = y_ref[...]
    mu = jnp.mean(y, axis=1, keepdims=True)
    yc = y - mu
    var = jnp.mean(yc * yc, axis=1, keepdims=True)
    out = yc * lax.rsqrt(var + LN_EPS) * g_ref[...] + b_ref[...]
    for o_ref in o_refs:
        o_ref[...] = out.astype(o_ref.dtype)


def _layer_norm(y, g, b, *, tr, also_bf16):
    s, d = y.shape
    out_shape = [jax.ShapeDtypeStruct((s, d), F32)]
    out_specs = [pl.BlockSpec((tr, d), lambda i: (i, 0))]
    if also_bf16:
        out_shape.append(jax.ShapeDtypeStruct((s, d), BF16))
        out_specs.append(pl.BlockSpec((tr, d), lambda i: (i, 0)))
    return pl.pallas_call(
        _ln_kernel,
        grid=(s // tr,),
        in_specs=[pl.BlockSpec((tr, d), lambda i: (i, 0)),
                  pl.BlockSpec((1, d), lambda i: (0, 0)),
                  pl.BlockSpec((1, d), lambda i: (0, 0))],
        out_specs=out_specs,
        out_shape=out_shape,
        compiler_params=_params("parallel"),
        name="layer_norm",
    )(y, g.reshape(1, d), b.reshape(1, d))


def _idx_kernel(qi_ref, ki_ref, sm_ref, bias_ref, hi_scr, lo_scr, wb_scr, *, tq, tk, nkt, hi, group,
                topk, wscale):
    qb = pl.program_id(0)
    nk = ((qb + 1) * tq + tk - 1) // tk
    reps = tk // LANES
    i16 = jnp.int16
    low16 = -2 ** 15

    for h in range(hi):
        wb_scr[h] = jnp.broadcast_to(sm_ref[:, h:h + 1] * wscale, (tq, LANES))

    row = qb * tq + lax.broadcasted_iota(jnp.int32, (tq, tk), 0)
    col = lax.broadcasted_iota(jnp.int32, (tq, tk), 1)

    def lanes(a):
        return jnp.concatenate([a] * reps, axis=1) if reps > 1 else a

    def fold(a):
        out = a[:, :LANES]
        for r in range(1, reps):
            out = out + a[:, r * LANES:(r + 1) * LANES]
        return out

    def score_tile(kb, carry):
        kt = ki_ref[0, pl.ds(pl.multiple_of(kb * tk, tk), tk), :]
        acc = jnp.zeros((tq, tk), F32)
        for g in range(hi // group):
            qs = qi_ref[g * group:(g + 1) * group].reshape(group * tq, IDX_DIM)
            d = _dot_nt(qs, kt)
            for u in range(group):
                acc = acc + jnp.maximum(d[u * tq:(u + 1) * tq], 0.0) * lanes(wb_scr[g * group + u])
        bits = pltpu.bitcast(acc + 0.0, jnp.int32)
        key = jnp.where(kb * tk + col <= row, bits ^ ((bits >> 31) & jnp.int32(0x7FFFFFFF)), INT_MIN)
        hi_scr[kb] = (key >> 16).astype(i16)
        lo_scr[kb] = ((key & 0xFFFF) + low16).astype(i16)
        return carry

    lax.fori_loop(0, nk, score_tile, 0)

    def count_ge(ref, cand):
        cand_t = lanes(cand.astype(i16))

        def step(kb, c):
            return c + fold(jnp.where(ref[kb] >= cand_t, i16(1), i16(0)))

        c = lax.fori_loop(0, nk, step, jnp.zeros((tq, LANES), i16))
        return jnp.sum(c.astype(F32), axis=1, keepdims=True)

    def search(ref, base_cnt):
        def bit_step(it, v):
            cand = v + jnp.left_shift(jnp.int32(1), 15 - it)
            return jnp.where(base_cnt + count_ge(ref, cand) >= float(topk), cand, v)

        return lax.fori_loop(0, 16, bit_step, jnp.full((tq, LANES), low16, jnp.int32))

    v_hi = search(hi_scr, 0.0)
    n_above = count_ge(hi_scr, v_hi + 1)
    v_hi_t = lanes(v_hi.astype(i16))

    def keep_equal(kb, carry):
        lo_scr[kb] = jnp.where(hi_scr[kb] == v_hi_t, lo_scr[kb], i16(low16))
        return carry

    lax.fori_loop(0, nk, keep_equal, 0)
    v_lo = search(lo_scr, n_above)
    v_lo = jnp.where(v_hi == low16, jnp.maximum(v_lo, low16 + 1), v_lo)
    v_lo_t = lanes(v_lo.astype(i16))
    zero = jnp.zeros((tq, tk), bias_ref.dtype)
    neg = jnp.full((tq, tk), NEG_BIG, bias_ref.dtype)

    def emit(kb, carry):
        h16 = hi_scr[kb]
        sel = (h16 > v_hi_t) | ((h16 == v_hi_t) & (lo_scr[kb] >= v_lo_t))
        bias_ref[0, kb] = jnp.where(sel, zero, neg)
        return carry

    lax.fori_loop(0, nk, emit, 0)

    def fill(kb, carry):
        bias_ref[0, kb] = neg
        return carry

    lax.fori_loop(nk, nkt, fill, 0)


def _indexer(main, small, lay, *, s, tq, tk, topk):
    nkt = s // tk
    hi = IDX_HEADS
    group = min(8, hi)
    qi0, ki0 = lay["qi"][0], lay["ki"][0]
    assert qi0 % hi == 0
    return pl.pallas_call(
        functools.partial(_idx_kernel, tq=tq, tk=tk, nkt=nkt, hi=hi, group=group, topk=topk,
                          wscale=IDX_HEADS ** -0.5 * IDX_DIM ** -0.5),
        grid=(s // tq,),
        in_specs=[pl.BlockSpec((hi, tq, LANES), lambda i: (qi0 // hi, i, 0)),
                  pl.BlockSpec((1, s, LANES), lambda i: (ki0, 0, 0)),
                  pl.BlockSpec((tq, LANES), lambda i: (i, 0))],
        out_specs=pl.BlockSpec((1, nkt, tq, tk), lambda i: (i, 0, 0, 0)),
        out_shape=jax.ShapeDtypeStruct((s // tq, nkt, tq, tk), BF16),
        scratch_shapes=[pltpu.VMEM((nkt, tq, tk), jnp.int16),
                        pltpu.VMEM((nkt, tq, tk), jnp.int16),
                        pltpu.VMEM((hi, tq, LANES), F32)],
        compiler_params=_params("parallel"),
        name="indexer",
    )(main, main, small)


def _attn_kernel(q_ref, k_ref, v_ref, b_ref, g_ref, o_ref, bias_scr, m_scr, acc_scr, *pipe, nh, tq, tk):
    qi = pl.program_id(0)
    kb = pl.program_id(1)

    @pl.when(kb == 0)
    def _():
        m_scr[...] = jnp.full(m_scr.shape, NEG_BIG, F32)
        acc_scr[...] = jnp.zeros(acc_scr.shape, F32)

    @pl.when(kb * tk <= qi * tq + (tq - 1))
    def _():
        na, nb, sq, sk = b_ref.shape
        for a in range(na):
            for b in range(nb):
                bias_scr[a * sq:(a + 1) * sq, b * sk:(b + 1) * sk] = b_ref[a, b].astype(F32)

        ones = jnp.ones((tk, LANES), BF16)

        def qk(h, s_ref):
            s_ref[...] = _dot_nt(q_ref[h], k_ref[h])

        def soft(h, s_ref, p_ref, al_ref):
            s = s_ref[...] + bias_scr[...]
            m_prev = m_scr[h]
            m_new = jnp.maximum(m_prev, jnp.max(s, axis=1, keepdims=True))
            m_scr[h] = m_new
            p_ref[...] = jnp.exp2(s - m_new[:, :1]).astype(BF16)
            al_ref[...] = jnp.exp2(m_prev - m_new)

        def pv(h, p_ref, al_ref):
            al = al_ref[...]
            v_ext = jnp.concatenate([v_ref[h], ones], axis=1)
            acc_scr[h] = acc_scr[h] * jnp.concatenate([al, al], axis=1) + _dot(p_ref[...], v_ext)

        s_a, s_b, p_a, p_b, al_a, al_b = pipe
        qk(0, s_a)
        qk(1, s_b)
        soft(0, s_a, p_a, al_a)

        def pair(j, carry):
            qk(2 * j, s_a)
            soft(2 * j - 1, s_b, p_b, al_b)
            pv(2 * j - 2, p_a, al_a)
            qk(2 * j + 1, s_b)
            soft(2 * j, s_a, p_a, al_a)
            pv(2 * j - 1, p_b, al_b)
            return carry

        lax.fori_loop(1, nh // 2, pair, 0)
        soft(nh - 1, s_b, p_b, al_b)
        pv(nh - 2, p_a, al_a)
        pv(nh - 1, p_b, al_b)

    @pl.when(kb == pl.num_programs(1) - 1)
    def _():
        ss = jnp.zeros((tq, LANES), F32)
        for h in range(nh):
            o_h = acc_scr[h, :, :LANES] / acc_scr[h, :, LANES:]
            acc_scr[h, :, :LANES] = o_h
            ss = ss + o_h * o_h
        ms = jnp.sum(ss, axis=1, keepdims=True) * (1.0 / (nh * LANES))
        r = lax.rsqrt(ms + RMS_EPS)
        for h in range(nh):
            sl = slice(h * LANES, (h + 1) * LANES)
            o_ref[:, sl] = (acc_scr[h, :, :LANES] * r * g_ref[:, sl]).astype(o_ref.dtype)


def _attention(main, bias, g, lay, *, s, tq, tk):
    nh = A_HEADS
    nqb, nkt, sq, sk = bias.shape
    q0, k0, v0 = lay["qa"][0], lay["ka"][0], lay["va"][0]
    assert q0 % nh == 0 and k0 % nh == 0 and v0 % nh == 0

    def last_kb(i):
        return (i * tq + tq - 1) // tk

    return pl.pallas_call(
        functools.partial(_attn_kernel, nh=nh, tq=tq, tk=tk),
        grid=(s // tq, s // tk),
        in_specs=[pl.BlockSpec((nh, tq, LANES), lambda i, j: (q0 // nh, i, 0)),
                  pl.BlockSpec((nh, tk, LANES), lambda i, j: (k0 // nh, jnp.minimum(j, last_kb(i)), 0)),
                  pl.BlockSpec((nh, tk, LANES), lambda i, j: (v0 // nh, jnp.minimum(j, last_kb(i)), 0)),
                  pl.BlockSpec((tq // sq, tk // sk, sq, sk),
                               lambda i, j: (i, jnp.minimum(j, last_kb(i)), 0, 0)),
                  pl.BlockSpec((1, nh * LANES), lambda i, j: (0, 0))],
        out_specs=pl.BlockSpec((tq, nh * LANES), lambda i, j: (i, 0)),
        out_shape=jax.ShapeDtypeStruct((s, nh * LANES), BF16),
        scratch_shapes=[pltpu.VMEM((tq, tk), F32),
                        pltpu.VMEM((nh, tq, LANES), F32),
                        pltpu.VMEM((nh, tq, 2 * LANES), F32),
                        pltpu.VMEM((tq, tk), F32), pltpu.VMEM((tq, tk), F32),
                        pltpu.VMEM((tq, tk), BF16), pltpu.VMEM((tq, tk), BF16),
                        pltpu.VMEM((tq, LANES), F32), pltpu.VMEM((tq, LANES), F32)],
        compiler_params=_params("parallel", "arbitrary"),
        name="attention",
    )(main, main, main, bias, g.reshape(1, nh * LANES))


def _gla_kernel(q_ref, k_ref, v_ref, gg_ref, sm_ref, w2_ref, bgk_ref, gn_ref, o_ref, st_scr, *,
                rows, chunk, dk, dv):
    @pl.when(pl.program_id(1) == 0)
    def _():
        st_scr[...] = jnp.zeros(st_scr.shape, F32)

    r_i = lax.broadcasted_iota(jnp.int32, (chunk, chunk), 0)
    c_i = lax.broadcasted_iota(jnp.int32, (chunk, chunk), 1)
    lower = r_i >= c_i
    tri = jnp.where(lower, 1.0, 0.0).astype(BF16)
    mid = chunk // 2

    def wide(ref, r0):
        return jnp.concatenate([ref[j, pl.ds(r0, chunk), :] for j in range(ref.shape[0])], axis=1)

    def step(c, carry):
        r0 = pl.multiple_of(c * chunk, chunk)
        q = wide(q_ref, r0).astype(F32) * (dk ** -0.5)
        k = wide(k_ref, r0).astype(F32)
        v = wide(v_ref, r0)
        z = _dot(sm_ref[pl.ds(r0, chunk), :].astype(BF16), w2_ref[0]) + bgk_ref[0]
        g = (jnp.minimum(z, 0.0) - jnp.log1p(jnp.exp(-jnp.abs(z)))) * (1.0 / G_TAU)
        g_hi = g.astype(BF16)
        rem = g - g_hi.astype(F32)
        g_mid = rem.astype(BF16)
        g_lo = (rem - g_mid.astype(F32)).astype(BF16)
        b = _dot(tri, g_hi) + _dot(tri, g_mid) + _dot(tri, g_lo)
        b_mid = b[mid:mid + 1, :]
        b_last = b[chunk - 1:chunk, :]
        qe = (q * jnp.exp(b - b_mid)).astype(BF16)
        ke = (k * jnp.exp(b_mid - b)).astype(BF16)
        att = jnp.where(lower, _dot_nt(qe, ke), 0.0)
        st = st_scr[...]
        o = _dot(att.astype(BF16), v) + _dot_nt((q * jnp.exp(b)).astype(BF16), st.astype(BF16))
        kd = (k * jnp.exp(b_last - b)).astype(BF16)
        st_scr[...] = st * jnp.exp(b_last) + _dot_tn(v, kd)
        ms = jnp.mean(o * o, axis=1, keepdims=True)
        gate = wide(gg_ref, r0).astype(F32)
        out = (o * lax.rsqrt(ms + RMS_EPS)) * gn_ref[...] * (gate * jax.nn.sigmoid(gate))
        o_ref[pl.ds(r0, chunk), :] = out.astype(o_ref.dtype)
        return carry

    lax.fori_loop(0, rows // chunk, step, 0)


def _gla(main, small, w2, bgk, gn, lay, *, s, rows, chunk, dk, dv):
    nbk, nbv = dk // LANES, dv // LANES
    q0, k0, v0, g0 = lay["qg"][0], lay["kg"][0], lay["vg"][0], lay["gg"][0]
    assert q0 % nbk == 0 and k0 % nbk == 0 and v0 % nbv == 0 and g0 % nbv == 0
    return pl.pallas_call(
        functools.partial(_gla_kernel, rows=rows, chunk=chunk, dk=dk, dv=dv),
        grid=(G_HEADS, s // rows),
        in_specs=[pl.BlockSpec((nbk, rows, LANES), lambda h, r: (q0 // nbk + h, r, 0)),
                  pl.BlockSpec((nbk, rows, LANES), lambda h, r: (k0 // nbk + h, r, 0)),
                  pl.BlockSpec((nbv, rows, LANES), lambda h, r: (v0 // nbv + h, r, 0)),
                  pl.BlockSpec((nbv, rows, LANES), lambda h, r: (g0 // nbv + h, r, 0)),
                  pl.BlockSpec((rows, LANES), lambda h, r: (r, 0)),
                  pl.BlockSpec((1, LANES, dk), lambda h, r: (h, 0, 0)),
                  pl.BlockSpec((1, 1, dk), lambda h, r: (h, 0, 0)),
                  pl.BlockSpec((1, dv), lambda h, r: (0, 0))],
        out_specs=pl.BlockSpec((rows, dv), lambda h, r: (r, h)),
        out_shape=jax.ShapeDtypeStruct((s, G_HEADS * dv), BF16),
        scratch_shapes=[pltpu.VMEM((dv, dk), F32)],
        compiler_params=_params("parallel", "arbitrary"),
        name="gla",
    )(main, main, main, main, small, w2, bgk, gn.reshape(1, dv))


def _ffn_up_kernel(h_ref, wg_ref, wv_ref, cwg_ref, cwv_ref, cbg_ref, cbv_ref, o_ref, w_scr, tail_scr,
                   *, tm, tn, rc):
    @pl.when(pl.program_id(1) == 0)
    def _():
        w_scr[:, :tn] = wg_ref[0].astype(BF16)
        w_scr[:, tn:] = wv_ref[0].astype(BF16)
        tail_scr[...] = jnp.zeros(tail_scr.shape, F32)

    cw = jnp.concatenate([cwg_ref[0], cwv_ref[0]], axis=1)
    cb = jnp.concatenate([cbg_ref[...], cbv_ref[...]], axis=1)
    w0, w1, w2 = cw[0:1, :], cw[1:2, :], cw[2:3, :]

    def taps(a):
        return w2 * a + w1 * pltpu.roll(a, 1, 0) + w0 * pltpu.roll(a, 2, 0) + cb

    tail = tail_scr[...]
    for r in range(tm // rc):
        rows = slice(r * rc, (r + 1) * rc)
        hw = _dot(h_ref[rows, :], w_scr[...])
        top = taps(jnp.concatenate([tail, hw[0:8]], axis=0))[8:16]
        u = jnp.concatenate([top, taps(hw)[8:]], axis=0)
        gate, val = u[:, :tn], u[:, tn:]
        o_ref[rows, :] = (gate * jax.nn.sigmoid(gate) * val).astype(o_ref.dtype)
        tail = hw[rc - 8:rc, :]
    tail_scr[...] = tail


def _ffn_up(hb, w_up, conv_w, conv_b, *, d_ff, tm, tn):
    s, d = hb.shape
    nj = d_ff // tn
    return pl.pallas_call(
        functools.partial(_ffn_up_kernel, tm=tm, tn=tn, rc=min(tm, 512)),
        grid=(nj, s // tm),
        in_specs=[pl.BlockSpec((tm, d), lambda j, i: (i, 0)),
                  pl.BlockSpec((1, d, tn), lambda j, i: (0, 0, j)),
                  pl.BlockSpec((1, d, tn), lambda j, i: (0, 0, nj + j)),
                  pl.BlockSpec((1, CONV_W, tn), lambda j, i: (0, 0, j)),
                  pl.BlockSpec((1, CONV_W, tn), lambda j, i: (0, 0, nj + j)),
                  pl.BlockSpec((1, tn), lambda j, i: (0, j)),
                  pl.BlockSpec((1, tn), lambda j, i: (0, nj + j))],
        out_specs=pl.BlockSpec((tm, tn), lambda j, i: (i, j)),
        out_shape=jax.ShapeDtypeStruct((s, d_ff), BF16),
        scratch_shapes=[pltpu.VMEM((d, 2 * tn), BF16), pltpu.VMEM((8, 2 * tn), F32)],
        compiler_params=_params("parallel", "arbitrary"),
        name="ffn_up",
    )(hb, w_up, w_up, conv_w, conv_w, conv_b, conv_b)


def _layout(d_model):
    a_width = A_HEADS * A_HEAD_DIM
    g_width = d_model - a_width
    g_kwidth = g_width // 2
    names = ("qa", "ka", "va", "qi", "ki", "wi", "qg", "kg", "vg", "glr", "gg")
    sizes = (a_width, a_width, a_width, IDX_HEADS * IDX_DIM, IDX_DIM, IDX_HEADS,
             g_kwidth, g_kwidth, g_width, G_LOWRANK, g_width)
    offs = np.concatenate([[0], np.cumsum(sizes)])
    src = {n: (int(offs[i]), int(offs[i + 1])) for i, n in enumerate(names)}
    order = ("qi", "qa", "ka", "va", "qg", "kg", "vg", "gg", "ki")
    lay, blk = {}, 0
    for n in order:
        width = src[n][1] - src[n][0]
        assert width % LANES == 0
        lay[n] = (blk, width // LANES)
        blk += width // LANES
    return src, order, lay, blk


def kernel(x, w_in, w_gk2, b_gk, attn_out_g, gla_norm_g, w_o, ln1_g, ln1_b, w_up, conv_w, conv_b,
           w_down, ln2_g, ln2_b):
    assert x.shape[0] == 1 and w_in.shape[0] == DEPTH == 1
    _, s, d = x.shape
    x2 = x[0]
    src, order, lay, nb_main = _layout(d)
    g_width = d - A_HEADS * A_HEAD_DIM
    dv = g_width // G_HEADS
    dk = dv // 2
    d_ff = w_down.shape[1]
    topk = min(TOPK_MAX, s // 4)

    w = w_in[0]
    w_main = jnp.concatenate([w[:, src[n][0]:src[n][1]] for n in order] + [jnp.zeros((d, LANES), F32)],
                             axis=1).astype(BF16)
    n_small = IDX_HEADS + G_LOWRANK
    assert n_small <= LANES
    w_small = jnp.concatenate([w[:, src["wi"][0]:src["wi"][1]], w[:, src["glr"][0]:src["glr"][1]],
                               jnp.zeros((d, LANES - n_small), F32)], axis=1).astype(BF16)
    w2 = jnp.zeros((LANES, G_HEADS * dk), F32).at[IDX_HEADS:n_small].set(w_gk2[0])
    w2 = w2.reshape(LANES, G_HEADS, dk).transpose(1, 0, 2).astype(BF16)
    bgk = b_gk[0].reshape(G_HEADS, 1, dk)
    wd = w_down[0].astype(BF16)
    n_main = (nb_main + 1) * LANES
    q_lo, q_hi = lay["qa"][0] * LANES, (lay["qa"][0] + lay["qa"][1]) * LANES
    col = np.ones((1, n_main), np.float32)
    col[:, q_lo:q_hi] = A_HEAD_DIM ** -0.5 * LOG2E
    col_scale = jnp.asarray(col)

    tm = _tile(s, 1024)
    main, small = _proj_blocks(x2, w_main, w_small, col_scale, tm=_tile(s, 512), tn=_tile(n_main, 1280))
    bias = _indexer(main, small, lay, s=s, tq=256, tk=256, topk=topk)
    ta = _tile(s, 512)
    o_a = _attention(main, bias, attn_out_g[0], lay, s=s, tq=ta, tk=ta)
    o_g = _gla(main, small, w2, bgk, gla_norm_g[0], lay, s=s, rows=_tile(s, 512), chunk=64, dk=dk, dv=dv)
    y1 = _w_o(o_a, o_g, w_o, x2, tm=tm, tn=_tile(d, 512), alpha=DN_ALPHA)
    h, hb = _layer_norm(y1, ln1_g[0], ln1_b[0], tr=_tile(s, 256, 8), also_bf16=True)

    act = _ffn_up(hb, w_up, conv_w, conv_b[0].reshape(1, 2 * d_ff), d_ff=d_ff, tm=tm, tn=_tile(d_ff, 256))
    y2 = _matmul(act, wd, tm=_tile(s, 512), tn=_tile(d, 512), tk=d_ff, out_dtype=F32, residual=h,
                 alpha=DN_ALPHA, name="w_down")
    (out,) = _layer_norm(y2, ln2_g[0], ln2_b[0], tr=_tile(s, 256, 8), also_bf16=False)
    return out[None]
```

```python
import functools

import numpy as np
import jax
import jax.numpy as jnp
from jax import lax
from jax.experimental import pallas as pl
from jax.experimental.pallas import tpu as pltpu

A_HEADS = 16
A_HEAD_DIM = 128
IDX_HEADS = 32
IDX_DIM = 128
TOPK_MAX = 256
G_HEADS = 4
G_LOWRANK = 16
G_TAU = 16.0
CONV_W = 3
LN_EPS = 1e-5
RMS_EPS = 1e-6
DEPTH = 1
DN_ALPHA = (2 * DEPTH) ** 0.25

LANES = 128
VMEM_LIMIT = 56 * 1024 * 1024

BF16 = jnp.bfloat16
F32 = jnp.float32
NEG_BIG = -1e30
LOG2E = 1.4426950408889634
INT_MIN = -2 ** 31


def _params(*sem):
    return pltpu.CompilerParams(dimension_semantics=sem, vmem_limit_bytes=VMEM_LIMIT)


def _tile(n, max_tile, quantum=LANES):
    best = None
    for t in range(quantum, min(n, max_tile) + 1, quantum):
        if n % t == 0:
            best = t
    assert best is not None, (n, max_tile, quantum)
    return best


def _dot(a, b):
    return jnp.dot(a, b, preferred_element_type=F32)


def _dot_nt(a, b):
    return lax.dot_general(a, b, (((1,), (1,)), ((), ())), preferred_element_type=F32)


def _dot_tn(a, b):
    return lax.dot_general(a, b, (((0,), (0,)), ((), ())), preferred_element_type=F32)


def _proj_kernel(x_ref, w_ref, ws_ref, sc_ref, o_ref, os_ref, xb_scr):
    @pl.when(pl.program_id(1) == 0)
    def _():
        xb_scr[...] = x_ref[...].astype(BF16)
        os_ref[...] = _dot(xb_scr[...], ws_ref[...])

    res = _dot(xb_scr[...], w_ref[...]) * sc_ref[...]
    for c in range(o_ref.shape[0]):
        o_ref[c] = res[:, c * LANES:(c + 1) * LANES].astype(o_ref.dtype)


def _proj_blocks(x, wb, wsb, col_scale, *, tm, tn):
    s, k = x.shape
    n = wb.shape[1]
    return pl.pallas_call(
        _proj_kernel,
        grid=(s // tm, n // tn),
        in_specs=[pl.BlockSpec((tm, k), lambda i, j: (i, 0)),
                  pl.BlockSpec((k, tn), lambda i, j: (0, j)),
                  pl.BlockSpec((k, LANES), lambda i, j: (0, 0)),
                  pl.BlockSpec((1, tn), lambda i, j: (0, j))],
        out_specs=[pl.BlockSpec((tn // LANES, tm, LANES), lambda i, j: (j, i, 0)),
                   pl.BlockSpec((tm, LANES), lambda i, j: (i, 0))],
        out_shape=[jax.ShapeDtypeStruct((n // LANES, s, LANES), BF16),
                   jax.ShapeDtypeStruct((s, LANES), F32)],
        scratch_shapes=[pltpu.VMEM((tm, k), BF16)],
        compiler_params=_params("parallel", "arbitrary"),
        name="proj",
    )(x, wb, wsb, col_scale)


def _mm_kernel(a_ref, b_ref, *rest, alpha, has_res, nk):
    if has_res:
        r_ref, o_ref = rest[0], rest[1]
        rest = rest[2:]
    else:
        r_ref, o_ref = None, rest[0]
        rest = rest[1:]

    def finish(acc):
        if has_res:
            acc = acc + alpha * r_ref[...]
        o_ref[...] = acc.astype(o_ref.dtype)

    if nk == 1:
        finish(_dot(a_ref[...], b_ref[...]))
        return
    acc_ref = rest[0]
    kk = pl.program_id(2)

    @pl.when(kk == 0)
    def _():
        acc_ref[...] = _dot(a_ref[...], b_ref[...])

    @pl.when(kk > 0)
    def _():
        acc_ref[...] += _dot(a_ref[...], b_ref[...])

    @pl.when(kk == nk - 1)
    def _():
        finish(acc_ref[...])


def _matmul(a, b, *, tm, tn, tk, out_dtype, residual=None, alpha=1.0, name="matmul"):
    m, k = a.shape
    n = b.shape[1]
    nk = k // tk
    in_specs = [pl.BlockSpec((tm, tk), lambda i, j, kk: (i, kk)),
                pl.BlockSpec((tk, tn), lambda i, j, kk: (kk, j))]
    args = [a, b]
    if residual is not None:
        in_specs.append(pl.BlockSpec((tm, tn), lambda i, j, kk: (i, j)))
        args.append(residual)
    scratch = [pltpu.VMEM((tm, tn), F32)] if nk > 1 else []
    return pl.pallas_call(
        functools.partial(_mm_kernel, alpha=alpha, has_res=residual is not None, nk=nk),
        grid=(m // tm, n // tn, nk),
        in_specs=in_specs,
        out_specs=pl.BlockSpec((tm, tn), lambda i, j, kk: (i, j)),
        out_shape=jax.ShapeDtypeStruct((m, n), out_dtype),
        scratch_shapes=scratch,
        compiler_params=_params("parallel", "parallel", "arbitrary"),
        name=name,
    )(*args)


def _wo_kernel(oa_ref, og_ref, wt_ref, wb_ref, x_ref, o_ref, w_scr, *, ka, alpha):
    @pl.when(pl.program_id(1) == 0)
    def _():
        w_scr[:ka, :] = wt_ref[0].astype(BF16)
        w_scr[ka:, :] = wb_ref[0].astype(BF16)

    acc = _dot(oa_ref[...], w_scr[:ka, :]) + _dot(og_ref[...], w_scr[ka:, :])
    o_ref[...] = acc + alpha * x_ref[...]


def _w_o(o_a, o_g, w_o, x, *, tm, tn, alpha):
    s, ka = o_a.shape
    kg = o_g.shape[1]
    d = w_o.shape[2]
    assert ka % kg == 0
    return pl.pallas_call(
        functools.partial(_wo_kernel, ka=ka, alpha=alpha),
        grid=(d // tn, s // tm),
        in_specs=[pl.BlockSpec((tm, ka), lambda j, i: (i, 0)),
                  pl.BlockSpec((tm, kg), lambda j, i: (i, 0)),
                  pl.BlockSpec((1, ka, tn), lambda j, i: (0, 0, j)),
                  pl.BlockSpec((1, kg, tn), lambda j, i: (0, ka // kg, j)),
                  pl.BlockSpec((tm, tn), lambda j, i: (i, j))],
        out_specs=pl.BlockSpec((tm, tn), lambda j, i: (i, j)),
        out_shape=jax.ShapeDtypeStruct((s, d), F32),
        scratch_shapes=[pltpu.VMEM((ka + kg, tn), BF16)],
        compiler_params=_params("parallel", "arbitrary"),
        name="w_o",
    )(o_a, o_g, w_o, w_o, x)


def _ln_kernel(y_ref, g_ref, b_ref, *o_refs):
    y = y_ref[...]
    mu = jnp.mean(y, axis=1, keepdims=True)
    yc = y - mu
    var = jnp.mean(yc * yc, axis=1, keepdims=True)
    out = yc * lax.rsqrt(var + LN_EPS) * g_ref[...] + b_ref[...]
    for o_ref in o_refs:
        o_ref[...] = out.astype(o_ref.dtype)


def _layer_norm(y, g, b, *, tr, also_bf16):
    s, d = y.shape
    out_shape = [jax.ShapeDtypeStruct((s, d), F32)]
    out_specs = [pl.BlockSpec((tr, d), lambda i: (i, 0))]
    if also_bf16:
        out_shape.append(jax.ShapeDtypeStruct((s, d), BF16))
        out_specs.append(pl.BlockSpec((tr, d), lambda i: (i, 0)))
    return pl.pallas_call(
        _ln_kernel,
        grid=(s // tr,),
        in_specs=[pl.BlockSpec((tr, d), lambda i: (i, 0)),
                  pl.BlockSpec((1, d), lambda i: (0, 0)),
                  pl.BlockSpec((1, d), lambda i: (0, 0))],
        out_specs=out_specs,
        out_shape=out_shape,
        compiler_params=_params("parallel"),
        name="layer_norm",
    )(y, g.reshape(1, d), b.reshape(1, d))


def _idx_kernel(qi_ref, ki_ref, sm_ref, bias_ref, hi_scr, lo_scr, wb_scr, *, tq, tk, nkt, hi, group,
                topk, wscale):
    qb = pl.program_id(0)
    nk = ((qb + 1) * tq + tk - 1) // tk
    reps = tk // LANES
    i16 = jnp.int16
    low16 = -2 ** 15

    for h in range(hi):
        wb_scr[h] = jnp.broadcast_to(sm_ref[:, h:h + 1] * wscale, (tq, LANES))

    row = qb * tq + lax.broadcasted_iota(jnp.int32, (group, tk), 0)
    col = lax.broadcasted_iota(jnp.int32, (group, tk), 1)

    def lanes(a):
        return jnp.concatenate([a] * reps, axis=1) if reps > 1 else a

    def fold(a):
        out = a[:, :LANES]
        for r in range(1, reps):
            out = out + a[:, r * LANES:(r + 1) * LANES]
        return out

    def score_tile(kb, carry):
        kt = ki_ref[0, pl.ds(pl.multiple_of(kb * tk, tk), tk), :]
        for r in range(tq // group):
            rows = slice(r * group, (r + 1) * group)
            qs = qi_ref[:, rows, :].reshape(hi * group, IDX_DIM)
            d = _dot_nt(qs, kt)
            acc = jnp.zeros((group, tk), F32)
            for h in range(hi):
                acc = acc + jnp.maximum(d[h * group:(h + 1) * group], 0.0) * lanes(wb_scr[h, rows, :])
            bits = pltpu.bitcast(acc + 0.0, jnp.int32)
            key = jnp.where(kb * tk + col <= row + r * group,
                            bits ^ ((bits >> 31) & jnp.int32(0x7FFFFFFF)), INT_MIN)
            hi_scr[kb, rows, :] = (key >> 16).astype(i16)
            lo_scr[kb, rows, :] = ((key & 0xFFFF) + low16).astype(i16)
        return carry

    lax.fori_loop(0, nk, score_tile, 0)

    def count_ge(ref, cand):
        cand_t = lanes(cand.astype(i16))

        def step(kb, c):
            return c + fold(jnp.where(ref[kb] >= cand_t, i16(1), i16(0)))

        c = lax.fori_loop(0, nk, step, jnp.zeros((tq, LANES), i16))
        return jnp.sum(c.astype(F32), axis=1, keepdims=True)

    def search(ref, base_cnt):
        def bit_step(it, v):
            cand = v + jnp.left_shift(jnp.int32(1), 15 - it)
            return jnp.where(base_cnt + count_ge(ref, cand) >= float(topk), cand, v)

        return lax.fori_loop(0, 16, bit_step, jnp.full((tq, LANES), low16, jnp.int32))

    v_hi = search(hi_scr, 0.0)
    n_above = count_ge(hi_scr, v_hi + 1)
    v_hi_t = lanes(v_hi.astype(i16))

    def keep_equal(kb, carry):
        lo_scr[kb] = jnp.where(hi_scr[kb] == v_hi_t, lo_scr[kb], i16(low16))
        return carry

    lax.fori_loop(0, nk, keep_equal, 0)
    v_lo = search(lo_scr, n_above)
    v_lo = jnp.where(v_hi == low16, jnp.maximum(v_lo, low16 + 1), v_lo)
    v_lo_t = lanes(v_lo.astype(i16))
    zero = jnp.zeros((tq, tk), bias_ref.dtype)
    neg = jnp.full((tq, tk), NEG_BIG, bias_ref.dtype)

    def emit(kb, carry):
        h16 = hi_scr[kb]
        sel = (h16 > v_hi_t) | ((h16 == v_hi_t) & (lo_scr[kb] >= v_lo_t))
        bias_ref[0, kb] = jnp.where(sel, zero, neg)
        return carry

    lax.fori_loop(0, nk, emit, 0)

    def fill(kb, carry):
        bias_ref[0, kb] = neg
        return carry

    lax.fori_loop(nk, nkt, fill, 0)


def _indexer(main, small, lay, *, s, tq, tk, topk):
    nkt = s // tk
    hi = IDX_HEADS
    group = min(tq, 64)
    qi0, ki0 = lay["qi"][0], lay["ki"][0]
    assert qi0 % hi == 0
    return pl.pallas_call(
        functools.partial(_idx_kernel, tq=tq, tk=tk, nkt=nkt, hi=hi, group=group, topk=topk,
                          wscale=IDX_HEADS ** -0.5 * IDX_DIM ** -0.5),
        grid=(s // tq,),
        in_specs=[pl.BlockSpec((hi, tq, LANES), lambda i: (qi0 // hi, i, 0)),
                  pl.BlockSpec((1, s, LANES), lambda i: (ki0, 0, 0)),
                  pl.BlockSpec((tq, LANES), lambda i: (i, 0))],
        out_specs=pl.BlockSpec((1, nkt, tq, tk), lambda i: (i, 0, 0, 0)),
        out_shape=jax.ShapeDtypeStruct((s // tq, nkt, tq, tk), BF16),
        scratch_shapes=[pltpu.VMEM((nkt, tq, tk), jnp.int16),
                        pltpu.VMEM((nkt, tq, tk), jnp.int16),
                        pltpu.VMEM((hi, tq, LANES), F32)],
        compiler_params=_params("parallel"),
        name="indexer",
    )(main, main, small)


def _attn_kernel(q_ref, k_ref, v_ref, b_ref, g_ref, o_ref, bias_scr, m_scr, acc_scr, *pipe, nh, tq, tk):
    qi = pl.program_id(0)
    kb = pl.program_id(1)

    @pl.when(kb == 0)
    def _():
        m_scr[...] = jnp.full(m_scr.shape, NEG_BIG, F32)
        acc_scr[...] = jnp.zeros(acc_scr.shape, F32)

    @pl.when(kb * tk <= qi * tq + (tq - 1))
    def _():
        na, nb, sq, sk = b_ref.shape
        for a in range(na):
            for b in range(nb):
                bias_scr[a * sq:(a + 1) * sq, b * sk:(b + 1) * sk] = b_ref[a, b].astype(F32)

        ones = jnp.ones((tk, LANES), BF16)

        def qk(h, s_ref):
            s_ref[...] = _dot_nt(q_ref[h], k_ref[h])

        def soft(h, s_ref, p_ref, al_ref):
            s = s_ref[...] + bias_scr[...]
            m_prev = m_scr[h]
            m_new = jnp.maximum(m_prev, jnp.max(s, axis=1, keepdims=True))
            m_scr[h] = m_new
            p_ref[...] = jnp.exp2(s - m_new[:, :1]).astype(BF16)
            al_ref[...] = jnp.exp2(m_prev - m_new)

        def pv(h, p_ref, al_ref):
            al = al_ref[...]
            v_ext = jnp.concatenate([v_ref[h], ones], axis=1)
            acc_scr[h] = acc_scr[h] * jnp.concatenate([al, al], axis=1) + _dot(p_ref[...], v_ext)

        s_a, s_b, p_a, p_b, al_a, al_b = pipe
        qk(0, s_a)
        qk(1, s_b)
        soft(0, s_a, p_a, al_a)

        def pair(j, carry):
            qk(2 * j, s_a)
            soft(2 * j - 1, s_b, p_b, al_b)
            pv(2 * j - 2, p_a, al_a)
            qk(2 * j + 1, s_b)
            soft(2 * j, s_a, p_a, al_a)
            pv(2 * j - 1, p_b, al_b)
            return carry

        lax.fori_loop(1, nh // 2, pair, 0)
        soft(nh - 1, s_b, p_b, al_b)
        pv(nh - 2, p_a, al_a)
        pv(nh - 1, p_b, al_b)

    @pl.when(kb == pl.num_programs(1) - 1)
    def _():
        ss = jnp.zeros((tq, LANES), F32)
        for h in range(nh):
            o_h = acc_scr[h, :, :LANES] / acc_scr[h, :, LANES:]
            acc_scr[h, :, :LANES] = o_h
            ss = ss + o_h * o_h
        ms = jnp.sum(ss, axis=1, keepdims=True) * (1.0 / (nh * LANES))
        r = lax.rsqrt(ms + RMS_EPS)
        for h in range(nh):
            sl = slice(h * LANES, (h + 1) * LANES)
            o_ref[:, sl] = (acc_scr[h, :, :LANES] * r * g_ref[:, sl]).astype(o_ref.dtype)


def _attention(main, bias, g, lay, *, s, tq, tk):
    nh = A_HEADS
    nqb, nkt, sq, sk = bias.shape
    q0, k0, v0 = lay["qa"][0], lay["ka"][0], lay["va"][0]
    assert q0 % nh == 0 and k0 % nh == 0 and v0 % nh == 0

    def last_kb(i):
        return (i * tq + tq - 1) // tk

    return pl.pallas_call(
        functools.partial(_attn_kernel, nh=nh, tq=tq, tk=tk),
        grid=(s // tq, s // tk),
        in_specs=[pl.BlockSpec((nh, tq, LANES), lambda i, j: (q0 // nh, i, 0)),
                  pl.BlockSpec((nh, tk, LANES), lambda i, j: (k0 // nh, jnp.minimum(j, last_kb(i)), 0)),
                  pl.BlockSpec((nh, tk, LANES), lambda i, j: (v0 // nh, jnp.minimum(j, last_kb(i)), 0)),
                  pl.BlockSpec((tq // sq, tk // sk, sq, sk),
                               lambda i, j: (i, jnp.minimum(j, last_kb(i)), 0, 0)),
                  pl.BlockSpec((1, nh * LANES), lambda i, j: (0, 0))],
        out_specs=pl.BlockSpec((tq, nh * LANES), lambda i, j: (i, 0)),
        out_shape=jax.ShapeDtypeStruct((s, nh * LANES), BF16),
        scratch_shapes=[pltpu.VMEM((tq, tk), F32),
                        pltpu.VMEM((nh, tq, LANES), F32),
                        pltpu.VMEM((nh, tq, 2 * LANES), F32),
                        pltpu.VMEM((tq, tk), F32), pltpu.VMEM((tq, tk), F32),
                        pltpu.VMEM((tq, tk), BF16), pltpu.VMEM((tq, tk), BF16),
                        pltpu.VMEM((tq, LANES), F32), pltpu.VMEM((tq, LANES), F32)],
        compiler_params=_params("parallel", "arbitrary"),
        name="attention",
    )(main, main, main, bias, g.reshape(1, nh * LANES))


def _gla_kernel(q_ref, k_ref, v_ref, gg_ref, sm_ref, w2_ref, bgk_ref, gn_ref, o_ref, st_scr, *,
                rows, chunk, dk, dv):
    @pl.when(pl.program_id(1) == 0)
    def _():
        st_scr[...] = jnp.zeros(st_scr.shape, F32)

    r_i = lax.broadcasted_iota(jnp.int32, (chunk, chunk), 0)
    c_i = lax.broadcasted_iota(jnp.int32, (chunk, chunk), 1)
    lower = r_i >= c_i
    tri = jnp.where(lower, 1.0, 0.0).astype(BF16)
    mid = chunk // 2

    def wide(ref, r0):
        return jnp.concatenate([ref[j, pl.ds(r0, chunk), :] for j in range(ref.shape[0])], axis=1)

    def step(c, carry):
        r0 = pl.multiple_of(c * chunk, chunk)
        q = wide(q_ref, r0).astype(F32) * (dk ** -0.5)
        k = wide(k_ref, r0).astype(F32)
        v = wide(v_ref, r0)
        z = _dot(sm_ref[pl.ds(r0, chunk), :].astype(BF16), w2_ref[0]) + bgk_ref[0]
        g = (jnp.minimum(z, 0.0) - jnp.log1p(jnp.exp(-jnp.abs(z)))) * (1.0 / G_TAU)
        g_hi = g.astype(BF16)
        rem = g - g_hi.astype(F32)
        g_mid = rem.astype(BF16)
        g_lo = (rem - g_mid.astype(F32)).astype(BF16)
        b = _dot(tri, g_hi) + _dot(tri, g_mid) + _dot(tri, g_lo)
        b_mid = b[mid:mid + 1, :]
        b_last = b[chunk - 1:chunk, :]
        qe = (q * jnp.exp(b - b_mid)).astype(BF16)
        ke = (k * jnp.exp(b_mid - b)).astype(BF16)
        att = jnp.where(lower, _dot_nt(qe, ke), 0.0)
        st = st_scr[...]
        o = _dot(att.astype(BF16), v) + _dot_nt((q * jnp.exp(b)).astype(BF16), st.astype(BF16))
        kd = (k * jnp.exp(b_last - b)).astype(BF16)
        st_scr[...] = st * jnp.exp(b_last) + _dot_tn(v, kd)
        ms = jnp.mean(o * o, axis=1, keepdims=True)
        gate = wide(gg_ref, r0).astype(F32)
        out = (o * lax.rsqrt(ms + RMS_EPS)) * gn_ref[...] * (gate * jax.nn.sigmoid(gate))
        o_ref[pl.ds(r0, chunk), :] = out.astype(o_ref.dtype)
        return carry

    lax.fori_loop(0, rows // chunk, step, 0)


def _gla(main, small, w2, bgk, gn, lay, *, s, rows, chunk, dk, dv):
    nbk, nbv = dk // LANES, dv // LANES
    q0, k0, v0, g0 = lay["qg"][0], lay["kg"][0], lay["vg"][0], lay["gg"][0]
    assert q0 % nbk == 0 and k0 % nbk == 0 and v0 % nbv == 0 and g0 % nbv == 0
    return pl.pallas_call(
        functools.partial(_gla_kernel, rows=rows, chunk=chunk, dk=dk, dv=dv),
        grid=(G_HEADS, s // rows),
        in_specs=[pl.BlockSpec((nbk, rows, LANES), lambda h, r: (q0 // nbk + h, r, 0)),
                  pl.BlockSpec((nbk, rows, LANES), lambda h, r: (k0 // nbk + h, r, 0)),
                  pl.BlockSpec((nbv, rows, LANES), lambda h, r: (v0 // nbv + h, r, 0)),
                  pl.BlockSpec((nbv, rows, LANES), lambda h, r: (g0 // nbv + h, r, 0)),
                  pl.BlockSpec((rows, LANES), lambda h, r: (r, 0)),
                  pl.BlockSpec((1, LANES, dk), lambda h, r: (h, 0, 0)),
                  pl.BlockSpec((1, 1, dk), lambda h, r: (h, 0, 0)),
                  pl.BlockSpec((1, dv), lambda h, r: (0, 0))],
        out_specs=pl.BlockSpec((rows, dv), lambda h, r: (r, h)),
        out_shape=jax.ShapeDtypeStruct((s, G_HEADS * dv), BF16),
        scratch_shapes=[pltpu.VMEM((dv, dk), F32)],
        compiler_params=_params("parallel", "arbitrary"),
        name="gla",
    )(main, main, main, main, small, w2, bgk, gn.reshape(1, dv))


def _ffn_up_kernel(h_ref, wg_ref, wv_ref, cwg_ref, cwv_ref, cbg_ref, cbv_ref, o_ref, w_scr, tail_scr,
                   *, tm, tn, rc):
    @pl.when(pl.program_id(1) == 0)
    def _():
        w_scr[:, :tn] = wg_ref[0].astype(BF16)
        w_scr[:, tn:] = wv_ref[0].astype(BF16)
        tail_scr[...] = jnp.zeros(tail_scr.shape, F32)

    cw = jnp.concatenate([cwg_ref[0], cwv_ref[0]], axis=1)
    cb = jnp.concatenate([cbg_ref[...], cbv_ref[...]], axis=1)
    w0, w1, w2 = cw[0:1, :], cw[1:2, :], cw[2:3, :]

    def taps(a):
        return w2 * a + w1 * pltpu.roll(a, 1, 0) + w0 * pltpu.roll(a, 2, 0) + cb

    tail = tail_scr[...]
    for r in range(tm // rc):
        rows = slice(r * rc, (r + 1) * rc)
        hw = _dot(h_ref[rows, :], w_scr[...])
        top = taps(jnp.concatenate([tail, hw[0:8]], axis=0))[8:16]
        u = jnp.concatenate([top, taps(hw)[8:]], axis=0)
        gate, val = u[:, :tn], u[:, tn:]
        o_ref[rows, :] = (gate * jax.nn.sigmoid(gate) * val).astype(o_ref.dtype)
        tail = hw[rc - 8:rc, :]
    tail_scr[...] = tail


def _ffn_up(hb, w_up, conv_w, conv_b, *, d_ff, tm, tn):
    s, d = hb.shape
    nj = d_ff // tn
    return pl.pallas_call(
        functools.partial(_ffn_up_kernel, tm=tm, tn=tn, rc=min(tm, 512)),
        grid=(nj, s // tm),
        in_specs=[pl.BlockSpec((tm, d), lambda j, i: (i, 0)),
                  pl.BlockSpec((1, d, tn), lambda j, i: (0, 0, j)),
                  pl.BlockSpec((1, d, tn), lambda j, i: (0, 0, nj + j)),
                  pl.BlockSpec((1, CONV_W, tn), lambda j, i: (0, 0, j)),
                  pl.BlockSpec((1, CONV_W, tn), lambda j, i: (0, 0, nj + j)),
                  pl.BlockSpec((1, tn), lambda j, i: (0, j)),
                  pl.BlockSpec((1, tn), lambda j, i: (0, nj + j))],
        out_specs=pl.BlockSpec((tm, tn), lambda j, i: (i, j)),
        out_shape=jax.ShapeDtypeStruct((s, d_ff), BF16),
        scratch_shapes=[pltpu.VMEM((d, 2 * tn), BF16), pltpu.VMEM((8, 2 * tn), F32)],
        compiler_params=_params("parallel", "arbitrary"),
        name="ffn_up",
    )(hb, w_up, w_up, conv_w, conv_w, conv_b, conv_b)


def _layout(d_model):
    a_width = A_HEADS * A_HEAD_DIM
    g_width = d_model - a_width
    g_kwidth = g_width // 2
    names = ("qa", "ka", "va", "qi", "ki", "wi", "qg", "kg", "vg", "glr", "gg")
    sizes = (a_width, a_width, a_width, IDX_HEADS * IDX_DIM, IDX_DIM, IDX_HEADS,
             g_kwidth, g_kwidth, g_width, G_LOWRANK, g_width)
    offs = np.concatenate([[0], np.cumsum(sizes)])
    src = {n: (int(offs[i]), int(offs[i + 1])) for i, n in enumerate(names)}
    order = ("qi", "qa", "ka", "va", "qg", "kg", "vg", "gg", "ki")
    lay, blk = {}, 0
    for n in order:
        width = src[n][1] - src[n][0]
        assert width % LANES == 0
        lay[n] = (blk, width // LANES)
        blk += width // LANES
    return src, order, lay, blk


def kernel(x, w_in, w_gk2, b_gk, attn_out_g, gla_norm_g, w_o, ln1_g, ln1_b, w_up, conv_w, conv_b,
           w_down, ln2_g, ln2_b):
    assert x.shape[0] == 1 and w_in.shape[0] == DEPTH == 1
    _, s, d = x.shape
    x2 = x[0]
    src, order, lay, nb_main = _layout(d)
    g_width = d - A_HEADS * A_HEAD_DIM
    dv = g_width // G_HEADS
    dk = dv // 2
    d_ff = w_down.shape[1]
    topk = min(TOPK_MAX, s // 4)

    w = w_in[0]
    w_main = jnp.concatenate([w[:, src[n][0]:src[n][1]] for n in order] + [jnp.zeros((d, LANES), F32)],
                             axis=1).astype(BF16)
    n_small = IDX_HEADS + G_LOWRANK
    assert n_small <= LANES
    w_small = jnp.concatenate([w[:, src["wi"][0]:src["wi"][1]], w[:, src["glr"][0]:src["glr"][1]],
                               jnp.zeros((d, LANES - n_small), F32)], axis=1).astype(BF16)
    w2 = jnp.zeros((LANES, G_HEADS * dk), F32).at[IDX_HEADS:n_small].set(w_gk2[0])
    w2 = w2.reshape(LANES, G_HEADS, dk).transpose(1, 0, 2).astype(BF16)
    bgk = b_gk[0].reshape(G_HEADS, 1, dk)
    wd = w_down[0].astype(BF16)
    n_main = (nb_main + 1) * LANES
    q_lo, q_hi = lay["qa"][0] * LANES, (lay["qa"][0] + lay["qa"][1]) * LANES
    col = np.ones((1, n_main), np.float32)
    col[:, q_lo:q_hi] = A_HEAD_DIM ** -0.5 * LOG2E
    col_scale = jnp.asarray(col)

    tm = _tile(s, 1024)
    main, small = _proj_blocks(x2, w_main, w_small, col_scale, tm=_tile(s, 512), tn=_tile(n_main, 1280))
    bias = _indexer(main, small, lay, s=s, tq=256, tk=256, topk=topk)
    ta = _tile(s, 512)
    o_a = _attention(main, bias, attn_out_g[0], lay, s=s, tq=ta, tk=ta)
    o_g = _gla(main, small, w2, bgk, gla_norm_g[0], lay, s=s, rows=_tile(s, 512), chunk=64, dk=dk, dv=dv)
    y1 = _w_o(o_a, o_g, w_o, x2, tm=tm, tn=_tile(d, 512), alpha=DN_ALPHA)
    h, hb = _layer_norm(y1, ln1_g[0], ln1_b[0], tr=_tile(s, 256, 8), also_bf16=True)

    act = _ffn_up(hb, w_up, conv_w, conv_b[0].reshape(1, 2 * d_ff), d_ff=d_ff, tm=tm, tn=_tile(d_ff, 256))
    y2 = _matmul(act, wd, tm=_tile(s, 512), tn=_tile(d, 512), tk=d_ff, out_dtype=F32, residual=h,
                 alpha=DN_ALPHA, name="w_down")
    (out,) = _layer_norm(y2, ln2_g[0], ln2_b[0], tr=_tile(s, 256, 8), also_bf16=False)
    return out[None]
```

```python
import functools

import numpy as np
import jax
import jax.numpy as jnp
from jax import lax
from jax.experimental import pallas as pl
from jax.experimental.pallas import tpu as pltpu

A_HEADS = 16
A_HEAD_DIM = 128
IDX_HEADS = 32
IDX_DIM = 128
TOPK_MAX = 256
G_HEADS = 4
G_LOWRANK = 16
G_TAU = 16.0
CONV_W = 3
LN_EPS = 1e-5
RMS_EPS = 1e-6
DEPTH = 1
DN_ALPHA = (2 * DEPTH) ** 0.25

LANES = 128
VMEM_LIMIT = 56 * 1024 * 1024

BF16 = jnp.bfloat16
F32 = jnp.float32
NEG_BIG = -1e30
LOG2E = 1.4426950408889634
INT_MIN = -2 ** 31
COUNT_UNROLL = 4


def _params(*sem):
    return pltpu.CompilerParams(dimension_semantics=sem, vmem_limit_bytes=VMEM_LIMIT)


def _tile(n, max_tile, quantum=LANES):
    best = None
    for t in range(quantum, min(n, max_tile) + 1, quantum):
        if n % t == 0:
            best = t
    assert best is not None, (n, max_tile, quantum)
    return best


def _dot(a, b):
    return jnp.dot(a, b, preferred_element_type=F32)


def _dot_nt(a, b):
    return lax.dot_general(a, b, (((1,), (1,)), ((), ())), preferred_element_type=F32)


def _dot_tn(a, b):
    return lax.dot_general(a, b, (((0,), (0,)), ((), ())), preferred_element_type=F32)


def _proj_kernel(x_ref, w_ref, ws_ref, sc_ref, o_ref, os_ref, xb_scr):
    @pl.when(pl.program_id(1) == 0)
    def _():
        xb_scr[...] = x_ref[...].astype(BF16)
        os_ref[...] = _dot(xb_scr[...], ws_ref[...])

    res = _dot(xb_scr[...], w_ref[...]) * sc_ref[...]
    for c in range(o_ref.shape[0]):
        o_ref[c] = res[:, c * LANES:(c + 1) * LANES].astype(o_ref.dtype)


def _proj_blocks(x, wb, wsb, col_scale, *, tm, tn):
    s, k = x.shape
    n = wb.shape[1]
    return pl.pallas_call(
        _proj_kernel,
        grid=(s // tm, n // tn),
        in_specs=[pl.BlockSpec((tm, k), lambda i, j: (i, 0)),
                  pl.BlockSpec((k, tn), lambda i, j: (0, j)),
                  pl.BlockSpec((k, LANES), lambda i, j: (0, 0)),
                  pl.BlockSpec((1, tn), lambda i, j: (0, j))],
        out_specs=[pl.BlockSpec((tn // LANES, tm, LANES), lambda i, j: (j, i, 0)),
                   pl.BlockSpec((tm, LANES), lambda i, j: (i, 0))],
        out_shape=[jax.ShapeDtypeStruct((n // LANES, s, LANES), BF16),
                   jax.ShapeDtypeStruct((s, LANES), F32)],
        scratch_shapes=[pltpu.VMEM((tm, k), BF16)],
        compiler_params=_params("parallel", "arbitrary"),
        name="proj",
    )(x, wb, wsb, col_scale)


def _mm_kernel(a_ref, b_ref, *rest, alpha, has_res, nk):
    if has_res:
        r_ref, o_ref = rest[0], rest[1]
        rest = rest[2:]
    else:
        r_ref, o_ref = None, rest[0]
        rest = rest[1:]

    def finish(acc):
        if has_res:
            acc = acc + alpha * r_ref[...]
        o_ref[...] = acc.astype(o_ref.dtype)

    if nk == 1:
        finish(_dot(a_ref[...], b_ref[...]))
        return
    acc_ref = rest[0]
    kk = pl.program_id(2)

    @pl.when(kk == 0)
    def _():
        acc_ref[...] = _dot(a_ref[...], b_ref[...])

    @pl.when(kk > 0)
    def _():
        acc_ref[...] += _dot(a_ref[...], b_ref[...])

    @pl.when(kk == nk - 1)
    def _():
        finish(acc_ref[...])


def _matmul(a, b, *, tm, tn, tk, out_dtype, residual=None, alpha=1.0, name="matmul"):
    m, k = a.shape
    n = b.shape[1]
    nk = k // tk
    in_specs = [pl.BlockSpec((tm, tk), lambda i, j, kk: (i, kk)),
                pl.BlockSpec((tk, tn), lambda i, j, kk: (kk, j))]
    args = [a, b]
    if residual is not None:
        in_specs.append(pl.BlockSpec((tm, tn), lambda i, j, kk: (i, j)))
        args.append(residual)
    scratch = [pltpu.VMEM((tm, tn), F32)] if nk > 1 else []
    return pl.pallas_call(
        functools.partial(_mm_kernel, alpha=alpha, has_res=residual is not None, nk=nk),
        grid=(m // tm, n // tn, nk),
        in_specs=in_specs,
        out_specs=pl.BlockSpec((tm, tn), lambda i, j, kk: (i, j)),
        out_shape=jax.ShapeDtypeStruct((m, n), out_dtype),
        scratch_shapes=scratch,
        compiler_params=_params("parallel", "parallel", "arbitrary"),
        name=name,
    )(*args)


def _wo_kernel(oa_ref, og_ref, wt_ref, wb_ref, x_ref, o_ref, w_scr, *, ka, alpha):
    @pl.when(pl.program_id(1) == 0)
    def _():
        w_scr[:ka, :] = wt_ref[0].astype(BF16)
        w_scr[ka:, :] = wb_ref[0].astype(BF16)

    acc = _dot(oa_ref[...], w_scr[:ka, :]) + _dot(og_ref[...], w_scr[ka:, :])
    o_ref[...] = acc + alpha * x_ref[...]


def _w_o(o_a, o_g, w_o, x, *, tm, tn, alpha):
    s, ka = o_a.shape
    kg = o_g.shape[1]
    d = w_o.shape[2]
    assert ka % kg == 0
    return pl.pallas_call(
        functools.partial(_wo_kernel, ka=ka, alpha=alpha),
        grid=(d // tn, s // tm),
        in_specs=[pl.BlockSpec((tm, ka), lambda j, i: (i, 0)),
                  pl.BlockSpec((tm, kg), lambda j, i: (i, 0)),
                  pl.BlockSpec((1, ka, tn), lambda j, i: (0, 0, j)),
                  pl.BlockSpec((1, kg, tn), lambda j, i: (0, ka // kg, j)),
                  pl.BlockSpec((tm, tn), lambda j, i: (i, j))],
        out_specs=pl.BlockSpec((tm, tn), lambda j, i: (i, j)),
        out_shape=jax.ShapeDtypeStruct((s, d), F32),
        scratch_shapes=[pltpu.VMEM((ka + kg, tn), BF16)],
        compiler_params=_params("parallel", "arbitrary"),
        name="w_o",
    )(o_a, o_g, w_o, w_o, x)


def _ln_kernel(y_ref, g_ref, b_ref, *o_refs):
    y = y_ref[...]
    mu = jnp.mean(y, axis=1, keepdims=True)
    yc = y - mu
    var = jnp.mean(yc * yc, axis=1, keepdims=True)
    out = yc * lax.rsqrt(var + LN_EPS) * g_ref[...] + b_ref[...]
    for o_ref in o_refs:
        o_ref[...] = out.astype(o_ref.dtype)


def _layer_norm(y, g, b, *, tr, also_bf16):
    s, d = y.shape
    out_shape = [jax.ShapeDtypeStruct((s, d), F32)]
    out_specs = [pl.BlockSpec((tr, d), lambda i: (i, 0))]
    if also_bf16:
        out_shape.append(jax.ShapeDtypeStruct((s, d), BF16))
        out_specs.append(pl.BlockSpec((tr, d), lambda i: (i, 0)))
    return pl.pallas_call(
        _ln_kernel,
        grid=(s // tr,),
        in_specs=[pl.BlockSpec((tr, d), lambda i: (i, 0)),
                  pl.BlockSpec((1, d), lambda i: (0, 0)),
                  pl.BlockSpec((1, d), lambda i: (0, 0))],
        out_specs=out_specs,
        out_shape=out_shape,
        compiler_params=_params("parallel"),
        name="layer_norm",
    )(y, g.reshape(1, d), b.reshape(1, d))


def _idx_kernel(qi_ref, ki_ref, sm_ref, bias_ref, hi_scr, lo_scr, wb_scr, *, tq, tk, nkt, hi, group,
                topk, wscale):
    qb = pl.program_id(0)
    nk = ((qb + 1) * tq + tk - 1) // tk
    reps = tk // LANES
    i16 = jnp.int16
    low16 = -2 ** 15

    for h in range(hi):
        wb_scr[h] = jnp.broadcast_to(sm_ref[:, h:h + 1] * wscale, (tq, LANES))

    row = qb * tq + lax.broadcasted_iota(jnp.int32, (group, tk), 0)
    col = lax.broadcasted_iota(jnp.int32, (group, tk), 1)

    def lanes(a):
        return jnp.concatenate([a] * reps, axis=1) if reps > 1 else a

    def fold(a):
        out = a[:, :LANES]
        for r in range(1, reps):
            out = out + a[:, r * LANES:(r + 1) * LANES]
        return out

    def score_tile(kb, carry):
        kt = ki_ref[0, pl.ds(pl.multiple_of(kb * tk, tk), tk), :]
        for r in range(tq // group):
            rows = slice(r * group, (r + 1) * group)
            qs = qi_ref[:, rows, :].reshape(hi * group, IDX_DIM)
            d = _dot_nt(qs, kt)
            acc = jnp.zeros((group, tk), F32)
            for h in range(hi):
                acc = acc + jnp.maximum(d[h * group:(h + 1) * group], 0.0) * lanes(wb_scr[h, rows, :])
            bits = pltpu.bitcast(acc + 0.0, jnp.int32)
            key = jnp.where(kb * tk + col <= row + r * group,
                            bits ^ ((bits >> 31) & jnp.int32(0x7FFFFFFF)), INT_MIN)
            hi_scr[kb, rows, :] = (key >> 16).astype(i16)
            lo_scr[kb, rows, :] = ((key & 0xFFFF) + low16).astype(i16)
        return carry

    lax.fori_loop(0, nk, score_tile, 0)

    def count_ge(ref, cand):
        cand_t = lanes(cand.astype(i16))

        def step(kb, c):
            return c + fold(jnp.where(ref[kb] >= cand_t, i16(1), i16(0)))

        def step_many(q, c):
            for u in range(COUNT_UNROLL):
                c = step(q * COUNT_UNROLL + u, c)
            return c

        n_many = nk // COUNT_UNROLL
        c = lax.fori_loop(0, n_many, step_many, jnp.zeros((tq, LANES), i16))
        c = lax.fori_loop(n_many * COUNT_UNROLL, nk, step, c)
        return jnp.sum(c.astype(F32), axis=1, keepdims=True)

    def search(ref, base_cnt):
        def bit_step(it, v):
            cand = v + jnp.left_shift(jnp.int32(1), 15 - it)
            return jnp.where(base_cnt + count_ge(ref, cand) >= float(topk), cand, v)

        return lax.fori_loop(0, 16, bit_step, jnp.full((tq, LANES), low16, jnp.int32))

    v_hi = search(hi_scr, 0.0)
    n_above = count_ge(hi_scr, v_hi + 1)
    v_hi_t = lanes(v_hi.astype(i16))

    def keep_equal(kb, carry):
        lo_scr[kb] = jnp.where(hi_scr[kb] == v_hi_t, lo_scr[kb], i16(low16))
        return carry

    lax.fori_loop(0, nk, keep_equal, 0)
    v_lo = search(lo_scr, n_above)
    v_lo = jnp.where(v_hi == low16, jnp.maximum(v_lo, low16 + 1), v_lo)
    v_lo_t = lanes(v_lo.astype(i16))
    zero = jnp.zeros((tq, tk), bias_ref.dtype)
    neg = jnp.full((tq, tk), NEG_BIG, bias_ref.dtype)

    def emit(kb, carry):
        h16 = hi_scr[kb]
        sel = (h16 > v_hi_t) | ((h16 == v_hi_t) & (lo_scr[kb] >= v_lo_t))
        bias_ref[0, kb] = jnp.where(sel, zero, neg)
        return carry

    lax.fori_loop(0, nk, emit, 0)

    def fill(kb, carry):
        bias_ref[0, kb] = neg
        return carry

    lax.fori_loop(nk, nkt, fill, 0)


def _indexer(main, small, lay, *, s, tq, tk, topk):
    nkt = s // tk
    hi = IDX_HEADS
    group = min(tq, 64)
    qi0, ki0 = lay["qi"][0], lay["ki"][0]
    assert qi0 % hi == 0
    return pl.pallas_call(
        functools.partial(_idx_kernel, tq=tq, tk=tk, nkt=nkt, hi=hi, group=group, topk=topk,
                          wscale=IDX_HEADS ** -0.5 * IDX_DIM ** -0.5),
        grid=(s // tq,),
        in_specs=[pl.BlockSpec((hi, tq, LANES), lambda i: (qi0 // hi, i, 0)),
                  pl.BlockSpec((1, s, LANES), lambda i: (ki0, 0, 0)),
                  pl.BlockSpec((tq, LANES), lambda i: (i, 0))],
        out_specs=pl.BlockSpec((1, nkt, tq, tk), lambda i: (i, 0, 0, 0)),
        out_shape=jax.ShapeDtypeStruct((s // tq, nkt, tq, tk), BF16),
        scratch_shapes=[pltpu.VMEM((nkt, tq, tk), jnp.int16),
                        pltpu.VMEM((nkt, tq, tk), jnp.int16),
                        pltpu.VMEM((hi, tq, LANES), F32)],
        compiler_params=_params("parallel"),
        name="indexer",
    )(main, main, small)


def _attn_kernel(q_ref, k_ref, v_ref, b_ref, g_ref, o_ref, bias_scr, m_scr, acc_scr, *pipe, nh, tq, tk):
    qi = pl.program_id(0)
    kb = pl.program_id(1)

    @pl.when(kb == 0)
    def _():
        m_scr[...] = jnp.full(m_scr.shape, NEG_BIG, F32)
        acc_scr[...] = jnp.zeros(acc_scr.shape, F32)

    @pl.when(kb * tk <= qi * tq + (tq - 1))
    def _():
        na, nb, sq, sk = b_ref.shape
        for a in range(na):
            for b in range(nb):
                bias_scr[a * sq:(a + 1) * sq, b * sk:(b + 1) * sk] = b_ref[a, b].astype(F32)

        ones = jnp.ones((tk, LANES), BF16)

        def qk(h, s_ref):
            s_ref[...] = _dot_nt(q_ref[h], k_ref[h])

        def soft(h, s_ref, p_ref, al_ref):
            s = s_ref[...] + bias_scr[...]
            m_prev = m_scr[h]
            m_new = jnp.maximum(m_prev, jnp.max(s, axis=1, keepdims=True))
            m_scr[h] = m_new
            p_ref[...] = jnp.exp2(s - m_new[:, :1]).astype(BF16)
            al_ref[...] = jnp.exp2(m_prev - m_new)

        def pv(h, p_ref, al_ref):
            al = al_ref[...]
            v_ext = jnp.concatenate([v_ref[h], ones], axis=1)
            acc_scr[h] = acc_scr[h] * jnp.concatenate([al, al], axis=1) + _dot(p_ref[...], v_ext)

        s_a, s_b, p_a, p_b, al_a, al_b = pipe
        qk(0, s_a)
        qk(1, s_b)
        soft(0, s_a, p_a, al_a)

        def pair(j, carry):
            qk(2 * j, s_a)
            soft(2 * j - 1, s_b, p_b, al_b)
            pv(2 * j - 2, p_a, al_a)
            qk(2 * j + 1, s_b)
            soft(2 * j, s_a, p_a, al_a)
            pv(2 * j - 1, p_b, al_b)
            return carry

        lax.fori_loop(1, nh // 2, pair, 0)
        soft(nh - 1, s_b, p_b, al_b)
        pv(nh - 2, p_a, al_a)
        pv(nh - 1, p_b, al_b)

    @pl.when(kb == pl.num_programs(1) - 1)
    def _():
        ss = jnp.zeros((tq, LANES), F32)
        for h in range(nh):
            o_h = acc_scr[h, :, :LANES] / acc_scr[h, :, LANES:]
            acc_scr[h, :, :LANES] = o_h
            ss = ss + o_h * o_h
        ms = jnp.sum(ss, axis=1, keepdims=True) * (1.0 / (nh * LANES))
        r = lax.rsqrt(ms + RMS_EPS)
        for h in range(nh):
            sl = slice(h * LANES, (h + 1) * LANES)
            o_ref[:, sl] = (acc_scr[h, :, :LANES] * r * g_ref[:, sl]).astype(o_ref.dtype)


def _attention(main, bias, g, lay, *, s, tq, tk):
    nh = A_HEADS
    nqb, nkt, sq, sk = bias.shape
    q0, k0, v0 = lay["qa"][0], lay["ka"][0], lay["va"][0]
    assert q0 % nh == 0 and k0 % nh == 0 and v0 % nh == 0

    def last_kb(i):
        return (i * tq + tq - 1) // tk

    return pl.pallas_call(
        functools.partial(_attn_kernel, nh=nh, tq=tq, tk=tk),
        grid=(s // tq, s // tk),
        in_specs=[pl.BlockSpec((nh, tq, LANES), lambda i, j: (q0 // nh, i, 0)),
                  pl.BlockSpec((nh, tk, LANES), lambda i, j: (k0 // nh, jnp.minimum(j, last_kb(i)), 0)),
                  pl.BlockSpec((nh, tk, LANES), lambda i, j: (v0 // nh, jnp.minimum(j, last_kb(i)), 0)),
                  pl.BlockSpec((tq // sq, tk // sk, sq, sk),
                               lambda i, j: (i, jnp.minimum(j, last_kb(i)), 0, 0)),
                  pl.BlockSpec((1, nh * LANES), lambda i, j: (0, 0))],
        out_specs=pl.BlockSpec((tq, nh * LANES), lambda i, j: (i, 0)),
        out_shape=jax.ShapeDtypeStruct((s, nh * LANES), BF16),
        scratch_shapes=[pltpu.VMEM((tq, tk), F32),
                        pltpu.VMEM((nh, tq, LANES), F32),
                        pltpu.VMEM((nh, tq, 2 * LANES), F32),
                        pltpu.VMEM((tq, tk), F32), pltpu.VMEM((tq, tk), F32),
                        pltpu.VMEM((tq, tk), BF16), pltpu.VMEM((tq, tk), BF16),
                        pltpu.VMEM((tq, LANES), F32), pltpu.VMEM((tq, LANES), F32)],
        compiler_params=_params("parallel", "arbitrary"),
        name="attention",
    )(main, main, main, bias, g.reshape(1, nh * LANES))


def _gla_kernel(q_ref, k_ref, v_ref, gg_ref, sm_ref, w2_ref, bgk_ref, gn_ref, o_ref, st_scr, *,
                rows, chunk, dk, dv):
    @pl.when(pl.program_id(1) == 0)
    def _():
        st_scr[...] = jnp.zeros(st_scr.shape, F32)

    r_i = lax.broadcasted_iota(jnp.int32, (chunk, chunk), 0)
    c_i = lax.broadcasted_iota(jnp.int32, (chunk, chunk), 1)
    lower = r_i >= c_i
    tri = jnp.where(lower, 1.0, 0.0).astype(BF16)
    mid = chunk // 2

    def wide(ref, r0):
        return jnp.concatenate([ref[j, pl.ds(r0, chunk), :] for j in range(ref.shape[0])], axis=1)

    def step(c, carry):
        r0 = pl.multiple_of(c * chunk, chunk)
        q = wide(q_ref, r0).astype(F32) * (dk ** -0.5)
        k = wide(k_ref, r0).astype(F32)
        v = wide(v_ref, r0)
        z = _dot(sm_ref[pl.ds(r0, chunk), :].astype(BF16), w2_ref[0]) + bgk_ref[0]
        g = (jnp.minimum(z, 0.0) - jnp.log1p(jnp.exp(-jnp.abs(z)))) * (1.0 / G_TAU)
        g_hi = g.astype(BF16)
        rem = g - g_hi.astype(F32)
        g_mid = rem.astype(BF16)
        g_lo = (rem - g_mid.astype(F32)).astype(BF16)
        b = _dot(tri, g_hi) + _dot(tri, g_mid) + _dot(tri, g_lo)
        b_mid = b[mid:mid + 1, :]
        b_last = b[chunk - 1:chunk, :]
        qe = (q * jnp.exp(b - b_mid)).astype(BF16)
        ke = (k * jnp.exp(b_mid - b)).astype(BF16)
        att = jnp.where(lower, _dot_nt(qe, ke), 0.0)
        st = st_scr[...]
        o = _dot(att.astype(BF16), v) + _dot_nt((q * jnp.exp(b)).astype(BF16), st.astype(BF16))
        kd = (k * jnp.exp(b_last - b)).astype(BF16)
        st_scr[...] = st * jnp.exp(b_last) + _dot_tn(v, kd)
        ms = jnp.mean(o * o, axis=1, keepdims=True)
        gate = wide(gg_ref, r0).astype(F32)
        out = (o * lax.rsqrt(ms + RMS_EPS)) * gn_ref[...] * (gate * jax.nn.sigmoid(gate))
        o_ref[pl.ds(r0, chunk), :] = out.astype(o_ref.dtype)
        return carry

    lax.fori_loop(0, rows // chunk, step, 0)


def _gla(main, small, w2, bgk, gn, lay, *, s, rows, chunk, dk, dv):
    nbk, nbv = dk // LANES, dv // LANES
    q0, k0, v0, g0 = lay["qg"][0], lay["kg"][0], lay["vg"][0], lay["gg"][0]
    assert q0 % nbk == 0 and k0 % nbk == 0 and v0 % nbv == 0 and g0 % nbv == 0
    return pl.pallas_call(
        functools.partial(_gla_kernel, rows=rows, chunk=chunk, dk=dk, dv=dv),
        grid=(G_HEADS, s // rows),
        in_specs=[pl.BlockSpec((nbk, rows, LANES), lambda h, r: (q0 // nbk + h, r, 0)),
                  pl.BlockSpec((nbk, rows, LANES), lambda h, r: (k0 // nbk + h, r, 0)),
                  pl.BlockSpec((nbv, rows, LANES), lambda h, r: (v0 // nbv + h, r, 0)),
                  pl.BlockSpec((nbv, rows, LANES), lambda h, r: (g0 // nbv + h, r, 0)),
                  pl.BlockSpec((rows, LANES), lambda h, r: (r, 0)),
                  pl.BlockSpec((1, LANES, dk), lambda h, r: (h, 0, 0)),
                  pl.BlockSpec((1, 1, dk), lambda h, r: (h, 0, 0)),
                  pl.BlockSpec((1, dv), lambda h, r: (0, 0))],
        out_specs=pl.BlockSpec((rows, dv), lambda h, r: (r, h)),
        out_shape=jax.ShapeDtypeStruct((s, G_HEADS * dv), BF16),
        scratch_shapes=[pltpu.VMEM((dv, dk), F32)],
        compiler_params=_params("parallel", "arbitrary"),
        name="gla",
    )(main, main, main, main, small, w2, bgk, gn.reshape(1, dv))


def _ffn_up_kernel(h_ref, wg_ref, wv_ref, cwg_ref, cwv_ref, cbg_ref, cbv_ref, o_ref, w_scr, tail_scr,
                   *, tm, tn, rc):
    @pl.when(pl.program_id(1) == 0)
    def _():
        w_scr[:, :tn] = wg_ref[0].astype(BF16)
        w_scr[:, tn:] = wv_ref[0].astype(BF16)
        tail_scr[...] = jnp.zeros(tail_scr.shape, F32)

    cw = jnp.concatenate([cwg_ref[0], cwv_ref[0]], axis=1)
    cb = jnp.concatenate([cbg_ref[...], cbv_ref[...]], axis=1)
    w0, w1, w2 = cw[0:1, :], cw[1:2, :], cw[2:3, :]

    def taps(a):
        return w2 * a + w1 * pltpu.roll(a, 1, 0) + w0 * pltpu.roll(a, 2, 0) + cb

    tail = tail_scr[...]
    for r in range(tm // rc):
        rows = slice(r * rc, (r + 1) * rc)
        hw = _dot(h_ref[rows, :], w_scr[...])
        top = taps(jnp.concatenate([tail, hw[0:8]], axis=0))[8:16]
        u = jnp.concatenate([top, taps(hw)[8:]], axis=0)
        gate, val = u[:, :tn], u[:, tn:]
        o_ref[rows, :] = (gate * jax.nn.sigmoid(gate) * val).astype(o_ref.dtype)
        tail = hw[rc - 8:rc, :]
    tail_scr[...] = tail


def _ffn_up(hb, w_up, conv_w, conv_b, *, d_ff, tm, tn):
    s, d = hb.shape
    nj = d_ff // tn
    return pl.pallas_call(
        functools.partial(_ffn_up_kernel, tm=tm, tn=tn, rc=min(tm, 512)),
        grid=(nj, s // tm),
        in_specs=[pl.BlockSpec((tm, d), lambda j, i: (i, 0)),
                  pl.BlockSpec((1, d, tn), lambda j, i: (0, 0, j)),
                  pl.BlockSpec((1, d, tn), lambda j, i: (0, 0, nj + j)),
                  pl.BlockSpec((1, CONV_W, tn), lambda j, i: (0, 0, j)),
                  pl.BlockSpec((1, CONV_W, tn), lambda j, i: (0, 0, nj + j)),
                  pl.BlockSpec((1, tn), lambda j, i: (0, j)),
                  pl.BlockSpec((1, tn), lambda j, i: (0, nj + j))],
        out_specs=pl.BlockSpec((tm, tn), lambda j, i: (i, j)),
        out_shape=jax.ShapeDtypeStruct((s, d_ff), BF16),
        scratch_shapes=[pltpu.VMEM((d, 2 * tn), BF16), pltpu.VMEM((8, 2 * tn), F32)],
        compiler_params=_params("parallel", "arbitrary"),
        name="ffn_up",
    )(hb, w_up, w_up, conv_w, conv_w, conv_b, conv_b)


def _layout(d_model):
    a_width = A_HEADS * A_HEAD_DIM
    g_width = d_model - a_width
    g_kwidth = g_width // 2
    names = ("qa", "ka", "va", "qi", "ki", "wi", "qg", "kg", "vg", "glr", "gg")
    sizes = (a_width, a_width, a_width, IDX_HEADS * IDX_DIM, IDX_DIM, IDX_HEADS,
             g_kwidth, g_kwidth, g_width, G_LOWRANK, g_width)
    offs = np.concatenate([[0], np.cumsum(sizes)])
    src = {n: (int(offs[i]), int(offs[i + 1])) for i, n in enumerate(names)}
    order = ("qi", "qa", "ka", "va", "qg", "kg", "vg", "gg", "ki")
    lay, blk = {}, 0
    for n in order:
        width = src[n][1] - src[n][0]
        assert width % LANES == 0
        lay[n] = (blk, width // LANES)
        blk += width // LANES
    return src, order, lay, blk


def kernel(x, w_in, w_gk2, b_gk, attn_out_g, gla_norm_g, w_o, ln1_g, ln1_b, w_up, conv_w, conv_b,
           w_down, ln2_g, ln2_b):
    assert x.shape[0] == 1 and w_in.shape[0] == DEPTH == 1
    _, s, d = x.shape
    x2 = x[0]
    src, order, lay, nb_main = _layout(d)
    g_width = d - A_HEADS * A_HEAD_DIM
    dv = g_width // G_HEADS
    dk = dv // 2
    d_ff = w_down.shape[1]
    topk = min(TOPK_MAX, s // 4)

    w = w_in[0]
    w_main = jnp.concatenate([w[:, src[n][0]:src[n][1]] for n in order] + [jnp.zeros((d, LANES), F32)],
                             axis=1).astype(BF16)
    n_small = IDX_HEADS + G_LOWRANK
    assert n_small <= LANES
    w_small = jnp.concatenate([w[:, src["wi"][0]:src["wi"][1]], w[:, src["glr"][0]:src["glr"][1]],
                               jnp.zeros((d, LANES - n_small), F32)], axis=1).astype(BF16)
    w2 = jnp.zeros((LANES, G_HEADS * dk), F32).at[IDX_HEADS:n_small].set(w_gk2[0])
    w2 = w2.reshape(LANES, G_HEADS, dk).transpose(1, 0, 2).astype(BF16)
    bgk = b_gk[0].reshape(G_HEADS, 1, dk)
    wd = w_down[0].astype(BF16)
    n_main = (nb_main + 1) * LANES
    q_lo, q_hi = lay["qa"][0] * LANES, (lay["qa"][0] + lay["qa"][1]) * LANES
    col = np.ones((1, n_main), np.float32)
    col[:, q_lo:q_hi] = A_HEAD_DIM ** -0.5 * LOG2E
    col_scale = jnp.asarray(col)

    tm = _tile(s, 1024)
    main, small = _proj_blocks(x2, w_main, w_small, col_scale, tm=_tile(s, 512), tn=_tile(n_main, 1280))
    bias = _indexer(main, small, lay, s=s, tq=256, tk=256, topk=topk)
    ta = _tile(s, 512)
    o_a = _attention(main, bias, attn_out_g[0], lay, s=s, tq=ta, tk=ta)
    o_g = _gla(main, small, w2, bgk, gla_norm_g[0], lay, s=s, rows=_tile(s, 512), chunk=64, dk=dk, dv=dv)
    y1 = _w_o(o_a, o_g, w_o, x2, tm=tm, tn=_tile(d, 512), alpha=DN_ALPHA)
    h, hb = _layer_norm(y1, ln1_g[0], ln1_b[0], tr=_tile(s, 256, 8), also_bf16=True)

    act = _ffn_up(hb, w_up, conv_w, conv_b[0].reshape(1, 2 * d_ff), d_ff=d_ff, tm=tm, tn=_tile(d_ff, 256))
    y2 = _matmul(act, wd, tm=_tile(s, 512), tn=_tile(d, 512), tk=d_ff, out_dtype=F32, residual=h,
                 alpha=DN_ALPHA, name="w_down")
    (out,) = _layer_norm(y2, ln2_g[0], ln2_b[0], tr=_tile(s, 256, 8), also_bf16=False)
    return out[None]
```

```python
import functools

import numpy as np
import jax
import jax.numpy as jnp
from jax import lax
from jax.experimental import pallas as pl
from jax.experimental.pallas import tpu as pltpu

A_HEADS = 16
A_HEAD_DIM = 128
IDX_HEADS = 32
IDX_DIM = 128
TOPK_MAX = 256
G_HEADS = 4
G_LOWRANK = 16
G_TAU = 16.0
CONV_W = 3
LN_EPS = 1e-5
RMS_EPS = 1e-6
DEPTH = 1
DN_ALPHA = (2 * DEPTH) ** 0.25

LANES = 128
VMEM_LIMIT = 56 * 1024 * 1024

BF16 = jnp.bfloat16
F32 = jnp.float32
NEG_BIG = -1e30
LOG2E = 1.4426950408889634
INT_MIN = -2 ** 31
COUNT_UNROLL = 4


def _params(*sem):
    return pltpu.CompilerParams(dimension_semantics=sem, vmem_limit_bytes=VMEM_LIMIT)


def _tile(n, max_tile, quantum=LANES):
    best = None
    for t in range(quantum, min(n, max_tile) + 1, quantum):
        if n % t == 0:
            best = t
    assert best is not None, (n, max_tile, quantum)
    return best


def _dot(a, b):
    return jnp.dot(a, b, preferred_element_type=F32)


def _dot_nt(a, b):
    return lax.dot_general(a, b, (((1,), (1,)), ((), ())), preferred_element_type=F32)


def _dot_tn(a, b):
    return lax.dot_general(a, b, (((0,), (0,)), ((), ())), preferred_element_type=F32)


def _proj_kernel(x_ref, w_ref, ws_ref, sc_ref, o_ref, os_ref, xb_scr):
    @pl.when(pl.program_id(1) == 0)
    def _():
        xb_scr[...] = x_ref[...].astype(BF16)
        os_ref[...] = _dot(xb_scr[...], ws_ref[...])

    res = _dot(xb_scr[...], w_ref[...]) * sc_ref[...]
    for c in range(o_ref.shape[0]):
        o_ref[c] = res[:, c * LANES:(c + 1) * LANES].astype(o_ref.dtype)


def _proj_blocks(x, wb, wsb, col_scale, *, tm, tn):
    s, k = x.shape
    n = wb.shape[1]
    return pl.pallas_call(
        _proj_kernel,
        grid=(s // tm, n // tn),
        in_specs=[pl.BlockSpec((tm, k), lambda i, j: (i, 0)),
                  pl.BlockSpec((k, tn), lambda i, j: (0, j)),
                  pl.BlockSpec((k, LANES), lambda i, j: (0, 0)),
                  pl.BlockSpec((1, tn), lambda i, j: (0, j))],
        out_specs=[pl.BlockSpec((tn // LANES, tm, LANES), lambda i, j: (j, i, 0)),
                   pl.BlockSpec((tm, LANES), lambda i, j: (i, 0))],
        out_shape=[jax.ShapeDtypeStruct((n // LANES, s, LANES), BF16),
                   jax.ShapeDtypeStruct((s, LANES), F32)],
        scratch_shapes=[pltpu.VMEM((tm, k), BF16)],
        compiler_params=_params("parallel", "arbitrary"),
        name="proj",
    )(x, wb, wsb, col_scale)


def _mm_kernel(a_ref, b_ref, *rest, alpha, has_res, nk):
    if has_res:
        r_ref, o_ref = rest[0], rest[1]
        rest = rest[2:]
    else:
        r_ref, o_ref = None, rest[0]
        rest = rest[1:]

    def finish(acc):
        if has_res:
            acc = acc + alpha * r_ref[...]
        o_ref[...] = acc.astype(o_ref.dtype)

    if nk == 1:
        finish(_dot(a_ref[...], b_ref[...]))
        return
    acc_ref = rest[0]
    kk = pl.program_id(2)

    @pl.when(kk == 0)
    def _():
        acc_ref[...] = _dot(a_ref[...], b_ref[...])

    @pl.when(kk > 0)
    def _():
        acc_ref[...] += _dot(a_ref[...], b_ref[...])

    @pl.when(kk == nk - 1)
    def _():
        finish(acc_ref[...])


def _matmul(a, b, *, tm, tn, tk, out_dtype, residual=None, alpha=1.0, name="matmul"):
    m, k = a.shape
    n = b.shape[1]
    nk = k // tk
    in_specs = [pl.BlockSpec((tm, tk), lambda i, j, kk: (i, kk)),
                pl.BlockSpec((tk, tn), lambda i, j, kk: (kk, j))]
    args = [a, b]
    if residual is not None:
        in_specs.append(pl.BlockSpec((tm, tn), lambda i, j, kk: (i, j)))
        args.append(residual)
    scratch = [pltpu.VMEM((tm, tn), F32)] if nk > 1 else []
    return pl.pallas_call(
        functools.partial(_mm_kernel, alpha=alpha, has_res=residual is not None, nk=nk),
        grid=(m // tm, n // tn, nk),
        in_specs=in_specs,
        out_specs=pl.BlockSpec((tm, tn), lambda i, j, kk: (i, j)),
        out_shape=jax.ShapeDtypeStruct((m, n), out_dtype),
        scratch_shapes=scratch,
        compiler_params=_params("parallel", "parallel", "arbitrary"),
        name=name,
    )(*args)


def _wo_kernel(oa_ref, og_ref, wt_ref, wb_ref, x_ref, o_ref, w_scr, *, ka, alpha):
    @pl.when(pl.program_id(1) == 0)
    def _():
        w_scr[:ka, :] = wt_ref[0].astype(BF16)
        w_scr[ka:, :] = wb_ref[0].astype(BF16)

    acc = _dot(oa_ref[...], w_scr[:ka, :]) + _dot(og_ref[...], w_scr[ka:, :])
    o_ref[...] = acc + alpha * x_ref[...]


def _w_o(o_a, o_g, w_o, x, *, tm, tn, alpha):
    s, ka = o_a.shape
    kg = o_g.shape[1]
    d = w_o.shape[2]
    assert ka % kg == 0
    return pl.pallas_call(
        functools.partial(_wo_kernel, ka=ka, alpha=alpha),
        grid=(d // tn, s // tm),
        in_specs=[pl.BlockSpec((tm, ka), lambda j, i: (i, 0)),
                  pl.BlockSpec((tm, kg), lambda j, i: (i, 0)),
                  pl.BlockSpec((1, ka, tn), lambda j, i: (0, 0, j)),
                  pl.BlockSpec((1, kg, tn), lambda j, i: (0, ka // kg, j)),
                  pl.BlockSpec((tm, tn), lambda j, i: (i, j))],
        out_specs=pl.BlockSpec((tm, tn), lambda j, i: (i, j)),
        out_shape=jax.ShapeDtypeStruct((s, d), F32),
        scratch_shapes=[pltpu.VMEM((ka + kg, tn), BF16)],
        compiler_params=_params("parallel", "arbitrary"),
        name="w_o",
    )(o_a, o_g, w_o, w_o, x)


def _ln_kernel(y_ref, g_ref, b_ref, *o_refs):
    y = y_ref[...]
    mu = jnp.mean(y, axis=1, keepdims=True)
    yc = y - mu
    var = jnp.mean(yc * yc, axis=1, keepdims=True)
    out = yc * lax.rsqrt(var + LN_EPS) * g_ref[...] + b_ref[...]
    for o_ref in o_refs:
        o_ref[...] = out.astype(o_ref.dtype)


def _layer_norm(y, g, b, *, tr, also_bf16):
    s, d = y.shape
    out_shape = [jax.ShapeDtypeStruct((s, d), F32)]
    out_specs = [pl.BlockSpec((tr, d), lambda i: (i, 0))]
    if also_bf16:
        out_shape.append(jax.ShapeDtypeStruct((s, d), BF16))
        out_specs.append(pl.BlockSpec((tr, d), lambda i: (i, 0)))
    return pl.pallas_call(
        _ln_kernel,
        grid=(s // tr,),
        in_specs=[pl.BlockSpec((tr, d), lambda i: (i, 0)),
                  pl.BlockSpec((1, d), lambda i: (0, 0)),
                  pl.BlockSpec((1, d), lambda i: (0, 0))],
        out_specs=out_specs,
        out_shape=out_shape,
        compiler_params=_params("parallel"),
        name="layer_norm",
    )(y, g.reshape(1, d), b.reshape(1, d))


def _idx_kernel(qi_ref, ki_ref, sm_ref, bias_ref, hi_scr, lo_scr, wb_scr, *, tq, tk, nkt, hi, group,
                topk, wscale):
    qb = pl.program_id(0)
    nk = ((qb + 1) * tq + tk - 1) // tk
    reps = tk // LANES
    i16 = jnp.int16
    low16 = -2 ** 15

    for h in range(hi):
        wb_scr[h] = jnp.broadcast_to(sm_ref[:, h:h + 1] * wscale, (tq, LANES))

    row = qb * tq + lax.broadcasted_iota(jnp.int32, (group, tk), 0)
    col = lax.broadcasted_iota(jnp.int32, (group, tk), 1)

    def lanes(a):
        return jnp.concatenate([a] * reps, axis=1) if reps > 1 else a

    def fold(a):
        out = a[:, :LANES]
        for r in range(1, reps):
            out = out + a[:, r * LANES:(r + 1) * LANES]
        return out

    def score_tile(kb, carry):
        kt = ki_ref[0, pl.ds(pl.multiple_of(kb * tk, tk), tk), :]
        for r in range(tq // group):
            rows = slice(r * group, (r + 1) * group)
            qs = qi_ref[:, rows, :].reshape(hi * group, IDX_DIM)
            d = _dot_nt(qs, kt)
            acc = jnp.zeros((group, tk), F32)
            for h in range(hi):
                acc = acc + jnp.maximum(d[h * group:(h + 1) * group], 0.0) * lanes(wb_scr[h, rows, :])
            bits = pltpu.bitcast(acc + 0.0, jnp.int32)
            key = jnp.where(kb * tk + col <= row + r * group,
                            bits ^ ((bits >> 31) & jnp.int32(0x7FFFFFFF)), INT_MIN)
            hi_scr[kb, rows, :] = (key >> 16).astype(i16)
            lo_scr[kb, rows, :] = ((key & 0xFFFF) + low16).astype(i16)
        return carry

    lax.fori_loop(0, nk, score_tile, 0)

    def count_ge(ref, cand):
        cand_t = lanes(cand.astype(i16))

        def step(kb, c):
            return c + fold(jnp.where(ref[kb] >= cand_t, i16(1), i16(0)))

        def step_many(q, c):
            for u in range(COUNT_UNROLL):
                c = step(q * COUNT_UNROLL + u, c)
            return c

        n_many = nk // COUNT_UNROLL
        c = lax.fori_loop(0, n_many, step_many, jnp.zeros((tq, LANES), i16))
        c = lax.fori_loop(n_many * COUNT_UNROLL, nk, step, c)
        return jnp.sum(c.astype(F32), axis=1, keepdims=True)

    def search(ref, base_cnt, cnt_start, stop_when_exact):
        def bit_step(it, v, cnt_v):
            cand = v + jnp.left_shift(jnp.int32(1), 15 - it)
            cnt = base_cnt + count_ge(ref, cand)
            ok = cnt >= float(topk)
            return jnp.where(ok, cand, v), jnp.where(ok, cnt, cnt_v)

        v0 = jnp.full((tq, LANES), low16, jnp.int32)
        if not stop_when_exact:
            return lax.fori_loop(0, 16, lambda it, c: bit_step(it, *c), (v0, cnt_start))

        def unsettled(state):
            it, _, cnt_v = state
            return (it < 16) & (jnp.max(jnp.where(cnt_v == float(topk), 0.0, 1.0)) > 0.0)

        def body(state):
            it, v, cnt_v = state
            return (it + 1, *bit_step(it, v, cnt_v))

        _, v, cnt_v = lax.while_loop(unsettled, body, (jnp.int32(0), v0, cnt_start))
        return v, cnt_v

    n_all = jnp.zeros((tq, 1), F32) + (nk * tk).astype(F32)
    v_hi, n_from_hi = search(hi_scr, 0.0, n_all, False)
    n_above = count_ge(hi_scr, v_hi + 1)
    v_hi_t = lanes(v_hi.astype(i16))

    def keep_equal(kb, carry):
        lo_scr[kb] = jnp.where(hi_scr[kb] == v_hi_t, lo_scr[kb], i16(low16))
        return carry

    lax.fori_loop(0, nk, keep_equal, 0)
    v_lo, _ = search(lo_scr, n_above, n_from_hi, True)
    v_lo = jnp.where(v_hi == low16, jnp.maximum(v_lo, low16 + 1), v_lo)
    v_lo_t = lanes(v_lo.astype(i16))
    zero = jnp.zeros((tq, tk), bias_ref.dtype)
    neg = jnp.full((tq, tk), NEG_BIG, bias_ref.dtype)

    def emit(kb, carry):
        h16 = hi_scr[kb]
        sel = (h16 > v_hi_t) | ((h16 == v_hi_t) & (lo_scr[kb] >= v_lo_t))
        bias_ref[0, kb] = jnp.where(sel, zero, neg)
        return carry

    lax.fori_loop(0, nk, emit, 0)

    def fill(kb, carry):
        bias_ref[0, kb] = neg
        return carry

    lax.fori_loop(nk, nkt, fill, 0)


def _indexer(main, small, lay, *, s, tq, tk, topk):
    nkt = s // tk
    hi = IDX_HEADS
    group = min(tq, 64)
    qi0, ki0 = lay["qi"][0], lay["ki"][0]
    assert qi0 % hi == 0
    return pl.pallas_call(
        functools.partial(_idx_kernel, tq=tq, tk=tk, nkt=nkt, hi=hi, group=group, topk=topk,
                          wscale=IDX_HEADS ** -0.5 * IDX_DIM ** -0.5),
        grid=(s // tq,),
        in_specs=[pl.BlockSpec((hi, tq, LANES), lambda i: (qi0 // hi, i, 0)),
                  pl.BlockSpec((1, s, LANES), lambda i: (ki0, 0, 0)),
                  pl.BlockSpec((tq, LANES), lambda i: (i, 0))],
        out_specs=pl.BlockSpec((1, nkt, tq, tk), lambda i: (i, 0, 0, 0)),
        out_shape=jax.ShapeDtypeStruct((s // tq, nkt, tq, tk), BF16),
        scratch_shapes=[pltpu.VMEM((nkt, tq, tk), jnp.int16),
                        pltpu.VMEM((nkt, tq, tk), jnp.int16),
                        pltpu.VMEM((hi, tq, LANES), F32)],
        compiler_params=_params("parallel"),
        name="indexer",
    )(main, main, small)


def _attn_kernel(q_ref, k_ref, v_ref, b_ref, g_ref, o_ref, bias_scr, m_scr, acc_scr, *pipe, nh, tq, tk):
    qi = pl.program_id(0)
    kb = pl.program_id(1)

    @pl.when(kb == 0)
    def _():
        m_scr[...] = jnp.full(m_scr.shape, NEG_BIG, F32)
        acc_scr[...] = jnp.zeros(acc_scr.shape, F32)

    @pl.when(kb * tk <= qi * tq + (tq - 1))
    def _():
        na, nb, sq, sk = b_ref.shape
        for a in range(na):
            for b in range(nb):
                bias_scr[a * sq:(a + 1) * sq, b * sk:(b + 1) * sk] = b_ref[a, b].astype(F32)

        ones = jnp.ones((tk, LANES), BF16)

        def qk(h, s_ref):
            s_ref[...] = _dot_nt(q_ref[h], k_ref[h])

        def soft(h, s_ref, p_ref, al_ref):
            s = s_ref[...] + bias_scr[...]
            m_prev = m_scr[h]
            m_new = jnp.maximum(m_prev, jnp.max(s, axis=1, keepdims=True))
            m_scr[h] = m_new
            p_ref[...] = jnp.exp2(s - m_new[:, :1]).astype(BF16)
            al_ref[...] = jnp.exp2(m_prev - m_new)

        def pv(h, p_ref, al_ref):
            al = al_ref[...]
            v_ext = jnp.concatenate([v_ref[h], ones], axis=1)
            acc_scr[h] = acc_scr[h] * jnp.concatenate([al, al], axis=1) + _dot(p_ref[...], v_ext)

        s_a, s_b, p_a, p_b, al_a, al_b = pipe
        qk(0, s_a)
        qk(1, s_b)
        soft(0, s_a, p_a, al_a)

        def pair(j, carry):
            qk(2 * j, s_a)
            soft(2 * j - 1, s_b, p_b, al_b)
            pv(2 * j - 2, p_a, al_a)
            qk(2 * j + 1, s_b)
            soft(2 * j, s_a, p_a, al_a)
            pv(2 * j - 1, p_b, al_b)
            return carry

        lax.fori_loop(1, nh // 2, pair, 0)
        soft(nh - 1, s_b, p_b, al_b)
        pv(nh - 2, p_a, al_a)
        pv(nh - 1, p_b, al_b)

    @pl.when(kb == pl.num_programs(1) - 1)
    def _():
        ss = jnp.zeros((tq, LANES), F32)
        for h in range(nh):
            o_h = acc_scr[h, :, :LANES] / acc_scr[h, :, LANES:]
            acc_scr[h, :, :LANES] = o_h
            ss = ss + o_h * o_h
        ms = jnp.sum(ss, axis=1, keepdims=True) * (1.0 / (nh * LANES))
        r = lax.rsqrt(ms + RMS_EPS)
        for h in range(nh):
            sl = slice(h * LANES, (h + 1) * LANES)
            o_ref[:, sl] = (acc_scr[h, :, :LANES] * r * g_ref[:, sl]).astype(o_ref.dtype)


def _attention(main, bias, g, lay, *, s, tq, tk):
    nh = A_HEADS
    nqb, nkt, sq, sk = bias.shape
    q0, k0, v0 = lay["qa"][0], lay["ka"][0], lay["va"][0]
    assert q0 % nh == 0 and k0 % nh == 0 and v0 % nh == 0

    def last_kb(i):
        return (i * tq + tq - 1) // tk

    return pl.pallas_call(
        functools.partial(_attn_kernel, nh=nh, tq=tq, tk=tk),
        grid=(s // tq, s // tk),
        in_specs=[pl.BlockSpec((nh, tq, LANES), lambda i, j: (q0 // nh, i, 0)),
                  pl.BlockSpec((nh, tk, LANES), lambda i, j: (k0 // nh, jnp.minimum(j, last_kb(i)), 0)),
                  pl.BlockSpec((nh, tk, LANES), lambda i, j: (v0 // nh, jnp.minimum(j, last_kb(i)), 0)),
                  pl.BlockSpec((tq // sq, tk // sk, sq, sk),
                               lambda i, j: (i, jnp.minimum(j, last_kb(i)), 0, 0)),
                  pl.BlockSpec((1, nh * LANES), lambda i, j: (0, 0))],
        out_specs=pl.BlockSpec((tq, nh * LANES), lambda i, j: (i, 0)),
        out_shape=jax.ShapeDtypeStruct((s, nh * LANES), BF16),
        scratch_shapes=[pltpu.VMEM((tq, tk), F32),
                        pltpu.VMEM((nh, tq, LANES), F32),
                        pltpu.VMEM((nh, tq, 2 * LANES), F32),
                        pltpu.VMEM((tq, tk), F32), pltpu.VMEM((tq, tk), F32),
                        pltpu.VMEM((tq, tk), BF16), pltpu.VMEM((tq, tk), BF16),
                        pltpu.VMEM((tq, LANES), F32), pltpu.VMEM((tq, LANES), F32)],
        compiler_params=_params("parallel", "arbitrary"),
        name="attention",
    )(main, main, main, bias, g.reshape(1, nh * LANES))


def _gla_kernel(q_ref, k_ref, v_ref, gg_ref, sm_ref, w2_ref, bgk_ref, gn_ref, o_ref, st_scr, *,
                rows, chunk, dk, dv):
    @pl.when(pl.program_id(1) == 0)
    def _():
        st_scr[...] = jnp.zeros(st_scr.shape, F32)

    r_i = lax.broadcasted_iota(jnp.int32, (chunk, chunk), 0)
    c_i = lax.broadcasted_iota(jnp.int32, (chunk, chunk), 1)
    lower = r_i >= c_i
    tri = jnp.where(lower, 1.0, 0.0).astype(BF16)
    mid = chunk // 2

    def wide(ref, r0):
        return jnp.concatenate([ref[j, pl.ds(r0, chunk), :] for j in range(ref.shape[0])], axis=1)

    def step(c, carry):
        r0 = pl.multiple_of(c * chunk, chunk)
        q = wide(q_ref, r0).astype(F32) * (dk ** -0.5)
        k = wide(k_ref, r0).astype(F32)
        v = wide(v_ref, r0)
        z = _dot(sm_ref[pl.ds(r0, chunk), :].astype(BF16), w2_ref[0]) + bgk_ref[0]
        g = (jnp.minimum(z, 0.0) - jnp.log1p(jnp.exp(-jnp.abs(z)))) * (1.0 / G_TAU)
        g_hi = g.astype(BF16)
        rem = g - g_hi.astype(F32)
        g_mid = rem.astype(BF16)
        g_lo = (rem - g_mid.astype(F32)).astype(BF16)
        b = _dot(tri, g_hi) + _dot(tri, g_mid) + _dot(tri, g_lo)
        b_mid = b[mid:mid + 1, :]
        b_last = b[chunk - 1:chunk, :]
        qe = (q * jnp.exp(b - b_mid)).astype(BF16)
        ke = (k * jnp.exp(b_mid - b)).astype(BF16)
        att = jnp.where(lower, _dot_nt(qe, ke), 0.0)
        st = st_scr[...]
        o = _dot(att.astype(BF16), v) + _dot_nt((q * jnp.exp(b)).astype(BF16), st.astype(BF16))
        kd = (k * jnp.exp(b_last - b)).astype(BF16)
        st_scr[...] = st * jnp.exp(b_last) + _dot_tn(v, kd)
        ms = jnp.mean(o * o, axis=1, keepdims=True)
        gate = wide(gg_ref, r0).astype(F32)
        out = (o * lax.rsqrt(ms + RMS_EPS)) * gn_ref[...] * (gate * jax.nn.sigmoid(gate))
        o_ref[pl.ds(r0, chunk), :] = out.astype(o_ref.dtype)
        return carry

    lax.fori_loop(0, rows // chunk, step, 0)


def _gla(main, small, w2, bgk, gn, lay, *, s, rows, chunk, dk, dv):
    nbk, nbv = dk // LANES, dv // LANES
    q0, k0, v0, g0 = lay["qg"][0], lay["kg"][0], lay["vg"][0], lay["gg"][0]
    assert q0 % nbk == 0 and k0 % nbk == 0 and v0 % nbv == 0 and g0 % nbv == 0
    return pl.pallas_call(
        functools.partial(_gla_kernel, rows=rows, chunk=chunk, dk=dk, dv=dv),
        grid=(G_HEADS, s // rows),
        in_specs=[pl.BlockSpec((nbk, rows, LANES), lambda h, r: (q0 // nbk + h, r, 0)),
                  pl.BlockSpec((nbk, rows, LANES), lambda h, r: (k0 // nbk + h, r, 0)),
                  pl.BlockSpec((nbv, rows, LANES), lambda h, r: (v0 // nbv + h, r, 0)),
                  pl.BlockSpec((nbv, rows, LANES), lambda h, r: (g0 // nbv + h, r, 0)),
                  pl.BlockSpec((rows, LANES), lambda h, r: (r, 0)),
                  pl.BlockSpec((1, LANES, dk), lambda h, r: (h, 0, 0)),
                  pl.BlockSpec((1, 1, dk), lambda h, r: (h, 0, 0)),
                  pl.BlockSpec((1, dv), lambda h, r: (0, 0))],
        out_specs=pl.BlockSpec((rows, dv), lambda h, r: (r, h)),
        out_shape=jax.ShapeDtypeStruct((s, G_HEADS * dv), BF16),
        scratch_shapes=[pltpu.VMEM((dv, dk), F32)],
        compiler_params=_params("parallel", "arbitrary"),
        name="gla",
    )(main, main, main, main, small, w2, bgk, gn.reshape(1, dv))


def _ffn_up_kernel(h_ref, wg_ref, wv_ref, cwg_ref, cwv_ref, cbg_ref, cbv_ref, o_ref, w_scr, tail_scr,
                   *, tm, tn, rc):
    @pl.when(pl.program_id(1) == 0)
    def _():
        w_scr[:, :tn] = wg_ref[0].astype(BF16)
        w_scr[:, tn:] = wv_ref[0].astype(BF16)
        tail_scr[...] = jnp.zeros(tail_scr.shape, F32)

    cw = jnp.concatenate([cwg_ref[0], cwv_ref[0]], axis=1)
    cb = jnp.concatenate([cbg_ref[...], cbv_ref[...]], axis=1)
    w0, w1, w2 = cw[0:1, :], cw[1:2, :], cw[2:3, :]

    def taps(a):
        return w2 * a + w1 * pltpu.roll(a, 1, 0) + w0 * pltpu.roll(a, 2, 0) + cb

    tail = tail_scr[...]
    for r in range(tm // rc):
        rows = slice(r * rc, (r + 1) * rc)
        hw = _dot(h_ref[rows, :], w_scr[...])
        top = taps(jnp.concatenate([tail, hw[0:8]], axis=0))[8:16]
        u = jnp.concatenate([top, taps(hw)[8:]], axis=0)
        gate, val = u[:, :tn], u[:, tn:]
        o_ref[rows, :] = (gate * jax.nn.sigmoid(gate) * val).astype(o_ref.dtype)
        tail = hw[rc - 8:rc, :]
    tail_scr[...] = tail


def _ffn_up(hb, w_up, conv_w, conv_b, *, d_ff, tm, tn):
    s, d = hb.shape
    nj = d_ff // tn
    return pl.pallas_call(
        functools.partial(_ffn_up_kernel, tm=tm, tn=tn, rc=min(tm, 512)),
        grid=(nj, s // tm),
        in_specs=[pl.BlockSpec((tm, d), lambda j, i: (i, 0)),
                  pl.BlockSpec((1, d, tn), lambda j, i: (0, 0, j)),
                  pl.BlockSpec((1, d, tn), lambda j, i: (0, 0, nj + j)),
                  pl.BlockSpec((1, CONV_W, tn), lambda j, i: (0, 0, j)),
                  pl.BlockSpec((1, CONV_W, tn), lambda j, i: (0, 0, nj + j)),
                  pl.BlockSpec((1, tn), lambda j, i: (0, j)),
                  pl.BlockSpec((1, tn), lambda j, i: (0, nj + j))],
        out_specs=pl.BlockSpec((tm, tn), lambda j, i: (i, j)),
        out_shape=jax.ShapeDtypeStruct((s, d_ff), BF16),
        scratch_shapes=[pltpu.VMEM((d, 2 * tn), BF16), pltpu.VMEM((8, 2 * tn), F32)],
        compiler_params=_params("parallel", "arbitrary"),
        name="ffn_up",
    )(hb, w_up, w_up, conv_w, conv_w, conv_b, conv_b)


def _layout(d_model):
    a_width = A_HEADS * A_HEAD_DIM
    g_width = d_model - a_width
    g_kwidth = g_width // 2
    names = ("qa", "ka", "va", "qi", "ki", "wi", "qg", "kg", "vg", "glr", "gg")
    sizes = (a_width, a_width, a_width, IDX_HEADS * IDX_DIM, IDX_DIM, IDX_HEADS,
             g_kwidth, g_kwidth, g_width, G_LOWRANK, g_width)
    offs = np.concatenate([[0], np.cumsum(sizes)])
    src = {n: (int(offs[i]), int(offs[i + 1])) for i, n in enumerate(names)}
    order = ("qi", "qa", "ka", "va", "qg", "kg", "vg", "gg", "ki")
    lay, blk = {}, 0
    for n in order:
        width = src[n][1] - src[n][0]
        assert width % LANES == 0
        lay[n] = (blk, width // LANES)
        blk += width // LANES
    return src, order, lay, blk


def kernel(x, w_in, w_gk2, b_gk, attn_out_g, gla_norm_g, w_o, ln1_g, ln1_b, w_up, conv_w, conv_b,
           w_down, ln2_g, ln2_b):
    assert x.shape[0] == 1 and w_in.shape[0] == DEPTH == 1
    _, s, d = x.shape
    x2 = x[0]
    src, order, lay, nb_main = _layout(d)
    g_width = d - A_HEADS * A_HEAD_DIM
    dv = g_width // G_HEADS
    dk = dv // 2
    d_ff = w_down.shape[1]
    topk = min(TOPK_MAX, s // 4)

    w = w_in[0]
    w_main = jnp.concatenate([w[:, src[n][0]:src[n][1]] for n in order] + [jnp.zeros((d, LANES), F32)],
                             axis=1).astype(BF16)
    n_small = IDX_HEADS + G_LOWRANK
    assert n_small <= LANES
    w_small = jnp.concatenate([w[:, src["wi"][0]:src["wi"][1]], w[:, src["glr"][0]:src["glr"][1]],
                               jnp.zeros((d, LANES - n_small), F32)], axis=1).astype(BF16)
    w2 = jnp.zeros((LANES, G_HEADS * dk), F32).at[IDX_HEADS:n_small].set(w_gk2[0])
    w2 = w2.reshape(LANES, G_HEADS, dk).transpose(1, 0, 2).astype(BF16)
    bgk = b_gk[0].reshape(G_HEADS, 1, dk)
    wd = w_down[0].astype(BF16)
    n_main = (nb_main + 1) * LANES
    q_lo, q_hi = lay["qa"][0] * LANES, (lay["qa"][0] + lay["qa"][1]) * LANES
    col = np.ones((1, n_main), np.float32)
    col[:, q_lo:q_hi] = A_HEAD_DIM ** -0.5 * LOG2E
    col_scale = jnp.asarray(col)

    tm = _tile(s, 1024)
    main, small = _proj_blocks(x2, w_main, w_small, col_scale, tm=_tile(s, 512), tn=_tile(n_main, 1280))
    bias = _indexer(main, small, lay, s=s, tq=256, tk=256, topk=topk)
    ta = _tile(s, 512)
    o_a = _attention(main, bias, attn_out_g[0], lay, s=s, tq=ta, tk=ta)
    o_g = _gla(main, small, w2, bgk, gla_norm_g[0], lay, s=s, rows=_tile(s, 512), chunk=64, dk=dk, dv=dv)
    y1 = _w_o(o_a, o_g, w_o, x2, tm=tm, tn=_tile(d, 512), alpha=DN_ALPHA)
    h, hb = _layer_norm(y1, ln1_g[0], ln1_b[0], tr=_tile(s, 256, 8), also_bf16=True)

    act = _ffn_up(hb, w_up, conv_w, conv_b[0].reshape(1, 2 * d_ff), d_ff=d_ff, tm=tm, tn=_tile(d_ff, 256))
    y2 = _matmul(act, wd, tm=_tile(s, 512), tn=_tile(d, 512), tk=d_ff, out_dtype=F32, residual=h,
                 alpha=DN_ALPHA, name="w_down")
    (out,) = _layer_norm(y2, ln2_g[0], ln2_b[0], tr=_tile(s, 256, 8), also_bf16=False)
    return out[None]
```

```python
import functools

import numpy as np
import jax
import jax.numpy as jnp
from jax import lax
from jax.experimental import pallas as pl
from jax.experimental.pallas import tpu as pltpu

A_HEADS = 16
A_HEAD_DIM = 128
IDX_HEADS = 32
IDX_DIM = 128
TOPK_MAX = 256
G_HEADS = 4
G_LOWRANK = 16
G_TAU = 16.0
CONV_W = 3
LN_EPS = 1e-5
RMS_EPS = 1e-6
DEPTH = 1
DN_ALPHA = (2 * DEPTH) ** 0.25

LANES = 128
VMEM_LIMIT = 56 * 1024 * 1024

BF16 = jnp.bfloat16
F32 = jnp.float32
NEG_BIG = -1e30
LOG2E = 1.4426950408889634
INT_MIN = -2 ** 31
COUNT_UNROLL = 4


def _params(*sem):
    return pltpu.CompilerParams(dimension_semantics=sem, vmem_limit_bytes=VMEM_LIMIT)


def _tile(n, max_tile, quantum=LANES):
    best = None
    for t in range(quantum, min(n, max_tile) + 1, quantum):
        if n % t == 0:
            best = t
    assert best is not None, (n, max_tile, quantum)
    return best


def _dot(a, b):
    return jnp.dot(a, b, preferred_element_type=F32)


def _dot_nt(a, b):
    return lax.dot_general(a, b, (((1,), (1,)), ((), ())), preferred_element_type=F32)


def _dot_tn(a, b):
    return lax.dot_general(a, b, (((0,), (0,)), ((), ())), preferred_element_type=F32)


def _proj_kernel(x_ref, w_ref, ws_ref, sc_ref, o_ref, os_ref, xb_scr):
    @pl.when(pl.program_id(1) == 0)
    def _():
        xb_scr[...] = x_ref[...].astype(BF16)
        os_ref[...] = _dot(xb_scr[...], ws_ref[...])

    res = _dot(xb_scr[...], w_ref[...]) * sc_ref[...]
    for c in range(o_ref.shape[0]):
        o_ref[c] = res[:, c * LANES:(c + 1) * LANES].astype(o_ref.dtype)


def _proj_blocks(x, wb, wsb, col_scale, *, tm, tn):
    s, k = x.shape
    n = wb.shape[1]
    return pl.pallas_call(
        _proj_kernel,
        grid=(s // tm, n // tn),
        in_specs=[pl.BlockSpec((tm, k), lambda i, j: (i, 0)),
                  pl.BlockSpec((k, tn), lambda i, j: (0, j)),
                  pl.BlockSpec((k, LANES), lambda i, j: (0, 0)),
                  pl.BlockSpec((1, tn), lambda i, j: (0, j))],
        out_specs=[pl.BlockSpec((tn // LANES, tm, LANES), lambda i, j: (j, i, 0)),
                   pl.BlockSpec((tm, LANES), lambda i, j: (i, 0))],
        out_shape=[jax.ShapeDtypeStruct((n // LANES, s, LANES), BF16),
                   jax.ShapeDtypeStruct((s, LANES), F32)],
        scratch_shapes=[pltpu.VMEM((tm, k), BF16)],
        compiler_params=_params("parallel", "arbitrary"),
        name="proj",
    )(x, wb, wsb, col_scale)


def _mm_kernel(a_ref, b_ref, *rest, alpha, has_res, nk):
    if has_res:
        r_ref, o_ref = rest[0], rest[1]
        rest = rest[2:]
    else:
        r_ref, o_ref = None, rest[0]
        rest = rest[1:]

    def finish(acc):
        if has_res:
            acc = acc + alpha * r_ref[...]
        o_ref[...] = acc.astype(o_ref.dtype)

    if nk == 1:
        finish(_dot(a_ref[...], b_ref[...]))
        return
    acc_ref = rest[0]
    kk = pl.program_id(2)

    @pl.when(kk == 0)
    def _():
        acc_ref[...] = _dot(a_ref[...], b_ref[...])

    @pl.when(kk > 0)
    def _():
        acc_ref[...] += _dot(a_ref[...], b_ref[...])

    @pl.when(kk == nk - 1)
    def _():
        finish(acc_ref[...])


def _matmul(a, b, *, tm, tn, tk, out_dtype, residual=None, alpha=1.0, name="matmul"):
    m, k = a.shape
    n = b.shape[1]
    nk = k // tk
    in_specs = [pl.BlockSpec((tm, tk), lambda i, j, kk: (i, kk)),
                pl.BlockSpec((tk, tn), lambda i, j, kk: (kk, j))]
    args = [a, b]
    if residual is not None:
        in_specs.append(pl.BlockSpec((tm, tn), lambda i, j, kk: (i, j)))
        args.append(residual)
    scratch = [pltpu.VMEM((tm, tn), F32)] if nk > 1 else []
    return pl.pallas_call(
        functools.partial(_mm_kernel, alpha=alpha, has_res=residual is not None, nk=nk),
        grid=(m // tm, n // tn, nk),
        in_specs=in_specs,
        out_specs=pl.BlockSpec((tm, tn), lambda i, j, kk: (i, j)),
        out_shape=jax.ShapeDtypeStruct((m, n), out_dtype),
        scratch_shapes=scratch,
        compiler_params=_params("parallel", "parallel", "arbitrary"),
        name=name,
    )(*args)


def _wo_kernel(oa_ref, og_ref, wt_ref, wb_ref, x_ref, o_ref, w_scr, *, ka, alpha):
    @pl.when(pl.program_id(1) == 0)
    def _():
        w_scr[:ka, :] = wt_ref[0].astype(BF16)
        w_scr[ka:, :] = wb_ref[0].astype(BF16)

    acc = _dot(oa_ref[...], w_scr[:ka, :]) + _dot(og_ref[...], w_scr[ka:, :])
    o_ref[...] = acc + alpha * x_ref[...]


def _w_o(o_a, o_g, w_o, x, *, tm, tn, alpha):
    s, ka = o_a.shape
    kg = o_g.shape[1]
    d = w_o.shape[2]
    assert ka % kg == 0
    return pl.pallas_call(
        functools.partial(_wo_kernel, ka=ka, alpha=alpha),
        grid=(d // tn, s // tm),
        in_specs=[pl.BlockSpec((tm, ka), lambda j, i: (i, 0)),
                  pl.BlockSpec((tm, kg), lambda j, i: (i, 0)),
                  pl.BlockSpec((1, ka, tn), lambda j, i: (0, 0, j)),
                  pl.BlockSpec((1, kg, tn), lambda j, i: (0, ka // kg, j)),
                  pl.BlockSpec((tm, tn), lambda j, i: (i, j))],
        out_specs=pl.BlockSpec((tm, tn), lambda j, i: (i, j)),
        out_shape=jax.ShapeDtypeStruct((s, d), F32),
        scratch_shapes=[pltpu.VMEM((ka + kg, tn), BF16)],
        compiler_params=_params("parallel", "arbitrary"),
        name="w_o",
    )(o_a, o_g, w_o, w_o, x)


def _ln_kernel(y_ref, g_ref, b_ref, *o_refs):
    y = y_ref[...]
    mu = jnp.mean(y, axis=1, keepdims=True)
    yc = y - mu
    var = jnp.mean(yc * yc, axis=1, keepdims=True)
    out = yc * lax.rsqrt(var + LN_EPS) * g_ref[...] + b_ref[...]
    for o_ref in o_refs:
        o_ref[...] = out.astype(o_ref.dtype)


def _layer_norm(y, g, b, *, tr, also_bf16):
    s, d = y.shape
    out_shape = [jax.ShapeDtypeStruct((s, d), F32)]
    out_specs = [pl.BlockSpec((tr, d), lambda i: (i, 0))]
    if also_bf16:
        out_shape.append(jax.ShapeDtypeStruct((s, d), BF16))
        out_specs.append(pl.BlockSpec((tr, d), lambda i: (i, 0)))
    return pl.pallas_call(
        _ln_kernel,
        grid=(s // tr,),
        in_specs=[pl.BlockSpec((tr, d), lambda i: (i, 0)),
                  pl.BlockSpec((1, d), lambda i: (0, 0)),
                  pl.BlockSpec((1, d), lambda i: (0, 0))],
        out_specs=out_specs,
        out_shape=out_shape,
        compiler_params=_params("parallel"),
        name="layer_norm",
    )(y, g.reshape(1, d), b.reshape(1, d))


def _idx_kernel(qi_ref, ki_ref, sm_ref, bias_ref, hi_scr, lo_scr, wb_scr, *, tq, tk, nkt, hi, group,
                topk, wscale):
    qb = pl.program_id(0)
    nk = ((qb + 1) * tq + tk - 1) // tk
    reps = tk // LANES
    i16 = jnp.int16
    low16 = -2 ** 15

    for h in range(hi):
        wb_scr[h] = jnp.broadcast_to(sm_ref[:, h:h + 1] * wscale, (tq, LANES))

    row = qb * tq + lax.broadcasted_iota(jnp.int32, (group, tk), 0)
    col = lax.broadcasted_iota(jnp.int32, (group, tk), 1)

    def lanes(a):
        return jnp.concatenate([a] * reps, axis=1) if reps > 1 else a

    def fold(a):
        out = a[:, :LANES]
        for r in range(1, reps):
            out = out + a[:, r * LANES:(r + 1) * LANES]
        return out

    def score_tile(kb, carry):
        kt = ki_ref[0, pl.ds(pl.multiple_of(kb * tk, tk), tk), :]
        for r in range(tq // group):
            rows = slice(r * group, (r + 1) * group)
            qs = qi_ref[:, rows, :].reshape(hi * group, IDX_DIM)
            d = _dot_nt(qs, kt)
            acc = jnp.zeros((group, tk), F32)
            for h in range(hi):
                acc = acc + jnp.maximum(d[h * group:(h + 1) * group], 0.0) * lanes(wb_scr[h, rows, :])
            bits = pltpu.bitcast(acc + 0.0, jnp.int32)
            key = jnp.where(kb * tk + col <= row + r * group,
                            bits ^ ((bits >> 31) & jnp.int32(0x7FFFFFFF)), INT_MIN)
            hi_scr[kb, rows, :] = (key >> 16).astype(i16)
            lo_scr[kb, rows, :] = ((key & 0xFFFF) + low16).astype(i16)
        return carry

    lax.fori_loop(0, nk, score_tile, 0)

    def count_ge(ref, cand):
        cand_t = lanes(cand.astype(i16))

        def step(kb, c):
            return c + fold(jnp.where(ref[kb] >= cand_t, i16(1), i16(0)))

        def step_many(q, c):
            for u in range(COUNT_UNROLL):
                c = step(q * COUNT_UNROLL + u, c)
            return c

        n_many = nk // COUNT_UNROLL
        c = lax.fori_loop(0, n_many, step_many, jnp.zeros((tq, LANES), i16))
        c = lax.fori_loop(n_many * COUNT_UNROLL, nk, step, c)
        return jnp.sum(c.astype(F32), axis=1, keepdims=True)

    kf = float(topk)
    unknown = jnp.full((tq, 1), -1.0, F32)

    def any_row(flag):
        return jnp.max(jnp.where(flag, 1.0, 0.0)) > 0.0

    def bit_search(ref, base_cnt, cnt_start):
        def body(state):
            it, v, cnt_v = state
            cand = v + jnp.left_shift(jnp.int32(1), 15 - it)
            cnt = base_cnt + count_ge(ref, cand)
            ok = cnt >= kf
            return it + 1, jnp.where(ok, cand, v), jnp.where(ok, cnt, cnt_v)

        v0 = jnp.full((tq, LANES), low16, jnp.int32)
        _, v, cnt_v = lax.while_loop(lambda st: (st[0] < 16) & any_row(st[2] != kf), body,
                                     (jnp.int32(0), v0, cnt_start))
        return v, cnt_v

    def bracket_search(ref):
        def class_max(kb, m):
            x = ref[kb]
            return jnp.where(x > m, x, m)

        cm = lax.fori_loop(0, nk, class_max, jnp.full((tq, tk), low16, i16))
        cm = cm.astype(jnp.int32).astype(F32)
        hi0 = jnp.max(cm, axis=1, keepdims=True) + 1.0
        lo0 = jnp.min(cm, axis=1, keepdims=True) if tk >= topk else jnp.full((tq, 1), float(low16), F32)

        def rep(a):
            return jnp.broadcast_to(a, (tq, LANES)).astype(jnp.int32)

        def open_rows(lo, hi, cnt_lo):
            return (hi[:, :1] - lo[:, :1] > 1) & (cnt_lo != kf)

        def body(state):
            lo, hi, cnt_lo = state
            mid = lo + jnp.right_shift(hi - lo, 1)
            cnt = count_ge(ref, mid)
            ok = cnt >= kf
            return jnp.where(ok, mid, lo), jnp.where(ok, hi, mid), jnp.where(ok, cnt, cnt_lo)

        lo, _, cnt_lo = lax.while_loop(lambda st: any_row(open_rows(*st)), body,
                                       (rep(lo0), rep(hi0), unknown))
        return lo, cnt_lo

    v_hi, n_from_hi = bracket_search(hi_scr)
    n_above = count_ge(hi_scr, v_hi + 1)
    v_hi_t = lanes(v_hi.astype(i16))

    def keep_equal(kb, carry):
        lo_scr[kb] = jnp.where(hi_scr[kb] == v_hi_t, lo_scr[kb], i16(low16))
        return carry

    lax.fori_loop(0, nk, keep_equal, 0)
    v_lo, _ = bit_search(lo_scr, n_above, n_from_hi)
    v_lo = jnp.where(v_hi == low16, jnp.maximum(v_lo, low16 + 1), v_lo)
    v_lo_t = lanes(v_lo.astype(i16))
    zero = jnp.zeros((tq, tk), bias_ref.dtype)
    neg = jnp.full((tq, tk), NEG_BIG, bias_ref.dtype)

    def emit(kb, carry):
        h16 = hi_scr[kb]
        sel = (h16 > v_hi_t) | ((h16 == v_hi_t) & (lo_scr[kb] >= v_lo_t))
        bias_ref[0, kb] = jnp.where(sel, zero, neg)
        return carry

    lax.fori_loop(0, nk, emit, 0)

    def fill(kb, carry):
        bias_ref[0, kb] = neg
        return carry

    lax.fori_loop(nk, nkt, fill, 0)


def _indexer(main, small, lay, *, s, tq, tk, topk):
    nkt = s // tk
    hi = IDX_HEADS
    group = min(tq, 64)
    qi0, ki0 = lay["qi"][0], lay["ki"][0]
    assert qi0 % hi == 0
    return pl.pallas_call(
        functools.partial(_idx_kernel, tq=tq, tk=tk, nkt=nkt, hi=hi, group=group, topk=topk,
                          wscale=IDX_HEADS ** -0.5 * IDX_DIM ** -0.5),
        grid=(s // tq,),
        in_specs=[pl.BlockSpec((hi, tq, LANES), lambda i: (qi0 // hi, i, 0)),
                  pl.BlockSpec((1, s, LANES), lambda i: (ki0, 0, 0)),
                  pl.BlockSpec((tq, LANES), lambda i: (i, 0))],
        out_specs=pl.BlockSpec((1, nkt, tq, tk), lambda i: (i, 0, 0, 0)),
        out_shape=jax.ShapeDtypeStruct((s // tq, nkt, tq, tk), BF16),
        scratch_shapes=[pltpu.VMEM((nkt, tq, tk), jnp.int16),
                        pltpu.VMEM((nkt, tq, tk), jnp.int16),
                        pltpu.VMEM((hi, tq, LANES), F32)],
        compiler_params=_params("parallel"),
        name="indexer",
    )(main, main, small)


def _attn_kernel(q_ref, k_ref, v_ref, b_ref, g_ref, o_ref, bias_scr, m_scr, acc_scr, *pipe, nh, tq, tk):
    qi = pl.program_id(0)
    kb = pl.program_id(1)

    @pl.when(kb == 0)
    def _():
        m_scr[...] = jnp.full(m_scr.shape, NEG_BIG, F32)
        acc_scr[...] = jnp.zeros(acc_scr.shape, F32)

    @pl.when(kb * tk <= qi * tq + (tq - 1))
    def _():
        na, nb, sq, sk = b_ref.shape
        for a in range(na):
            for b in range(nb):
                bias_scr[a * sq:(a + 1) * sq, b * sk:(b + 1) * sk] = b_ref[a, b].astype(F32)

        ones = jnp.ones((tk, LANES), BF16)

        def qk(h, s_ref):
            s_ref[...] = _dot_nt(q_ref[h], k_ref[h])

        def soft(h, s_ref, p_ref, al_ref):
            s = s_ref[...] + bias_scr[...]
            m_prev = m_scr[h]
            m_new = jnp.maximum(m_prev, jnp.max(s, axis=1, keepdims=True))
            m_scr[h] = m_new
            p_ref[...] = jnp.exp2(s - m_new[:, :1]).astype(BF16)
            al_ref[...] = jnp.exp2(m_prev - m_new)

        def pv(h, p_ref, al_ref):
            al = al_ref[...]
            v_ext = jnp.concatenate([v_ref[h], ones], axis=1)
            acc_scr[h] = acc_scr[h] * jnp.concatenate([al, al], axis=1) + _dot(p_ref[...], v_ext)

        s_a, s_b, p_a, p_b, al_a, al_b = pipe
        qk(0, s_a)
        qk(1, s_b)
        soft(0, s_a, p_a, al_a)

        def pair(j, carry):
            qk(2 * j, s_a)
            soft(2 * j - 1, s_b, p_b, al_b)
            pv(2 * j - 2, p_a, al_a)
            qk(2 * j + 1, s_b)
            soft(2 * j, s_a, p_a, al_a)
            pv(2 * j - 1, p_b, al_b)
            return carry

        lax.fori_loop(1, nh // 2, pair, 0)
        soft(nh - 1, s_b, p_b, al_b)
        pv(nh - 2, p_a, al_a)
        pv(nh - 1, p_b, al_b)

    @pl.when(kb == pl.num_programs(1) - 1)
    def _():
        ss = jnp.zeros((tq, LANES), F32)
        for h in range(nh):
            o_h = acc_scr[h, :, :LANES] / acc_scr[h, :, LANES:]
            acc_scr[h, :, :LANES] = o_h
            ss = ss + o_h * o_h
        ms = jnp.sum(ss, axis=1, keepdims=True) * (1.0 / (nh * LANES))
        r = lax.rsqrt(ms + RMS_EPS)
        for h in range(nh):
            sl = slice(h * LANES, (h + 1) * LANES)
            o_ref[:, sl] = (acc_scr[h, :, :LANES] * r * g_ref[:, sl]).astype(o_ref.dtype)


def _attention(main, bias, g, lay, *, s, tq, tk):
    nh = A_HEADS
    nqb, nkt, sq, sk = bias.shape
    q0, k0, v0 = lay["qa"][0], lay["ka"][0], lay["va"][0]
    assert q0 % nh == 0 and k0 % nh == 0 and v0 % nh == 0

    def last_kb(i):
        return (i * tq + tq - 1) // tk

    return pl.pallas_call(
        functools.partial(_attn_kernel, nh=nh, tq=tq, tk=tk),
        grid=(s // tq, s // tk),
        in_specs=[pl.BlockSpec((nh, tq, LANES), lambda i, j: (q0 // nh, i, 0)),
                  pl.BlockSpec((nh, tk, LANES), lambda i, j: (k0 // nh, jnp.minimum(j, last_kb(i)), 0)),
                  pl.BlockSpec((nh, tk, LANES), lambda i, j: (v0 // nh, jnp.minimum(j, last_kb(i)), 0)),
                  pl.BlockSpec((tq // sq, tk // sk, sq, sk),
                               lambda i, j: (i, jnp.minimum(j, last_kb(i)), 0, 0)),
                  pl.BlockSpec((1, nh * LANES), lambda i, j: (0, 0))],
        out_specs=pl.BlockSpec((tq, nh * LANES), lambda i, j: (i, 0)),
        out_shape=jax.ShapeDtypeStruct((s, nh * LANES), BF16),
        scratch_shapes=[pltpu.VMEM((tq, tk), F32),
                        pltpu.VMEM((nh, tq, LANES), F32),
                        pltpu.VMEM((nh, tq, 2 * LANES), F32),
                        pltpu.VMEM((tq, tk), F32), pltpu.VMEM((tq, tk), F32),
                        pltpu.VMEM((tq, tk), BF16), pltpu.VMEM((tq, tk), BF16),
                        pltpu.VMEM((tq, LANES), F32), pltpu.VMEM((tq, LANES), F32)],
        compiler_params=_params("parallel", "arbitrary"),
        name="attention",
    )(main, main, main, bias, g.reshape(1, nh * LANES))


def _gla_kernel(q_ref, k_ref, v_ref, gg_ref, sm_ref, w2_ref, bgk_ref, gn_ref, o_ref, st_scr, *,
                rows, chunk, dk, dv):
    @pl.when(pl.program_id(1) == 0)
    def _():
        st_scr[...] = jnp.zeros(st_scr.shape, F32)

    r_i = lax.broadcasted_iota(jnp.int32, (chunk, chunk), 0)
    c_i = lax.broadcasted_iota(jnp.int32, (chunk, chunk), 1)
    lower = r_i >= c_i
    tri = jnp.where(lower, 1.0, 0.0).astype(BF16)
    mid = chunk // 2

    def wide(ref, r0):
        return jnp.concatenate([ref[j, pl.ds(r0, chunk), :] for j in range(ref.shape[0])], axis=1)

    def step(c, carry):
        r0 = pl.multiple_of(c * chunk, chunk)
        q = wide(q_ref, r0).astype(F32) * (dk ** -0.5)
        k = wide(k_ref, r0).astype(F32)
        v = wide(v_ref, r0)
        z = _dot(sm_ref[pl.ds(r0, chunk), :].astype(BF16), w2_ref[0]) + bgk_ref[0]
        g = (jnp.minimum(z, 0.0) - jnp.log1p(jnp.exp(-jnp.abs(z)))) * (1.0 / G_TAU)
        g_hi = g.astype(BF16)
        rem = g - g_hi.astype(F32)
        g_mid = rem.astype(BF16)
        g_lo = (rem - g_mid.astype(F32)).astype(BF16)
        b = _dot(tri, g_hi) + _dot(tri, g_mid) + _dot(tri, g_lo)
        b_mid = b[mid:mid + 1, :]
        b_last = b[chunk - 1:chunk, :]
        qe = (q * jnp.exp(b - b_mid)).astype(BF16)
        ke = (k * jnp.exp(b_mid - b)).astype(BF16)
        att = jnp.where(lower, _dot_nt(qe, ke), 0.0)
        st = st_scr[...]
        o = _dot(att.astype(BF16), v) + _dot_nt((q * jnp.exp(b)).astype(BF16), st.astype(BF16))
        kd = (k * jnp.exp(b_last - b)).astype(BF16)
        st_scr[...] = st * jnp.exp(b_last) + _dot_tn(v, kd)
        ms = jnp.mean(o * o, axis=1, keepdims=True)
        gate = wide(gg_ref, r0).astype(F32)
        out = (o * lax.rsqrt(ms + RMS_EPS)) * gn_ref[...] * (gate * jax.nn.sigmoid(gate))
        o_ref[pl.ds(r0, chunk), :] = out.astype(o_ref.dtype)
        return carry

    lax.fori_loop(0, rows // chunk, step, 0)


def _gla(main, small, w2, bgk, gn, lay, *, s, rows, chunk, dk, dv):
    nbk, nbv = dk // LANES, dv // LANES
    q0, k0, v0, g0 = lay["qg"][0], lay["kg"][0], lay["vg"][0], lay["gg"][0]
    assert q0 % nbk == 0 and k0 % nbk == 0 and v0 % nbv == 0 and g0 % nbv == 0
    return pl.pallas_call(
        functools.partial(_gla_kernel, rows=rows, chunk=chunk, dk=dk, dv=dv),
        grid=(G_HEADS, s // rows),
        in_specs=[pl.BlockSpec((nbk, rows, LANES), lambda h, r: (q0 // nbk + h, r, 0)),
                  pl.BlockSpec((nbk, rows, LANES), lambda h, r: (k0 // nbk + h, r, 0)),
                  pl.BlockSpec((nbv, rows, LANES), lambda h, r: (v0 // nbv + h, r, 0)),
                  pl.BlockSpec((nbv, rows, LANES), lambda h, r: (g0 // nbv + h, r, 0)),
                  pl.BlockSpec((rows, LANES), lambda h, r: (r, 0)),
                  pl.BlockSpec((1, LANES, dk), lambda h, r: (h, 0, 0)),
                  pl.BlockSpec((1, 1, dk), lambda h, r: (h, 0, 0)),
                  pl.BlockSpec((1, dv), lambda h, r: (0, 0))],
        out_specs=pl.BlockSpec((rows, dv), lambda h, r: (r, h)),
        out_shape=jax.ShapeDtypeStruct((s, G_HEADS * dv), BF16),
        scratch_shapes=[pltpu.VMEM((dv, dk), F32)],
        compiler_params=_params("parallel", "arbitrary"),
        name="gla",
    )(main, main, main, main, small, w2, bgk, gn.reshape(1, dv))


def _ffn_up_kernel(h_ref, wg_ref, wv_ref, cwg_ref, cwv_ref, cbg_ref, cbv_ref, o_ref, w_scr, tail_scr,
                   *, tm, tn, rc):
    @pl.when(pl.program_id(1) == 0)
    def _():
        w_scr[:, :tn] = wg_ref[0].astype(BF16)
        w_scr[:, tn:] = wv_ref[0].astype(BF16)
        tail_scr[...] = jnp.zeros(tail_scr.shape, F32)

    cw = jnp.concatenate([cwg_ref[0], cwv_ref[0]], axis=1)
    cb = jnp.concatenate([cbg_ref[...], cbv_ref[...]], axis=1)
    w0, w1, w2 = cw[0:1, :], cw[1:2, :], cw[2:3, :]

    def taps(a):
        return w2 * a + w1 * pltpu.roll(a, 1, 0) + w0 * pltpu.roll(a, 2, 0) + cb

    tail = tail_scr[...]
    for r in range(tm // rc):
        rows = slice(r * rc, (r + 1) * rc)
        hw = _dot(h_ref[rows, :], w_scr[...])
        top = taps(jnp.concatenate([tail, hw[0:8]], axis=0))[8:16]
        u = jnp.concatenate([top, taps(hw)[8:]], axis=0)
        gate, val = u[:, :tn], u[:, tn:]
        o_ref[rows, :] = (gate * jax.nn.sigmoid(gate) * val).astype(o_ref.dtype)
        tail = hw[rc - 8:rc, :]
    tail_scr[...] = tail


def _ffn_up(hb, w_up, conv_w, conv_b, *, d_ff, tm, tn):
    s, d = hb.shape
    nj = d_ff // tn
    return pl.pallas_call(
        functools.partial(_ffn_up_kernel, tm=tm, tn=tn, rc=min(tm, 512)),
        grid=(nj, s // tm),
        in_specs=[pl.BlockSpec((tm, d), lambda j, i: (i, 0)),
                  pl.BlockSpec((1, d, tn), lambda j, i: (0, 0, j)),
                  pl.BlockSpec((1, d, tn), lambda j, i: (0, 0, nj + j)),
                  pl.BlockSpec((1, CONV_W, tn), lambda j, i: (0, 0, j)),
                  pl.BlockSpec((1, CONV_W, tn), lambda j, i: (0, 0, nj + j)),
                  pl.BlockSpec((1, tn), lambda j, i: (0, j)),
                  pl.BlockSpec((1, tn), lambda j, i: (0, nj + j))],
        out_specs=pl.BlockSpec((tm, tn), lambda j, i: (i, j)),
        out_shape=jax.ShapeDtypeStruct((s, d_ff), BF16),
        scratch_shapes=[pltpu.VMEM((d, 2 * tn), BF16), pltpu.VMEM((8, 2 * tn), F32)],
        compiler_params=_params("parallel", "arbitrary"),
        name="ffn_up",
    )(hb, w_up, w_up, conv_w, conv_w, conv_b, conv_b)


def _layout(d_model):
    a_width = A_HEADS * A_HEAD_DIM
    g_width = d_model - a_width
    g_kwidth = g_width // 2
    names = ("qa", "ka", "va", "qi", "ki", "wi", "qg", "kg", "vg", "glr", "gg")
    sizes = (a_width, a_width, a_width, IDX_HEADS * IDX_DIM, IDX_DIM, IDX_HEADS,
             g_kwidth, g_kwidth, g_width, G_LOWRANK, g_width)
    offs = np.concatenate([[0], np.cumsum(sizes)])
    src = {n: (int(offs[i]), int(offs[i + 1])) for i, n in enumerate(names)}
    order = ("qi", "qa", "ka", "va", "qg", "kg", "vg", "gg", "ki")
    lay, blk = {}, 0
    for n in order:
        width = src[n][1] - src[n][0]
        assert width % LANES == 0
        lay[n] = (blk, width // LANES)
        blk += width // LANES
    return src, order, lay, blk


def kernel(x, w_in, w_gk2, b_gk, attn_out_g, gla_norm_g, w_o, ln1_g, ln1_b, w_up, conv_w, conv_b,
           w_down, ln2_g, ln2_b):
    assert x.shape[0] == 1 and w_in.shape[0] == DEPTH == 1
    _, s, d = x.shape
    x2 = x[0]
    src, order, lay, nb_main = _layout(d)
    g_width = d - A_HEADS * A_HEAD_DIM
    dv = g_width // G_HEADS
    dk = dv // 2
    d_ff = w_down.shape[1]
    topk = min(TOPK_MAX, s // 4)

    w = w_in[0]
    w_main = jnp.concatenate([w[:, src[n][0]:src[n][1]] for n in order] + [jnp.zeros((d, LANES), F32)],
                             axis=1).astype(BF16)
    n_small = IDX_HEADS + G_LOWRANK
    assert n_small <= LANES
    w_small = jnp.concatenate([w[:, src["wi"][0]:src["wi"][1]], w[:, src["glr"][0]:src["glr"][1]],
                               jnp.zeros((d, LANES - n_small), F32)], axis=1).astype(BF16)
    w2 = jnp.zeros((LANES, G_HEADS * dk), F32).at[IDX_HEADS:n_small].set(w_gk2[0])
    w2 = w2.reshape(LANES, G_HEADS, dk).transpose(1, 0, 2).astype(BF16)
    bgk = b_gk[0].reshape(G_HEADS, 1, dk)
    wd = w_down[0].astype(BF16)
    n_main = (nb_main + 1) * LANES
    q_lo, q_hi = lay["qa"][0] * LANES, (lay["qa"][0] + lay["qa"][1]) * LANES
    col = np.ones((1, n_main), np.float32)
    col[:, q_lo:q_hi] = A_HEAD_DIM ** -0.5 * LOG2E
    col_scale = jnp.asarray(col)

    tm = _tile(s, 1024)
    main, small = _proj_blocks(x2, w_main, w_small, col_scale, tm=_tile(s, 512), tn=_tile(n_main, 1280))
    bias = _indexer(main, small, lay, s=s, tq=256, tk=256, topk=topk)
    ta = _tile(s, 512)
    o_a = _attention(main, bias, attn_out_g[0], lay, s=s, tq=ta, tk=ta)
    o_g = _gla(main, small, w2, bgk, gla_norm_g[0], lay, s=s, rows=_tile(s, 512), chunk=64, dk=dk, dv=dv)
    y1 = _w_o(o_a, o_g, w_o, x2, tm=tm, tn=_tile(d, 512), alpha=DN_ALPHA)
    h, hb = _layer_norm(y1, ln1_g[0], ln1_b[0], tr=_tile(s, 256, 8), also_bf16=True)

    act = _ffn_up(hb, w_up, conv_w, conv_b[0].reshape(1, 2 * d_ff), d_ff=d_ff, tm=tm, tn=_tile(d_ff, 256))
    y2 = _matmul(act, wd, tm=_tile(s, 512), tn=_tile(d, 512), tk=d_ff, out_dtype=F32, residual=h,
                 alpha=DN_ALPHA, name="w_down")
    (out,) = _layer_norm(y2, ln2_g[0], ln2_b[0], tr=_tile(s, 256, 8), also_bf16=False)
    return out[None]
```

```python
import functools

import numpy as np
import jax
import jax.numpy as jnp
from jax import lax
from jax.experimental import pallas as pl
from jax.experimental.pallas import tpu as pltpu

A_HEADS = 16
A_HEAD_DIM = 128
IDX_HEADS = 32
IDX_DIM = 128
TOPK_MAX = 256
G_HEADS = 4
G_LOWRANK = 16
G_TAU = 16.0
CONV_W = 3
LN_EPS = 1e-5
RMS_EPS = 1e-6
DEPTH = 1
DN_ALPHA = (2 * DEPTH) ** 0.25

LANES = 128
VMEM_LIMIT = 56 * 1024 * 1024

BF16 = jnp.bfloat16
F32 = jnp.float32
NEG_BIG = -1e30
LOG2E = 1.4426950408889634
INT_MIN = -2 ** 31
COUNT_UNROLL = 4


def _params(*sem):
    return pltpu.CompilerParams(dimension_semantics=sem, vmem_limit_bytes=VMEM_LIMIT)


def _tile(n, max_tile, quantum=LANES):
    best = None
    for t in range(quantum, min(n, max_tile) + 1, quantum):
        if n % t == 0:
            best = t
    assert best is not None, (n, max_tile, quantum)
    return best


def _dot(a, b):
    return jnp.dot(a, b, preferred_element_type=F32)


def _dot_nt(a, b):
    return lax.dot_general(a, b, (((1,), (1,)), ((), ())), preferred_element_type=F32)


def _dot_tn(a, b):
    return lax.dot_general(a, b, (((0,), (0,)), ((), ())), preferred_element_type=F32)


def _proj_kernel(x_ref, w_ref, ws_ref, sc_ref, o_ref, os_ref, xb_scr):
    @pl.when(pl.program_id(1) == 0)
    def _():
        xb_scr[...] = x_ref[...].astype(BF16)
        os_ref[...] = _dot(xb_scr[...], ws_ref[...])

    res = _dot(xb_scr[...], w_ref[...]) * sc_ref[...]
    for c in range(o_ref.shape[0]):
        o_ref[c] = res[:, c * LANES:(c + 1) * LANES].astype(o_ref.dtype)


def _proj_blocks(x, wb, wsb, col_scale, *, tm, tn):
    s, k = x.shape
    n = wb.shape[1]
    return pl.pallas_call(
        _proj_kernel,
        grid=(s // tm, n // tn),
        in_specs=[pl.BlockSpec((tm, k), lambda i, j: (i, 0)),
                  pl.BlockSpec((k, tn), lambda i, j: (0, j)),
                  pl.BlockSpec((k, LANES), lambda i, j: (0, 0)),
                  pl.BlockSpec((1, tn), lambda i, j: (0, j))],
        out_specs=[pl.BlockSpec((tn // LANES, tm, LANES), lambda i, j: (j, i, 0)),
                   pl.BlockSpec((tm, LANES), lambda i, j: (i, 0))],
        out_shape=[jax.ShapeDtypeStruct((n // LANES, s, LANES), BF16),
                   jax.ShapeDtypeStruct((s, LANES), F32)],
        scratch_shapes=[pltpu.VMEM((tm, k), BF16)],
        compiler_params=_params("parallel", "arbitrary"),
        name="proj",
    )(x, wb, wsb, col_scale)


def _mm_kernel(a_ref, b_ref, *rest, alpha, has_res, nk):
    if has_res:
        r_ref, o_ref = rest[0], rest[1]
        rest = rest[2:]
    else:
        r_ref, o_ref = None, rest[0]
        rest = rest[1:]

    def finish(acc):
        if has_res:
            acc = acc + alpha * r_ref[...]
        o_ref[...] = acc.astype(o_ref.dtype)

    if nk == 1:
        finish(_dot(a_ref[...], b_ref[...]))
        return
    acc_ref = rest[0]
    kk = pl.program_id(2)

    @pl.when(kk == 0)
    def _():
        acc_ref[...] = _dot(a_ref[...], b_ref[...])

    @pl.when(kk > 0)
    def _():
        acc_ref[...] += _dot(a_ref[...], b_ref[...])

    @pl.when(kk == nk - 1)
    def _():
        finish(acc_ref[...])


def _matmul(a, b, *, tm, tn, tk, out_dtype, residual=None, alpha=1.0, name="matmul"):
    m, k = a.shape
    n = b.shape[1]
    nk = k // tk
    in_specs = [pl.BlockSpec((tm, tk), lambda i, j, kk: (i, kk)),
                pl.BlockSpec((tk, tn), lambda i, j, kk: (kk, j))]
    args = [a, b]
    if residual is not None:
        in_specs.append(pl.BlockSpec((tm, tn), lambda i, j, kk: (i, j)))
        args.append(residual)
    scratch = [pltpu.VMEM((tm, tn), F32)] if nk > 1 else []
    return pl.pallas_call(
        functools.partial(_mm_kernel, alpha=alpha, has_res=residual is not None, nk=nk),
        grid=(m // tm, n // tn, nk),
        in_specs=in_specs,
        out_specs=pl.BlockSpec((tm, tn), lambda i, j, kk: (i, j)),
        out_shape=jax.ShapeDtypeStruct((m, n), out_dtype),
        scratch_shapes=scratch,
        compiler_params=_params("parallel", "parallel", "arbitrary"),
        name=name,
    )(*args)


def _wo_kernel(oa_ref, og_ref, wt_ref, wb_ref, x_ref, o_ref, w_scr, *, ka, alpha):
    @pl.when(pl.program_id(1) == 0)
    def _():
        w_scr[:ka, :] = wt_ref[0].astype(BF16)
        w_scr[ka:, :] = wb_ref[0].astype(BF16)

    acc = _dot(oa_ref[...], w_scr[:ka, :]) + _dot(og_ref[...], w_scr[ka:, :])
    o_ref[...] = acc + alpha * x_ref[...]


def _w_o(o_a, o_g, w_o, x, *, tm, tn, alpha):
    s, ka = o_a.shape
    kg = o_g.shape[1]
    d = w_o.shape[2]
    assert ka % kg == 0
    return pl.pallas_call(
        functools.partial(_wo_kernel, ka=ka, alpha=alpha),
        grid=(d // tn, s // tm),
        in_specs=[pl.BlockSpec((tm, ka), lambda j, i: (i, 0)),
                  pl.BlockSpec((tm, kg), lambda j, i: (i, 0)),
                  pl.BlockSpec((1, ka, tn), lambda j, i: (0, 0, j)),
                  pl.BlockSpec((1, kg, tn), lambda j, i: (0, ka // kg, j)),
                  pl.BlockSpec((tm, tn), lambda j, i: (i, j))],
        out_specs=pl.BlockSpec((tm, tn), lambda j, i: (i, j)),
        out_shape=jax.ShapeDtypeStruct((s, d), F32),
        scratch_shapes=[pltpu.VMEM((ka + kg, tn), BF16)],
        compiler_params=_params("parallel", "arbitrary"),
        name="w_o",
    )(o_a, o_g, w_o, w_o, x)


def _ln_kernel(y_ref, g_ref, b_ref, *o_refs):
    y = y_ref[...]
    mu = jnp.mean(y, axis=1, keepdims=True)
    yc = y - mu
    var = jnp.mean(yc * yc, axis=1, keepdims=True)
    out = yc * lax.rsqrt(var + LN_EPS) * g_ref[...] + b_ref[...]
    for o_ref in o_refs:
        o_ref[...] = out.astype(o_ref.dtype)


def _layer_norm(y, g, b, *, tr, also_bf16):
    s, d = y.shape
    out_shape = [jax.ShapeDtypeStruct((s, d), F32)]
    out_specs = [pl.BlockSpec((tr, d), lambda i: (i, 0))]
    if also_bf16:
        out_shape.append(jax.ShapeDtypeStruct((s, d), BF16))
        out_specs.append(pl.BlockSpec((tr, d), lambda i: (i, 0)))
    return pl.pallas_call(
        _ln_kernel,
        grid=(s // tr,),
        in_specs=[pl.BlockSpec((tr, d), lambda i: (i, 0)),
                  pl.BlockSpec((1, d), lambda i: (0, 0)),
                  pl.BlockSpec((1, d), lambda i: (0, 0))],
        out_specs=out_specs,
        out_shape=out_shape,
        compiler_params=_params("parallel"),
        name="layer_norm",
    )(y, g.reshape(1, d), b.reshape(1, d))


def _idx_kernel(qi_ref, ki_ref, sm_ref, bias_ref, hi_scr, lo_scr, wb_scr, *, tq, tk, nkt, hi, group,
                topk, wscale):
    qb = pl.program_id(0)
    nk = ((qb + 1) * tq + tk - 1) // tk
    reps = tk // LANES
    i16 = jnp.int16
    low16 = -2 ** 15

    for h in range(hi):
        wb_scr[h] = jnp.broadcast_to(sm_ref[:, h:h + 1] * wscale, (tq, LANES))

    row = qb * tq + lax.broadcasted_iota(jnp.int32, (group, tk), 0)
    col = lax.broadcasted_iota(jnp.int32, (group, tk), 1)

    def lanes(a):
        return jnp.concatenate([a] * reps, axis=1) if reps > 1 else a

    def fold(a):
        out = a[:, :LANES]
        for r in range(1, reps):
            out = out + a[:, r * LANES:(r + 1) * LANES]
        return out

    def score_tile(kb, carry):
        kt = ki_ref[0, pl.ds(pl.multiple_of(kb * tk, tk), tk), :]
        for r in range(tq // group):
            rows = slice(r * group, (r + 1) * group)
            qs = qi_ref[:, rows, :].reshape(hi * group, IDX_DIM)
            d = _dot_nt(qs, kt)
            acc = jnp.zeros((group, tk), F32)
            for h in range(hi):
                acc = acc + jnp.maximum(d[h * group:(h + 1) * group], 0.0) * lanes(wb_scr[h, rows, :])
            bits = pltpu.bitcast(acc + 0.0, jnp.int32)
            key = jnp.where(kb * tk + col <= row + r * group,
                            bits ^ ((bits >> 31) & jnp.int32(0x7FFFFFFF)), INT_MIN)
            hi_scr[kb, rows, :] = (key >> 16).astype(i16)
            lo_scr[kb, rows, :] = ((key & 0xFFFF) + low16).astype(i16)
        return carry

    lax.fori_loop(0, nk, score_tile, 0)

    def count_ge(ref, cand):
        cand_t = lanes(cand.astype(i16))

        def step(kb, c):
            return c + fold(jnp.where(ref[kb] >= cand_t, i16(1), i16(0)))

        def step_many(q, c):
            for u in range(COUNT_UNROLL):
                c = step(q * COUNT_UNROLL + u, c)
            return c

        n_many = nk // COUNT_UNROLL
        c = lax.fori_loop(0, n_many, step_many, jnp.zeros((tq, LANES), i16))
        c = lax.fori_loop(n_many * COUNT_UNROLL, nk, step, c)
        return jnp.sum(c.astype(F32), axis=1, keepdims=True)

    kf = float(topk)

    def bit_search(ref, base_cnt, cnt_start, stop_when_exact):
        def body(state):
            it, v, cnt_v = state
            cand = v + jnp.left_shift(jnp.int32(1), 15 - it)
            cnt = base_cnt + count_ge(ref, cand)
            ok = cnt >= kf
            return it + 1, jnp.where(ok, cand, v), jnp.where(ok, cnt, cnt_v)

        def unsettled(state):
            go = state[0] < 16
            if stop_when_exact:
                go = go & (jnp.max(jnp.where(state[2] == kf, 0.0, 1.0)) > 0.0)
            return go

        v0 = jnp.full((tq, LANES), low16, jnp.int32)
        _, v, cnt_v = lax.while_loop(unsettled, body, (jnp.int32(0), v0, cnt_start))
        return v, cnt_v

    n_all = jnp.zeros((tq, 1), F32) + (nk * tk).astype(F32)
    v_hi, n_from_hi = bit_search(hi_scr, 0.0, n_all, False)
    n_above = count_ge(hi_scr, v_hi + 1)
    v_hi_t = lanes(v_hi.astype(i16))

    def keep_equal(kb, carry):
        lo_scr[kb] = jnp.where(hi_scr[kb] == v_hi_t, lo_scr[kb], i16(low16))
        return carry

    lax.fori_loop(0, nk, keep_equal, 0)
    v_lo, _ = bit_search(lo_scr, n_above, n_from_hi, True)
    v_lo = jnp.where(v_hi == low16, jnp.maximum(v_lo, low16 + 1), v_lo)
    v_lo_t = lanes(v_lo.astype(i16))
    zero = jnp.zeros((tq, tk), bias_ref.dtype)
    neg = jnp.full((tq, tk), NEG_BIG, bias_ref.dtype)

    def emit(kb, carry):
        h16 = hi_scr[kb]
        sel = (h16 > v_hi_t) | ((h16 == v_hi_t) & (lo_scr[kb] >= v_lo_t))
        bias_ref[0, kb] = jnp.where(sel, zero, neg)
        return carry

    lax.fori_loop(0, nk, emit, 0)

    def fill(kb, carry):
        bias_ref[0, kb] = neg
        return carry

    lax.fori_loop(nk, nkt, fill, 0)


def _indexer(main, small, lay, *, s, tq, tk, topk):
    nkt = s // tk
    hi = IDX_HEADS
    group = min(tq, 64)
    qi0, ki0 = lay["qi"][0], lay["ki"][0]
    assert qi0 % hi == 0
    return pl.pallas_call(
        functools.partial(_idx_kernel, tq=tq, tk=tk, nkt=nkt, hi=hi, group=group, topk=topk,
                          wscale=IDX_HEADS ** -0.5 * IDX_DIM ** -0.5),
        grid=(s // tq,),
        in_specs=[pl.BlockSpec((hi, tq, LANES), lambda i: (qi0 // hi, i, 0)),
                  pl.BlockSpec((1, s, LANES), lambda i: (ki0, 0, 0)),
                  pl.BlockSpec((tq, LANES), lambda i: (i, 0))],
        out_specs=pl.BlockSpec((1, nkt, tq, tk), lambda i: (i, 0, 0, 0)),
        out_shape=jax.ShapeDtypeStruct((s // tq, nkt, tq, tk), BF16),
        scratch_shapes=[pltpu.VMEM((nkt, tq, tk), jnp.int16),
                        pltpu.VMEM((nkt, tq, tk), jnp.int16),
                        pltpu.VMEM((hi, tq, LANES), F32)],
        compiler_params=_params("parallel"),
        name="indexer",
    )(main, main, small)


def _attn_kernel(q_ref, k_ref, v_ref, b_ref, g_ref, o_ref, bias_scr, m_scr, acc_scr, *pipe, nh, tq, tk):
    qi = pl.program_id(0)
    kb = pl.program_id(1)

    @pl.when(kb == 0)
    def _():
        m_scr[...] = jnp.full(m_scr.shape, NEG_BIG, F32)
        acc_scr[...] = jnp.zeros(acc_scr.shape, F32)

    @pl.when(kb * tk <= qi * tq + (tq - 1))
    def _():
        na, nb, sq, sk = b_ref.shape
        for a in range(na):
            for b in range(nb):
                bias_scr[a * sq:(a + 1) * sq, b * sk:(b + 1) * sk] = b_ref[a, b].astype(F32)

        ones = jnp.ones((tk, LANES), BF16)

        def qk(h, s_ref):
            s_ref[...] = _dot_nt(q_ref[h], k_ref[h])

        def soft(h, s_ref, p_ref, al_ref):
            s = s_ref[...] + bias_scr[...]
            m_prev = m_scr[h]
            m_new = jnp.maximum(m_prev, jnp.max(s, axis=1, keepdims=True))
            m_scr[h] = m_new
            p_ref[...] = jnp.exp2(s - m_new[:, :1]).astype(BF16)
            al_ref[...] = jnp.exp2(m_prev - m_new)

        def pv(h, p_ref, al_ref):
            al = al_ref[...]
            v_ext = jnp.concatenate([v_ref[h], ones], axis=1)
            acc_scr[h] = acc_scr[h] * jnp.concatenate([al, al], axis=1) + _dot(p_ref[...], v_ext)

        s_a, s_b, p_a, p_b, al_a, al_b = pipe
        qk(0, s_a)
        qk(1, s_b)
        soft(0, s_a, p_a, al_a)

        def pair(j, carry):
            qk(2 * j, s_a)
            soft(2 * j - 1, s_b, p_b, al_b)
            pv(2 * j - 2, p_a, al_a)
            qk(2 * j + 1, s_b)
            soft(2 * j, s_a, p_a, al_a)
            pv(2 * j - 1, p_b, al_b)
            return carry

        lax.fori_loop(1, nh // 2, pair, 0)
        soft(nh - 1, s_b, p_b, al_b)
        pv(nh - 2, p_a, al_a)
        pv(nh - 1, p_b, al_b)

    @pl.when(kb == pl.num_programs(1) - 1)
    def _():
        ss = jnp.zeros((tq, LANES), F32)
        for h in range(nh):
            o_h = acc_scr[h, :, :LANES] / acc_scr[h, :, LANES:]
            acc_scr[h, :, :LANES] = o_h
            ss = ss + o_h * o_h
        ms = jnp.sum(ss, axis=1, keepdims=True) * (1.0 / (nh * LANES))
        r = lax.rsqrt(ms + RMS_EPS)
        for h in range(nh):
            sl = slice(h * LANES, (h + 1) * LANES)
            o_ref[:, sl] = (acc_scr[h, :, :LANES] * r * g_ref[:, sl]).astype(o_ref.dtype)


def _attention(main, bias, g, lay, *, s, tq, tk):
    nh = A_HEADS
    nqb, nkt, sq, sk = bias.shape
    q0, k0, v0 = lay["qa"][0], lay["ka"][0], lay["va"][0]
    assert q0 % nh == 0 and k0 % nh == 0 and v0 % nh == 0

    def last_kb(i):
        return (i * tq + tq - 1) // tk

    return pl.pallas_call(
        functools.partial(_attn_kernel, nh=nh, tq=tq, tk=tk),
        grid=(s // tq, s // tk),
        in_specs=[pl.BlockSpec((nh, tq, LANES), lambda i, j: (q0 // nh, i, 0)),
                  pl.BlockSpec((nh, tk, LANES), lambda i, j: (k0 // nh, jnp.minimum(j, last_kb(i)), 0)),
                  pl.BlockSpec((nh, tk, LANES), lambda i, j: (v0 // nh, jnp.minimum(j, last_kb(i)), 0)),
                  pl.BlockSpec((tq // sq, tk // sk, sq, sk),
                               lambda i, j: (i, jnp.minimum(j, last_kb(i)), 0, 0)),
                  pl.BlockSpec((1, nh * LANES), lambda i, j: (0, 0))],
        out_specs=pl.BlockSpec((tq, nh * LANES), lambda i, j: (i, 0)),
        out_shape=jax.ShapeDtypeStruct((s, nh * LANES), BF16),
        scratch_shapes=[pltpu.VMEM((tq, tk), F32),
                        pltpu.VMEM((nh, tq, LANES), F32),
                        pltpu.VMEM((nh, tq, 2 * LANES), F32),
                        pltpu.VMEM((tq, tk), F32), pltpu.VMEM((tq, tk), F32),
                        pltpu.VMEM((tq, tk), BF16), pltpu.VMEM((tq, tk), BF16),
                        pltpu.VMEM((tq, LANES), F32), pltpu.VMEM((tq, LANES), F32)],
        compiler_params=_params("parallel", "arbitrary"),
        name="attention",
    )(main, main, main, bias, g.reshape(1, nh * LANES))


def _gla_kernel(q_ref, k_ref, v_ref, gg_ref, sm_ref, w2_ref, bgk_ref, gn_ref, o_ref, st_scr, *,
                rows, chunk, nheads, dk, dv):
    @pl.when(pl.program_id(0) == 0)
    def _():
        st_scr[...] = jnp.zeros(st_scr.shape, F32)

    r_i = lax.broadcasted_iota(jnp.int32, (chunk, chunk), 0)
    c_i = lax.broadcasted_iota(jnp.int32, (chunk, chunk), 1)
    lower = r_i >= c_i
    tri = jnp.where(lower, 1.0, 0.0).astype(BF16)
    mid = chunk // 2
    nbk, nbv = dk // LANES, dv // LANES

    def wide(ref, first, n, r0):
        return jnp.concatenate([ref[first + j, pl.ds(r0, chunk), :] for j in range(n)], axis=1)

    def step(c, carry):
        r0 = pl.multiple_of(c * chunk, chunk)
        z_all = _dot(sm_ref[pl.ds(r0, chunk), :].astype(BF16), w2_ref[...]) + bgk_ref[...]
        for h in range(nheads):
            q = wide(q_ref, h * nbk, nbk, r0).astype(F32) * (dk ** -0.5)
            k = wide(k_ref, h * nbk, nbk, r0).astype(F32)
            v = wide(v_ref, h * nbv, nbv, r0)
            z = z_all[:, h * dk:(h + 1) * dk]
            g = (jnp.minimum(z, 0.0) - jnp.log1p(jnp.exp(-jnp.abs(z)))) * (1.0 / G_TAU)
            g_hi = g.astype(BF16)
            rem = g - g_hi.astype(F32)
            g_mid = rem.astype(BF16)
            g_lo = (rem - g_mid.astype(F32)).astype(BF16)
            b = _dot(tri, g_hi) + _dot(tri, g_mid) + _dot(tri, g_lo)
            b_mid = b[mid:mid + 1, :]
            b_last = b[chunk - 1:chunk, :]
            qe = (q * jnp.exp(b - b_mid)).astype(BF16)
            ke = (k * jnp.exp(b_mid - b)).astype(BF16)
            att = jnp.where(lower, _dot_nt(qe, ke), 0.0)
            st = st_scr[h]
            o = _dot(att.astype(BF16), v) + _dot_nt((q * jnp.exp(b)).astype(BF16), st.astype(BF16))
            kd = (k * jnp.exp(b_last - b)).astype(BF16)
            st_scr[h] = st * jnp.exp(b_last) + _dot_tn(v, kd)
            ms = jnp.mean(o * o, axis=1, keepdims=True)
            gate = wide(gg_ref, h * nbv, nbv, r0).astype(F32)
            out = (o * lax.rsqrt(ms + RMS_EPS)) * gn_ref[...] * (gate * jax.nn.sigmoid(gate))
            o_ref[pl.ds(r0, chunk), h * dv:(h + 1) * dv] = out.astype(o_ref.dtype)
        return carry

    lax.fori_loop(0, rows // chunk, step, 0)


def _gla(main, small, w2, bgk, gn, lay, *, s, rows, chunk, dk, dv):
    nh = G_HEADS
    nqk, nv = nh * dk // LANES, nh * dv // LANES
    q0, k0, v0, g0 = lay["qg"][0], lay["kg"][0], lay["vg"][0], lay["gg"][0]
    assert q0 % nqk == 0 and k0 % nqk == 0 and v0 % nv == 0 and g0 % nv == 0
    return pl.pallas_call(
        functools.partial(_gla_kernel, rows=rows, chunk=chunk, nheads=nh, dk=dk, dv=dv),
        grid=(s // rows,),
        in_specs=[pl.BlockSpec((nqk, rows, LANES), lambda r: (q0 // nqk, r, 0)),
                  pl.BlockSpec((nqk, rows, LANES), lambda r: (k0 // nqk, r, 0)),
                  pl.BlockSpec((nv, rows, LANES), lambda r: (v0 // nv, r, 0)),
                  pl.BlockSpec((nv, rows, LANES), lambda r: (g0 // nv, r, 0)),
                  pl.BlockSpec((rows, LANES), lambda r: (r, 0)),
                  pl.BlockSpec((LANES, nh * dk), lambda r: (0, 0)),
                  pl.BlockSpec((1, nh * dk), lambda r: (0, 0)),
                  pl.BlockSpec((1, dv), lambda r: (0, 0))],
        out_specs=pl.BlockSpec((rows, nh * dv), lambda r: (r, 0)),
        out_shape=jax.ShapeDtypeStruct((s, nh * dv), BF16),
        scratch_shapes=[pltpu.VMEM((nh, dv, dk), F32)],
        compiler_params=_params("arbitrary"),
        name="gla",
    )(main, main, main, main, small, w2, bgk, gn.reshape(1, dv))


def _ffn_up_kernel(h_ref, wg_ref, wv_ref, cwg_ref, cwv_ref, cbg_ref, cbv_ref, o_ref, w_scr, tail_scr,
                   *, tm, tn, rc):
    @pl.when(pl.program_id(1) == 0)
    def _():
        w_scr[:, :tn] = wg_ref[0].astype(BF16)
        w_scr[:, tn:] = wv_ref[0].astype(BF16)
        tail_scr[...] = jnp.zeros(tail_scr.shape, F32)

    cw = jnp.concatenate([cwg_ref[0], cwv_ref[0]], axis=1)
    cb = jnp.concatenate([cbg_ref[...], cbv_ref[...]], axis=1)
    w0, w1, w2 = cw[0:1, :], cw[1:2, :], cw[2:3, :]

    def taps(a):
        return w2 * a + w1 * pltpu.roll(a, 1, 0) + w0 * pltpu.roll(a, 2, 0) + cb

    tail = tail_scr[...]
    for r in range(tm // rc):
        rows = slice(r * rc, (r + 1) * rc)
        hw = _dot(h_ref[rows, :], w_scr[...])
        top = taps(jnp.concatenate([tail, hw[0:8]], axis=0))[8:16]
        u = jnp.concatenate([top, taps(hw)[8:]], axis=0)
        gate, val = u[:, :tn], u[:, tn:]
        o_ref[rows, :] = (gate * jax.nn.sigmoid(gate) * val).astype(o_ref.dtype)
        tail = hw[rc - 8:rc, :]
    tail_scr[...] = tail


def _ffn_up(hb, w_up, conv_w, conv_b, *, d_ff, tm, tn):
    s, d = hb.shape
    nj = d_ff // tn
    return pl.pallas_call(
        functools.partial(_ffn_up_kernel, tm=tm, tn=tn, rc=min(tm, 512)),
        grid=(nj, s // tm),
        in_specs=[pl.BlockSpec((tm, d), lambda j, i: (i, 0)),
                  pl.BlockSpec((1, d, tn), lambda j, i: (0, 0, j)),
                  pl.BlockSpec((1, d, tn), lambda j, i: (0, 0, nj + j)),
                  pl.BlockSpec((1, CONV_W, tn), lambda j, i: (0, 0, j)),
                  pl.BlockSpec((1, CONV_W, tn), lambda j, i: (0, 0, nj + j)),
                  pl.BlockSpec((1, tn), lambda j, i: (0, j)),
                  pl.BlockSpec((1, tn), lambda j, i: (0, nj + j))],
        out_specs=pl.BlockSpec((tm, tn), lambda j, i: (i, j)),
        out_shape=jax.ShapeDtypeStruct((s, d_ff), BF16),
        scratch_shapes=[pltpu.VMEM((d, 2 * tn), BF16), pltpu.VMEM((8, 2 * tn), F32)],
        compiler_params=_params("parallel", "arbitrary"),
        name="ffn_up",
    )(hb, w_up, w_up, conv_w, conv_w, conv_b, conv_b)


def _layout(d_model):
    a_width = A_HEADS * A_HEAD_DIM
    g_width = d_model - a_width
    g_kwidth = g_width // 2
    names = ("qa", "ka", "va", "qi", "ki", "wi", "qg", "kg", "vg", "glr", "gg")
    sizes = (a_width, a_width, a_width, IDX_HEADS * IDX_DIM, IDX_DIM, IDX_HEADS,
             g_kwidth, g_kwidth, g_width, G_LOWRANK, g_width)
    offs = np.concatenate([[0], np.cumsum(sizes)])
    src = {n: (int(offs[i]), int(offs[i + 1])) for i, n in enumerate(names)}
    order = ("qi", "qa", "ka", "va", "qg", "kg", "vg", "gg", "ki")
    lay, blk = {}, 0
    for n in order:
        width = src[n][1] - src[n][0]
        assert width % LANES == 0
        lay[n] = (blk, width // LANES)
        blk += width // LANES
    return src, order, lay, blk


def kernel(x, w_in, w_gk2, b_gk, attn_out_g, gla_norm_g, w_o, ln1_g, ln1_b, w_up, conv_w, conv_b,
           w_down, ln2_g, ln2_b):
    assert x.shape[0] == 1 and w_in.shape[0] == DEPTH == 1
    _, s, d = x.shape
    x2 = x[0]
    src, order, lay, nb_main = _layout(d)
    g_width = d - A_HEADS * A_HEAD_DIM
    dv = g_width // G_HEADS
    dk = dv // 2
    d_ff = w_down.shape[1]
    topk = min(TOPK_MAX, s // 4)

    w = w_in[0]
    w_main = jnp.concatenate([w[:, src[n][0]:src[n][1]] for n in order] + [jnp.zeros((d, LANES), F32)],
                             axis=1).astype(BF16)
    n_small = IDX_HEADS + G_LOWRANK
    assert n_small <= LANES
    w_small = jnp.concatenate([w[:, src["wi"][0]:src["wi"][1]], w[:, src["glr"][0]:src["glr"][1]],
                               jnp.zeros((d, LANES - n_small), F32)], axis=1).astype(BF16)
    w2 = jnp.zeros((LANES, G_HEADS * dk), F32).at[IDX_HEADS:n_small].set(w_gk2[0]).astype(BF16)
    bgk = b_gk[0].reshape(1, G_HEADS * dk)
    wd = w_down[0].astype(BF16)
    n_main = (nb_main + 1) * LANES
    q_lo, q_hi = lay["qa"][0] * LANES, (lay["qa"][0] + lay["qa"][1]) * LANES
    col = np.ones((1, n_main), np.float32)
    col[:, q_lo:q_hi] = A_HEAD_DIM ** -0.5 * LOG2E
    col_scale = jnp.asarray(col)

    tm = _tile(s, 1024)
    main, small = _proj_blocks(x2, w_main, w_small, col_scale, tm=_tile(s, 512), tn=_tile(n_main, 1280))
    bias = _indexer(main, small, lay, s=s, tq=256, tk=256, topk=topk)
    ta = _tile(s, 512)
    o_a = _attention(main, bias, attn_out_g[0], lay, s=s, tq=ta, tk=ta)
    o_g = _gla(main, small, w2, bgk, gla_norm_g[0], lay, s=s, rows=_tile(s, 512), chunk=64, dk=dk, dv=dv)
    y1 = _w_o(o_a, o_g, w_o, x2, tm=tm, tn=_tile(d, 512), alpha=DN_ALPHA)
    h, hb = _layer_norm(y1, ln1_g[0], ln1_b[0], tr=_tile(s, 256, 8), also_bf16=True)

    act = _ffn_up(hb, w_up, conv_w, conv_b[0].reshape(1, 2 * d_ff), d_ff=d_ff, tm=tm, tn=_tile(d_ff, 256))
    y2 = _matmul(act, wd, tm=_tile(s, 512), tn=_tile(d, 512), tk=d_ff, out_dtype=F32, residual=h,
                 alpha=DN_ALPHA, name="w_down")
    (out,) = _layer_norm(y2, ln2_g[0], ln2_b[0], tr=_tile(s, 256, 8), also_bf16=False)
    return out[None]
```

```python
import functools

import numpy as np
import jax
import jax.numpy as jnp
from jax import lax
from jax.experimental import pallas as pl
from jax.experimental.pallas import tpu as pltpu

A_HEADS = 16
A_HEAD_DIM = 128
IDX_HEADS = 32
IDX_DIM = 128
TOPK_MAX = 256
G_HEADS = 4
G_LOWRANK = 16
G_TAU = 16.0
CONV_W = 3
LN_EPS = 1e-5
RMS_EPS = 1e-6
DEPTH = 1
DN_ALPHA = (2 * DEPTH) ** 0.25

LANES = 128
VMEM_LIMIT = 56 * 1024 * 1024

BF16 = jnp.bfloat16
F32 = jnp.float32
NEG_BIG = -1e30
LOG2E = 1.4426950408889634
INT_MIN = -2 ** 31
COUNT_UNROLL = 4


def _params(*sem):
    return pltpu.CompilerParams(dimension_semantics=sem, vmem_limit_bytes=VMEM_LIMIT)


def _tile(n, max_tile, quantum=LANES):
    best = None
    for t in range(quantum, min(n, max_tile) + 1, quantum):
        if n % t == 0:
            best = t
    assert best is not None, (n, max_tile, quantum)
    return best


def _dot(a, b):
    return jnp.dot(a, b, preferred_element_type=F32)


def _dot_nt(a, b):
    return lax.dot_general(a, b, (((1,), (1,)), ((), ())), preferred_element_type=F32)


def _dot_tn(a, b):
    return lax.dot_general(a, b, (((0,), (0,)), ((), ())), preferred_element_type=F32)


def _relayout_kernel(src_ref, aux_ref, kind_ref, a_ref, b_ref, o_ref, *, shifts, n_wi, n_small):
    del src_ref, aux_ref
    kind = kind_ref[pl.program_id(0)]

    @pl.when(kind == 0)
    def _():
        o_ref[...] = a_ref[0].astype(BF16)

    for i, sh in enumerate(shifts):
        @pl.when(kind == i + 1)
        def _(sh=sh):
            o_ref[...] = jnp.concatenate([a_ref[0][:, sh:], b_ref[0][:, :sh]], axis=1).astype(BF16)

    @pl.when(kind == len(shifts) + 1)
    def _():
        o_ref[...] = jnp.zeros(o_ref.shape, BF16)

    @pl.when(kind == len(shifts) + 2)
    def _():
        lane = lax.broadcasted_iota(jnp.int32, o_ref.shape, 1)
        o_ref[...] = jnp.where(lane < n_wi, a_ref[0], jnp.where(lane < n_small, b_ref[0], 0.0)).astype(BF16)


def _relayout_w_in(w_in, src, order):
    _, d, n_in = w_in.shape
    last_blk = -(-n_in // LANES) - 1
    starts = [src[n][0] + t * LANES for n in order for t in range((src[n][1] - src[n][0]) // LANES)]
    shifts = tuple(sorted({o % LANES for o in starts} - {0}))
    wi0, glr0 = src["wi"][0], src["glr"][0]
    n_wi, n_small = src["wi"][1] - wi0, src["wi"][1] - wi0 + src["glr"][1] - glr0
    assert wi0 % LANES == 0 and glr0 % LANES == n_wi and n_small <= LANES
    src_blk = [o // LANES for o in starts] + [0, wi0 // LANES]
    kind = [0 if o % LANES == 0 else 1 + shifts.index(o % LANES) for o in starts]
    kind += [len(shifts) + 1, len(shifts) + 2]
    aux_blk, prev = [], 0
    for sb, kd in zip(src_blk, kind):
        if 1 <= kd <= len(shifts):
            prev = min(sb + 1, last_blk)
        elif kd == len(shifts) + 2:
            prev = glr0 // LANES
        aux_blk.append(prev)
    nb = len(kind)
    tables = [jnp.asarray(np.asarray(t, np.int32)) for t in (src_blk, aux_blk, kind)]
    return pl.pallas_call(
        functools.partial(_relayout_kernel, shifts=shifts, n_wi=n_wi, n_small=n_small),
        grid_spec=pltpu.PrefetchScalarGridSpec(
            num_scalar_prefetch=3,
            grid=(nb,),
            in_specs=[pl.BlockSpec((1, d, LANES), lambda b, s_, a_, k_: (0, 0, s_[b])),
                      pl.BlockSpec((1, d, LANES), lambda b, s_, a_, k_: (0, 0, a_[b]))],
            out_specs=pl.BlockSpec((d, LANES), lambda b, s_, a_, k_: (0, b))),
        out_shape=jax.ShapeDtypeStruct((d, nb * LANES), BF16),
        compiler_params=_params("arbitrary"),
        name="relayout_w_in",
    )(*tables, w_in, w_in)


def _proj_kernel(x_ref, w_ref, ws_ref, sc_ref, o_ref, os_ref, xb_scr):
    @pl.when(pl.program_id(1) == 0)
    def _():
        xb_scr[...] = x_ref[...].astype(BF16)
        os_ref[...] = _dot(xb_scr[...], ws_ref[...])

    res = _dot(xb_scr[...], w_ref[...]) * sc_ref[...]
    for c in range(o_ref.shape[0]):
        o_ref[c] = res[:, c * LANES:(c + 1) * LANES].astype(o_ref.dtype)


def _proj_blocks(x, wb, col_scale, *, tm, tn):
    s, k = x.shape
    n = wb.shape[1] - LANES
    return pl.pallas_call(
        _proj_kernel,
        grid=(s // tm, n // tn),
        in_specs=[pl.BlockSpec((tm, k), lambda i, j: (i, 0)),
                  pl.BlockSpec((k, tn), lambda i, j: (0, j)),
                  pl.BlockSpec((k, LANES), lambda i, j: (0, n // LANES)),
                  pl.BlockSpec((1, tn), lambda i, j: (0, j))],
        out_specs=[pl.BlockSpec((tn // LANES, tm, LANES), lambda i, j: (j, i, 0)),
                   pl.BlockSpec((tm, LANES), lambda i, j: (i, 0))],
        out_shape=[jax.ShapeDtypeStruct((n // LANES, s, LANES), BF16),
                   jax.ShapeDtypeStruct((s, LANES), F32)],
        scratch_shapes=[pltpu.VMEM((tm, k), BF16)],
        compiler_params=_params("parallel", "arbitrary"),
        name="proj",
    )(x, wb, wb, col_scale)


def _mm_kernel(a_ref, b_ref, *rest, alpha, has_res, nk):
    if has_res:
        r_ref, o_ref = rest[0], rest[1]
        rest = rest[2:]
    else:
        r_ref, o_ref = None, rest[0]
        rest = rest[1:]

    def finish(acc):
        if has_res:
            acc = acc + alpha * r_ref[...]
        o_ref[...] = acc.astype(o_ref.dtype)

    if nk == 1:
        finish(_dot(a_ref[...], b_ref[...]))
        return
    acc_ref = rest[0]
    kk = pl.program_id(2)

    @pl.when(kk == 0)
    def _():
        acc_ref[...] = _dot(a_ref[...], b_ref[...])

    @pl.when(kk > 0)
    def _():
        acc_ref[...] += _dot(a_ref[...], b_ref[...])

    @pl.when(kk == nk - 1)
    def _():
        finish(acc_ref[...])


def _matmul(a, b, *, tm, tn, tk, out_dtype, residual=None, alpha=1.0, name="matmul"):
    m, k = a.shape
    n = b.shape[1]
    nk = k // tk
    in_specs = [pl.BlockSpec((tm, tk), lambda i, j, kk: (i, kk)),
                pl.BlockSpec((tk, tn), lambda i, j, kk: (kk, j))]
    args = [a, b]
    if residual is not None:
        in_specs.append(pl.BlockSpec((tm, tn), lambda i, j, kk: (i, j)))
        args.append(residual)
    scratch = [pltpu.VMEM((tm, tn), F32)] if nk > 1 else []
    return pl.pallas_call(
        functools.partial(_mm_kernel, alpha=alpha, has_res=residual is not None, nk=nk),
        grid=(m // tm, n // tn, nk),
        in_specs=in_specs,
        out_specs=pl.BlockSpec((tm, tn), lambda i, j, kk: (i, j)),
        out_shape=jax.ShapeDtypeStruct((m, n), out_dtype),
        scratch_shapes=scratch,
        compiler_params=_params("parallel", "parallel", "arbitrary"),
        name=name,
    )(*args)


def _wo_kernel(oa_ref, og_ref, wt_ref, wb_ref, x_ref, o_ref, w_scr, *, ka, alpha):
    @pl.when(pl.program_id(1) == 0)
    def _():
        w_scr[:ka, :] = wt_ref[0].astype(BF16)
        w_scr[ka:, :] = wb_ref[0].astype(BF16)

    acc = _dot(oa_ref[...], w_scr[:ka, :]) + _dot(og_ref[...], w_scr[ka:, :])
    o_ref[...] = acc + alpha * x_ref[...]


def _w_o(o_a, o_g, w_o, x, *, tm, tn, alpha):
    s, ka = o_a.shape
    kg = o_g.shape[1]
    d = w_o.shape[2]
    assert ka % kg == 0
    return pl.pallas_call(
        functools.partial(_wo_kernel, ka=ka, alpha=alpha),
        grid=(d // tn, s // tm),
        in_specs=[pl.BlockSpec((tm, ka), lambda j, i: (i, 0)),
                  pl.BlockSpec((tm, kg), lambda j, i: (i, 0)),
                  pl.BlockSpec((1, ka, tn), lambda j, i: (0, 0, j)),
                  pl.BlockSpec((1, kg, tn), lambda j, i: (0, ka // kg, j)),
                  pl.BlockSpec((tm, tn), lambda j, i: (i, j))],
        out_specs=pl.BlockSpec((tm, tn), lambda j, i: (i, j)),
        out_shape=jax.ShapeDtypeStruct((s, d), F32),
        scratch_shapes=[pltpu.VMEM((ka + kg, tn), BF16)],
        compiler_params=_params("parallel", "arbitrary"),
        name="w_o",
    )(o_a, o_g, w_o, w_o, x)


def _ln_kernel(y_ref, g_ref, b_ref, *o_refs):
    y = y_ref[...]
    mu = jnp.mean(y, axis=1, keepdims=True)
    yc = y - mu
    var = jnp.mean(yc * yc, axis=1, keepdims=True)
    out = yc * lax.rsqrt(var + LN_EPS) * g_ref[...] + b_ref[...]
    for o_ref in o_refs:
        o_ref[...] = out.astype(o_ref.dtype)


def _layer_norm(y, g, b, *, tr, also_bf16):
    s, d = y.shape
    out_shape = [jax.ShapeDtypeStruct((s, d), F32)]
    out_specs = [pl.BlockSpec((tr, d), lambda i: (i, 0))]
    if also_bf16:
        out_shape.append(jax.ShapeDtypeStruct((s, d), BF16))
        out_specs.append(pl.BlockSpec((tr, d), lambda i: (i, 0)))
    return pl.pallas_call(
        _ln_kernel,
        grid=(s // tr,),
        in_specs=[pl.BlockSpec((tr, d), lambda i: (i, 0)),
                  pl.BlockSpec((1, d), lambda i: (0, 0)),
                  pl.BlockSpec((1, d), lambda i: (0, 0))],
        out_specs=out_specs,
        out_shape=out_shape,
        compiler_params=_params("parallel"),
        name="layer_norm",
    )(y, g.reshape(1, d), b.reshape(1, d))


def _idx_kernel(qi_ref, ki_ref, sm_ref, bias_ref, hi_scr, lo_scr, wb_scr, *, tq, tk, nkt, hi, group,
                topk, wscale):
    qb = pl.program_id(0)
    nk = ((qb + 1) * tq + tk - 1) // tk
    reps = tk // LANES
    i16 = jnp.int16
    low16 = -2 ** 15

    for h in range(hi):
        wb_scr[h] = jnp.broadcast_to(sm_ref[:, h:h + 1] * wscale, (tq, LANES))

    row = qb * tq + lax.broadcasted_iota(jnp.int32, (group, tk), 0)
    col = lax.broadcasted_iota(jnp.int32, (group, tk), 1)

    def lanes(a):
        return jnp.concatenate([a] * reps, axis=1) if reps > 1 else a

    def fold(a):
        out = a[:, :LANES]
        for r in range(1, reps):
            out = out + a[:, r * LANES:(r + 1) * LANES]
        return out

    def score_tile(kb, carry):
        kt = ki_ref[0, pl.ds(pl.multiple_of(kb * tk, tk), tk), :]
        for r in range(tq // group):
            rows = slice(r * group, (r + 1) * group)
            qs = qi_ref[:, rows, :].reshape(hi * group, IDX_DIM)
            d = _dot_nt(qs, kt)
            acc = jnp.zeros((group, tk), F32)
            for h in range(hi):
                acc = acc + jnp.maximum(d[h * group:(h + 1) * group], 0.0) * lanes(wb_scr[h, rows, :])
            bits = pltpu.bitcast(acc + 0.0, jnp.int32)
            key = jnp.where(kb * tk + col <= row + r * group,
                            bits ^ ((bits >> 31) & jnp.int32(0x7FFFFFFF)), INT_MIN)
            hi_scr[kb, rows, :] = (key >> 16).astype(i16)
            lo_scr[kb, rows, :] = ((key & 0xFFFF) + low16).astype(i16)
        return carry

    lax.fori_loop(0, nk, score_tile, 0)

    def count_ge(ref, cand):
        cand_t = lanes(cand.astype(i16))

        def step(kb, c):
            return c + fold(jnp.where(ref[kb] >= cand_t, i16(1), i16(0)))

        def step_many(q, c):
            for u in range(COUNT_UNROLL):
                c = step(q * COUNT_UNROLL + u, c)
            return c

        n_many = nk // COUNT_UNROLL
        c = lax.fori_loop(0, n_many, step_many, jnp.zeros((tq, LANES), i16))
        c = lax.fori_loop(n_many * COUNT_UNROLL, nk, step, c)
        return jnp.sum(c.astype(F32), axis=1, keepdims=True)

    kf = float(topk)

    def bit_search(ref, base_cnt, cnt_start, stop_when_exact):
        def body(state):
            it, v, cnt_v = state
            cand = v + jnp.left_shift(jnp.int32(1), 15 - it)
            cnt = base_cnt + count_ge(ref, cand)
            ok = cnt >= kf
            return it + 1, jnp.where(ok, cand, v), jnp.where(ok, cnt, cnt_v)

        def unsettled(state):
            go = state[0] < 16
            if stop_when_exact:
                go = go & (jnp.max(jnp.where(state[2] == kf, 0.0, 1.0)) > 0.0)
            return go

        v0 = jnp.full((tq, LANES), low16, jnp.int32)
        _, v, cnt_v = lax.while_loop(unsettled, body, (jnp.int32(0), v0, cnt_start))
        return v, cnt_v

    n_all = jnp.zeros((tq, 1), F32) + (nk * tk).astype(F32)
    v_hi, n_from_hi = bit_search(hi_scr, 0.0, n_all, False)
    n_above = count_ge(hi_scr, v_hi + 1)
    v_hi_t = lanes(v_hi.astype(i16))

    def keep_equal(kb, carry):
        lo_scr[kb] = jnp.where(hi_scr[kb] == v_hi_t, lo_scr[kb], i16(low16))
        return carry

    lax.fori_loop(0, nk, keep_equal, 0)
    v_lo, _ = bit_search(lo_scr, n_above, n_from_hi, True)
    v_lo = jnp.where(v_hi == low16, jnp.maximum(v_lo, low16 + 1), v_lo)
    v_lo_t = lanes(v_lo.astype(i16))
    zero = jnp.zeros((tq, tk), bias_ref.dtype)
    neg = jnp.full((tq, tk), NEG_BIG, bias_ref.dtype)

    def emit(kb, carry):
        h16 = hi_scr[kb]
        sel = (h16 > v_hi_t) | ((h16 == v_hi_t) & (lo_scr[kb] >= v_lo_t))
        bias_ref[0, kb] = jnp.where(sel, zero, neg)
        return carry

    lax.fori_loop(0, nk, emit, 0)

    def fill(kb, carry):
        bias_ref[0, kb] = neg
        return carry

    lax.fori_loop(nk, nkt, fill, 0)


def _indexer(main, small, lay, *, s, tq, tk, topk):
    nkt = s // tk
    hi = IDX_HEADS
    group = min(tq, 64)
    qi0, ki0 = lay["qi"][0], lay["ki"][0]
    assert qi0 % hi == 0
    return pl.pallas_call(
        functools.partial(_idx_kernel, tq=tq, tk=tk, nkt=nkt, hi=hi, group=group, topk=topk,
                          wscale=IDX_HEADS ** -0.5 * IDX_DIM ** -0.5),
        grid=(s // tq,),
        in_specs=[pl.BlockSpec((hi, tq, LANES), lambda i: (qi0 // hi, i, 0)),
                  pl.BlockSpec((1, s, LANES), lambda i: (ki0, 0, 0)),
                  pl.BlockSpec((tq, LANES), lambda i: (i, 0))],
        out_specs=pl.BlockSpec((1, nkt, tq, tk), lambda i: (i, 0, 0, 0)),
        out_shape=jax.ShapeDtypeStruct((s // tq, nkt, tq, tk), BF16),
        scratch_shapes=[pltpu.VMEM((nkt, tq, tk), jnp.int16),
                        pltpu.VMEM((nkt, tq, tk), jnp.int16),
                        pltpu.VMEM((hi, tq, LANES), F32)],
        compiler_params=_params("parallel"),
        name="indexer",
    )(main, main, small)


def _attn_kernel(q_ref, k_ref, v_ref, b_ref, g_ref, o_ref, bias_scr, m_scr, acc_scr, *pipe, nh, tq, tk):
    qi = pl.program_id(0)
    kb = pl.program_id(1)

    @pl.when(kb == 0)
    def _():
        m_scr[...] = jnp.full(m_scr.shape, NEG_BIG, F32)
        acc_scr[...] = jnp.zeros(acc_scr.shape, F32)

    @pl.when(kb * tk <= qi * tq + (tq - 1))
    def _():
        na, nb, sq, sk = b_ref.shape
        for a in range(na):
            for b in range(nb):
                bias_scr[a * sq:(a + 1) * sq, b * sk:(b + 1) * sk] = b_ref[a, b].astype(F32)

        ones = jnp.ones((tk, LANES), BF16)

        def qk(h, s_ref):
            s_ref[...] = _dot_nt(q_ref[h], k_ref[h])

        def soft(h, s_ref, p_ref, al_ref):
            s = s_ref[...] + bias_scr[...]
            m_prev = m_scr[h]
            m_new = jnp.maximum(m_prev, jnp.max(s, axis=1, keepdims=True))
            m_scr[h] = m_new
            p_ref[...] = jnp.exp2(s - m_new[:, :1]).astype(BF16)
            al_ref[...] = jnp.exp2(m_prev - m_new)

        def pv(h, p_ref, al_ref):
            al = al_ref[...]
            v_ext = jnp.concatenate([v_ref[h], ones], axis=1)
            acc_scr[h] = acc_scr[h] * jnp.concatenate([al, al], axis=1) + _dot(p_ref[...], v_ext)

        s_a, s_b, p_a, p_b, al_a, al_b = pipe
        qk(0, s_a)
        qk(1, s_b)
        soft(0, s_a, p_a, al_a)

        def pair(j, carry):
            qk(2 * j, s_a)
            soft(2 * j - 1, s_b, p_b, al_b)
            pv(2 * j - 2, p_a, al_a)
            qk(2 * j + 1, s_b)
            soft(2 * j, s_a, p_a, al_a)
            pv(2 * j - 1, p_b, al_b)
            return carry

        lax.fori_loop(1, nh // 2, pair, 0)
        soft(nh - 1, s_b, p_b, al_b)
        pv(nh - 2, p_a, al_a)
        pv(nh - 1, p_b, al_b)

    @pl.when(kb == pl.num_programs(1) - 1)
    def _():
        ss = jnp.zeros((tq, LANES), F32)
        for h in range(nh):
            o_h = acc_scr[h, :, :LANES] / acc_scr[h, :, LANES:]
            acc_scr[h, :, :LANES] = o_h
            ss = ss + o_h * o_h
        ms = jnp.sum(ss, axis=1, keepdims=True) * (1.0 / (nh * LANES))
        r = lax.rsqrt(ms + RMS_EPS)
        for h in range(nh):
            sl = slice(h * LANES, (h + 1) * LANES)
            o_ref[:, sl] = (acc_scr[h, :, :LANES] * r * g_ref[:, sl]).astype(o_ref.dtype)


def _attention(main, bias, g, lay, *, s, tq, tk):
    nh = A_HEADS
    nqb, nkt, sq, sk = bias.shape
    q0, k0, v0 = lay["qa"][0], lay["ka"][0], lay["va"][0]
    assert q0 % nh == 0 and k0 % nh == 0 and v0 % nh == 0

    def last_kb(i):
        return (i * tq + tq - 1) // tk

    return pl.pallas_call(
        functools.partial(_attn_kernel, nh=nh, tq=tq, tk=tk),
        grid=(s // tq, s // tk),
        in_specs=[pl.BlockSpec((nh, tq, LANES), lambda i, j: (q0 // nh, i, 0)),
                  pl.BlockSpec((nh, tk, LANES), lambda i, j: (k0 // nh, jnp.minimum(j, last_kb(i)), 0)),
                  pl.BlockSpec((nh, tk, LANES), lambda i, j: (v0 // nh, jnp.minimum(j, last_kb(i)), 0)),
                  pl.BlockSpec((tq // sq, tk // sk, sq, sk),
                               lambda i, j: (i, jnp.minimum(j, last_kb(i)), 0, 0)),
                  pl.BlockSpec((1, nh * LANES), lambda i, j: (0, 0))],
        out_specs=pl.BlockSpec((tq, nh * LANES), lambda i, j: (i, 0)),
        out_shape=jax.ShapeDtypeStruct((s, nh * LANES), BF16),
        scratch_shapes=[pltpu.VMEM((tq, tk), F32),
                        pltpu.VMEM((nh, tq, LANES), F32),
                        pltpu.VMEM((nh, tq, 2 * LANES), F32),
                        pltpu.VMEM((tq, tk), F32), pltpu.VMEM((tq, tk), F32),
                        pltpu.VMEM((tq, tk), BF16), pltpu.VMEM((tq, tk), BF16),
                        pltpu.VMEM((tq, LANES), F32), pltpu.VMEM((tq, LANES), F32)],
        compiler_params=_params("parallel", "arbitrary"),
        name="attention",
    )(main, main, main, bias, g.reshape(1, nh * LANES))


def _gla_kernel(q_ref, k_ref, v_ref, gg_ref, sm_ref, w2_ref, bgk_ref, gn_ref, o_ref, st_scr, *,
                rows, chunk, nheads, dk, dv):
    @pl.when(pl.program_id(0) == 0)
    def _():
        st_scr[...] = jnp.zeros(st_scr.shape, F32)

    r_i = lax.broadcasted_iota(jnp.int32, (chunk, chunk), 0)
    c_i = lax.broadcasted_iota(jnp.int32, (chunk, chunk), 1)
    lower = r_i >= c_i
    tri = jnp.where(lower, 1.0, 0.0).astype(BF16)
    mid = chunk // 2
    nbk, nbv = dk // LANES, dv // LANES

    def wide(ref, first, n, r0):
        return jnp.concatenate([ref[first + j, pl.ds(r0, chunk), :] for j in range(n)], axis=1)

    def step(c, carry):
        r0 = pl.multiple_of(c * chunk, chunk)
        z_all = _dot(sm_ref[pl.ds(r0, chunk), :].astype(BF16), w2_ref[...]) + bgk_ref[...]
        for h in range(nheads):
            q = wide(q_ref, h * nbk, nbk, r0).astype(F32) * (dk ** -0.5)
            k = wide(k_ref, h * nbk, nbk, r0).astype(F32)
            v = wide(v_ref, h * nbv, nbv, r0)
            z = z_all[:, h * dk:(h + 1) * dk]
            g = (jnp.minimum(z, 0.0) - jnp.log1p(jnp.exp(-jnp.abs(z)))) * (1.0 / G_TAU)
            g_hi = g.astype(BF16)
            rem = g - g_hi.astype(F32)
            g_mid = rem.astype(BF16)
            g_lo = (rem - g_mid.astype(F32)).astype(BF16)
            b = _dot(tri, g_hi) + _dot(tri, g_mid) + _dot(tri, g_lo)
            b_mid = b[mid:mid + 1, :]
            b_last = b[chunk - 1:chunk, :]
            qe = (q * jnp.exp(b - b_mid)).astype(BF16)
            ke = (k * jnp.exp(b_mid - b)).astype(BF16)
            att = jnp.where(lower, _dot_nt(qe, ke), 0.0)
            st = st_scr[h]
            o = _dot(att.astype(BF16), v) + _dot_nt((q * jnp.exp(b)).astype(BF16), st.astype(BF16))
            kd = (k * jnp.exp(b_last - b)).astype(BF16)
            st_scr[h] = st * jnp.exp(b_last) + _dot_tn(v, kd)
            ms = jnp.mean(o * o, axis=1, keepdims=True)
            gate = wide(gg_ref, h * nbv, nbv, r0).astype(F32)
            out = (o * lax.rsqrt(ms + RMS_EPS)) * gn_ref[...] * (gate * jax.nn.sigmoid(gate))
            o_ref[pl.ds(r0, chunk), h * dv:(h + 1) * dv] = out.astype(o_ref.dtype)
        return carry

    lax.fori_loop(0, rows // chunk, step, 0)


def _gla(main, small, w2, bgk, gn, lay, *, s, rows, chunk, dk, dv):
    nh = G_HEADS
    nqk, nv = nh * dk // LANES, nh * dv // LANES
    q0, k0, v0, g0 = lay["qg"][0], lay["kg"][0], lay["vg"][0], lay["gg"][0]
    assert q0 % nqk == 0 and k0 % nqk == 0 and v0 % nv == 0 and g0 % nv == 0
    return pl.pallas_call(
        functools.partial(_gla_kernel, rows=rows, chunk=chunk, nheads=nh, dk=dk, dv=dv),
        grid=(s // rows,),
        in_specs=[pl.BlockSpec((nqk, rows, LANES), lambda r: (q0 // nqk, r, 0)),
                  pl.BlockSpec((nqk, rows, LANES), lambda r: (k0 // nqk, r, 0)),
                  pl.BlockSpec((nv, rows, LANES), lambda r: (v0 // nv, r, 0)),
                  pl.BlockSpec((nv, rows, LANES), lambda r: (g0 // nv, r, 0)),
                  pl.BlockSpec((rows, LANES), lambda r: (r, 0)),
                  pl.BlockSpec((LANES, nh * dk), lambda r: (0, 0)),
                  pl.BlockSpec((1, nh * dk), lambda r: (0, 0)),
                  pl.BlockSpec((1, dv), lambda r: (0, 0))],
        out_specs=pl.BlockSpec((rows, nh * dv), lambda r: (r, 0)),
        out_shape=jax.ShapeDtypeStruct((s, nh * dv), BF16),
        scratch_shapes=[pltpu.VMEM((nh, dv, dk), F32)],
        compiler_params=_params("arbitrary"),
        name="gla",
    )(main, main, main, main, small, w2, bgk, gn.reshape(1, dv))


def _ffn_up_kernel(h_ref, wg_ref, wv_ref, cwg_ref, cwv_ref, cbg_ref, cbv_ref, o_ref, w_scr, tail_scr,
                   *, tm, tn, rc):
    @pl.when(pl.program_id(1) == 0)
    def _():
        w_scr[:, :tn] = wg_ref[0].astype(BF16)
        w_scr[:, tn:] = wv_ref[0].astype(BF16)
        tail_scr[...] = jnp.zeros(tail_scr.shape, F32)

    cw = jnp.concatenate([cwg_ref[0], cwv_ref[0]], axis=1)
    cb = jnp.concatenate([cbg_ref[...], cbv_ref[...]], axis=1)
    w0, w1, w2 = cw[0:1, :], cw[1:2, :], cw[2:3, :]

    def taps(a):
        return w2 * a + w1 * pltpu.roll(a, 1, 0) + w0 * pltpu.roll(a, 2, 0) + cb

    tail = tail_scr[...]
    for r in range(tm // rc):
        rows = slice(r * rc, (r + 1) * rc)
        hw = _dot(h_ref[rows, :], w_scr[...])
        top = taps(jnp.concatenate([tail, hw[0:8]], axis=0))[8:16]
        u = jnp.concatenate([top, taps(hw)[8:]], axis=0)
        gate, val = u[:, :tn], u[:, tn:]
        o_ref[rows, :] = (gate * jax.nn.sigmoid(gate) * val).astype(o_ref.dtype)
        tail = hw[rc - 8:rc, :]
    tail_scr[...] = tail


def _ffn_up(hb, w_up, conv_w, conv_b, *, d_ff, tm, tn):
    s, d = hb.shape
    nj = d_ff // tn
    return pl.pallas_call(
        functools.partial(_ffn_up_kernel, tm=tm, tn=tn, rc=min(tm, 512)),
        grid=(nj, s // tm),
        in_specs=[pl.BlockSpec((tm, d), lambda j, i: (i, 0)),
                  pl.BlockSpec((1, d, tn), lambda j, i: (0, 0, j)),
                  pl.BlockSpec((1, d, tn), lambda j, i: (0, 0, nj + j)),
                  pl.BlockSpec((1, CONV_W, tn), lambda j, i: (0, 0, j)),
                  pl.BlockSpec((1, CONV_W, tn), lambda j, i: (0, 0, nj + j)),
                  pl.BlockSpec((1, tn), lambda j, i: (0, j)),
                  pl.BlockSpec((1, tn), lambda j, i: (0, nj + j))],
        out_specs=pl.BlockSpec((tm, tn), lambda j, i: (i, j)),
        out_shape=jax.ShapeDtypeStruct((s, d_ff), BF16),
        scratch_shapes=[pltpu.VMEM((d, 2 * tn), BF16), pltpu.VMEM((8, 2 * tn), F32)],
        compiler_params=_params("parallel", "arbitrary"),
        name="ffn_up",
    )(hb, w_up, w_up, conv_w, conv_w, conv_b, conv_b)


def _layout(d_model):
    a_width = A_HEADS * A_HEAD_DIM
    g_width = d_model - a_width
    g_kwidth = g_width // 2
    names = ("qa", "ka", "va", "qi", "ki", "wi", "qg", "kg", "vg", "glr", "gg")
    sizes = (a_width, a_width, a_width, IDX_HEADS * IDX_DIM, IDX_DIM, IDX_HEADS,
             g_kwidth, g_kwidth, g_width, G_LOWRANK, g_width)
    offs = np.concatenate([[0], np.cumsum(sizes)])
    src = {n: (int(offs[i]), int(offs[i + 1])) for i, n in enumerate(names)}
    order = ("qi", "qa", "ka", "va", "qg", "kg", "vg", "gg", "ki")
    lay, blk = {}, 0
    for n in order:
        width = src[n][1] - src[n][0]
        assert width % LANES == 0
        lay[n] = (blk, width // LANES)
        blk += width // LANES
    return src, order, lay, blk


def kernel(x, w_in, w_gk2, b_gk, attn_out_g, gla_norm_g, w_o, ln1_g, ln1_b, w_up, conv_w, conv_b,
           w_down, ln2_g, ln2_b):
    assert x.shape[0] == 1 and w_in.shape[0] == DEPTH == 1
    _, s, d = x.shape
    x2 = x[0]
    src, order, lay, nb_main = _layout(d)
    g_width = d - A_HEADS * A_HEAD_DIM
    dv = g_width // G_HEADS
    dk = dv // 2
    d_ff = w_down.shape[1]
    topk = min(TOPK_MAX, s // 4)

    w_proj = _relayout_w_in(w_in, src, order)
    n_small = IDX_HEADS + G_LOWRANK
    w2 = jnp.zeros((LANES, G_HEADS * dk), F32).at[IDX_HEADS:n_small].set(w_gk2[0]).astype(BF16)
    bgk = b_gk[0].reshape(1, G_HEADS * dk)
    wd = w_down[0].astype(BF16)
    n_main = (nb_main + 1) * LANES
    q_lo, q_hi = lay["qa"][0] * LANES, (lay["qa"][0] + lay["qa"][1]) * LANES
    col = np.ones((1, n_main), np.float32)
    col[:, q_lo:q_hi] = A_HEAD_DIM ** -0.5 * LOG2E
    col_scale = jnp.asarray(col)

    tm = _tile(s, 1024)
    main, small = _proj_blocks(x2, w_proj, col_scale, tm=_tile(s, 512), tn=_tile(n_main, 1280))
    bias = _indexer(main, small, lay, s=s, tq=256, tk=256, topk=topk)
    ta = _tile(s, 512)
    o_a = _attention(main, bias, attn_out_g[0], lay, s=s, tq=ta, tk=ta)
    o_g = _gla(main, small, w2, bgk, gla_norm_g[0], lay, s=s, rows=_tile(s, 512), chunk=64, dk=dk, dv=dv)
    y1 = _w_o(o_a, o_g, w_o, x2, tm=tm, tn=_tile(d, 512), alpha=DN_ALPHA)
    h, hb = _layer_norm(y1, ln1_g[0], ln1_b[0], tr=_tile(s, 256, 8), also_bf16=True)

    act = _ffn_up(hb, w_up, conv_w, conv_b[0].reshape(1, 2 * d_ff), d_ff=d_ff, tm=tm, tn=_tile(d_ff, 256))
    y2 = _matmul(act, wd, tm=_tile(s, 512), tn=_tile(d, 512), tk=d_ff, out_dtype=F32, residual=h,
                 alpha=DN_ALPHA, name="w_down")
    (out,) = _layer_norm(y2, ln2_g[0], ln2_b[0], tr=_tile(s, 256, 8), also_bf16=False)
    return out[None]
```

```python
import functools

import numpy as np
import jax
import jax.numpy as jnp
from jax import lax
from jax.experimental import pallas as pl
from jax.experimental.pallas import tpu as pltpu

A_HEADS = 16
A_HEAD_DIM = 128
IDX_HEADS = 32
IDX_DIM = 128
TOPK_MAX = 256
G_HEADS = 4
G_LOWRANK = 16
G_TAU = 16.0
CONV_W = 3
LN_EPS = 1e-5
RMS_EPS = 1e-6
DEPTH = 1
DN_ALPHA = (2 * DEPTH) ** 0.25

LANES = 128
VMEM_LIMIT = 56 * 1024 * 1024

BF16 = jnp.bfloat16
F32 = jnp.float32
NEG_BIG = -1e30
LOG2E = 1.4426950408889634
INT_MIN = -2 ** 31
COUNT_UNROLL = 4


def _params(*sem):
    return pltpu.CompilerParams(dimension_semantics=sem, vmem_limit_bytes=VMEM_LIMIT)


def _tile(n, max_tile, quantum=LANES):
    best = None
    for t in range(quantum, min(n, max_tile) + 1, quantum):
        if n % t == 0:
            best = t
    assert best is not None, (n, max_tile, quantum)
    return best


def _dot(a, b):
    return jnp.dot(a, b, preferred_element_type=F32)


def _dot_nt(a, b):
    return lax.dot_general(a, b, (((1,), (1,)), ((), ())), preferred_element_type=F32)


def _dot_tn(a, b):
    return lax.dot_general(a, b, (((0,), (0,)), ((), ())), preferred_element_type=F32)


def _relayout_kernel(off_ref, kind_ref, a_ref, b_ref, o_ref, *, n_wi, n_small):
    del off_ref
    kind = kind_ref[pl.program_id(0)]

    @pl.when(kind == 0)
    def _():
        o_ref[...] = a_ref[0].astype(BF16)

    @pl.when(kind == 1)
    def _():
        o_ref[...] = jnp.zeros(o_ref.shape, BF16)

    @pl.when(kind == 2)
    def _():
        r = lax.broadcasted_iota(jnp.int32, o_ref.shape, 0)
        o_ref[...] = jnp.where(r < n_wi, a_ref[0], jnp.where(r < n_small, b_ref[0], 0.0)).astype(BF16)


def _relayout_w_in(w_t, src, order):
    _, n_in, d = w_t.shape
    starts = [src[n][0] + t * LANES for n in order for t in range((src[n][1] - src[n][0]) // LANES)]
    wi0, glr0 = src["wi"][0], src["glr"][0]
    n_wi = src["wi"][1] - wi0
    n_small = n_wi + src["glr"][1] - glr0
    glr_win = glr0 - n_wi
    assert n_small <= LANES and glr_win >= 0 and wi0 + LANES <= n_in and glr_win + LANES <= n_in
    assert all(o % 8 == 0 for o in starts + [wi0, glr_win])
    row_off8 = [o // 8 for o in starts + [0, wi0]]
    kind = [0] * len(starts) + [1, 2]
    nb = len(kind)
    tables = [jnp.asarray(np.asarray(t, np.int32)) for t in (row_off8, kind)]
    return pl.pallas_call(
        functools.partial(_relayout_kernel, n_wi=n_wi, n_small=n_small),
        grid_spec=pltpu.PrefetchScalarGridSpec(
            num_scalar_prefetch=2,
            grid=(nb,),
            in_specs=[pl.BlockSpec((pl.Element(1), pl.Element(LANES), pl.Element(d)),
                                   lambda b, off, kd: (0, off[b] * 8, 0)),
                      pl.BlockSpec((pl.Element(1), pl.Element(LANES), pl.Element(d)),
                                   lambda b, off, kd: (0, glr_win, 0))],
            out_specs=pl.BlockSpec((LANES, d), lambda b, off, kd: (b, 0))),
        out_shape=jax.ShapeDtypeStruct((nb * LANES, d), BF16),
        compiler_params=_params("arbitrary"),
        name="relayout_w_in",
    )(*tables, w_t, w_t)


def _proj_kernel(x_ref, w_ref, ws_ref, sc_ref, o_ref, os_ref, xb_scr):
    @pl.when(pl.program_id(1) == 0)
    def _():
        xb_scr[...] = x_ref[...].astype(BF16)
        os_ref[...] = _dot_nt(xb_scr[...], ws_ref[...])

    res = _dot_nt(xb_scr[...], w_ref[...]) * sc_ref[...]
    for c in range(o_ref.shape[0]):
        o_ref[c] = res[:, c * LANES:(c + 1) * LANES].astype(o_ref.dtype)


def _proj_blocks(x, wt, col_scale, *, tm, tn):
    s, k = x.shape
    n = wt.shape[0] - LANES
    wb = wt
    return pl.pallas_call(
        _proj_kernel,
        grid=(s // tm, n // tn),
        in_specs=[pl.BlockSpec((tm, k), lambda i, j: (i, 0)),
                  pl.BlockSpec((tn, k), lambda i, j: (j, 0)),
                  pl.BlockSpec((LANES, k), lambda i, j: (n // LANES, 0)),
                  pl.BlockSpec((1, tn), lambda i, j: (0, j))],
        out_specs=[pl.BlockSpec((tn // LANES, tm, LANES), lambda i, j: (j, i, 0)),
                   pl.BlockSpec((tm, LANES), lambda i, j: (i, 0))],
        out_shape=[jax.ShapeDtypeStruct((n // LANES, s, LANES), BF16),
                   jax.ShapeDtypeStruct((s, LANES), F32)],
        scratch_shapes=[pltpu.VMEM((tm, k), BF16)],
        compiler_params=_params("parallel", "arbitrary"),
        name="proj",
    )(x, wb, wb, col_scale)


def _mm_kernel(a_ref, b_ref, *rest, alpha, has_res, nk):
    if has_res:
        r_ref, o_ref = rest[0], rest[1]
        rest = rest[2:]
    else:
        r_ref, o_ref = None, rest[0]
        rest = rest[1:]

    def finish(acc):
        if has_res:
            acc = acc + alpha * r_ref[...]
        o_ref[...] = acc.astype(o_ref.dtype)

    if nk == 1:
        finish(_dot(a_ref[...], b_ref[...]))
        return
    acc_ref = rest[0]
    kk = pl.program_id(2)

    @pl.when(kk == 0)
    def _():
        acc_ref[...] = _dot(a_ref[...], b_ref[...])

    @pl.when(kk > 0)
    def _():
        acc_ref[...] += _dot(a_ref[...], b_ref[...])

    @pl.when(kk == nk - 1)
    def _():
        finish(acc_ref[...])


def _matmul(a, b, *, tm, tn, tk, out_dtype, residual=None, alpha=1.0, name="matmul"):
    m, k = a.shape
    n = b.shape[1]
    nk = k // tk
    in_specs = [pl.BlockSpec((tm, tk), lambda i, j, kk: (i, kk)),
                pl.BlockSpec((tk, tn), lambda i, j, kk: (kk, j))]
    args = [a, b]
    if residual is not None:
        in_specs.append(pl.BlockSpec((tm, tn), lambda i, j, kk: (i, j)))
        args.append(residual)
    scratch = [pltpu.VMEM((tm, tn), F32)] if nk > 1 else []
    return pl.pallas_call(
        functools.partial(_mm_kernel, alpha=alpha, has_res=residual is not None, nk=nk),
        grid=(m // tm, n // tn, nk),
        in_specs=in_specs,
        out_specs=pl.BlockSpec((tm, tn), lambda i, j, kk: (i, j)),
        out_shape=jax.ShapeDtypeStruct((m, n), out_dtype),
        scratch_shapes=scratch,
        compiler_params=_params("parallel", "parallel", "arbitrary"),
        name=name,
    )(*args)


def _wo_kernel(oa_ref, og_ref, wt_ref, wb_ref, x_ref, o_ref, w_scr, *, ka, alpha):
    @pl.when(pl.program_id(1) == 0)
    def _():
        w_scr[:ka, :] = wt_ref[0].astype(BF16)
        w_scr[ka:, :] = wb_ref[0].astype(BF16)

    acc = _dot(oa_ref[...], w_scr[:ka, :]) + _dot(og_ref[...], w_scr[ka:, :])
    o_ref[...] = acc + alpha * x_ref[...]


def _w_o(o_a, o_g, w_o, x, *, tm, tn, alpha):
    s, ka = o_a.shape
    kg = o_g.shape[1]
    d = w_o.shape[2]
    assert ka % kg == 0
    return pl.pallas_call(
        functools.partial(_wo_kernel, ka=ka, alpha=alpha),
        grid=(d // tn, s // tm),
        in_specs=[pl.BlockSpec((tm, ka), lambda j, i: (i, 0)),
                  pl.BlockSpec((tm, kg), lambda j, i: (i, 0)),
                  pl.BlockSpec((1, ka, tn), lambda j, i: (0, 0, j)),
                  pl.BlockSpec((1, kg, tn), lambda j, i: (0, ka // kg, j)),
                  pl.BlockSpec((tm, tn), lambda j, i: (i, j))],
        out_specs=pl.BlockSpec((tm, tn), lambda j, i: (i, j)),
        out_shape=jax.ShapeDtypeStruct((s, d), F32),
        scratch_shapes=[pltpu.VMEM((ka + kg, tn), BF16)],
        compiler_params=_params("parallel", "arbitrary"),
        name="w_o",
    )(o_a, o_g, w_o, w_o, x)


def _ln_kernel(y_ref, g_ref, b_ref, *o_refs):
    y = y_ref[...]
    mu = jnp.mean(y, axis=1, keepdims=True)
    yc = y - mu
    var = jnp.mean(yc * yc, axis=1, keepdims=True)
    out = yc * lax.rsqrt(var + LN_EPS) * g_ref[...] + b_ref[...]
    for o_ref in o_refs:
        o_ref[...] = out.astype(o_ref.dtype)


def _layer_norm(y, g, b, *, tr, also_bf16):
    s, d = y.shape
    out_shape = [jax.ShapeDtypeStruct((s, d), F32)]
    out_specs = [pl.BlockSpec((tr, d), lambda i: (i, 0))]
    if also_bf16:
        out_shape.append(jax.ShapeDtypeStruct((s, d), BF16))
        out_specs.append(pl.BlockSpec((tr, d), lambda i: (i, 0)))
    return pl.pallas_call(
        _ln_kernel,
        grid=(s // tr,),
        in_specs=[pl.BlockSpec((tr, d), lambda i: (i, 0)),
                  pl.BlockSpec((1, d), lambda i: (0, 0)),
                  pl.BlockSpec((1, d), lambda i: (0, 0))],
        out_specs=out_specs,
        out_shape=out_shape,
        compiler_params=_params("parallel"),
        name="layer_norm",
    )(y, g.reshape(1, d), b.reshape(1, d))


def _idx_kernel(qi_ref, ki_ref, sm_ref, bias_ref, hi_scr, lo_scr, wb_scr, *, tq, tk, nkt, hi, group,
                topk, wscale):
    qb = pl.program_id(0)
    nk = ((qb + 1) * tq + tk - 1) // tk
    reps = tk // LANES
    i16 = jnp.int16
    low16 = -2 ** 15

    for h in range(hi):
        wb_scr[h] = jnp.broadcast_to(sm_ref[:, h:h + 1] * wscale, (tq, LANES))

    row = qb * tq + lax.broadcasted_iota(jnp.int32, (group, tk), 0)
    col = lax.broadcasted_iota(jnp.int32, (group, tk), 1)

    def lanes(a):
        return jnp.concatenate([a] * reps, axis=1) if reps > 1 else a

    def fold(a):
        out = a[:, :LANES]
        for r in range(1, reps):
            out = out + a[:, r * LANES:(r + 1) * LANES]
        return out

    def score_tile(kb, carry):
        kt = ki_ref[0, pl.ds(pl.multiple_of(kb * tk, tk), tk), :]
        for r in range(tq // group):
            rows = slice(r * group, (r + 1) * group)
            qs = qi_ref[:, rows, :].reshape(hi * group, IDX_DIM)
            d = _dot_nt(qs, kt)
            acc = jnp.zeros((group, tk), F32)
            for h in range(hi):
                acc = acc + jnp.maximum(d[h * group:(h + 1) * group], 0.0) * lanes(wb_scr[h, rows, :])
            bits = pltpu.bitcast(acc + 0.0, jnp.int32)
            key = jnp.where(kb * tk + col <= row + r * group,
                            bits ^ ((bits >> 31) & jnp.int32(0x7FFFFFFF)), INT_MIN)
            hi_scr[kb, rows, :] = (key >> 16).astype(i16)
            lo_scr[kb, rows, :] = ((key & 0xFFFF) + low16).astype(i16)
        return carry

    lax.fori_loop(0, nk, score_tile, 0)

    def count_ge(ref, cand):
        cand_t = lanes(cand.astype(i16))

        def step(kb, c):
            return c + fold(jnp.where(ref[kb] >= cand_t, i16(1), i16(0)))

        def step_many(q, c):
            for u in range(COUNT_UNROLL):
                c = step(q * COUNT_UNROLL + u, c)
            return c

        n_many = nk // COUNT_UNROLL
        c = lax.fori_loop(0, n_many, step_many, jnp.zeros((tq, LANES), i16))
        c = lax.fori_loop(n_many * COUNT_UNROLL, nk, step, c)
        return jnp.sum(c.astype(F32), axis=1, keepdims=True)

    kf = float(topk)

    def bit_search(ref, base_cnt, cnt_start, stop_when_exact):
        def body(state):
            it, v, cnt_v = state
            cand = v + jnp.left_shift(jnp.int32(1), 15 - it)
            cnt = base_cnt + count_ge(ref, cand)
            ok = cnt >= kf
            return it + 1, jnp.where(ok, cand, v), jnp.where(ok, cnt, cnt_v)

        def unsettled(state):
            go = state[0] < 16
            if stop_when_exact:
                go = go & (jnp.max(jnp.where(state[2] == kf, 0.0, 1.0)) > 0.0)
            return go

        v0 = jnp.full((tq, LANES), low16, jnp.int32)
        _, v, cnt_v = lax.while_loop(unsettled, body, (jnp.int32(0), v0, cnt_start))
        return v, cnt_v

    n_all = jnp.zeros((tq, 1), F32) + (nk * tk).astype(F32)
    v_hi, n_from_hi = bit_search(hi_scr, 0.0, n_all, False)
    n_above = count_ge(hi_scr, v_hi + 1)
    v_hi_t = lanes(v_hi.astype(i16))

    def keep_equal(kb, carry):
        lo_scr[kb] = jnp.where(hi_scr[kb] == v_hi_t, lo_scr[kb], i16(low16))
        return carry

    lax.fori_loop(0, nk, keep_equal, 0)
    v_lo, _ = bit_search(lo_scr, n_above, n_from_hi, True)
    v_lo = jnp.where(v_hi == low16, jnp.maximum(v_lo, low16 + 1), v_lo)
    v_lo_t = lanes(v_lo.astype(i16))
    zero = jnp.zeros((tq, tk), bias_ref.dtype)
    neg = jnp.full((tq, tk), NEG_BIG, bias_ref.dtype)

    def emit(kb, carry):
        h16 = hi_scr[kb]
        sel = (h16 > v_hi_t) | ((h16 == v_hi_t) & (lo_scr[kb] >= v_lo_t))
        bias_ref[0, kb] = jnp.where(sel, zero, neg)
        return carry

    lax.fori_loop(0, nk, emit, 0)

    def fill(kb, carry):
        bias_ref[0, kb] = neg
        return carry

    lax.fori_loop(nk, nkt, fill, 0)


def _indexer(main, small, lay, *, s, tq, tk, topk):
    nkt = s // tk
    hi = IDX_HEADS
    group = min(tq, 64)
    qi0, ki0 = lay["qi"][0], lay["ki"][0]
    assert qi0 % hi == 0
    return pl.pallas_call(
        functools.partial(_idx_kernel, tq=tq, tk=tk, nkt=nkt, hi=hi, group=group, topk=topk,
                          wscale=IDX_HEADS ** -0.5 * IDX_DIM ** -0.5),
        grid=(s // tq,),
        in_specs=[pl.BlockSpec((hi, tq, LANES), lambda i: (qi0 // hi, i, 0)),
                  pl.BlockSpec((1, s, LANES), lambda i: (ki0, 0, 0)),
                  pl.BlockSpec((tq, LANES), lambda i: (i, 0))],
        out_specs=pl.BlockSpec((1, nkt, tq, tk), lambda i: (i, 0, 0, 0)),
        out_shape=jax.ShapeDtypeStruct((s // tq, nkt, tq, tk), BF16),
        scratch_shapes=[pltpu.VMEM((nkt, tq, tk), jnp.int16),
                        pltpu.VMEM((nkt, tq, tk), jnp.int16),
                        pltpu.VMEM((hi, tq, LANES), F32)],
        compiler_params=_params("parallel"),
        name="indexer",
    )(main, main, small)


def _attn_kernel(q_ref, k_ref, v_ref, b_ref, g_ref, o_ref, bias_scr, m_scr, acc_scr, *pipe, nh, tq, tk):
    qi = pl.program_id(0)
    kb = pl.program_id(1)

    @pl.when(kb == 0)
    def _():
        m_scr[...] = jnp.full(m_scr.shape, NEG_BIG, F32)
        acc_scr[...] = jnp.zeros(acc_scr.shape, F32)

    @pl.when(kb * tk <= qi * tq + (tq - 1))
    def _():
        na, nb, sq, sk = b_ref.shape
        for a in range(na):
            for b in range(nb):
                bias_scr[a * sq:(a + 1) * sq, b * sk:(b + 1) * sk] = b_ref[a, b].astype(F32)

        ones = jnp.ones((tk, LANES), BF16)

        def qk(h, s_ref):
            s_ref[...] = _dot_nt(q_ref[h], k_ref[h])

        def soft(h, s_ref, p_ref, al_ref):
            s = s_ref[...] + bias_scr[...]
            m_prev = m_scr[h]
            m_new = jnp.maximum(m_prev, jnp.max(s, axis=1, keepdims=True))
            m_scr[h] = m_new
            p_ref[...] = jnp.exp2(s - m_new[:, :1]).astype(BF16)
            al_ref[...] = jnp.exp2(m_prev - m_new)

        def pv(h, p_ref, al_ref):
            al = al_ref[...]
            v_ext = jnp.concatenate([v_ref[h], ones], axis=1)
            acc_scr[h] = acc_scr[h] * jnp.concatenate([al, al], axis=1) + _dot(p_ref[...], v_ext)

        s_a, s_b, p_a, p_b, al_a, al_b = pipe
        qk(0, s_a)
        qk(1, s_b)
        soft(0, s_a, p_a, al_a)

        def pair(j, carry):
            qk(2 * j, s_a)
            soft(2 * j - 1, s_b, p_b, al_b)
            pv(2 * j - 2, p_a, al_a)
            qk(2 * j + 1, s_b)
            soft(2 * j, s_a, p_a, al_a)
            pv(2 * j - 1, p_b, al_b)
            return carry

        lax.fori_loop(1, nh // 2, pair, 0)
        soft(nh - 1, s_b, p_b, al_b)
        pv(nh - 2, p_a, al_a)
        pv(nh - 1, p_b, al_b)

    @pl.when(kb == pl.num_programs(1) - 1)
    def _():
        ss = jnp.zeros((tq, LANES), F32)
        for h in range(nh):
            o_h = acc_scr[h, :, :LANES] / acc_scr[h, :, LANES:]
            acc_scr[h, :, :LANES] = o_h
            ss = ss + o_h * o_h
        ms = jnp.sum(ss, axis=1, keepdims=True) * (1.0 / (nh * LANES))
        r = lax.rsqrt(ms + RMS_EPS)
        for h in range(nh):
            sl = slice(h * LANES, (h + 1) * LANES)
            o_ref[:, sl] = (acc_scr[h, :, :LANES] * r * g_ref[:, sl]).astype(o_ref.dtype)


def _attention(main, bias, g, lay, *, s, tq, tk):
    nh = A_HEADS
    nqb, nkt, sq, sk = bias.shape
    q0, k0, v0 = lay["qa"][0], lay["ka"][0], lay["va"][0]
    assert q0 % nh == 0 and k0 % nh == 0 and v0 % nh == 0

    def last_kb(i):
        return (i * tq + tq - 1) // tk

    return pl.pallas_call(
        functools.partial(_attn_kernel, nh=nh, tq=tq, tk=tk),
        grid=(s // tq, s // tk),
        in_specs=[pl.BlockSpec((nh, tq, LANES), lambda i, j: (q0 // nh, i, 0)),
                  pl.BlockSpec((nh, tk, LANES), lambda i, j: (k0 // nh, jnp.minimum(j, last_kb(i)), 0)),
                  pl.BlockSpec((nh, tk, LANES), lambda i, j: (v0 // nh, jnp.minimum(j, last_kb(i)), 0)),
                  pl.BlockSpec((tq // sq, tk // sk, sq, sk),
                               lambda i, j: (i, jnp.minimum(j, last_kb(i)), 0, 0)),
                  pl.BlockSpec((1, nh * LANES), lambda i, j: (0, 0))],
        out_specs=pl.BlockSpec((tq, nh * LANES), lambda i, j: (i, 0)),
        out_shape=jax.ShapeDtypeStruct((s, nh * LANES), BF16),
        scratch_shapes=[pltpu.VMEM((tq, tk), F32),
                        pltpu.VMEM((nh, tq, LANES), F32),
                        pltpu.VMEM((nh, tq, 2 * LANES), F32),
                        pltpu.VMEM((tq, tk), F32), pltpu.VMEM((tq, tk), F32),
                        pltpu.VMEM((tq, tk), BF16), pltpu.VMEM((tq, tk), BF16),
                        pltpu.VMEM((tq, LANES), F32), pltpu.VMEM((tq, LANES), F32)],
        compiler_params=_params("parallel", "arbitrary"),
        name="attention",
    )(main, main, main, bias, g.reshape(1, nh * LANES))


def _gla_kernel(q_ref, k_ref, v_ref, gg_ref, sm_ref, w2_ref, bgk_ref, gn_ref, o_ref, st_scr, *,
                rows, chunk, nheads, dk, dv):
    @pl.when(pl.program_id(0) == 0)
    def _():
        st_scr[...] = jnp.zeros(st_scr.shape, F32)

    r_i = lax.broadcasted_iota(jnp.int32, (chunk, chunk), 0)
    c_i = lax.broadcasted_iota(jnp.int32, (chunk, chunk), 1)
    lower = r_i >= c_i
    tri = jnp.where(lower, 1.0, 0.0).astype(BF16)
    mid = chunk // 2
    nbk, nbv = dk // LANES, dv // LANES

    def wide(ref, first, n, r0):
        return jnp.concatenate([ref[first + j, pl.ds(r0, chunk), :] for j in range(n)], axis=1)

    def step(c, carry):
        r0 = pl.multiple_of(c * chunk, chunk)
        z_all = _dot(sm_ref[pl.ds(r0, chunk), :].astype(BF16), w2_ref[...]) + bgk_ref[...]
        for h in range(nheads):
            q = wide(q_ref, h * nbk, nbk, r0).astype(F32) * (dk ** -0.5)
            k = wide(k_ref, h * nbk, nbk, r0).astype(F32)
            v = wide(v_ref, h * nbv, nbv, r0)
            z = z_all[:, h * dk:(h + 1) * dk]
            g = (jnp.minimum(z, 0.0) - jnp.log1p(jnp.exp(-jnp.abs(z)))) * (1.0 / G_TAU)
            g_hi = g.astype(BF16)
            rem = g - g_hi.astype(F32)
            g_mid = rem.astype(BF16)
            g_lo = (rem - g_mid.astype(F32)).astype(BF16)
            b = _dot(tri, g_hi) + _dot(tri, g_mid) + _dot(tri, g_lo)
            b_mid = b[mid:mid + 1, :]
            b_last = b[chunk - 1:chunk, :]
            qe = (q * jnp.exp(b - b_mid)).astype(BF16)
            ke = (k * jnp.exp(b_mid - b)).astype(BF16)
            att = jnp.where(lower, _dot_nt(qe, ke), 0.0)
            st = st_scr[h]
            o = _dot(att.astype(BF16), v) + _dot_nt((q * jnp.exp(b)).astype(BF16), st.astype(BF16))
            kd = (k * jnp.exp(b_last - b)).astype(BF16)
            st_scr[h] = st * jnp.exp(b_last) + _dot_tn(v, kd)
            ms = jnp.mean(o * o, axis=1, keepdims=True)
            gate = wide(gg_ref, h * nbv, nbv, r0).astype(F32)
            out = (o * lax.rsqrt(ms + RMS_EPS)) * gn_ref[...] * (gate * jax.nn.sigmoid(gate))
            o_ref[pl.ds(r0, chunk), h * dv:(h + 1) * dv] = out.astype(o_ref.dtype)
        return carry

    lax.fori_loop(0, rows // chunk, step, 0)


def _gla(main, small, w2, bgk, gn, lay, *, s, rows, chunk, dk, dv):
    nh = G_HEADS
    nqk, nv = nh * dk // LANES, nh * dv // LANES
    q0, k0, v0, g0 = lay["qg"][0], lay["kg"][0], lay["vg"][0], lay["gg"][0]
    assert q0 % nqk == 0 and k0 % nqk == 0 and v0 % nv == 0 and g0 % nv == 0
    return pl.pallas_call(
        functools.partial(_gla_kernel, rows=rows, chunk=chunk, nheads=nh, dk=dk, dv=dv),
        grid=(s // rows,),
        in_specs=[pl.BlockSpec((nqk, rows, LANES), lambda r: (q0 // nqk, r, 0)),
                  pl.BlockSpec((nqk, rows, LANES), lambda r: (k0 // nqk, r, 0)),
                  pl.BlockSpec((nv, rows, LANES), lambda r: (v0 // nv, r, 0)),
                  pl.BlockSpec((nv, rows, LANES), lambda r: (g0 // nv, r, 0)),
                  pl.BlockSpec((rows, LANES), lambda r: (r, 0)),
                  pl.BlockSpec((LANES, nh * dk), lambda r: (0, 0)),
                  pl.BlockSpec((1, nh * dk), lambda r: (0, 0)),
                  pl.BlockSpec((1, dv), lambda r: (0, 0))],
        out_specs=pl.BlockSpec((rows, nh * dv), lambda r: (r, 0)),
        out_shape=jax.ShapeDtypeStruct((s, nh * dv), BF16),
        scratch_shapes=[pltpu.VMEM((nh, dv, dk), F32)],
        compiler_params=_params("arbitrary"),
        name="gla",
    )(main, main, main, main, small, w2, bgk, gn.reshape(1, dv))


def _ffn_up_kernel(h_ref, wg_ref, wv_ref, cwg_ref, cwv_ref, cbg_ref, cbv_ref, o_ref, w_scr, tail_scr,
                   *, tm, tn, rc):
    @pl.when(pl.program_id(1) == 0)
    def _():
        w_scr[:, :tn] = wg_ref[0].astype(BF16)
        w_scr[:, tn:] = wv_ref[0].astype(BF16)
        tail_scr[...] = jnp.zeros(tail_scr.shape, F32)

    cw = jnp.concatenate([cwg_ref[0], cwv_ref[0]], axis=1)
    cb = jnp.concatenate([cbg_ref[...], cbv_ref[...]], axis=1)
    w0, w1, w2 = cw[0:1, :], cw[1:2, :], cw[2:3, :]

    def taps(a):
        return w2 * a + w1 * pltpu.roll(a, 1, 0) + w0 * pltpu.roll(a, 2, 0) + cb

    tail = tail_scr[...]
    for r in range(tm // rc):
        rows = slice(r * rc, (r + 1) * rc)
        hw = _dot(h_ref[rows, :], w_scr[...])
        top = taps(jnp.concatenate([tail, hw[0:8]], axis=0))[8:16]
        u = jnp.concatenate([top, taps(hw)[8:]], axis=0)
        gate, val = u[:, :tn], u[:, tn:]
        o_ref[rows, :] = (gate * jax.nn.sigmoid(gate) * val).astype(o_ref.dtype)
        tail = hw[rc - 8:rc, :]
    tail_scr[...] = tail


def _ffn_up(hb, w_up, conv_w, conv_b, *, d_ff, tm, tn):
    s, d = hb.shape
    nj = d_ff // tn
    return pl.pallas_call(
        functools.partial(_ffn_up_kernel, tm=tm, tn=tn, rc=min(tm, 512)),
        grid=(nj, s // tm),
        in_specs=[pl.BlockSpec((tm, d), lambda j, i: (i, 0)),
                  pl.BlockSpec((1, d, tn), lambda j, i: (0, 0, j)),
                  pl.BlockSpec((1, d, tn), lambda j, i: (0, 0, nj + j)),
                  pl.BlockSpec((1, CONV_W, tn), lambda j, i: (0, 0, j)),
                  pl.BlockSpec((1, CONV_W, tn), lambda j, i: (0, 0, nj + j)),
                  pl.BlockSpec((1, tn), lambda j, i: (0, j)),
                  pl.BlockSpec((1, tn), lambda j, i: (0, nj + j))],
        out_specs=pl.BlockSpec((tm, tn), lambda j, i: (i, j)),
        out_shape=jax.ShapeDtypeStruct((s, d_ff), BF16),
        scratch_shapes=[pltpu.VMEM((d, 2 * tn), BF16), pltpu.VMEM((8, 2 * tn), F32)],
        compiler_params=_params("parallel", "arbitrary"),
        name="ffn_up",
    )(hb, w_up, w_up, conv_w, conv_w, conv_b, conv_b)


def _layout(d_model):
    a_width = A_HEADS * A_HEAD_DIM
    g_width = d_model - a_width
    g_kwidth = g_width // 2
    names = ("qa", "ka", "va", "qi", "ki", "wi", "qg", "kg", "vg", "glr", "gg")
    sizes = (a_width, a_width, a_width, IDX_HEADS * IDX_DIM, IDX_DIM, IDX_HEADS,
             g_kwidth, g_kwidth, g_width, G_LOWRANK, g_width)
    offs = np.concatenate([[0], np.cumsum(sizes)])
    src = {n: (int(offs[i]), int(offs[i + 1])) for i, n in enumerate(names)}
    order = ("qi", "qa", "ka", "va", "qg", "kg", "vg", "gg", "ki")
    lay, blk = {}, 0
    for n in order:
        width = src[n][1] - src[n][0]
        assert width % LANES == 0
        lay[n] = (blk, width // LANES)
        blk += width // LANES
    return src, order, lay, blk


def kernel(x, w_in, w_gk2, b_gk, attn_out_g, gla_norm_g, w_o, ln1_g, ln1_b, w_up, conv_w, conv_b,
           w_down, ln2_g, ln2_b):
    assert x.shape[0] == 1 and w_in.shape[0] == DEPTH == 1
    _, s, d = x.shape
    x2 = x[0]
    src, order, lay, nb_main = _layout(d)
    g_width = d - A_HEADS * A_HEAD_DIM
    dv = g_width // G_HEADS
    dk = dv // 2
    d_ff = w_down.shape[1]
    topk = min(TOPK_MAX, s // 4)

    w_proj = _relayout_w_in(jnp.swapaxes(w_in, 1, 2), src, order)
    n_small = IDX_HEADS + G_LOWRANK
    w2 = jnp.zeros((LANES, G_HEADS * dk), F32).at[IDX_HEADS:n_small].set(w_gk2[0]).astype(BF16)
    bgk = b_gk[0].reshape(1, G_HEADS * dk)
    wd = w_down[0].astype(BF16)
    n_main = (nb_main + 1) * LANES
    q_lo, q_hi = lay["qa"][0] * LANES, (lay["qa"][0] + lay["qa"][1]) * LANES
    col = np.ones((1, n_main), np.float32)
    col[:, q_lo:q_hi] = A_HEAD_DIM ** -0.5 * LOG2E
    col_scale = jnp.asarray(col)

    tm = _tile(s, 1024)
    main, small = _proj_blocks(x2, w_proj, col_scale, tm=_tile(s, 512), tn=_tile(n_main, 1280))
    bias = _indexer(main, small, lay, s=s, tq=256, tk=256, topk=topk)
    ta = _tile(s, 512)
    o_a = _attention(main, bias, attn_out_g[0], lay, s=s, tq=ta, tk=ta)
    o_g = _gla(main, small, w2, bgk, gla_norm_g[0], lay, s=s, rows=_tile(s, 512), chunk=64, dk=dk, dv=dv)
    y1 = _w_o(o_a, o_g, w_o, x2, tm=tm, tn=_tile(d, 512), alpha=DN_ALPHA)
    h, hb = _layer_norm(y1, ln1_g[0], ln1_b[0], tr=_tile(s, 256, 8), also_bf16=True)

    act = _ffn_up(hb, w_up, conv_w, conv_b[0].reshape(1, 2 * d_ff), d_ff=d_ff, tm=tm, tn=_tile(d_ff, 256))
    y2 = _matmul(act, wd, tm=_tile(s, 512), tn=_tile(d, 512), tk=d_ff, out_dtype=F32, residual=h,
                 alpha=DN_ALPHA, name="w_down")
    (out,) = _layer_norm(y2, ln2_g[0], ln2_b[0], tr=_tile(s, 256, 8), also_bf16=False)
    return out[None]
```

```python
import functools

import numpy as np
import jax
import jax.numpy as jnp
from jax import lax
from jax.experimental import pallas as pl
from jax.experimental.pallas import tpu as pltpu

A_HEADS = 16
A_HEAD_DIM = 128
IDX_HEADS = 32
IDX_DIM = 128
TOPK_MAX = 256
G_HEADS = 4
G_LOWRANK = 16
G_TAU = 16.0
CONV_W = 3
LN_EPS = 1e-5
RMS_EPS = 1e-6
DEPTH = 1
DN_ALPHA = (2 * DEPTH) ** 0.25

LANES = 128
VMEM_LIMIT = 56 * 1024 * 1024

BF16 = jnp.bfloat16
F32 = jnp.float32
NEG_BIG = -1e30
LOG2E = 1.4426950408889634
INT_MIN = -2 ** 31
COUNT_UNROLL = 4


def _params(*sem):
    return pltpu.CompilerParams(dimension_semantics=sem, vmem_limit_bytes=VMEM_LIMIT)


def _tile(n, max_tile, quantum=LANES):
    best = None
    for t in range(quantum, min(n, max_tile) + 1, quantum):
        if n % t == 0:
            best = t
    assert best is not None, (n, max_tile, quantum)
    return best


def _dot(a, b):
    return jnp.dot(a, b, preferred_element_type=F32)


def _dot_nt(a, b):
    return lax.dot_general(a, b, (((1,), (1,)), ((), ())), preferred_element_type=F32)


def _dot_tn(a, b):
    return lax.dot_general(a, b, (((0,), (0,)), ((), ())), preferred_element_type=F32)


def _relayout_kernel(off_ref, kind_ref, a_ref, b_ref, o_ref, *, n_wi, n_small):
    del off_ref
    kind = kind_ref[pl.program_id(0)]

    @pl.when(kind == 0)
    def _():
        o_ref[...] = a_ref[0].astype(BF16)

    @pl.when(kind == 1)
    def _():
        o_ref[...] = jnp.zeros(o_ref.shape, BF16)

    @pl.when(kind == 2)
    def _():
        r = lax.broadcasted_iota(jnp.int32, o_ref.shape, 0)
        o_ref[...] = jnp.where(r < n_wi, a_ref[0], jnp.where(r < n_small, b_ref[0], 0.0)).astype(BF16)


def _relayout_w_in(w_t, src, order):
    _, n_in, d = w_t.shape
    starts = [src[n][0] + t * LANES for n in order for t in range((src[n][1] - src[n][0]) // LANES)]
    wi0, glr0 = src["wi"][0], src["glr"][0]
    n_wi = src["wi"][1] - wi0
    n_small = n_wi + src["glr"][1] - glr0
    glr_win = glr0 - n_wi
    assert n_small <= LANES and glr_win >= 0 and wi0 + LANES <= n_in and glr_win + LANES <= n_in
    assert all(o % 8 == 0 for o in starts + [wi0, glr_win])
    row_off8 = [o // 8 for o in starts + [0, wi0]]
    kind = [0] * len(starts) + [1, 2]
    nb = len(kind)
    tables = [jnp.asarray(np.asarray(t, np.int32)) for t in (row_off8, kind)]
    return pl.pallas_call(
        functools.partial(_relayout_kernel, n_wi=n_wi, n_small=n_small),
        grid_spec=pltpu.PrefetchScalarGridSpec(
            num_scalar_prefetch=2,
            grid=(nb,),
            in_specs=[pl.BlockSpec((pl.Element(1), pl.Element(LANES), pl.Element(d)),
                                   lambda b, off, kd: (0, off[b] * 8, 0)),
                      pl.BlockSpec((pl.Element(1), pl.Element(LANES), pl.Element(d)),
                                   lambda b, off, kd: (0, glr_win, 0))],
            out_specs=pl.BlockSpec((LANES, d), lambda b, off, kd: (b, 0))),
        out_shape=jax.ShapeDtypeStruct((nb * LANES, d), BF16),
        compiler_params=_params("arbitrary"),
        name="relayout_w_in",
    )(*tables, w_t, w_t)


def _proj_kernel(x_ref, w_ref, ws_ref, sc_ref, o_ref, os_ref, xb_scr):
    @pl.when(pl.program_id(1) == 0)
    def _():
        xb_scr[...] = x_ref[...].astype(BF16)
        os_ref[...] = _dot_nt(xb_scr[...], ws_ref[...])

    res = _dot_nt(xb_scr[...], w_ref[...]) * sc_ref[...]
    for c in range(o_ref.shape[0]):
        o_ref[c] = res[:, c * LANES:(c + 1) * LANES].astype(o_ref.dtype)


def _proj_blocks(x, wt, col_scale, *, tm, tn):
    s, k = x.shape
    n = wt.shape[0] - LANES
    wb = wt
    return pl.pallas_call(
        _proj_kernel,
        grid=(s // tm, n // tn),
        in_specs=[pl.BlockSpec((tm, k), lambda i, j: (i, 0)),
                  pl.BlockSpec((tn, k), lambda i, j: (j, 0)),
                  pl.BlockSpec((LANES, k), lambda i, j: (n // LANES, 0)),
                  pl.BlockSpec((1, tn), lambda i, j: (0, j))],
        out_specs=[pl.BlockSpec((tn // LANES, tm, LANES), lambda i, j: (j, i, 0)),
                   pl.BlockSpec((tm, LANES), lambda i, j: (i, 0))],
        out_shape=[jax.ShapeDtypeStruct((n // LANES, s, LANES), BF16),
                   jax.ShapeDtypeStruct((s, LANES), F32)],
        scratch_shapes=[pltpu.VMEM((tm, k), BF16)],
        compiler_params=_params("parallel", "arbitrary"),
        name="proj",
    )(x, wb, wb, col_scale)


def _mm_kernel(a_ref, b_ref, *rest, alpha, has_res, nk):
    if has_res:
        r_ref, o_ref = rest[0], rest[1]
        rest = rest[2:]
    else:
        r_ref, o_ref = None, rest[0]
        rest = rest[1:]

    def finish(acc):
        if has_res:
            acc = acc + alpha * r_ref[...]
        o_ref[...] = acc.astype(o_ref.dtype)

    if nk == 1:
        finish(_dot(a_ref[...], b_ref[...]))
        return
    acc_ref = rest[0]
    kk = pl.program_id(2)

    @pl.when(kk == 0)
    def _():
        acc_ref[...] = _dot(a_ref[...], b_ref[...])

    @pl.when(kk > 0)
    def _():
        acc_ref[...] += _dot(a_ref[...], b_ref[...])

    @pl.when(kk == nk - 1)
    def _():
        finish(acc_ref[...])


def _matmul(a, b, *, tm, tn, tk, out_dtype, residual=None, alpha=1.0, name="matmul"):
    m, k = a.shape
    n = b.shape[1]
    nk = k // tk
    in_specs = [pl.BlockSpec((tm, tk), lambda i, j, kk: (i, kk)),
                pl.BlockSpec((tk, tn), lambda i, j, kk: (kk, j))]
    args = [a, b]
    if residual is not None:
        in_specs.append(pl.BlockSpec((tm, tn), lambda i, j, kk: (i, j)))
        args.append(residual)
    scratch = [pltpu.VMEM((tm, tn), F32)] if nk > 1 else []
    return pl.pallas_call(
        functools.partial(_mm_kernel, alpha=alpha, has_res=residual is not None, nk=nk),
        grid=(m // tm, n // tn, nk),
        in_specs=in_specs,
        out_specs=pl.BlockSpec((tm, tn), lambda i, j, kk: (i, j)),
        out_shape=jax.ShapeDtypeStruct((m, n), out_dtype),
        scratch_shapes=scratch,
        compiler_params=_params("parallel", "parallel", "arbitrary"),
        name=name,
    )(*args)


def _wo_kernel(oa_ref, og_ref, wt_ref, wb_ref, x_ref, o_ref, w_scr, *, ka, alpha):
    @pl.when(pl.program_id(1) == 0)
    def _():
        w_scr[:ka, :] = wt_ref[0].astype(BF16)
        w_scr[ka:, :] = wb_ref[0].astype(BF16)

    acc = _dot(oa_ref[...], w_scr[:ka, :]) + _dot(og_ref[...], w_scr[ka:, :])
    o_ref[...] = acc + alpha * x_ref[...]


def _w_o(o_a, o_g, w_o, x, *, tm, tn, alpha):
    s, ka = o_a.shape
    kg = o_g.shape[1]
    d = w_o.shape[2]
    assert ka % kg == 0
    return pl.pallas_call(
        functools.partial(_wo_kernel, ka=ka, alpha=alpha),
        grid=(d // tn, s // tm),
        in_specs=[pl.BlockSpec((tm, ka), lambda j, i: (i, 0)),
                  pl.BlockSpec((tm, kg), lambda j, i: (i, 0)),
                  pl.BlockSpec((1, ka, tn), lambda j, i: (0, 0, j)),
                  pl.BlockSpec((1, kg, tn), lambda j, i: (0, ka // kg, j)),
                  pl.BlockSpec((tm, tn), lambda j, i: (i, j))],
        out_specs=pl.BlockSpec((tm, tn), lambda j, i: (i, j)),
        out_shape=jax.ShapeDtypeStruct((s, d), F32),
        scratch_shapes=[pltpu.VMEM((ka + kg, tn), BF16)],
        compiler_params=_params("parallel", "arbitrary"),
        name="w_o",
    )(o_a, o_g, w_o, w_o, x)


def _ln_kernel(y_ref, g_ref, b_ref, *o_refs):
    y = y_ref[...]
    mu = jnp.mean(y, axis=1, keepdims=True)
    yc = y - mu
    var = jnp.mean(yc * yc, axis=1, keepdims=True)
    out = yc * lax.rsqrt(var + LN_EPS) * g_ref[...] + b_ref[...]
    for o_ref in o_refs:
        o_ref[...] = out.astype(o_ref.dtype)


def _layer_norm(y, g, b, *, tr, also_bf16):
    s, d = y.shape
    out_shape = [jax.ShapeDtypeStruct((s, d), F32)]
    out_specs = [pl.BlockSpec((tr, d), lambda i: (i, 0))]
    if also_bf16:
        out_shape.append(jax.ShapeDtypeStruct((s, d), BF16))
        out_specs.append(pl.BlockSpec((tr, d), lambda i: (i, 0)))
    return pl.pallas_call(
        _ln_kernel,
        grid=(s // tr,),
        in_specs=[pl.BlockSpec((tr, d), lambda i: (i, 0)),
                  pl.BlockSpec((1, d), lambda i: (0, 0)),
                  pl.BlockSpec((1, d), lambda i: (0, 0))],
        out_specs=out_specs,
        out_shape=out_shape,
        compiler_params=_params("parallel"),
        name="layer_norm",
    )(y, g.reshape(1, d), b.reshape(1, d))


def _idx_kernel(qi_ref, ki_ref, sm_ref, bias_ref, hi_scr, lo_scr, wb_scr, *, tq, tk, nkt, hi, group,
                topk, wscale):
    qb = pl.program_id(0)
    nk = ((qb + 1) * tq + tk - 1) // tk
    reps = tk // LANES
    i16 = jnp.int16
    low16 = -2 ** 15

    for h in range(hi):
        wb_scr[h] = jnp.broadcast_to(sm_ref[:, h:h + 1] * wscale, (tq, LANES))

    row = qb * tq + lax.broadcasted_iota(jnp.int32, (group, tk), 0)
    col = lax.broadcasted_iota(jnp.int32, (group, tk), 1)

    def lanes(a):
        return jnp.concatenate([a] * reps, axis=1) if reps > 1 else a

    def fold(a):
        out = a[:, :LANES]
        for r in range(1, reps):
            out = out + a[:, r * LANES:(r + 1) * LANES]
        return out

    def score_tile(kb, carry):
        kt = ki_ref[0, pl.ds(pl.multiple_of(kb * tk, tk), tk), :]
        for r in range(tq // group):
            rows = slice(r * group, (r + 1) * group)
            qs = qi_ref[:, rows, :].reshape(hi * group, IDX_DIM)
            d = _dot_nt(qs, kt)
            acc = jnp.zeros((group, tk), F32)
            for h in range(hi):
                acc = acc + jnp.maximum(d[h * group:(h + 1) * group], 0.0) * lanes(wb_scr[h, rows, :])
            bits = pltpu.bitcast(acc + 0.0, jnp.int32)
            key = jnp.where(kb * tk + col <= row + r * group,
                            bits ^ ((bits >> 31) & jnp.int32(0x7FFFFFFF)), INT_MIN)
            hi_scr[kb, rows, :] = (key >> 16).astype(i16)
            lo_scr[kb, rows, :] = ((key & 0xFFFF) + low16).astype(i16)
        return carry

    lax.fori_loop(0, nk, score_tile, 0)

    def count_ge(ref, cand):
        cand_t = lanes(cand.astype(i16))

        def step(kb, c):
            return c + fold(jnp.where(ref[kb] >= cand_t, i16(1), i16(0)))

        def step_many(q, c):
            for u in range(COUNT_UNROLL):
                c = step(q * COUNT_UNROLL + u, c)
            return c

        n_many = nk // COUNT_UNROLL
        c = lax.fori_loop(0, n_many, step_many, jnp.zeros((tq, LANES), i16))
        c = lax.fori_loop(n_many * COUNT_UNROLL, nk, step, c)
        return jnp.sum(c.astype(F32), axis=1, keepdims=True)

    kf = float(topk)

    def bit_search(ref, base_cnt, cnt_start, stop_when_exact):
        def body(state):
            it, v, cnt_v = state
            cand = v + jnp.left_shift(jnp.int32(1), 15 - it)
            cnt = base_cnt + count_ge(ref, cand)
            ok = cnt >= kf
            return it + 1, jnp.where(ok, cand, v), jnp.where(ok, cnt, cnt_v)

        def unsettled(state):
            go = state[0] < 16
            if stop_when_exact:
                go = go & (jnp.max(jnp.where(state[2] == kf, 0.0, 1.0)) > 0.0)
            return go

        v0 = jnp.full((tq, LANES), low16, jnp.int32)
        _, v, cnt_v = lax.while_loop(unsettled, body, (jnp.int32(0), v0, cnt_start))
        return v, cnt_v

    n_all = jnp.zeros((tq, 1), F32) + (nk * tk).astype(F32)
    v_hi, n_from_hi = bit_search(hi_scr, 0.0, n_all, False)
    n_above = count_ge(hi_scr, v_hi + 1)
    v_hi_t = lanes(v_hi.astype(i16))

    def keep_equal(kb, carry):
        lo_scr[kb] = jnp.where(hi_scr[kb] == v_hi_t, lo_scr[kb], i16(low16))
        return carry

    lax.fori_loop(0, nk, keep_equal, 0)
    v_lo, _ = bit_search(lo_scr, n_above, n_from_hi, True)
    v_lo = jnp.where(v_hi == low16, jnp.maximum(v_lo, low16 + 1), v_lo)
    v_lo_t = lanes(v_lo.astype(i16))
    zero = jnp.zeros((tq, tk), bias_ref.dtype)
    neg = jnp.full((tq, tk), NEG_BIG, bias_ref.dtype)

    def emit(kb, carry):
        h16 = hi_scr[kb]
        sel = (h16 > v_hi_t) | ((h16 == v_hi_t) & (lo_scr[kb] >= v_lo_t))
        bias_ref[0, kb] = jnp.where(sel, zero, neg)
        return carry

    lax.fori_loop(0, nk, emit, 0)

    def fill(kb, carry):
        bias_ref[0, kb] = neg
        return carry

    lax.fori_loop(nk, nkt, fill, 0)


def _indexer(main, small, lay, *, s, tq, tk, topk):
    nkt = s // tk
    hi = IDX_HEADS
    group = min(tq, 64)
    qi0, ki0 = lay["qi"][0], lay["ki"][0]
    assert qi0 % hi == 0
    return pl.pallas_call(
        functools.partial(_idx_kernel, tq=tq, tk=tk, nkt=nkt, hi=hi, group=group, topk=topk,
                          wscale=IDX_HEADS ** -0.5 * IDX_DIM ** -0.5),
        grid=(s // tq,),
        in_specs=[pl.BlockSpec((hi, tq, LANES), lambda i: (qi0 // hi, i, 0)),
                  pl.BlockSpec((1, s, LANES), lambda i: (ki0, 0, 0)),
                  pl.BlockSpec((tq, LANES), lambda i: (i, 0))],
        out_specs=pl.BlockSpec((1, nkt, tq, tk), lambda i: (i, 0, 0, 0)),
        out_shape=jax.ShapeDtypeStruct((s // tq, nkt, tq, tk), BF16),
        scratch_shapes=[pltpu.VMEM((nkt, tq, tk), jnp.int16),
                        pltpu.VMEM((nkt, tq, tk), jnp.int16),
                        pltpu.VMEM((hi, tq, LANES), F32)],
        compiler_params=_params("parallel"),
        name="indexer",
    )(main, main, small)


def _attn_kernel(qi_ref, kb_ref, q_ref, k_ref, v_ref, b_ref, g_ref, o_ref, bias_scr, m_scr, acc_scr, *pipe,
                 nh, tq, tk):
    step = pl.program_id(0)
    qi = qi_ref[step]
    kb = kb_ref[step]

    @pl.when(kb == 0)
    def _():
        m_scr[...] = jnp.full(m_scr.shape, NEG_BIG, F32)
        acc_scr[...] = jnp.zeros(acc_scr.shape, F32)

    na, nb, sq, sk = b_ref.shape
    for a in range(na):
        for b in range(nb):
            bias_scr[a * sq:(a + 1) * sq, b * sk:(b + 1) * sk] = b_ref[a, b].astype(F32)

    ones = jnp.ones((tk, LANES), BF16)

    def qk(h, s_ref):
        s_ref[...] = _dot_nt(q_ref[h], k_ref[h])

    def soft(h, s_ref, p_ref, al_ref):
        s = s_ref[...] + bias_scr[...]
        m_prev = m_scr[h]
        m_new = jnp.maximum(m_prev, jnp.max(s, axis=1, keepdims=True))
        m_scr[h] = m_new
        p_ref[...] = jnp.exp2(s - m_new[:, :1]).astype(BF16)
        al_ref[...] = jnp.exp2(m_prev - m_new)

    def pv(h, p_ref, al_ref):
        al = al_ref[...]
        v_ext = jnp.concatenate([v_ref[h], ones], axis=1)
        acc_scr[h] = acc_scr[h] * jnp.concatenate([al, al], axis=1) + _dot(p_ref[...], v_ext)

    s_a, s_b, p_a, p_b, al_a, al_b = pipe
    qk(0, s_a)
    qk(1, s_b)
    soft(0, s_a, p_a, al_a)

    def pair(j, carry):
        qk(2 * j, s_a)
        soft(2 * j - 1, s_b, p_b, al_b)
        pv(2 * j - 2, p_a, al_a)
        qk(2 * j + 1, s_b)
        soft(2 * j, s_a, p_a, al_a)
        pv(2 * j - 1, p_b, al_b)
        return carry

    lax.fori_loop(1, nh // 2, pair, 0)
    soft(nh - 1, s_b, p_b, al_b)
    pv(nh - 2, p_a, al_a)
    pv(nh - 1, p_b, al_b)

    @pl.when(kb == (qi * tq + tq - 1) // tk)
    def _():
        ss = jnp.zeros((tq, LANES), F32)
        for h in range(nh):
            o_h = acc_scr[h, :, :LANES] / acc_scr[h, :, LANES:]
            acc_scr[h, :, :LANES] = o_h
            ss = ss + o_h * o_h
        ms = jnp.sum(ss, axis=1, keepdims=True) * (1.0 / (nh * LANES))
        r = lax.rsqrt(ms + RMS_EPS)
        for h in range(nh):
            sl = slice(h * LANES, (h + 1) * LANES)
            o_ref[:, sl] = (acc_scr[h, :, :LANES] * r * g_ref[:, sl]).astype(o_ref.dtype)


def _attention(main, bias, g, lay, *, s, tq, tk):
    nh = A_HEADS
    nqb, nkt, sq, sk = bias.shape
    q0, k0, v0 = lay["qa"][0], lay["ka"][0], lay["va"][0]
    assert q0 % nh == 0 and k0 % nh == 0 and v0 % nh == 0
    pairs = [(i, j) for i in range(s // tq) for j in range((i * tq + tq - 1) // tk + 1)]
    qi_tab = jnp.asarray(np.asarray([p[0] for p in pairs], np.int32))
    kb_tab = jnp.asarray(np.asarray([p[1] for p in pairs], np.int32))
    return pl.pallas_call(
        functools.partial(_attn_kernel, nh=nh, tq=tq, tk=tk),
        grid_spec=pltpu.PrefetchScalarGridSpec(
            num_scalar_prefetch=2,
            grid=(len(pairs),),
            in_specs=[pl.BlockSpec((nh, tq, LANES), lambda t, qi, kb: (q0 // nh, qi[t], 0)),
                      pl.BlockSpec((nh, tk, LANES), lambda t, qi, kb: (k0 // nh, kb[t], 0)),
                      pl.BlockSpec((nh, tk, LANES), lambda t, qi, kb: (v0 // nh, kb[t], 0)),
                      pl.BlockSpec((tq // sq, tk // sk, sq, sk), lambda t, qi, kb: (qi[t], kb[t], 0, 0)),
                      pl.BlockSpec((1, nh * LANES), lambda t, qi, kb: (0, 0))],
            out_specs=pl.BlockSpec((tq, nh * LANES), lambda t, qi, kb: (qi[t], 0)),
            scratch_shapes=[pltpu.VMEM((tq, tk), F32),
                            pltpu.VMEM((nh, tq, LANES), F32),
                            pltpu.VMEM((nh, tq, 2 * LANES), F32),
                            pltpu.VMEM((tq, tk), F32), pltpu.VMEM((tq, tk), F32),
                            pltpu.VMEM((tq, tk), BF16), pltpu.VMEM((tq, tk), BF16),
                            pltpu.VMEM((tq, LANES), F32), pltpu.VMEM((tq, LANES), F32)]),
        out_shape=jax.ShapeDtypeStruct((s, nh * LANES), BF16),
        compiler_params=_params("arbitrary"),
        name="attention",
    )(qi_tab, kb_tab, main, main, main, bias, g.reshape(1, nh * LANES))


def _gla_kernel(q_ref, k_ref, v_ref, gg_ref, sm_ref, w2_ref, bgk_ref, gn_ref, o_ref, st_scr, *,
                rows, chunk, nheads, dk, dv):
    @pl.when(pl.program_id(0) == 0)
    def _():
        st_scr[...] = jnp.zeros(st_scr.shape, F32)

    r_i = lax.broadcasted_iota(jnp.int32, (chunk, chunk), 0)
    c_i = lax.broadcasted_iota(jnp.int32, (chunk, chunk), 1)
    lower = r_i >= c_i
    tri = jnp.where(lower, 1.0, 0.0).astype(BF16)
    mid = chunk // 2
    nbk, nbv = dk // LANES, dv // LANES

    def wide(ref, first, n, r0):
        return jnp.concatenate([ref[first + j, pl.ds(r0, chunk), :] for j in range(n)], axis=1)

    def step(c, carry):
        r0 = pl.multiple_of(c * chunk, chunk)
        z_all = _dot(sm_ref[pl.ds(r0, chunk), :].astype(BF16), w2_ref[...]) + bgk_ref[...]
        for h in range(nheads):
            q = wide(q_ref, h * nbk, nbk, r0).astype(F32) * (dk ** -0.5)
            k = wide(k_ref, h * nbk, nbk, r0).astype(F32)
            v = wide(v_ref, h * nbv, nbv, r0)
            z = z_all[:, h * dk:(h + 1) * dk]
            g = (jnp.minimum(z, 0.0) - jnp.log1p(jnp.exp(-jnp.abs(z)))) * (1.0 / G_TAU)
            g_hi = g.astype(BF16)
            rem = g - g_hi.astype(F32)
            g_mid = rem.astype(BF16)
            g_lo = (rem - g_mid.astype(F32)).astype(BF16)
            b = _dot(tri, g_hi) + _dot(tri, g_mid) + _dot(tri, g_lo)
            b_mid = b[mid:mid + 1, :]
            b_last = b[chunk - 1:chunk, :]
            qe = (q * jnp.exp(b - b_mid)).astype(BF16)
            ke = (k * jnp.exp(b_mid - b)).astype(BF16)
            att = jnp.where(lower, _dot_nt(qe, ke), 0.0)
            st = st_scr[h]
            o = _dot(att.astype(BF16), v) + _dot_nt((q * jnp.exp(b)).astype(BF16), st.astype(BF16))
            kd = (k * jnp.exp(b_last - b)).astype(BF16)
            st_scr[h] = st * jnp.exp(b_last) + _dot_tn(v, kd)
            ms = jnp.mean(o * o, axis=1, keepdims=True)
            gate = wide(gg_ref, h * nbv, nbv, r0).astype(F32)
            out = (o * lax.rsqrt(ms + RMS_EPS)) * gn_ref[...] * (gate * jax.nn.sigmoid(gate))
            o_ref[pl.ds(r0, chunk), h * dv:(h + 1) * dv] = out.astype(o_ref.dtype)
        return carry

    lax.fori_loop(0, rows // chunk, step, 0)


def _gla(main, small, w2, bgk, gn, lay, *, s, rows, chunk, dk, dv):
    nh = G_HEADS
    nqk, nv = nh * dk // LANES, nh * dv // LANES
    q0, k0, v0, g0 = lay["qg"][0], lay["kg"][0], lay["vg"][0], lay["gg"][0]
    assert q0 % nqk == 0 and k0 % nqk == 0 and v0 % nv == 0 and g0 % nv == 0
    return pl.pallas_call(
        functools.partial(_gla_kernel, rows=rows, chunk=chunk, nheads=nh, dk=dk, dv=dv),
        grid=(s // rows,),
        in_specs=[pl.BlockSpec((nqk, rows, LANES), lambda r: (q0 // nqk, r, 0)),
                  pl.BlockSpec((nqk, rows, LANES), lambda r: (k0 // nqk, r, 0)),
                  pl.BlockSpec((nv, rows, LANES), lambda r: (v0 // nv, r, 0)),
                  pl.BlockSpec((nv, rows, LANES), lambda r: (g0 // nv, r, 0)),
                  pl.BlockSpec((rows, LANES), lambda r: (r, 0)),
                  pl.BlockSpec((LANES, nh * dk), lambda r: (0, 0)),
                  pl.BlockSpec((1, nh * dk), lambda r: (0, 0)),
                  pl.BlockSpec((1, dv), lambda r: (0, 0))],
        out_specs=pl.BlockSpec((rows, nh * dv), lambda r: (r, 0)),
        out_shape=jax.ShapeDtypeStruct((s, nh * dv), BF16),
        scratch_shapes=[pltpu.VMEM((nh, dv, dk), F32)],
        compiler_params=_params("arbitrary"),
        name="gla",
    )(main, main, main, main, small, w2, bgk, gn.reshape(1, dv))


def _ffn_up_kernel(h_ref, wg_ref, wv_ref, cwg_ref, cwv_ref, cbg_ref, cbv_ref, o_ref, w_scr, tail_scr,
                   *, tm, tn, rc):
    @pl.when(pl.program_id(1) == 0)
    def _():
        w_scr[:, :tn] = wg_ref[0].astype(BF16)
        w_scr[:, tn:] = wv_ref[0].astype(BF16)
        tail_scr[...] = jnp.zeros(tail_scr.shape, F32)

    cw = jnp.concatenate([cwg_ref[0], cwv_ref[0]], axis=1)
    cb = jnp.concatenate([cbg_ref[...], cbv_ref[...]], axis=1)
    w0, w1, w2 = cw[0:1, :], cw[1:2, :], cw[2:3, :]

    def taps(a):
        return w2 * a + w1 * pltpu.roll(a, 1, 0) + w0 * pltpu.roll(a, 2, 0) + cb

    tail = tail_scr[...]
    for r in range(tm // rc):
        rows = slice(r * rc, (r + 1) * rc)
        hw = _dot(h_ref[rows, :], w_scr[...])
        top = taps(jnp.concatenate([tail, hw[0:8]], axis=0))[8:16]
        u = jnp.concatenate([top, taps(hw)[8:]], axis=0)
        gate, val = u[:, :tn], u[:, tn:]
        o_ref[rows, :] = (gate * jax.nn.sigmoid(gate) * val).astype(o_ref.dtype)
        tail = hw[rc - 8:rc, :]
    tail_scr[...] = tail


def _ffn_up(hb, w_up, conv_w, conv_b, *, d_ff, tm, tn):
    s, d = hb.shape
    nj = d_ff // tn
    return pl.pallas_call(
        functools.partial(_ffn_up_kernel, tm=tm, tn=tn, rc=min(tm, 512)),
        grid=(nj, s // tm),
        in_specs=[pl.BlockSpec((tm, d), lambda j, i: (i, 0)),
                  pl.BlockSpec((1, d, tn), lambda j, i: (0, 0, j)),
                  pl.BlockSpec((1, d, tn), lambda j, i: (0, 0, nj + j)),
                  pl.BlockSpec((1, CONV_W, tn), lambda j, i: (0, 0, j)),
                  pl.BlockSpec((1, CONV_W, tn), lambda j, i: (0, 0, nj + j)),
                  pl.BlockSpec((1, tn), lambda j, i: (0, j)),
                  pl.BlockSpec((1, tn), lambda j, i: (0, nj + j))],
        out_specs=pl.BlockSpec((tm, tn), lambda j, i: (i, j)),
        out_shape=jax.ShapeDtypeStruct((s, d_ff), BF16),
        scratch_shapes=[pltpu.VMEM((d, 2 * tn), BF16), pltpu.VMEM((8, 2 * tn), F32)],
        compiler_params=_params("parallel", "arbitrary"),
        name="ffn_up",
    )(hb, w_up, w_up, conv_w, conv_w, conv_b, conv_b)


def _layout(d_model):
    a_width = A_HEADS * A_HEAD_DIM
    g_width = d_model - a_width
    g_kwidth = g_width // 2
    names = ("qa", "ka", "va", "qi", "ki", "wi", "qg", "kg", "vg", "glr", "gg")
    sizes = (a_width, a_width, a_width, IDX_HEADS * IDX_DIM, IDX_DIM, IDX_HEADS,
             g_kwidth, g_kwidth, g_width, G_LOWRANK, g_width)
    offs = np.concatenate([[0], np.cumsum(sizes)])
    src = {n: (int(offs[i]), int(offs[i + 1])) for i, n in enumerate(names)}
    order = ("qi", "qa", "ka", "va", "qg", "kg", "vg", "gg", "ki")
    lay, blk = {}, 0
    for n in order:
        width = src[n][1] - src[n][0]
        assert width % LANES == 0
        lay[n] = (blk, width // LANES)
        blk += width // LANES
    return src, order, lay, blk


def kernel(x, w_in, w_gk2, b_gk, attn_out_g, gla_norm_g, w_o, ln1_g, ln1_b, w_up, conv_w, conv_b,
           w_down, ln2_g, ln2_b):
    assert x.shape[0] == 1 and w_in.shape[0] == DEPTH == 1
    _, s, d = x.shape
    x2 = x[0]
    src, order, lay, nb_main = _layout(d)
    g_width = d - A_HEADS * A_HEAD_DIM
    dv = g_width // G_HEADS
    dk = dv // 2
    d_ff = w_down.shape[1]
    topk = min(TOPK_MAX, s // 4)

    w_proj = _relayout_w_in(jnp.swapaxes(w_in, 1, 2), src, order)
    n_small = IDX_HEADS + G_LOWRANK
    w2 = jnp.zeros((LANES, G_HEADS * dk), F32).at[IDX_HEADS:n_small].set(w_gk2[0]).astype(BF16)
    bgk = b_gk[0].reshape(1, G_HEADS * dk)
    wd = w_down[0].astype(BF16)
    n_main = (nb_main + 1) * LANES
    q_lo, q_hi = lay["qa"][0] * LANES, (lay["qa"][0] + lay["qa"][1]) * LANES
    col = np.ones((1, n_main), np.float32)
    col[:, q_lo:q_hi] = A_HEAD_DIM ** -0.5 * LOG2E
    col_scale = jnp.asarray(col)

    tm = _tile(s, 1024)
    main, small = _proj_blocks(x2, w_proj, col_scale, tm=_tile(s, 512), tn=_tile(n_main, 1280))
    bias = _indexer(main, small, lay, s=s, tq=256, tk=256, topk=topk)
    ta = _tile(s, 512)
    o_a = _attention(main, bias, attn_out_g[0], lay, s=s, tq=ta, tk=ta)
    o_g = _gla(main, small, w2, bgk, gla_norm_g[0], lay, s=s, rows=_tile(s, 512), chunk=64, dk=dk, dv=dv)
    y1 = _w_o(o_a, o_g, w_o, x2, tm=tm, tn=_tile(d, 512), alpha=DN_ALPHA)
    h, hb = _layer_norm(y1, ln1_g[0], ln1_b[0], tr=_tile(s, 256, 8), also_bf16=True)

    act = _ffn_up(hb, w_up, conv_w, conv_b[0].reshape(1, 2 * d_ff), d_ff=d_ff, tm=tm, tn=_tile(d_ff, 256))
    y2 = _matmul(act, wd, tm=_tile(s, 512), tn=_tile(d, 512), tk=d_ff, out_dtype=F32, residual=h,
                 alpha=DN_ALPHA, name="w_down")
    (out,) = _layer_norm(y2, ln2_g[0], ln2_b[0], tr=_tile(s, 256, 8), also_bf16=False)
    return out[None]
```

```python
import functools

import numpy as np
import jax
import jax.numpy as jnp
from jax import lax
from jax.experimental import pallas as pl
from jax.experimental.pallas import tpu as pltpu

A_HEADS = 16
A_HEAD_DIM = 128
IDX_HEADS = 32
IDX_DIM = 128
TOPK_MAX = 256
G_HEADS = 4
G_LOWRANK = 16
G_TAU = 16.0
CONV_W = 3
LN_EPS = 1e-5
RMS_EPS = 1e-6
DEPTH = 1
DN_ALPHA = (2 * DEPTH) ** 0.25

LANES = 128
VMEM_LIMIT = 56 * 1024 * 1024

BF16 = jnp.bfloat16
F32 = jnp.float32
NEG_BIG = -1e30
LOG2E = 1.4426950408889634
INT_MIN = -2 ** 31
COUNT_UNROLL = 4


def _params(*sem):
    return pltpu.CompilerParams(dimension_semantics=sem, vmem_limit_bytes=VMEM_LIMIT)


def _tile(n, max_tile, quantum=LANES):
    best = None
    for t in range(quantum, min(n, max_tile) + 1, quantum):
        if n % t == 0:
            best = t
    assert best is not None, (n, max_tile, quantum)
    return best


def _dot(a, b):
    return jnp.dot(a, b, preferred_element_type=F32)


def _dot_nt(a, b):
    return lax.dot_general(a, b, (((1,), (1,)), ((), ())), preferred_element_type=F32)


def _dot_tn(a, b):
    return lax.dot_general(a, b, (((0,), (0,)), ((), ())), preferred_element_type=F32)


def _relayout_kernel(off_ref, kind_ref, a_ref, b_ref, o_ref, *, n_wi, n_small):
    del off_ref
    kind = kind_ref[pl.program_id(0)]

    @pl.when(kind == 0)
    def _():
        o_ref[...] = a_ref[0].astype(BF16)

    @pl.when(kind == 1)
    def _():
        o_ref[...] = jnp.zeros(o_ref.shape, BF16)

    @pl.when(kind == 2)
    def _():
        r = lax.broadcasted_iota(jnp.int32, o_ref.shape, 0)
        o_ref[...] = jnp.where(r < n_wi, a_ref[0], jnp.where(r < n_small, b_ref[0], 0.0)).astype(BF16)


def _relayout_w_in(w_t, src, order):
    _, n_in, d = w_t.shape
    starts = [src[n][0] + t * LANES for n in order for t in range((src[n][1] - src[n][0]) // LANES)]
    wi0, glr0 = src["wi"][0], src["glr"][0]
    n_wi = src["wi"][1] - wi0
    n_small = n_wi + src["glr"][1] - glr0
    glr_win = glr0 - n_wi
    assert n_small <= LANES and glr_win >= 0 and wi0 + LANES <= n_in and glr_win + LANES <= n_in
    assert all(o % 8 == 0 for o in starts + [wi0, glr_win])
    row_off8 = [o // 8 for o in starts + [0, wi0]]
    kind = [0] * len(starts) + [1, 2]
    nb = len(kind)
    tables = [jnp.asarray(np.asarray(t, np.int32)) for t in (row_off8, kind)]
    return pl.pallas_call(
        functools.partial(_relayout_kernel, n_wi=n_wi, n_small=n_small),
        grid_spec=pltpu.PrefetchScalarGridSpec(
            num_scalar_prefetch=2,
            grid=(nb,),
            in_specs=[pl.BlockSpec((pl.Element(1), pl.Element(LANES), pl.Element(d)),
                                   lambda b, off, kd: (0, off[b] * 8, 0)),
                      pl.BlockSpec((pl.Element(1), pl.Element(LANES), pl.Element(d)),
                                   lambda b, off, kd: (0, glr_win, 0))],
            out_specs=pl.BlockSpec((LANES, d), lambda b, off, kd: (b, 0))),
        out_shape=jax.ShapeDtypeStruct((nb * LANES, d), BF16),
        compiler_params=_params("arbitrary"),
        name="relayout_w_in",
    )(*tables, w_t, w_t)


def _proj_kernel(x_ref, w_ref, ws_ref, sc_ref, o_ref, os_ref, xb_scr):
    @pl.when(pl.program_id(1) == 0)
    def _():
        xb_scr[...] = x_ref[...].astype(BF16)
        os_ref[...] = _dot_nt(xb_scr[...], ws_ref[...])

    res = _dot_nt(xb_scr[...], w_ref[...]) * sc_ref[...]
    for c in range(o_ref.shape[0]):
        o_ref[c] = res[:, c * LANES:(c + 1) * LANES].astype(o_ref.dtype)


def _proj_blocks(x, wt, col_scale, *, tm, tn):
    s, k = x.shape
    n = wt.shape[0] - LANES
    wb = wt
    return pl.pallas_call(
        _proj_kernel,
        grid=(s // tm, n // tn),
        in_specs=[pl.BlockSpec((tm, k), lambda i, j: (i, 0)),
                  pl.BlockSpec((tn, k), lambda i, j: (j, 0)),
                  pl.BlockSpec((LANES, k), lambda i, j: (n // LANES, 0)),
                  pl.BlockSpec((1, tn), lambda i, j: (0, j))],
        out_specs=[pl.BlockSpec((tn // LANES, tm, LANES), lambda i, j: (j, i, 0)),
                   pl.BlockSpec((tm, LANES), lambda i, j: (i, 0))],
        out_shape=[jax.ShapeDtypeStruct((n // LANES, s, LANES), BF16),
                   jax.ShapeDtypeStruct((s, LANES), F32)],
        scratch_shapes=[pltpu.VMEM((tm, k), BF16)],
        compiler_params=_params("parallel", "arbitrary"),
        name="proj",
    )(x, wb, wb, col_scale)


def _mm_kernel(a_ref, b_ref, *rest, alpha, has_res, nk):
    if has_res:
        r_ref, o_ref = rest[0], rest[1]
        rest = rest[2:]
    else:
        r_ref, o_ref = None, rest[0]
        rest = rest[1:]

    def finish(acc):
        if has_res:
            acc = acc + alpha * r_ref[...]
        o_ref[...] = acc.astype(o_ref.dtype)

    if nk == 1:
        finish(_dot(a_ref[...], b_ref[...]))
        return
    acc_ref = rest[0]
    kk = pl.program_id(2)

    @pl.when(kk == 0)
    def _():
        acc_ref[...] = _dot(a_ref[...], b_ref[...])

    @pl.when(kk > 0)
    def _():
        acc_ref[...] += _dot(a_ref[...], b_ref[...])

    @pl.when(kk == nk - 1)
    def _():
        finish(acc_ref[...])


def _matmul(a, b, *, tm, tn, tk, out_dtype, residual=None, alpha=1.0, name="matmul"):
    m, k = a.shape
    n = b.shape[1]
    nk = k // tk
    in_specs = [pl.BlockSpec((tm, tk), lambda i, j, kk: (i, kk)),
                pl.BlockSpec((tk, tn), lambda i, j, kk: (kk, j))]
    args = [a, b]
    if residual is not None:
        in_specs.append(pl.BlockSpec((tm, tn), lambda i, j, kk: (i, j)))
        args.append(residual)
    scratch = [pltpu.VMEM((tm, tn), F32)] if nk > 1 else []
    return pl.pallas_call(
        functools.partial(_mm_kernel, alpha=alpha, has_res=residual is not None, nk=nk),
        grid=(m // tm, n // tn, nk),
        in_specs=in_specs,
        out_specs=pl.BlockSpec((tm, tn), lambda i, j, kk: (i, j)),
        out_shape=jax.ShapeDtypeStruct((m, n), out_dtype),
        scratch_shapes=scratch,
        compiler_params=_params("parallel", "parallel", "arbitrary"),
        name=name,
    )(*args)


def _wo_kernel(oa_ref, og_ref, wt_ref, wb_ref, x_ref, o_ref, w_scr, *, ka, alpha):
    @pl.when(pl.program_id(1) == 0)
    def _():
        w_scr[:ka, :] = wt_ref[0].astype(BF16)
        w_scr[ka:, :] = wb_ref[0].astype(BF16)

    acc = _dot(oa_ref[...], w_scr[:ka, :]) + _dot(og_ref[...], w_scr[ka:, :])
    o_ref[...] = acc + alpha * x_ref[...]


def _w_o(o_a, o_g, w_o, x, *, tm, tn, alpha):
    s, ka = o_a.shape
    kg = o_g.shape[1]
    d = w_o.shape[2]
    assert ka % kg == 0
    return pl.pallas_call(
        functools.partial(_wo_kernel, ka=ka, alpha=alpha),
        grid=(d // tn, s // tm),
        in_specs=[pl.BlockSpec((tm, ka), lambda j, i: (i, 0)),
                  pl.BlockSpec((tm, kg), lambda j, i: (i, 0)),
                  pl.BlockSpec((1, ka, tn), lambda j, i: (0, 0, j)),
                  pl.BlockSpec((1, kg, tn), lambda j, i: (0, ka // kg, j)),
                  pl.BlockSpec((tm, tn), lambda j, i: (i, j))],
        out_specs=pl.BlockSpec((tm, tn), lambda j, i: (i, j)),
        out_shape=jax.ShapeDtypeStruct((s, d), F32),
        scratch_shapes=[pltpu.VMEM((ka + kg, tn), BF16)],
        compiler_params=_params("parallel", "arbitrary"),
        name="w_o",
    )(o_a, o_g, w_o, w_o, x)


def _ln_kernel(y_ref, g_ref, b_ref, *o_refs):
    y = y_ref[...]
    mu = jnp.mean(y, axis=1, keepdims=True)
    yc = y - mu
    var = jnp.mean(yc * yc, axis=1, keepdims=True)
    out = yc * lax.rsqrt(var + LN_EPS) * g_ref[...] + b_ref[...]
    for o_ref in o_refs:
        o_ref[...] = out.astype(o_ref.dtype)


def _layer_norm(y, g, b, *, tr, also_bf16):
    s, d = y.shape
    out_shape = [jax.ShapeDtypeStruct((s, d), F32)]
    out_specs = [pl.BlockSpec((tr, d), lambda i: (i, 0))]
    if also_bf16:
        out_shape.append(jax.ShapeDtypeStruct((s, d), BF16))
        out_specs.append(pl.BlockSpec((tr, d), lambda i: (i, 0)))
    return pl.pallas_call(
        _ln_kernel,
        grid=(s // tr,),
        in_specs=[pl.BlockSpec((tr, d), lambda i: (i, 0)),
                  pl.BlockSpec((1, d), lambda i: (0, 0)),
                  pl.BlockSpec((1, d), lambda i: (0, 0))],
        out_specs=out_specs,
        out_shape=out_shape,
        compiler_params=_params("parallel"),
        name="layer_norm",
    )(y, g.reshape(1, d), b.reshape(1, d))


def _idx_kernel(qi_ref, ki_ref, sm_ref, bias_ref, hi_scr, lo_scr, wb_scr, *, tq, tk, nkt, hi, group,
                topk, wscale):
    qb = pl.program_id(0)
    nk = ((qb + 1) * tq + tk - 1) // tk
    reps = tk // LANES
    i16 = jnp.int16
    low16 = -2 ** 15

    for h in range(hi):
        wb_scr[h] = jnp.broadcast_to(sm_ref[:, h:h + 1] * wscale, (tq, LANES))

    row = qb * tq + lax.broadcasted_iota(jnp.int32, (group, tk), 0)
    col = lax.broadcasted_iota(jnp.int32, (group, tk), 1)

    def lanes(a):
        return jnp.concatenate([a] * reps, axis=1) if reps > 1 else a

    def fold(a):
        out = a[:, :LANES]
        for r in range(1, reps):
            out = out + a[:, r * LANES:(r + 1) * LANES]
        return out

    def score_tile(kb, carry):
        kt = ki_ref[0, pl.ds(pl.multiple_of(kb * tk, tk), tk), :]
        for r in range(tq // group):
            rows = slice(r * group, (r + 1) * group)
            qs = qi_ref[:, rows, :].reshape(hi * group, IDX_DIM)
            d = _dot_nt(qs, kt)
            acc = jnp.zeros((group, tk), F32)
            for h in range(hi):
                acc = acc + jnp.maximum(d[h * group:(h + 1) * group], 0.0) * lanes(wb_scr[h, rows, :])
            bits = pltpu.bitcast(acc + 0.0, jnp.int32)
            key = jnp.where(kb * tk + col <= row + r * group,
                            bits ^ ((bits >> 31) & jnp.int32(0x7FFFFFFF)), INT_MIN)
            hi_scr[kb, rows, :] = (key >> 16).astype(i16)
            lo_scr[kb, rows, :] = ((key & 0xFFFF) + low16).astype(i16)
        return carry

    lax.fori_loop(0, nk, score_tile, 0)

    def count_ge(ref, cand):
        cand_t = lanes(cand.astype(i16))

        def step(kb, c):
            return c + fold(jnp.where(ref[kb] >= cand_t, i16(1), i16(0)))

        def step_many(q, c):
            for u in range(COUNT_UNROLL):
                c = step(q * COUNT_UNROLL + u, c)
            return c

        n_many = nk // COUNT_UNROLL
        c = lax.fori_loop(0, n_many, step_many, jnp.zeros((tq, LANES), i16))
        c = lax.fori_loop(n_many * COUNT_UNROLL, nk, step, c)
        return jnp.sum(c.astype(F32), axis=1, keepdims=True)

    kf = float(topk)

    def bit_search(ref, base_cnt, cnt_start, stop_when_exact):
        def body(state):
            it, v, cnt_v = state
            cand = v + jnp.left_shift(jnp.int32(1), 15 - it)
            cnt = base_cnt + count_ge(ref, cand)
            ok = cnt >= kf
            return it + 1, jnp.where(ok, cand, v), jnp.where(ok, cnt, cnt_v)

        def unsettled(state):
            go = state[0] < 16
            if stop_when_exact:
                go = go & (jnp.max(jnp.where(state[2] == kf, 0.0, 1.0)) > 0.0)
            return go

        v0 = jnp.full((tq, LANES), low16, jnp.int32)
        _, v, cnt_v = lax.while_loop(unsettled, body, (jnp.int32(0), v0, cnt_start))
        return v, cnt_v

    n_all = jnp.zeros((tq, 1), F32) + (nk * tk).astype(F32)
    v_hi, n_from_hi = bit_search(hi_scr, 0.0, n_all, False)
    n_above = count_ge(hi_scr, v_hi + 1)
    v_hi_t = lanes(v_hi.astype(i16))

    def keep_equal(kb, carry):
        lo_scr[kb] = jnp.where(hi_scr[kb] == v_hi_t, lo_scr[kb], i16(low16))
        return carry

    lax.fori_loop(0, nk, keep_equal, 0)
    v_lo, n_sel = bit_search(lo_scr, n_above, n_from_hi, True)
    v_lo = jnp.where(v_hi == low16, jnp.maximum(v_lo, low16 + 1), v_lo)
    v_lo_t = lanes(v_lo.astype(i16))
    zero = jnp.zeros((tq, tk), bias_ref.dtype)
    neg = jnp.full((tq, tk), NEG_BIG, bias_ref.dtype)
    has_ties = jnp.max(jnp.where(n_sel > kf, 1.0, 0.0)) > 0.0

    @pl.when(jnp.logical_not(has_ties))
    def _():
        def emit(kb, carry):
            h16 = hi_scr[kb]
            sel = (h16 > v_hi_t) | ((h16 == v_hi_t) & (lo_scr[kb] >= v_lo_t))
            bias_ref[0, kb] = jnp.where(sel, zero, neg)
            return carry

        lax.fori_loop(0, nk, emit, 0)

    @pl.when(has_ties)
    def _():
        top16 = 2 ** 15 - 1
        n_gt = n_above + jnp.where(v_lo[:, :1] >= top16, 0.0, count_ge(lo_scr, jnp.minimum(v_lo + 1, top16)))
        need = kf - n_gt
        earlier = jnp.where(lax.broadcasted_iota(jnp.int32, (tk, tk), 0)
                            < lax.broadcasted_iota(jnp.int32, (tk, tk), 1), 1.0, 0.0).astype(BF16)
        one_b = jnp.ones((tq, tk), BF16)
        zero_b = jnp.zeros((tq, tk), BF16)

        def emit_ties(kb, seen):
            h16 = hi_scr[kb]
            l16 = lo_scr[kb]
            same_hi = h16 == v_hi_t
            larger = (h16 > v_hi_t) | (same_hi & (l16 > v_lo_t))
            tied = jnp.where(same_hi & (l16 == v_lo_t), one_b, zero_b)
            rank = seen + _dot(tied, earlier)
            keep = jnp.where(rank < need, 1.0, 0.0).astype(BF16)
            sel = larger | ((tied > 0.0) & (keep > 0.0))
            bias_ref[0, kb] = jnp.where(sel, zero, neg)
            return seen + jnp.sum(tied.astype(F32), axis=1, keepdims=True)

        lax.fori_loop(0, nk, emit_ties, jnp.zeros((tq, 1), F32))

    def fill(kb, carry):
        bias_ref[0, kb] = neg
        return carry

    lax.fori_loop(nk, nkt, fill, 0)


def _indexer(main, small, lay, *, s, tq, tk, topk):
    nkt = s // tk
    hi = IDX_HEADS
    group = min(tq, 64)
    qi0, ki0 = lay["qi"][0], lay["ki"][0]
    assert qi0 % hi == 0
    return pl.pallas_call(
        functools.partial(_idx_kernel, tq=tq, tk=tk, nkt=nkt, hi=hi, group=group, topk=topk,
                          wscale=IDX_HEADS ** -0.5 * IDX_DIM ** -0.5),
        grid=(s // tq,),
        in_specs=[pl.BlockSpec((hi, tq, LANES), lambda i: (qi0 // hi, i, 0)),
                  pl.BlockSpec((1, s, LANES), lambda i: (ki0, 0, 0)),
                  pl.BlockSpec((tq, LANES), lambda i: (i, 0))],
        out_specs=pl.BlockSpec((1, nkt, tq, tk), lambda i: (i, 0, 0, 0)),
        out_shape=jax.ShapeDtypeStruct((s // tq, nkt, tq, tk), BF16),
        scratch_shapes=[pltpu.VMEM((nkt, tq, tk), jnp.int16),
                        pltpu.VMEM((nkt, tq, tk), jnp.int16),
                        pltpu.VMEM((hi, tq, LANES), F32)],
        compiler_params=_params("parallel"),
        name="indexer",
    )(main, main, small)


def _attn_kernel(qi_ref, kb_ref, q_ref, k_ref, v_ref, b_ref, g_ref, o_ref, bias_scr, m_scr, acc_scr, *pipe,
                 nh, tq, tk):
    step = pl.program_id(0)
    qi = qi_ref[step]
    kb = kb_ref[step]

    @pl.when(kb == 0)
    def _():
        m_scr[...] = jnp.full(m_scr.shape, NEG_BIG, F32)
        acc_scr[...] = jnp.zeros(acc_scr.shape, F32)

    na, nb, sq, sk = b_ref.shape
    for a in range(na):
        for b in range(nb):
            bias_scr[a * sq:(a + 1) * sq, b * sk:(b + 1) * sk] = b_ref[a, b].astype(F32)

    ones = jnp.ones((tk, LANES), BF16)

    def qk(h, s_ref):
        s_ref[...] = _dot_nt(q_ref[h], k_ref[h])

    def soft(h, s_ref, p_ref, al_ref):
        s = s_ref[...] + bias_scr[...]
        m_prev = m_scr[h]
        m_new = jnp.maximum(m_prev, jnp.max(s, axis=1, keepdims=True))
        m_scr[h] = m_new
        p_ref[...] = jnp.exp2(s - m_new[:, :1]).astype(BF16)
        al_ref[...] = jnp.exp2(m_prev - m_new)

    def pv(h, p_ref, al_ref):
        al = al_ref[...]
        v_ext = jnp.concatenate([v_ref[h], ones], axis=1)
        acc_scr[h] = acc_scr[h] * jnp.concatenate([al, al], axis=1) + _dot(p_ref[...], v_ext)

    s_a, s_b, p_a, p_b, al_a, al_b = pipe
    qk(0, s_a)
    qk(1, s_b)
    soft(0, s_a, p_a, al_a)

    def pair(j, carry):
        qk(2 * j, s_a)
        soft(2 * j - 1, s_b, p_b, al_b)
        pv(2 * j - 2, p_a, al_a)
        qk(2 * j + 1, s_b)
        soft(2 * j, s_a, p_a, al_a)
        pv(2 * j - 1, p_b, al_b)
        return carry

    lax.fori_loop(1, nh // 2, pair, 0)
    soft(nh - 1, s_b, p_b, al_b)
    pv(nh - 2, p_a, al_a)
    pv(nh - 1, p_b, al_b)

    @pl.when(kb == (qi * tq + tq - 1) // tk)
    def _():
        ss = jnp.zeros((tq, LANES), F32)
        for h in range(nh):
            o_h = acc_scr[h, :, :LANES] / acc_scr[h, :, LANES:]
            acc_scr[h, :, :LANES] = o_h
            ss = ss + o_h * o_h
        ms = jnp.sum(ss, axis=1, keepdims=True) * (1.0 / (nh * LANES))
        r = lax.rsqrt(ms + RMS_EPS)
        for h in range(nh):
            sl = slice(h * LANES, (h + 1) * LANES)
            o_ref[:, sl] = (acc_scr[h, :, :LANES] * r * g_ref[:, sl]).astype(o_ref.dtype)


def _attention(main, bias, g, lay, *, s, tq, tk):
    nh = A_HEADS
    nqb, nkt, sq, sk = bias.shape
    q0, k0, v0 = lay["qa"][0], lay["ka"][0], lay["va"][0]
    assert q0 % nh == 0 and k0 % nh == 0 and v0 % nh == 0
    pairs = [(i, j) for i in range(s // tq) for j in range((i * tq + tq - 1) // tk + 1)]
    qi_tab = jnp.asarray(np.asarray([p[0] for p in pairs], np.int32))
    kb_tab = jnp.asarray(np.asarray([p[1] for p in pairs], np.int32))
    return pl.pallas_call(
        functools.partial(_attn_kernel, nh=nh, tq=tq, tk=tk),
        grid_spec=pltpu.PrefetchScalarGridSpec(
            num_scalar_prefetch=2,
            grid=(len(pairs),),
            in_specs=[pl.BlockSpec((nh, tq, LANES), lambda t, qi, kb: (q0 // nh, qi[t], 0)),
                      pl.BlockSpec((nh, tk, LANES), lambda t, qi, kb: (k0 // nh, kb[t], 0)),
                      pl.BlockSpec((nh, tk, LANES), lambda t, qi, kb: (v0 // nh, kb[t], 0)),
                      pl.BlockSpec((tq // sq, tk // sk, sq, sk), lambda t, qi, kb: (qi[t], kb[t], 0, 0)),
                      pl.BlockSpec((1, nh * LANES), lambda t, qi, kb: (0, 0))],
            out_specs=pl.BlockSpec((tq, nh * LANES), lambda t, qi, kb: (qi[t], 0)),
            scratch_shapes=[pltpu.VMEM((tq, tk), F32),
                            pltpu.VMEM((nh, tq, LANES), F32),
                            pltpu.VMEM((nh, tq, 2 * LANES), F32),
                            pltpu.VMEM((tq, tk), F32), pltpu.VMEM((tq, tk), F32),
                            pltpu.VMEM((tq, tk), BF16), pltpu.VMEM((tq, tk), BF16),
                            pltpu.VMEM((tq, LANES), F32), pltpu.VMEM((tq, LANES), F32)]),
        out_shape=jax.ShapeDtypeStruct((s, nh * LANES), BF16),
        compiler_params=_params("arbitrary"),
        name="attention",
    )(qi_tab, kb_tab, main, main, main, bias, g.reshape(1, nh * LANES))


def _gla_kernel(q_ref, k_ref, v_ref, gg_ref, sm_ref, w2_ref, bgk_ref, gn_ref, o_ref, st_scr, *,
                rows, chunk, nheads, dk, dv):
    @pl.when(pl.program_id(0) == 0)
    def _():
        st_scr[...] = jnp.zeros(st_scr.shape, F32)

    r_i = lax.broadcasted_iota(jnp.int32, (chunk, chunk), 0)
    c_i = lax.broadcasted_iota(jnp.int32, (chunk, chunk), 1)
    tri = jnp.where(r_i >= c_i, 1.0, 0.0).astype(BF16)
    nbk, nbv = dk // LANES, dv // LANES
    assert chunk >= 16 and chunk & (chunk - 1) == 0
    leaf = 2
    half_sizes = [chunk >> i for i in range(1, chunk.bit_length()) if chunk >> i >= leaf]
    quadrant = {m: (r_i // (2 * m) == c_i // (2 * m)) & (r_i % (2 * m) >= m) & (c_i % (2 * m) < m)
                for m in half_sizes}
    near = [(c_i == r_i - j) & (r_i % leaf >= j) for j in range(leaf)]
    sub = lax.broadcasted_iota(jnp.int32, (chunk // 8, 8, dk), 1)

    def reference_rows(b, m):
        if m >= 8:
            return jnp.concatenate(
                [jnp.broadcast_to(b[blk + m:blk + m + 1, :], (2 * m, dk)) for blk in range(0, chunk, 2 * m)],
                axis=0)
        b3 = b.reshape(chunk // 8, 8, dk)
        out = None
        for blk in range(0, 8, 2 * m):
            rows = jnp.broadcast_to(b3[:, blk + m:blk + m + 1, :], b3.shape)
            out = rows if out is None else jnp.where(sub >= blk, rows, out)
        return out.reshape(chunk, dk)

    def wide(ref, first, n, r0):
        return jnp.concatenate([ref[first + j, pl.ds(r0, chunk), :] for j in range(n)], axis=1)

    def step(c, carry):
        r0 = pl.multiple_of(c * chunk, chunk)
        z_all = _dot(sm_ref[pl.ds(r0, chunk), :].astype(BF16), w2_ref[...]) + bgk_ref[...]
        for h in range(nheads):
            q = wide(q_ref, h * nbk, nbk, r0).astype(F32) * (dk ** -0.5)
            k = wide(k_ref, h * nbk, nbk, r0).astype(F32)
            v = wide(v_ref, h * nbv, nbv, r0)
            z = z_all[:, h * dk:(h + 1) * dk]
            g = (jnp.minimum(z, 0.0) - jnp.log1p(jnp.exp(-jnp.abs(z)))) * (1.0 / G_TAU)
            g_hi = g.astype(BF16)
            rem = g - g_hi.astype(F32)
            g_mid = rem.astype(BF16)
            g_lo = (rem - g_mid.astype(F32)).astype(BF16)
            b = _dot(tri, g_hi) + _dot(tri, g_mid) + _dot(tri, g_lo)
            b_last = b[chunk - 1:chunk, :]
            att = jnp.zeros((chunk, chunk), F32)
            for m in half_sizes:
                b_ref_rows = reference_rows(b, m)
                qe = (q * jnp.exp(jnp.minimum(b - b_ref_rows, 0.0))).astype(BF16)
                ke = (k * jnp.exp(jnp.minimum(b_ref_rows - b, 0.0))).astype(BF16)
                att = att + jnp.where(quadrant[m], _dot_nt(qe, ke), 0.0)
            for j in range(leaf):
                kj = k if j == 0 else pltpu.roll(k, j, 0)
                bj = b if j == 0 else pltpu.roll(b, j, 0)
                pair = jnp.sum(q * kj * jnp.exp(jnp.minimum(b - bj, 0.0)), axis=1, keepdims=True)
                att = att + jnp.where(near[j], pair, 0.0)
            st = st_scr[h]
            o = _dot(att.astype(BF16), v) + _dot_nt((q * jnp.exp(b)).astype(BF16), st.astype(BF16))
            kd = (k * jnp.exp(b_last - b)).astype(BF16)
            st_scr[h] = st * jnp.exp(b_last) + _dot_tn(v, kd)
            ms = jnp.mean(o * o, axis=1, keepdims=True)
            gate = wide(gg_ref, h * nbv, nbv, r0).astype(F32)
            out = (o * lax.rsqrt(ms + RMS_EPS)) * gn_ref[...] * (gate * jax.nn.sigmoid(gate))
            o_ref[pl.ds(r0, chunk), h * dv:(h + 1) * dv] = out.astype(o_ref.dtype)
        return carry

    lax.fori_loop(0, rows // chunk, step, 0)


def _gla(main, small, w2, bgk, gn, lay, *, s, rows, chunk, dk, dv):
    nh = G_HEADS
    nqk, nv = nh * dk // LANES, nh * dv // LANES
    q0, k0, v0, g0 = lay["qg"][0], lay["kg"][0], lay["vg"][0], lay["gg"][0]
    assert q0 % nqk == 0 and k0 % nqk == 0 and v0 % nv == 0 and g0 % nv == 0
    return pl.pallas_call(
        functools.partial(_gla_kernel, rows=rows, chunk=chunk, nheads=nh, dk=dk, dv=dv),
        grid=(s // rows,),
        in_specs=[pl.BlockSpec((nqk, rows, LANES), lambda r: (q0 // nqk, r, 0)),
                  pl.BlockSpec((nqk, rows, LANES), lambda r: (k0 // nqk, r, 0)),
                  pl.BlockSpec((nv, rows, LANES), lambda r: (v0 // nv, r, 0)),
                  pl.BlockSpec((nv, rows, LANES), lambda r: (g0 // nv, r, 0)),
                  pl.BlockSpec((rows, LANES), lambda r: (r, 0)),
                  pl.BlockSpec((LANES, nh * dk), lambda r: (0, 0)),
                  pl.BlockSpec((1, nh * dk), lambda r: (0, 0)),
                  pl.BlockSpec((1, dv), lambda r: (0, 0))],
        out_specs=pl.BlockSpec((rows, nh * dv), lambda r: (r, 0)),
        out_shape=jax.ShapeDtypeStruct((s, nh * dv), BF16),
        scratch_shapes=[pltpu.VMEM((nh, dv, dk), F32)],
        compiler_params=_params("arbitrary"),
        name="gla",
    )(main, main, main, main, small, w2, bgk, gn.reshape(1, dv))


def _ffn_up_kernel(h_ref, wg_ref, wv_ref, cwg_ref, cwv_ref, cbg_ref, cbv_ref, o_ref, w_scr, tail_scr,
                   *, tm, tn, rc):
    @pl.when(pl.program_id(1) == 0)
    def _():
        w_scr[:, :tn] = wg_ref[0].astype(BF16)
        w_scr[:, tn:] = wv_ref[0].astype(BF16)
        tail_scr[...] = jnp.zeros(tail_scr.shape, F32)

    cw = jnp.concatenate([cwg_ref[0], cwv_ref[0]], axis=1)
    cb = jnp.concatenate([cbg_ref[...], cbv_ref[...]], axis=1)
    w0, w1, w2 = cw[0:1, :], cw[1:2, :], cw[2:3, :]

    def taps(a):
        return w2 * a + w1 * pltpu.roll(a, 1, 0) + w0 * pltpu.roll(a, 2, 0) + cb

    tail = tail_scr[...]
    for r in range(tm // rc):
        rows = slice(r * rc, (r + 1) * rc)
        hw = _dot(h_ref[rows, :], w_scr[...])
        top = taps(jnp.concatenate([tail, hw[0:8]], axis=0))[8:16]
        u = jnp.concatenate([top, taps(hw)[8:]], axis=0)
        gate, val = u[:, :tn], u[:, tn:]
        o_ref[rows, :] = (gate * jax.nn.sigmoid(gate) * val).astype(o_ref.dtype)
        tail = hw[rc - 8:rc, :]
    tail_scr[...] = tail


def _ffn_up(hb, w_up, conv_w, conv_b, *, d_ff, tm, tn):
    s, d = hb.shape
    nj = d_ff // tn
    return pl.pallas_call(
        functools.partial(_ffn_up_kernel, tm=tm, tn=tn, rc=min(tm, 512)),
        grid=(nj, s // tm),
        in_specs=[pl.BlockSpec((tm, d), lambda j, i: (i, 0)),
                  pl.BlockSpec((1, d, tn), lambda j, i: (0, 0, j)),
                  pl.BlockSpec((1, d, tn), lambda j, i: (0, 0, nj + j)),
                  pl.BlockSpec((1, CONV_W, tn), lambda j, i: (0, 0, j)),
                  pl.BlockSpec((1, CONV_W, tn), lambda j, i: (0, 0, nj + j)),
                  pl.BlockSpec((1, tn), lambda j, i: (0, j)),
                  pl.BlockSpec((1, tn), lambda j, i: (0, nj + j))],
        out_specs=pl.BlockSpec((tm, tn), lambda j, i: (i, j)),
        out_shape=jax.ShapeDtypeStruct((s, d_ff), BF16),
        scratch_shapes=[pltpu.VMEM((d, 2 * tn), BF16), pltpu.VMEM((8, 2 * tn), F32)],
        compiler_params=_params("parallel", "arbitrary"),
        name="ffn_up",
    )(hb, w_up, w_up, conv_w, conv_w, conv_b, conv_b)


def _layout(d_model):
    a_width = A_HEADS * A_HEAD_DIM
    g_width = d_model - a_width
    g_kwidth = g_width // 2
    names = ("qa", "ka", "va", "qi", "ki", "wi", "qg", "kg", "vg", "glr", "gg")
    sizes = (a_width, a_width, a_width, IDX_HEADS * IDX_DIM, IDX_DIM, IDX_HEADS,
             g_kwidth, g_kwidth, g_width, G_LOWRANK, g_width)
    offs = np.concatenate([[0], np.cumsum(sizes)])
    src = {n: (int(offs[i]), int(offs[i + 1])) for i, n in enumerate(names)}
    order = ("qi", "qa", "ka", "va", "qg", "kg", "vg", "gg", "ki")
    lay, blk = {}, 0
    for n in order:
        width = src[n][1] - src[n][0]
        assert width % LANES == 0
        lay[n] = (blk, width // LANES)
        blk += width // LANES
    return src, order, lay, blk


def kernel(x, w_in, w_gk2, b_gk, attn_out_g, gla_norm_g, w_o, ln1_g, ln1_b, w_up, conv_w, conv_b,
           w_down, ln2_g, ln2_b):
    assert x.shape[0] == 1 and w_in.shape[0] == DEPTH == 1
    _, s, d = x.shape
    x2 = x[0]
    src, order, lay, nb_main = _layout(d)
    g_width = d - A_HEADS * A_HEAD_DIM
    dv = g_width // G_HEADS
    dk = dv // 2
    d_ff = w_down.shape[1]
    topk = min(TOPK_MAX, s // 4)

    w_proj = _relayout_w_in(jnp.swapaxes(w_in, 1, 2), src, order)
    n_small = IDX_HEADS + G_LOWRANK
    w2 = jnp.zeros((LANES, G_HEADS * dk), F32).at[IDX_HEADS:n_small].set(w_gk2[0]).astype(BF16)
    bgk = b_gk[0].reshape(1, G_HEADS * dk)
    wd = w_down[0].astype(BF16)
    n_main = (nb_main + 1) * LANES
    q_lo, q_hi = lay["qa"][0] * LANES, (lay["qa"][0] + lay["qa"][1]) * LANES
    col = np.ones((1, n_main), np.float32)
    col[:, q_lo:q_hi] = A_HEAD_DIM ** -0.5 * LOG2E
    col_scale = jnp.asarray(col)

    tm = _tile(s, 1024)
    main, small = _proj_blocks(x2, w_proj, col_scale, tm=_tile(s, 512), tn=_tile(n_main, 1280))
    bias = _indexer(main, small, lay, s=s, tq=256, tk=256, topk=topk)
    ta = _tile(s, 512)
    o_a = _attention(main, bias, attn_out_g[0], lay, s=s, tq=ta, tk=ta)
    o_g = _gla(main, small, w2, bgk, gla_norm_g[0], lay, s=s, rows=_tile(s, 512), chunk=64, dk=dk, dv=dv)
    y1 = _w_o(o_a, o_g, w_o, x2, tm=tm, tn=_tile(d, 512), alpha=DN_ALPHA)
    h, hb = _layer_norm(y1, ln1_g[0], ln1_b[0], tr=_tile(s, 256, 8), also_bf16=True)

    act = _ffn_up(hb, w_up, conv_w, conv_b[0].reshape(1, 2 * d_ff), d_ff=d_ff, tm=tm, tn=_tile(d_ff, 256))
    y2 = _matmul(act, wd, tm=_tile(s, 512), tn=_tile(d, 512), tk=d_ff, out_dtype=F32, residual=h,
                 alpha=DN_ALPHA, name="w_down")
    (out,) = _layer_norm(y2, ln2_g[0], ln2_b[0], tr=_tile(s, 256, 8), also_bf16=False)
    return out[None]
```

```python
import functools

import numpy as np
import jax
import jax.numpy as jnp
from jax import lax
from jax.experimental import pallas as pl
from jax.experimental.pallas import tpu as pltpu

A_HEADS = 16
A_HEAD_DIM = 128
IDX_HEADS = 32
IDX_DIM = 128
TOPK_MAX = 256
G_HEADS = 4
G_LOWRANK = 16
G_TAU = 16.0
CONV_W = 3
LN_EPS = 1e-5
RMS_EPS = 1e-6
DEPTH = 1
DN_ALPHA = (2 * DEPTH) ** 0.25

LANES = 128
VMEM_LIMIT = 56 * 1024 * 1024

BF16 = jnp.bfloat16
F32 = jnp.float32
NEG_BIG = -1e30
LOG2E = 1.4426950408889634
INT_MIN = -2 ** 31
COUNT_UNROLL = 4


def _params(*sem):
    return pltpu.CompilerParams(dimension_semantics=sem, vmem_limit_bytes=VMEM_LIMIT)


def _tile(n, max_tile, quantum=LANES):
    best = None
    for t in range(quantum, min(n, max_tile) + 1, quantum):
        if n % t == 0:
            best = t
    assert best is not None, (n, max_tile, quantum)
    return best


def _dot(a, b):
    return jnp.dot(a, b, preferred_element_type=F32)


def _dot_nt(a, b):
    return lax.dot_general(a, b, (((1,), (1,)), ((), ())), preferred_element_type=F32)


def _dot_tn(a, b):
    return lax.dot_general(a, b, (((0,), (0,)), ((), ())), preferred_element_type=F32)


def _relayout_kernel(off_ref, kind_ref, a_ref, b_ref, o_ref, *, n_wi, n_small):
    del off_ref
    kind = kind_ref[pl.program_id(0)]

    @pl.when(kind == 0)
    def _():
        o_ref[...] = a_ref[0].astype(BF16)

    @pl.when(kind == 1)
    def _():
        o_ref[...] = jnp.zeros(o_ref.shape, BF16)

    @pl.when(kind == 2)
    def _():
        r = lax.broadcasted_iota(jnp.int32, o_ref.shape, 0)
        o_ref[...] = jnp.where(r < n_wi, a_ref[0], jnp.where(r < n_small, b_ref[0], 0.0)).astype(BF16)


def _relayout_w_in(w_t, src, order):
    _, n_in, d = w_t.shape
    starts = [src[n][0] + t * LANES for n in order for t in range((src[n][1] - src[n][0]) // LANES)]
    wi0, glr0 = src["wi"][0], src["glr"][0]
    n_wi = src["wi"][1] - wi0
    n_small = n_wi + src["glr"][1] - glr0
    glr_win = glr0 - n_wi
    assert n_small <= LANES and glr_win >= 0 and wi0 + LANES <= n_in and glr_win + LANES <= n_in
    assert all(o % 8 == 0 for o in starts + [wi0, glr_win])
    row_off8 = [o // 8 for o in starts + [0, wi0]]
    kind = [0] * len(starts) + [1, 2]
    nb = len(kind)
    tables = [jnp.asarray(np.asarray(t, np.int32)) for t in (row_off8, kind)]
    return pl.pallas_call(
        functools.partial(_relayout_kernel, n_wi=n_wi, n_small=n_small),
        grid_spec=pltpu.PrefetchScalarGridSpec(
            num_scalar_prefetch=2,
            grid=(nb,),
            in_specs=[pl.BlockSpec((pl.Element(1), pl.Element(LANES), pl.Element(d)),
                                   lambda b, off, kd: (0, off[b] * 8, 0)),
                      pl.BlockSpec((pl.Element(1), pl.Element(LANES), pl.Element(d)),
                                   lambda b, off, kd: (0, glr_win, 0))],
            out_specs=pl.BlockSpec((LANES, d), lambda b, off, kd: (b, 0))),
        out_shape=jax.ShapeDtypeStruct((nb * LANES, d), BF16),
        compiler_params=_params("arbitrary"),
        name="relayout_w_in",
    )(*tables, w_t, w_t)


def _proj_kernel(x_ref, w_ref, ws_ref, sc_ref, o_ref, os_ref, xb_scr):
    @pl.when(pl.program_id(1) == 0)
    def _():
        xb_scr[...] = x_ref[...].astype(BF16)
        os_ref[...] = _dot_nt(xb_scr[...], ws_ref[...])

    res = _dot_nt(xb_scr[...], w_ref[...]) * sc_ref[...]
    for c in range(o_ref.shape[0]):
        o_ref[c] = res[:, c * LANES:(c + 1) * LANES].astype(o_ref.dtype)


def _proj_blocks(x, wt, col_scale, *, tm, tn):
    s, k = x.shape
    n = wt.shape[0] - LANES
    return pl.pallas_call(
        _proj_kernel,
        grid=(s // tm, n // tn),
        in_specs=[pl.BlockSpec((tm, k), lambda i, j: (i, 0)),
                  pl.BlockSpec((tn, k), lambda i, j: (j, 0)),
                  pl.BlockSpec((LANES, k), lambda i, j: (n // LANES, 0)),
                  pl.BlockSpec((1, tn), lambda i, j: (0, j))],
        out_specs=[pl.BlockSpec((tn // LANES, tm, LANES), lambda i, j: (j, i, 0)),
                   pl.BlockSpec((tm, LANES), lambda i, j: (i, 0))],
        out_shape=[jax.ShapeDtypeStruct((n // LANES, s, LANES), BF16),
                   jax.ShapeDtypeStruct((s, LANES), F32)],
        scratch_shapes=[pltpu.VMEM((tm, k), BF16)],
        compiler_params=_params("parallel", "arbitrary"),
        name="proj",
    )(x, wt, wt, col_scale)


def _mm_kernel(a_ref, b_ref, *rest, alpha, has_res, nk):
    if has_res:
        r_ref, o_ref = rest[0], rest[1]
        rest = rest[2:]
    else:
        r_ref, o_ref = None, rest[0]
        rest = rest[1:]

    def finish(acc):
        if has_res:
            acc = acc + alpha * r_ref[...]
        o_ref[...] = acc.astype(o_ref.dtype)

    if nk == 1:
        finish(_dot(a_ref[...], b_ref[...]))
        return
    acc_ref = rest[0]
    kk = pl.program_id(2)

    @pl.when(kk == 0)
    def _():
        acc_ref[...] = _dot(a_ref[...], b_ref[...])

    @pl.when(kk > 0)
    def _():
        acc_ref[...] += _dot(a_ref[...], b_ref[...])

    @pl.when(kk == nk - 1)
    def _():
        finish(acc_ref[...])


def _matmul(a, b, *, tm, tn, tk, out_dtype, residual=None, alpha=1.0, name="matmul"):
    m, k = a.shape
    n = b.shape[1]
    nk = k // tk
    in_specs = [pl.BlockSpec((tm, tk), lambda i, j, kk: (i, kk)),
                pl.BlockSpec((tk, tn), lambda i, j, kk: (kk, j))]
    args = [a, b]
    if residual is not None:
        in_specs.append(pl.BlockSpec((tm, tn), lambda i, j, kk: (i, j)))
        args.append(residual)
    scratch = [pltpu.VMEM((tm, tn), F32)] if nk > 1 else []
    return pl.pallas_call(
        functools.partial(_mm_kernel, alpha=alpha, has_res=residual is not None, nk=nk),
        grid=(m // tm, n // tn, nk),
        in_specs=in_specs,
        out_specs=pl.BlockSpec((tm, tn), lambda i, j, kk: (i, j)),
        out_shape=jax.ShapeDtypeStruct((m, n), out_dtype),
        scratch_shapes=scratch,
        compiler_params=_params("parallel", "parallel", "arbitrary"),
        name=name,
    )(*args)


def _wo_kernel(oa_ref, og_ref, wt_ref, wb_ref, x_ref, o_ref, w_scr, *, ka, alpha):
    @pl.when(pl.program_id(1) == 0)
    def _():
        w_scr[:ka, :] = wt_ref[0].astype(BF16)
        w_scr[ka:, :] = wb_ref[0].astype(BF16)

    acc = _dot(oa_ref[...], w_scr[:ka, :]) + _dot(og_ref[...], w_scr[ka:, :])
    o_ref[...] = acc + alpha * x_ref[...]


def _w_o(o_a, o_g, w_o, x, *, tm, tn, alpha):
    s, ka = o_a.shape
    kg = o_g.shape[1]
    d = w_o.shape[2]
    assert ka % kg == 0
    return pl.pallas_call(
        functools.partial(_wo_kernel, ka=ka, alpha=alpha),
        grid=(d // tn, s // tm),
        in_specs=[pl.BlockSpec((tm, ka), lambda j, i: (i, 0)),
                  pl.BlockSpec((tm, kg), lambda j, i: (i, 0)),
                  pl.BlockSpec((1, ka, tn), lambda j, i: (0, 0, j)),
                  pl.BlockSpec((1, kg, tn), lambda j, i: (0, ka // kg, j)),
                  pl.BlockSpec((tm, tn), lambda j, i: (i, j))],
        out_specs=pl.BlockSpec((tm, tn), lambda j, i: (i, j)),
        out_shape=jax.ShapeDtypeStruct((s, d), F32),
        scratch_shapes=[pltpu.VMEM((ka + kg, tn), BF16)],
        compiler_params=_params("parallel", "arbitrary"),
        name="w_o",
    )(o_a, o_g, w_o, w_o, x)


def _ln_kernel(y_ref, g_ref, b_ref, *o_refs):
    y = y_ref[...]
    mu = jnp.mean(y, axis=1, keepdims=True)
    yc = y - mu
    var = jnp.mean(yc * yc, axis=1, keepdims=True)
    out = yc * lax.rsqrt(var + LN_EPS) * g_ref[...] + b_ref[...]
    for o_ref in o_refs:
        o_ref[...] = out.astype(o_ref.dtype)


def _layer_norm(y, g, b, *, tr, also_bf16):
    s, d = y.shape
    out_shape = [jax.ShapeDtypeStruct((s, d), F32)]
    out_specs = [pl.BlockSpec((tr, d), lambda i: (i, 0))]
    if also_bf16:
        out_shape.append(jax.ShapeDtypeStruct((s, d), BF16))
        out_specs.append(pl.BlockSpec((tr, d), lambda i: (i, 0)))
    return pl.pallas_call(
        _ln_kernel,
        grid=(s // tr,),
        in_specs=[pl.BlockSpec((tr, d), lambda i: (i, 0)),
                  pl.BlockSpec((1, d), lambda i: (0, 0)),
                  pl.BlockSpec((1, d), lambda i: (0, 0))],
        out_specs=out_specs,
        out_shape=out_shape,
        compiler_params=_params("parallel"),
        name="layer_norm",
    )(y, g.reshape(1, d), b.reshape(1, d))


def _idx_kernel(qi_ref, ki_ref, sm_ref, bias_ref, hi_scr, lo_scr, wb_scr, *, tq, tk, nkt, hi, group,
                topk, wscale):
    qb = pl.program_id(0)
    nk = ((qb + 1) * tq + tk - 1) // tk
    reps = tk // LANES
    i16 = jnp.int16
    low16 = -2 ** 15

    for h in range(hi):
        wb_scr[h] = jnp.broadcast_to(sm_ref[:, h:h + 1] * wscale, (tq, LANES))

    row = qb * tq + lax.broadcasted_iota(jnp.int32, (group, tk), 0)
    col = lax.broadcasted_iota(jnp.int32, (group, tk), 1)

    def lanes(a):
        return jnp.concatenate([a] * reps, axis=1) if reps > 1 else a

    def fold(a):
        out = a[:, :LANES]
        for r in range(1, reps):
            out = out + a[:, r * LANES:(r + 1) * LANES]
        return out

    def score_tile(kb, carry):
        kt = ki_ref[0, pl.ds(pl.multiple_of(kb * tk, tk), tk), :]
        for r in range(tq // group):
            rows = slice(r * group, (r + 1) * group)
            qs = qi_ref[:, rows, :].reshape(hi * group, IDX_DIM)
            d = _dot_nt(qs, kt)
            acc = jnp.zeros((group, tk), F32)
            for h in range(hi):
                acc = acc + jnp.maximum(d[h * group:(h + 1) * group], 0.0) * lanes(wb_scr[h, rows, :])
            bits = pltpu.bitcast(acc + 0.0, jnp.int32)
            key = jnp.where(kb * tk + col <= row + r * group,
                            bits ^ ((bits >> 31) & jnp.int32(0x7FFFFFFF)), INT_MIN)
            hi_scr[kb, rows, :] = (key >> 16).astype(i16)
            lo_scr[kb, rows, :] = ((key & 0xFFFF) + low16).astype(i16)
        return carry

    lax.fori_loop(0, nk, score_tile, 0)

    def count_ge(ref, cand):
        cand_t = lanes(cand.astype(i16))

        def step(kb, c):
            return c + fold(jnp.where(ref[kb] >= cand_t, i16(1), i16(0)))

        def step_many(q, c):
            for u in range(COUNT_UNROLL):
                c = step(q * COUNT_UNROLL + u, c)
            return c

        n_many = nk // COUNT_UNROLL
        c = lax.fori_loop(0, n_many, step_many, jnp.zeros((tq, LANES), i16))
        c = lax.fori_loop(n_many * COUNT_UNROLL, nk, step, c)
        return jnp.sum(c.astype(F32), axis=1, keepdims=True)

    kf = float(topk)

    def bit_search(ref, base_cnt, cnt_start, stop_when_exact):
        def body(state):
            it, v, cnt_v = state
            cand = v + jnp.left_shift(jnp.int32(1), 15 - it)
            cnt = base_cnt + count_ge(ref, cand)
            ok = cnt >= kf
            return it + 1, jnp.where(ok, cand, v), jnp.where(ok, cnt, cnt_v)

        def unsettled(state):
            go = state[0] < 16
            if stop_when_exact:
                go = go & (jnp.max(jnp.where(state[2] == kf, 0.0, 1.0)) > 0.0)
            return go

        v0 = jnp.full((tq, LANES), low16, jnp.int32)
        _, v, cnt_v = lax.while_loop(unsettled, body, (jnp.int32(0), v0, cnt_start))
        return v, cnt_v

    n_all = jnp.zeros((tq, 1), F32) + (nk * tk).astype(F32)
    v_hi, n_from_hi = bit_search(hi_scr, 0.0, n_all, False)
    n_above = count_ge(hi_scr, v_hi + 1)
    v_hi_t = lanes(v_hi.astype(i16))

    def keep_equal(kb, carry):
        lo_scr[kb] = jnp.where(hi_scr[kb] == v_hi_t, lo_scr[kb], i16(low16))
        return carry

    lax.fori_loop(0, nk, keep_equal, 0)
    v_lo, n_sel = bit_search(lo_scr, n_above, n_from_hi, True)
    v_lo = jnp.where(v_hi == low16, jnp.maximum(v_lo, low16 + 1), v_lo)
    v_lo_t = lanes(v_lo.astype(i16))
    zero = jnp.zeros((tq, tk), bias_ref.dtype)
    neg = jnp.full((tq, tk), NEG_BIG, bias_ref.dtype)
    has_ties = jnp.max(jnp.where(n_sel > kf, 1.0, 0.0)) > 0.0

    @pl.when(jnp.logical_not(has_ties))
    def _():
        def emit(kb, carry):
            h16 = hi_scr[kb]
            sel = (h16 > v_hi_t) | ((h16 == v_hi_t) & (lo_scr[kb] >= v_lo_t))
            bias_ref[0, kb] = jnp.where(sel, zero, neg)
            return carry

        lax.fori_loop(0, nk, emit, 0)

    @pl.when(has_ties)
    def _():
        top16 = 2 ** 15 - 1
        n_gt = n_above + jnp.where(v_lo[:, :1] >= top16, 0.0, count_ge(lo_scr, jnp.minimum(v_lo + 1, top16)))
        need = kf - n_gt
        earlier = jnp.where(lax.broadcasted_iota(jnp.int32, (tk, tk), 0)
                            < lax.broadcasted_iota(jnp.int32, (tk, tk), 1), 1.0, 0.0).astype(BF16)
        one_b = jnp.ones((tq, tk), BF16)
        zero_b = jnp.zeros((tq, tk), BF16)

        def emit_ties(kb, seen):
            h16 = hi_scr[kb]
            l16 = lo_scr[kb]
            same_hi = h16 == v_hi_t
            larger = (h16 > v_hi_t) | (same_hi & (l16 > v_lo_t))
            tied = jnp.where(same_hi & (l16 == v_lo_t), one_b, zero_b)
            rank = seen + _dot(tied, earlier)
            keep = jnp.where(rank < need, 1.0, 0.0).astype(BF16)
            sel = larger | ((tied > 0.0) & (keep > 0.0))
            bias_ref[0, kb] = jnp.where(sel, zero, neg)
            return seen + jnp.sum(tied.astype(F32), axis=1, keepdims=True)

        lax.fori_loop(0, nk, emit_ties, jnp.zeros((tq, 1), F32))

    def fill(kb, carry):
        bias_ref[0, kb] = neg
        return carry

    lax.fori_loop(nk, nkt, fill, 0)


def _indexer(main, small, lay, *, s, tq, tk, topk):
    nkt = s // tk
    hi = IDX_HEADS
    group = min(tq, 64)
    qi0, ki0 = lay["qi"][0], lay["ki"][0]
    assert qi0 % hi == 0
    return pl.pallas_call(
        functools.partial(_idx_kernel, tq=tq, tk=tk, nkt=nkt, hi=hi, group=group, topk=topk,
                          wscale=IDX_HEADS ** -0.5 * IDX_DIM ** -0.5),
        grid=(s // tq,),
        in_specs=[pl.BlockSpec((hi, tq, LANES), lambda i: (qi0 // hi, i, 0)),
                  pl.BlockSpec((1, s, LANES), lambda i: (ki0, 0, 0)),
                  pl.BlockSpec((tq, LANES), lambda i: (i, 0))],
        out_specs=pl.BlockSpec((1, nkt, tq, tk), lambda i: (i, 0, 0, 0)),
        out_shape=jax.ShapeDtypeStruct((s // tq, nkt, tq, tk), BF16),
        scratch_shapes=[pltpu.VMEM((nkt, tq, tk), jnp.int16),
                        pltpu.VMEM((nkt, tq, tk), jnp.int16),
                        pltpu.VMEM((hi, tq, LANES), F32)],
        compiler_params=_params("parallel"),
        name="indexer",
    )(main, main, small)


def _attn_kernel(qi_ref, kb_ref, q_ref, k_ref, v_ref, b_ref, g_ref, o_ref, bias_scr, m_scr, acc_scr, *pipe,
                 nh, tq, tk):
    step = pl.program_id(0)
    qi = qi_ref[step]
    kb = kb_ref[step]

    @pl.when(kb == 0)
    def _():
        m_scr[...] = jnp.full(m_scr.shape, NEG_BIG, F32)
        acc_scr[...] = jnp.zeros(acc_scr.shape, F32)

    na, nb, sq, sk = b_ref.shape
    for a in range(na):
        for b in range(nb):
            bias_scr[a * sq:(a + 1) * sq, b * sk:(b + 1) * sk] = b_ref[a, b].astype(F32)

    ones = jnp.ones((tk, LANES), BF16)

    def qk(h, s_ref):
        s_ref[...] = _dot_nt(q_ref[h], k_ref[h])

    def soft(h, s_ref, p_ref, al_ref):
        s = s_ref[...] + bias_scr[...]
        m_prev = m_scr[h]
        m_new = jnp.maximum(m_prev, jnp.max(s, axis=1, keepdims=True))
        m_scr[h] = m_new
        p_ref[...] = jnp.exp2(s - m_new[:, :1]).astype(BF16)
        al_ref[...] = jnp.exp2(m_prev - m_new)

    def pv(h, p_ref, al_ref):
        al = al_ref[...]
        v_ext = jnp.concatenate([v_ref[h], ones], axis=1)
        acc_scr[h] = acc_scr[h] * jnp.concatenate([al, al], axis=1) + _dot(p_ref[...], v_ext)

    s_a, s_b, p_a, p_b, al_a, al_b = pipe
    qk(0, s_a)
    qk(1, s_b)
    soft(0, s_a, p_a, al_a)

    def pair(j, carry):
        qk(2 * j, s_a)
        soft(2 * j - 1, s_b, p_b, al_b)
        pv(2 * j - 2, p_a, al_a)
        qk(2 * j + 1, s_b)
        soft(2 * j, s_a, p_a, al_a)
        pv(2 * j - 1, p_b, al_b)
        return carry

    lax.fori_loop(1, nh // 2, pair, 0)
    soft(nh - 1, s_b, p_b, al_b)
    pv(nh - 2, p_a, al_a)
    pv(nh - 1, p_b, al_b)

    @pl.when(kb == (qi * tq + tq - 1) // tk)
    def _():
        ss = jnp.zeros((tq, LANES), F32)
        for h in range(nh):
            o_h = acc_scr[h, :, :LANES] / acc_scr[h, :, LANES:]
            acc_scr[h, :, :LANES] = o_h
            ss = ss + o_h * o_h
        ms = jnp.sum(ss, axis=1, keepdims=True) * (1.0 / (nh * LANES))
        r = lax.rsqrt(ms + RMS_EPS)
        for h in range(nh):
            sl = slice(h * LANES, (h + 1) * LANES)
            o_ref[:, sl] = (acc_scr[h, :, :LANES] * r * g_ref[:, sl]).astype(o_ref.dtype)


def _attention(main, bias, g, lay, *, s, tq, tk):
    nh = A_HEADS
    nqb, nkt, sq, sk = bias.shape
    q0, k0, v0 = lay["qa"][0], lay["ka"][0], lay["va"][0]
    assert q0 % nh == 0 and k0 % nh == 0 and v0 % nh == 0
    pairs = [(i, j) for i in range(s // tq) for j in range((i * tq + tq - 1) // tk + 1)]
    qi_tab = jnp.asarray(np.asarray([p[0] for p in pairs], np.int32))
    kb_tab = jnp.asarray(np.asarray([p[1] for p in pairs], np.int32))
    return pl.pallas_call(
        functools.partial(_attn_kernel, nh=nh, tq=tq, tk=tk),
        grid_spec=pltpu.PrefetchScalarGridSpec(
            num_scalar_prefetch=2,
            grid=(len(pairs),),
            in_specs=[pl.BlockSpec((nh, tq, LANES), lambda t, qi, kb: (q0 // nh, qi[t], 0)),
                      pl.BlockSpec((nh, tk, LANES), lambda t, qi, kb: (k0 // nh, kb[t], 0)),
                      pl.BlockSpec((nh, tk, LANES), lambda t, qi, kb: (v0 // nh, kb[t], 0)),
                      pl.BlockSpec((tq // sq, tk // sk, sq, sk), lambda t, qi, kb: (qi[t], kb[t], 0, 0)),
                      pl.BlockSpec((1, nh * LANES), lambda t, qi, kb: (0, 0))],
            out_specs=pl.BlockSpec((tq, nh * LANES), lambda t, qi, kb: (qi[t], 0)),
            scratch_shapes=[pltpu.VMEM((tq, tk), F32),
                            pltpu.VMEM((nh, tq, LANES), F32),
                            pltpu.VMEM((nh, tq, 2 * LANES), F32),
                            pltpu.VMEM((tq, tk), F32), pltpu.VMEM((tq, tk), F32),
                            pltpu.VMEM((tq, tk), BF16), pltpu.VMEM((tq, tk), BF16),
                            pltpu.VMEM((tq, LANES), F32), pltpu.VMEM((tq, LANES), F32)]),
        out_shape=jax.ShapeDtypeStruct((s, nh * LANES), BF16),
        compiler_params=_params("arbitrary"),
        name="attention",
    )(qi_tab, kb_tab, main, main, main, bias, g.reshape(1, nh * LANES))


def _gla_kernel(q_ref, k_ref, v_ref, gg_ref, sm_ref, w2_ref, bgk_ref, gn_ref, o_ref, st_scr, *,
                rows, chunk, nheads, dk, dv):
    @pl.when(pl.program_id(0) == 0)
    def _():
        st_scr[...] = jnp.zeros(st_scr.shape, F32)

    r_i = lax.broadcasted_iota(jnp.int32, (chunk, chunk), 0)
    c_i = lax.broadcasted_iota(jnp.int32, (chunk, chunk), 1)
    tri = jnp.where(r_i >= c_i, 1.0, 0.0).astype(BF16)
    nbk, nbv = dk // LANES, dv // LANES
    assert chunk >= 16 and chunk & (chunk - 1) == 0
    leaf = 2
    half_sizes = [chunk >> i for i in range(1, chunk.bit_length()) if chunk >> i >= leaf]
    quadrant = {m: (r_i // (2 * m) == c_i // (2 * m)) & (r_i % (2 * m) >= m) & (c_i % (2 * m) < m)
                for m in half_sizes}
    near = [(c_i == r_i - j) & (r_i % leaf >= j) for j in range(leaf)]
    sub = lax.broadcasted_iota(jnp.int32, (chunk // 8, 8, dk), 1)

    def reference_rows(b, m):
        if m >= 8:
            return jnp.concatenate(
                [jnp.broadcast_to(b[blk + m:blk + m + 1, :], (2 * m, dk)) for blk in range(0, chunk, 2 * m)],
                axis=0)
        b3 = b.reshape(chunk // 8, 8, dk)
        out = None
        for blk in range(0, 8, 2 * m):
            rows = jnp.broadcast_to(b3[:, blk + m:blk + m + 1, :], b3.shape)
            out = rows if out is None else jnp.where(sub >= blk, rows, out)
        return out.reshape(chunk, dk)

    def wide(ref, first, n, r0):
        return jnp.concatenate([ref[first + j, pl.ds(r0, chunk), :] for j in range(n)], axis=1)

    def step(c, carry):
        r0 = pl.multiple_of(c * chunk, chunk)
        z_all = _dot(sm_ref[pl.ds(r0, chunk), :].astype(BF16), w2_ref[...]) + bgk_ref[...]
        for h in range(nheads):
            q = wide(q_ref, h * nbk, nbk, r0).astype(F32) * (dk ** -0.5)
            k = wide(k_ref, h * nbk, nbk, r0).astype(F32)
            v = wide(v_ref, h * nbv, nbv, r0)
            z = z_all[:, h * dk:(h + 1) * dk]
            g = (jnp.minimum(z, 0.0) - jnp.log1p(jnp.exp(-jnp.abs(z)))) * (1.0 / G_TAU)
            g_hi = g.astype(BF16)
            rem = g - g_hi.astype(F32)
            g_mid = rem.astype(BF16)
            g_lo = (rem - g_mid.astype(F32)).astype(BF16)
            b = _dot(tri, g_hi) + _dot(tri, g_mid) + _dot(tri, g_lo)
            b_last = b[chunk - 1:chunk, :]
            att = jnp.zeros((chunk, chunk), F32)
            for m in half_sizes:
                e = jnp.exp(-jnp.abs(b - reference_rows(b, m)))
                att = att + jnp.where(quadrant[m], _dot_nt((q * e).astype(BF16), (k * e).astype(BF16)), 0.0)
            for j in range(leaf):
                kj = k if j == 0 else pltpu.roll(k, j, 0)
                bj = b if j == 0 else pltpu.roll(b, j, 0)
                pair = jnp.sum(q * kj * jnp.exp(jnp.minimum(b - bj, 0.0)), axis=1, keepdims=True)
                att = att + jnp.where(near[j], pair, 0.0)
            st = st_scr[h]
            o = _dot(att.astype(BF16), v) + _dot_nt((q * jnp.exp(b)).astype(BF16), st.astype(BF16))
            kd = (k * jnp.exp(b_last - b)).astype(BF16)
            st_scr[h] = st * jnp.exp(b_last) + _dot_tn(v, kd)
            ms = jnp.mean(o * o, axis=1, keepdims=True)
            gate = wide(gg_ref, h * nbv, nbv, r0).astype(F32)
            out = (o * lax.rsqrt(ms + RMS_EPS)) * gn_ref[...] * (gate * jax.nn.sigmoid(gate))
            o_ref[pl.ds(r0, chunk), h * dv:(h + 1) * dv] = out.astype(o_ref.dtype)
        return carry

    lax.fori_loop(0, rows // chunk, step, 0)


def _gla(main, small, w2, bgk, gn, lay, *, s, rows, chunk, dk, dv):
    nh = G_HEADS
    nqk, nv = nh * dk // LANES, nh * dv // LANES
    q0, k0, v0, g0 = lay["qg"][0], lay["kg"][0], lay["vg"][0], lay["gg"][0]
    assert q0 % nqk == 0 and k0 % nqk == 0 and v0 % nv == 0 and g0 % nv == 0
    return pl.pallas_call(
        functools.partial(_gla_kernel, rows=rows, chunk=chunk, nheads=nh, dk=dk, dv=dv),
        grid=(s // rows,),
        in_specs=[pl.BlockSpec((nqk, rows, LANES), lambda r: (q0 // nqk, r, 0)),
                  pl.BlockSpec((nqk, rows, LANES), lambda r: (k0 // nqk, r, 0)),
                  pl.BlockSpec((nv, rows, LANES), lambda r: (v0 // nv, r, 0)),
                  pl.BlockSpec((nv, rows, LANES), lambda r: (g0 // nv, r, 0)),
                  pl.BlockSpec((rows, LANES), lambda r: (r, 0)),
                  pl.BlockSpec((LANES, nh * dk), lambda r: (0, 0)),
                  pl.BlockSpec((1, nh * dk), lambda r: (0, 0)),
                  pl.BlockSpec((1, dv), lambda r: (0, 0))],
        out_specs=pl.BlockSpec((rows, nh * dv), lambda r: (r, 0)),
        out_shape=jax.ShapeDtypeStruct((s, nh * dv), BF16),
        scratch_shapes=[pltpu.VMEM((nh, dv, dk), F32)],
        compiler_params=_params("arbitrary"),
        name="gla",
    )(main, main, main, main, small, w2, bgk, gn.reshape(1, dv))


def _ffn_up_kernel(h_ref, wg_ref, wv_ref, cwg_ref, cwv_ref, cbg_ref, cbv_ref, o_ref, w_scr, hw_scr,
                   *, tm, tn, bounds):
    @pl.when(pl.program_id(1) == 0)
    def _():
        w_scr[:, :tn] = wg_ref[0].astype(BF16)
        w_scr[:, tn:] = wv_ref[0].astype(BF16)
        hw_scr[0:8, :] = jnp.zeros((8, 2 * tn), F32)

    cw = jnp.concatenate([cwg_ref[0], cwv_ref[0]], axis=1)
    cb = jnp.concatenate([cbg_ref[...], cbv_ref[...]], axis=1)
    w0, w1, w2 = cw[0:1, :], cw[1:2, :], cw[2:3, :]

    for lo, hi in zip(bounds[:-1], bounds[1:]):
        hw = _dot(h_ref[lo:hi, :], w_scr[...])
        hw_scr[8 + lo:8 + hi, :] = hw
        u = w2 * hw + w1 * hw_scr[7 + lo:7 + hi, :] + w0 * hw_scr[6 + lo:6 + hi, :] + cb
        gate, val = u[:, :tn], u[:, tn:]
        o_ref[lo:hi, :] = (gate * jax.nn.sigmoid(gate) * val).astype(o_ref.dtype)
    hw_scr[0:8, :] = hw_scr[tm:tm + 8, :]


def _ffn_up(hb, w_up, conv_w, conv_b, *, d_ff, tm, tn):
    s, d = hb.shape
    nj = d_ff // tn
    return pl.pallas_call(
        functools.partial(_ffn_up_kernel, tm=tm, tn=tn, bounds=(0, tm // 2, tm)),
        grid=(nj, s // tm),
        in_specs=[pl.BlockSpec((tm, d), lambda j, i: (i, 0)),
                  pl.BlockSpec((1, d, tn), lambda j, i: (0, 0, j)),
                  pl.BlockSpec((1, d, tn), lambda j, i: (0, 0, nj + j)),
                  pl.BlockSpec((1, CONV_W, tn), lambda j, i: (0, 0, j)),
                  pl.BlockSpec((1, CONV_W, tn), lambda j, i: (0, 0, nj + j)),
                  pl.BlockSpec((1, tn), lambda j, i: (0, j)),
                  pl.BlockSpec((1, tn), lambda j, i: (0, nj + j))],
        out_specs=pl.BlockSpec((tm, tn), lambda j, i: (i, j)),
        out_shape=jax.ShapeDtypeStruct((s, d_ff), BF16),
        scratch_shapes=[pltpu.VMEM((d, 2 * tn), BF16), pltpu.VMEM((8 + tm, 2 * tn), F32)],
        compiler_params=_params("parallel", "arbitrary"),
        name="ffn_up",
    )(hb, w_up, w_up, conv_w, conv_w, conv_b, conv_b)


def _layout(d_model):
    a_width = A_HEADS * A_HEAD_DIM
    g_width = d_model - a_width
    g_kwidth = g_width // 2
    names = ("qa", "ka", "va", "qi", "ki", "wi", "qg", "kg", "vg", "glr", "gg")
    sizes = (a_width, a_width, a_width, IDX_HEADS * IDX_DIM, IDX_DIM, IDX_HEADS,
             g_kwidth, g_kwidth, g_width, G_LOWRANK, g_width)
    offs = np.concatenate([[0], np.cumsum(sizes)])
    src = {n: (int(offs[i]), int(offs[i + 1])) for i, n in enumerate(names)}
    order = ("qi", "qa", "ka", "va", "qg", "kg", "vg", "gg", "ki")
    lay, blk = {}, 0
    for n in order:
        width = src[n][1] - src[n][0]
        assert width % LANES == 0
        lay[n] = (blk, width // LANES)
        blk += width // LANES
    return src, order, lay, blk


def kernel(x, w_in, w_gk2, b_gk, attn_out_g, gla_norm_g, w_o, ln1_g, ln1_b, w_up, conv_w, conv_b,
           w_down, ln2_g, ln2_b):
    assert x.shape[0] == 1 and w_in.shape[0] == DEPTH == 1
    _, s, d = x.shape
    x2 = x[0]
    src, order, lay, nb_main = _layout(d)
    g_width = d - A_HEADS * A_HEAD_DIM
    dv = g_width // G_HEADS
    dk = dv // 2
    d_ff = w_down.shape[1]
    topk = min(TOPK_MAX, s // 4)

    w_proj = _relayout_w_in(jnp.swapaxes(w_in, 1, 2), src, order)
    n_small = IDX_HEADS + G_LOWRANK
    w2 = jnp.zeros((LANES, G_HEADS * dk), F32).at[IDX_HEADS:n_small].set(w_gk2[0]).astype(BF16)
    bgk = b_gk[0].reshape(1, G_HEADS * dk)
    wd = w_down[0].astype(BF16)
    n_main = (nb_main + 1) * LANES
    q_lo, q_hi = lay["qa"][0] * LANES, (lay["qa"][0] + lay["qa"][1]) * LANES
    col = np.ones((1, n_main), np.float32)
    col[:, q_lo:q_hi] = A_HEAD_DIM ** -0.5 * LOG2E
    col_scale = jnp.asarray(col)

    tm = _tile(s, 1024)
    main, small = _proj_blocks(x2, w_proj, col_scale, tm=_tile(s, 512), tn=_tile(n_main, 1280))
    bias = _indexer(main, small, lay, s=s, tq=256, tk=256, topk=topk)
    ta = _tile(s, 512)
    o_a = _attention(main, bias, attn_out_g[0], lay, s=s, tq=ta, tk=ta)
    o_g = _gla(main, small, w2, bgk, gla_norm_g[0], lay, s=s, rows=_tile(s, 512), chunk=64, dk=dk, dv=dv)
    y1 = _w_o(o_a, o_g, w_o, x2, tm=tm, tn=_tile(d, 512), alpha=DN_ALPHA)
    h, hb = _layer_norm(y1, ln1_g[0], ln1_b[0], tr=_tile(s, 256, 8), also_bf16=True)

    act = _ffn_up(hb, w_up, conv_w, conv_b[0].reshape(1, 2 * d_ff), d_ff=d_ff, tm=tm, tn=_tile(d_ff, 256))
    y2 = _matmul(act, wd, tm=_tile(s, 512), tn=_tile(d, 512), tk=d_ff, out_dtype=F32, residual=h,
                 alpha=DN_ALPHA, name="w_down")
    (out,) = _layer_norm(y2, ln2_g[0], ln2_b[0], tr=_tile(s, 256, 8), also_bf16=False)
    return out[None]
```

```python
import functools

import numpy as np
import jax
import jax.numpy as jnp
from jax import lax
from jax.experimental import pallas as pl
from jax.experimental.pallas import tpu as pltpu

A_HEADS = 16
A_HEAD_DIM = 128
IDX_HEADS = 32
IDX_DIM = 128
TOPK_MAX = 256
G_HEADS = 4
G_LOWRANK = 16
G_TAU = 16.0
CONV_W = 3
LN_EPS = 1e-5
RMS_EPS = 1e-6
DEPTH = 1
DN_ALPHA = (2 * DEPTH) ** 0.25

LANES = 128
VMEM_LIMIT = 56 * 1024 * 1024

BF16 = jnp.bfloat16
F32 = jnp.float32
NEG_BIG = -1e30
LOG2E = 1.4426950408889634
INT_MIN = -2 ** 31
COUNT_UNROLL = (8, 2, 1)


def _params(*sem):
    return pltpu.CompilerParams(dimension_semantics=sem, vmem_limit_bytes=VMEM_LIMIT)


def _tile(n, max_tile, quantum=LANES):
    best = None
    for t in range(quantum, min(n, max_tile) + 1, quantum):
        if n % t == 0:
            best = t
    assert best is not None, (n, max_tile, quantum)
    return best


def _dot(a, b):
    return jnp.dot(a, b, preferred_element_type=F32)


def _dot_nt(a, b):
    return lax.dot_general(a, b, (((1,), (1,)), ((), ())), preferred_element_type=F32)


def _dot_tn(a, b):
    return lax.dot_general(a, b, (((0,), (0,)), ((), ())), preferred_element_type=F32)


def _relayout_kernel(off_ref, kind_ref, a_ref, b_ref, o_ref, *, n_wi, n_small):
    del off_ref
    kind = kind_ref[pl.program_id(0)]

    @pl.when(kind == 0)
    def _():
        o_ref[...] = a_ref[0].astype(BF16)

    @pl.when(kind == 1)
    def _():
        o_ref[...] = jnp.zeros(o_ref.shape, BF16)

    @pl.when(kind == 2)
    def _():
        r = lax.broadcasted_iota(jnp.int32, o_ref.shape, 0)
        o_ref[...] = jnp.where(r < n_wi, a_ref[0], jnp.where(r < n_small, b_ref[0], 0.0)).astype(BF16)


def _relayout_w_in(w_t, src, order):
    _, n_in, d = w_t.shape
    starts = [src[n][0] + t * LANES for n in order for t in range((src[n][1] - src[n][0]) // LANES)]
    wi0, glr0 = src["wi"][0], src["glr"][0]
    n_wi = src["wi"][1] - wi0
    n_small = n_wi + src["glr"][1] - glr0
    glr_win = glr0 - n_wi
    assert n_small <= LANES and glr_win >= 0 and wi0 + LANES <= n_in and glr_win + LANES <= n_in
    assert all(o % 8 == 0 for o in starts + [wi0, glr_win])
    row_off8 = [o // 8 for o in starts + [0, wi0]]
    kind = [0] * len(starts) + [1, 2]
    nb = len(kind)
    tables = [jnp.asarray(np.asarray(t, np.int32)) for t in (row_off8, kind)]
    return pl.pallas_call(
        functools.partial(_relayout_kernel, n_wi=n_wi, n_small=n_small),
        grid_spec=pltpu.PrefetchScalarGridSpec(
            num_scalar_prefetch=2,
            grid=(nb,),
            in_specs=[pl.BlockSpec((pl.Element(1), pl.Element(LANES), pl.Element(d)),
                                   lambda b, off, kd: (0, off[b] * 8, 0)),
                      pl.BlockSpec((pl.Element(1), pl.Element(LANES), pl.Element(d)),
                                   lambda b, off, kd: (0, glr_win, 0))],
            out_specs=pl.BlockSpec((LANES, d), lambda b, off, kd: (b, 0))),
        out_shape=jax.ShapeDtypeStruct((nb * LANES, d), BF16),
        compiler_params=_params("arbitrary"),
        name="relayout_w_in",
    )(*tables, w_t, w_t)


def _proj_kernel(x_ref, w_ref, ws_ref, sc_ref, o_ref, os_ref, xb_scr):
    @pl.when(pl.program_id(1) == 0)
    def _():
        xb_scr[...] = x_ref[...].astype(BF16)
        os_ref[...] = _dot_nt(xb_scr[...], ws_ref[...])

    res = _dot_nt(xb_scr[...], w_ref[...]) * sc_ref[...]
    for c in range(o_ref.shape[0]):
        o_ref[c] = res[:, c * LANES:(c + 1) * LANES].astype(o_ref.dtype)


def _proj_blocks(x, wt, col_scale, *, tm, tn):
    s, k = x.shape
    n = wt.shape[0] - LANES
    return pl.pallas_call(
        _proj_kernel,
        grid=(s // tm, n // tn),
        in_specs=[pl.BlockSpec((tm, k), lambda i, j: (i, 0)),
                  pl.BlockSpec((tn, k), lambda i, j: (j, 0)),
                  pl.BlockSpec((LANES, k), lambda i, j: (n // LANES, 0)),
                  pl.BlockSpec((1, tn), lambda i, j: (0, j))],
        out_specs=[pl.BlockSpec((tn // LANES, tm, LANES), lambda i, j: (j, i, 0)),
                   pl.BlockSpec((tm, LANES), lambda i, j: (i, 0))],
        out_shape=[jax.ShapeDtypeStruct((n // LANES, s, LANES), BF16),
                   jax.ShapeDtypeStruct((s, LANES), F32)],
        scratch_shapes=[pltpu.VMEM((tm, k), BF16)],
        compiler_params=_params("parallel", "arbitrary"),
        name="proj",
    )(x, wt, wt, col_scale)


def _mm_kernel(a_ref, b_ref, *rest, alpha, has_res, nk):
    if has_res:
        r_ref, o_ref = rest[0], rest[1]
        rest = rest[2:]
    else:
        r_ref, o_ref = None, rest[0]
        rest = rest[1:]

    def finish(acc):
        if has_res:
            acc = acc + alpha * r_ref[...]
        o_ref[...] = acc.astype(o_ref.dtype)

    if nk == 1:
        finish(_dot(a_ref[...], b_ref[...]))
        return
    acc_ref = rest[0]
    kk = pl.program_id(2)

    @pl.when(kk == 0)
    def _():
        acc_ref[...] = _dot(a_ref[...], b_ref[...])

    @pl.when(kk > 0)
    def _():
        acc_ref[...] += _dot(a_ref[...], b_ref[...])

    @pl.when(kk == nk - 1)
    def _():
        finish(acc_ref[...])


def _matmul(a, b, *, tm, tn, tk, out_dtype, residual=None, alpha=1.0, name="matmul"):
    m, k = a.shape
    n = b.shape[1]
    nk = k // tk
    in_specs = [pl.BlockSpec((tm, tk), lambda i, j, kk: (i, kk)),
                pl.BlockSpec((tk, tn), lambda i, j, kk: (kk, j))]
    args = [a, b]
    if residual is not None:
        in_specs.append(pl.BlockSpec((tm, tn), lambda i, j, kk: (i, j)))
        args.append(residual)
    scratch = [pltpu.VMEM((tm, tn), F32)] if nk > 1 else []
    return pl.pallas_call(
        functools.partial(_mm_kernel, alpha=alpha, has_res=residual is not None, nk=nk),
        grid=(m // tm, n // tn, nk),
        in_specs=in_specs,
        out_specs=pl.BlockSpec((tm, tn), lambda i, j, kk: (i, j)),
        out_shape=jax.ShapeDtypeStruct((m, n), out_dtype),
        scratch_shapes=scratch,
        compiler_params=_params("parallel", "parallel", "arbitrary"),
        name=name,
    )(*args)


def _wo_kernel(oa_ref, og_ref, wt_ref, wb_ref, x_ref, o_ref, w_scr, *, ka, alpha):
    @pl.when(pl.program_id(1) == 0)
    def _():
        w_scr[:ka, :] = wt_ref[0].astype(BF16)
        w_scr[ka:, :] = wb_ref[0].astype(BF16)

    acc = _dot(oa_ref[...], w_scr[:ka, :]) + _dot(og_ref[...], w_scr[ka:, :])
    o_ref[...] = acc + alpha * x_ref[...]


def _w_o(o_a, o_g, w_o, x, *, tm, tn, alpha):
    s, ka = o_a.shape
    kg = o_g.shape[1]
    d = w_o.shape[2]
    assert ka % kg == 0
    return pl.pallas_call(
        functools.partial(_wo_kernel, ka=ka, alpha=alpha),
        grid=(d // tn, s // tm),
        in_specs=[pl.BlockSpec((tm, ka), lambda j, i: (i, 0)),
                  pl.BlockSpec((tm, kg), lambda j, i: (i, 0)),
                  pl.BlockSpec((1, ka, tn), lambda j, i: (0, 0, j)),
                  pl.BlockSpec((1, kg, tn), lambda j, i: (0, ka // kg, j)),
                  pl.BlockSpec((tm, tn), lambda j, i: (i, j))],
        out_specs=pl.BlockSpec((tm, tn), lambda j, i: (i, j)),
        out_shape=jax.ShapeDtypeStruct((s, d), F32),
        scratch_shapes=[pltpu.VMEM((ka + kg, tn), BF16)],
        compiler_params=_params("parallel", "arbitrary"),
        name="w_o",
    )(o_a, o_g, w_o, w_o, x)


def _ln_kernel(y_ref, g_ref, b_ref, *o_refs):
    y = y_ref[...]
    mu = jnp.mean(y, axis=1, keepdims=True)
    yc = y - mu
    var = jnp.mean(yc * yc, axis=1, keepdims=True)
    out = yc * lax.rsqrt(var + LN_EPS) * g_ref[...] + b_ref[...]
    for o_ref in o_refs:
        o_ref[...] = out.astype(o_ref.dtype)


def _layer_norm(y, g, b, *, tr, also_bf16):
    s, d = y.shape
    out_shape = [jax.ShapeDtypeStruct((s, d), F32)]
    out_specs = [pl.BlockSpec((tr, d), lambda i: (i, 0))]
    if also_bf16:
        out_shape.append(jax.ShapeDtypeStruct((s, d), BF16))
        out_specs.append(pl.BlockSpec((tr, d), lambda i: (i, 0)))
    return pl.pallas_call(
        _ln_kernel,
        grid=(s // tr,),
        in_specs=[pl.BlockSpec((tr, d), lambda i: (i, 0)),
                  pl.BlockSpec((1, d), lambda i: (0, 0)),
                  pl.BlockSpec((1, d), lambda i: (0, 0))],
        out_specs=out_specs,
        out_shape=out_shape,
        compiler_params=_params("parallel"),
        name="layer_norm",
    )(y, g.reshape(1, d), b.reshape(1, d))


def _idx_kernel(qi_ref, ki_ref, sm_ref, bias_ref, hi_scr, lo_scr, wb_scr, *, tq, tk, nkt, hi, group,
                topk, wscale):
    qb = pl.program_id(0)
    nk = ((qb + 1) * tq + tk - 1) // tk
    reps = tk // LANES
    i16 = jnp.int16
    low16 = -2 ** 15

    for h in range(hi):
        wb_scr[h] = jnp.broadcast_to(sm_ref[:, h:h + 1] * wscale, (tq, LANES))

    row = qb * tq + lax.broadcasted_iota(jnp.int32, (group, tk), 0)
    col = lax.broadcasted_iota(jnp.int32, (group, tk), 1)

    def lanes(a):
        return jnp.concatenate([a] * reps, axis=1) if reps > 1 else a

    def fold(a):
        out = a[:, :LANES]
        for r in range(1, reps):
            out = out + a[:, r * LANES:(r + 1) * LANES]
        return out

    def score_tile(kb, carry):
        kt = ki_ref[0, pl.ds(pl.multiple_of(kb * tk, tk), tk), :]
        for r in range(tq // group):
            rows = slice(r * group, (r + 1) * group)
            qs = qi_ref[:, rows, :].reshape(hi * group, IDX_DIM)
            d = _dot_nt(qs, kt)
            acc = jnp.zeros((group, tk), F32)
            for h in range(hi):
                acc = acc + jnp.maximum(d[h * group:(h + 1) * group], 0.0) * lanes(wb_scr[h, rows, :])
            bits = pltpu.bitcast(acc + 0.0, jnp.int32)
            key = jnp.where(kb * tk + col <= row + r * group,
                            bits ^ ((bits >> 31) & jnp.int32(0x7FFFFFFF)), INT_MIN)
            hi_scr[kb, rows, :] = (key >> 16).astype(i16)
            lo_scr[kb, rows, :] = ((key & 0xFFFF) + low16).astype(i16)
        return carry

    lax.fori_loop(0, nk, score_tile, 0)

    def count_ge(ref, cand):
        cand_t = lanes(cand.astype(i16))

        def step(kb, c):
            return c + fold(jnp.where(ref[kb] >= cand_t, i16(1), i16(0)))

        c = jnp.zeros((tq, LANES), i16)
        done = 0
        for width in COUNT_UNROLL:
            def step_many(q, c, width=width, done=done):
                for u in range(width):
                    c = step(done + q * width + u, c)
                return c

            n_groups = (nk - done) // width
            c = lax.fori_loop(0, n_groups, step_many, c)
            done = done + n_groups * width
        return jnp.sum(c.astype(F32), axis=1, keepdims=True)

    kf = float(topk)

    def bit_search(ref, base_cnt, cnt_start, stop_when_exact):
        def body(state):
            it, v, cnt_v = state
            cand = v + jnp.left_shift(jnp.int32(1), 15 - it)
            cnt = base_cnt + count_ge(ref, cand)
            ok = cnt >= kf
            return it + 1, jnp.where(ok, cand, v), jnp.where(ok, cnt, cnt_v)

        def unsettled(state):
            go = state[0] < 16
            if stop_when_exact:
                go = go & (jnp.max(jnp.where(state[2] == kf, 0.0, 1.0)) > 0.0)
            return go

        v0 = jnp.full((tq, LANES), low16, jnp.int32)
        _, v, cnt_v = lax.while_loop(unsettled, body, (jnp.int32(0), v0, cnt_start))
        return v, cnt_v

    n_all = jnp.zeros((tq, 1), F32) + (nk * tk).astype(F32)
    v_hi, n_from_hi = bit_search(hi_scr, 0.0, n_all, False)
    n_above = count_ge(hi_scr, v_hi + 1)
    v_hi_t = lanes(v_hi.astype(i16))

    def keep_equal(kb, carry):
        lo_scr[kb] = jnp.where(hi_scr[kb] == v_hi_t, lo_scr[kb], i16(low16))
        return carry

    lax.fori_loop(0, nk, keep_equal, 0)
    v_lo, n_sel = bit_search(lo_scr, n_above, n_from_hi, True)
    v_lo = jnp.where(v_hi == low16, jnp.maximum(v_lo, low16 + 1), v_lo)
    v_lo_t = lanes(v_lo.astype(i16))
    zero = jnp.zeros((tq, tk), bias_ref.dtype)
    neg = jnp.full((tq, tk), NEG_BIG, bias_ref.dtype)
    has_ties = jnp.max(jnp.where(n_sel > kf, 1.0, 0.0)) > 0.0

    @pl.when(jnp.logical_not(has_ties))
    def _():
        def emit(kb, carry):
            h16 = hi_scr[kb]
            sel = (h16 > v_hi_t) | ((h16 == v_hi_t) & (lo_scr[kb] >= v_lo_t))
            bias_ref[0, kb] = jnp.where(sel, zero, neg)
            return carry

        lax.fori_loop(0, nk, emit, 0)

    @pl.when(has_ties)
    def _():
        top16 = 2 ** 15 - 1
        n_gt = n_above + jnp.where(v_lo[:, :1] >= top16, 0.0, count_ge(lo_scr, jnp.minimum(v_lo + 1, top16)))
        need = kf - n_gt
        earlier = jnp.where(lax.broadcasted_iota(jnp.int32, (tk, tk), 0)
                            < lax.broadcasted_iota(jnp.int32, (tk, tk), 1), 1.0, 0.0).astype(BF16)
        one_b = jnp.ones((tq, tk), BF16)
        zero_b = jnp.zeros((tq, tk), BF16)

        def emit_ties(kb, seen):
            h16 = hi_scr[kb]
            l16 = lo_scr[kb]
            same_hi = h16 == v_hi_t
            larger = (h16 > v_hi_t) | (same_hi & (l16 > v_lo_t))
            tied = jnp.where(same_hi & (l16 == v_lo_t), one_b, zero_b)
            rank = seen + _dot(tied, earlier)
            keep = jnp.where(rank < need, 1.0, 0.0).astype(BF16)
            sel = larger | ((tied > 0.0) & (keep > 0.0))
            bias_ref[0, kb] = jnp.where(sel, zero, neg)
            return seen + jnp.sum(tied.astype(F32), axis=1, keepdims=True)

        lax.fori_loop(0, nk, emit_ties, jnp.zeros((tq, 1), F32))

    def fill(kb, carry):
        bias_ref[0, kb] = neg
        return carry

    lax.fori_loop(nk, nkt, fill, 0)


def _indexer(main, small, lay, *, s, tq, tk, topk):
    nkt = s // tk
    hi = IDX_HEADS
    group = min(tq, 64)
    qi0, ki0 = lay["qi"][0], lay["ki"][0]
    assert qi0 % hi == 0
    return pl.pallas_call(
        functools.partial(_idx_kernel, tq=tq, tk=tk, nkt=nkt, hi=hi, group=group, topk=topk,
                          wscale=IDX_HEADS ** -0.5 * IDX_DIM ** -0.5),
        grid=(s // tq,),
        in_specs=[pl.BlockSpec((hi, tq, LANES), lambda i: (qi0 // hi, i, 0)),
                  pl.BlockSpec((1, s, LANES), lambda i: (ki0, 0, 0)),
                  pl.BlockSpec((tq, LANES), lambda i: (i, 0))],
        out_specs=pl.BlockSpec((1, nkt, tq, tk), lambda i: (i, 0, 0, 0)),
        out_shape=jax.ShapeDtypeStruct((s // tq, nkt, tq, tk), BF16),
        scratch_shapes=[pltpu.VMEM((nkt, tq, tk), jnp.int16),
                        pltpu.VMEM((nkt, tq, tk), jnp.int16),
                        pltpu.VMEM((hi, tq, LANES), F32)],
        compiler_params=_params("parallel"),
        name="indexer",
    )(main, main, small)


def _attn_kernel(qi_ref, kb_ref, q_ref, k_ref, v_ref, b_ref, g_ref, o_ref, bias_scr, m_scr, acc_scr, *pipe,
                 nh, tq, tk):
    step = pl.program_id(0)
    qi = qi_ref[step]
    kb = kb_ref[step]

    @pl.when(kb == 0)
    def _():
        m_scr[...] = jnp.full(m_scr.shape, NEG_BIG, F32)
        acc_scr[...] = jnp.zeros(acc_scr.shape, F32)

    na, nb, sq, sk = b_ref.shape
    for a in range(na):
        for b in range(nb):
            bias_scr[a * sq:(a + 1) * sq, b * sk:(b + 1) * sk] = b_ref[a, b].astype(F32)

    ones = jnp.ones((tk, LANES), BF16)

    def qk(h, s_ref):
        s_ref[...] = _dot_nt(q_ref[h], k_ref[h])

    def soft(h, s_ref, p_ref, al_ref):
        s = s_ref[...] + bias_scr[...]
        m_prev = m_scr[h]
        m_new = jnp.maximum(m_prev, jnp.max(s, axis=1, keepdims=True))
        m_scr[h] = m_new
        p_ref[...] = jnp.exp2(s - m_new[:, :1]).astype(BF16)
        al_ref[...] = jnp.exp2(m_prev - m_new)

    def pv(h, p_ref, al_ref):
        al = al_ref[...]
        v_ext = jnp.concatenate([v_ref[h], ones], axis=1)
        acc_scr[h] = acc_scr[h] * jnp.concatenate([al, al], axis=1) + _dot(p_ref[...], v_ext)

    s_a, s_b, p_a, p_b, al_a, al_b = pipe
    qk(0, s_a)
    qk(1, s_b)
    soft(0, s_a, p_a, al_a)

    def pair(j, carry):
        qk(2 * j, s_a)
        soft(2 * j - 1, s_b, p_b, al_b)
        pv(2 * j - 2, p_a, al_a)
        qk(2 * j + 1, s_b)
        soft(2 * j, s_a, p_a, al_a)
        pv(2 * j - 1, p_b, al_b)
        return carry

    lax.fori_loop(1, nh // 2, pair, 0)
    soft(nh - 1, s_b, p_b, al_b)
    pv(nh - 2, p_a, al_a)
    pv(nh - 1, p_b, al_b)

    @pl.when(kb == (qi * tq + tq - 1) // tk)
    def _():
        ss = jnp.zeros((tq, LANES), F32)
        for h in range(nh):
            o_h = acc_scr[h, :, :LANES] / acc_scr[h, :, LANES:]
            acc_scr[h, :, :LANES] = o_h
            ss = ss + o_h * o_h
        ms = jnp.sum(ss, axis=1, keepdims=True) * (1.0 / (nh * LANES))
        r = lax.rsqrt(ms + RMS_EPS)
        for h in range(nh):
            sl = slice(h * LANES, (h + 1) * LANES)
            o_ref[:, sl] = (acc_scr[h, :, :LANES] * r * g_ref[:, sl]).astype(o_ref.dtype)


def _attention(main, bias, g, lay, *, s, tq, tk):
    nh = A_HEADS
    nqb, nkt, sq, sk = bias.shape
    q0, k0, v0 = lay["qa"][0], lay["ka"][0], lay["va"][0]
    assert q0 % nh == 0 and k0 % nh == 0 and v0 % nh == 0
    pairs = [(i, j) for i in range(s // tq) for j in range((i * tq + tq - 1) // tk + 1)]
    qi_tab = jnp.asarray(np.asarray([p[0] for p in pairs], np.int32))
    kb_tab = jnp.asarray(np.asarray([p[1] for p in pairs], np.int32))
    return pl.pallas_call(
        functools.partial(_attn_kernel, nh=nh, tq=tq, tk=tk),
        grid_spec=pltpu.PrefetchScalarGridSpec(
            num_scalar_prefetch=2,
            grid=(len(pairs),),
            in_specs=[pl.BlockSpec((nh, tq, LANES), lambda t, qi, kb: (q0 // nh, qi[t], 0)),
                      pl.BlockSpec((nh, tk, LANES), lambda t, qi, kb: (k0 // nh, kb[t], 0)),
                      pl.BlockSpec((nh, tk, LANES), lambda t, qi, kb: (v0 // nh, kb[t], 0)),
                      pl.BlockSpec((tq // sq, tk // sk, sq, sk), lambda t, qi, kb: (qi[t], kb[t], 0, 0)),
                      pl.BlockSpec((1, nh * LANES), lambda t, qi, kb: (0, 0))],
            out_specs=pl.BlockSpec((tq, nh * LANES), lambda t, qi, kb: (qi[t], 0)),
            scratch_shapes=[pltpu.VMEM((tq, tk), F32),
                            pltpu.VMEM((nh, tq, LANES), F32),
                            pltpu.VMEM((nh, tq, 2 * LANES), F32),
                            pltpu.VMEM((tq, tk), F32), pltpu.VMEM((tq, tk), F32),
                            pltpu.VMEM((tq, tk), BF16), pltpu.VMEM((tq, tk), BF16),
                            pltpu.VMEM((tq, LANES), F32), pltpu.VMEM((tq, LANES), F32)]),
        out_shape=jax.ShapeDtypeStruct((s, nh * LANES), BF16),
        compiler_params=_params("arbitrary"),
        name="attention",
    )(qi_tab, kb_tab, main, main, main, bias, g.reshape(1, nh * LANES))


def _gla_kernel(q_ref, k_ref, v_ref, gg_ref, sm_ref, w2_ref, bgk_ref, gn_ref, o_ref, st_scr, *,
                rows, chunk, nheads, dk, dv):
    @pl.when(pl.program_id(0) == 0)
    def _():
        st_scr[...] = jnp.zeros(st_scr.shape, F32)

    r_i = lax.broadcasted_iota(jnp.int32, (chunk, chunk), 0)
    c_i = lax.broadcasted_iota(jnp.int32, (chunk, chunk), 1)
    tri = jnp.where(r_i >= c_i, 1.0, 0.0).astype(BF16)
    nbk, nbv = dk // LANES, dv // LANES
    assert chunk >= 16 and chunk & (chunk - 1) == 0
    leaf = 2
    half_sizes = [chunk >> i for i in range(1, chunk.bit_length()) if chunk >> i >= leaf]
    quadrant = {m: (r_i // (2 * m) == c_i // (2 * m)) & (r_i % (2 * m) >= m) & (c_i % (2 * m) < m)
                for m in half_sizes}
    near = [(c_i == r_i - j) & (r_i % leaf >= j) for j in range(leaf)]
    sub = lax.broadcasted_iota(jnp.int32, (chunk // 8, 8, dk), 1)

    def reference_rows(b, m):
        if m >= 8:
            return jnp.concatenate(
                [jnp.broadcast_to(b[blk + m:blk + m + 1, :], (2 * m, dk)) for blk in range(0, chunk, 2 * m)],
                axis=0)
        b3 = b.reshape(chunk // 8, 8, dk)
        out = None
        for blk in range(0, 8, 2 * m):
            rows = jnp.broadcast_to(b3[:, blk + m:blk + m + 1, :], b3.shape)
            out = rows if out is None else jnp.where(sub >= blk, rows, out)
        return out.reshape(chunk, dk)

    def wide(ref, first, n, r0):
        return jnp.concatenate([ref[first + j, pl.ds(r0, chunk), :] for j in range(n)], axis=1)

    def step(c, carry):
        r0 = pl.multiple_of(c * chunk, chunk)
        z_all = _dot(sm_ref[pl.ds(r0, chunk), :].astype(BF16), w2_ref[...]) + bgk_ref[...]
        for h in range(nheads):
            q = wide(q_ref, h * nbk, nbk, r0).astype(F32) * (dk ** -0.5)
            k = wide(k_ref, h * nbk, nbk, r0).astype(F32)
            v = wide(v_ref, h * nbv, nbv, r0)
            z = z_all[:, h * dk:(h + 1) * dk]
            g = (jnp.minimum(z, 0.0) - jnp.log1p(jnp.exp(-jnp.abs(z)))) * (1.0 / G_TAU)
            g_hi = g.astype(BF16)
            rem = g - g_hi.astype(F32)
            g_mid = rem.astype(BF16)
            g_lo = (rem - g_mid.astype(F32)).astype(BF16)
            b = _dot(tri, g_hi) + _dot(tri, g_mid) + _dot(tri, g_lo)
            b_last = b[chunk - 1:chunk, :]
            att = jnp.zeros((chunk, chunk), F32)
            for m in half_sizes:
                e = jnp.exp(-jnp.abs(b - reference_rows(b, m)))
                att = att + jnp.where(quadrant[m], _dot_nt((q * e).astype(BF16), (k * e).astype(BF16)), 0.0)
            for j in range(leaf):
                kj = k if j == 0 else pltpu.roll(k, j, 0)
                bj = b if j == 0 else pltpu.roll(b, j, 0)
                pair = jnp.sum(q * kj * jnp.exp(jnp.minimum(b - bj, 0.0)), axis=1, keepdims=True)
                att = att + jnp.where(near[j], pair, 0.0)
            st = st_scr[h]
            o = _dot(att.astype(BF16), v) + _dot_nt((q * jnp.exp(b)).astype(BF16), st.astype(BF16))
            kd = (k * jnp.exp(b_last - b)).astype(BF16)
            st_scr[h] = st * jnp.exp(b_last) + _dot_tn(v, kd)
            ms = jnp.mean(o * o, axis=1, keepdims=True)
            gate = wide(gg_ref, h * nbv, nbv, r0).astype(F32)
            out = (o * lax.rsqrt(ms + RMS_EPS)) * gn_ref[...] * (gate * jax.nn.sigmoid(gate))
            o_ref[pl.ds(r0, chunk), h * dv:(h + 1) * dv] = out.astype(o_ref.dtype)
        return carry

    lax.fori_loop(0, rows // chunk, step, 0)


def _gla(main, small, w2, bgk, gn, lay, *, s, rows, chunk, dk, dv):
    nh = G_HEADS
    nqk, nv = nh * dk // LANES, nh * dv // LANES
    q0, k0, v0, g0 = lay["qg"][0], lay["kg"][0], lay["vg"][0], lay["gg"][0]
    assert q0 % nqk == 0 and k0 % nqk == 0 and v0 % nv == 0 and g0 % nv == 0
    return pl.pallas_call(
        functools.partial(_gla_kernel, rows=rows, chunk=chunk, nheads=nh, dk=dk, dv=dv),
        grid=(s // rows,),
        in_specs=[pl.BlockSpec((nqk, rows, LANES), lambda r: (q0 // nqk, r, 0)),
                  pl.BlockSpec((nqk, rows, LANES), lambda r: (k0 // nqk, r, 0)),
                  pl.BlockSpec((nv, rows, LANES), lambda r: (v0 // nv, r, 0)),
                  pl.BlockSpec((nv, rows, LANES), lambda r: (g0 // nv, r, 0)),
                  pl.BlockSpec((rows, LANES), lambda r: (r, 0)),
                  pl.BlockSpec((LANES, nh * dk), lambda r: (0, 0)),
                  pl.BlockSpec((1, nh * dk), lambda r: (0, 0)),
                  pl.BlockSpec((1, dv), lambda r: (0, 0))],
        out_specs=pl.BlockSpec((rows, nh * dv), lambda r: (r, 0)),
        out_shape=jax.ShapeDtypeStruct((s, nh * dv), BF16),
        scratch_shapes=[pltpu.VMEM((nh, dv, dk), F32)],
        compiler_params=_params("arbitrary"),
        name="gla",
    )(main, main, main, main, small, w2, bgk, gn.reshape(1, dv))


def _ffn_up_kernel(h_ref, wg_ref, wv_ref, cwg_ref, cwv_ref, cbg_ref, cbv_ref, o_ref, w_scr, hw_scr,
                   *, tm, tn, bounds):
    @pl.when(pl.program_id(1) == 0)
    def _():
        w_scr[:, :tn] = wg_ref[0].astype(BF16)
        w_scr[:, tn:] = wv_ref[0].astype(BF16)
        hw_scr[0:8, :] = jnp.zeros((8, 2 * tn), F32)

    cw = jnp.concatenate([cwg_ref[0], cwv_ref[0]], axis=1)
    cb = jnp.concatenate([cbg_ref[...], cbv_ref[...]], axis=1)
    w0, w1, w2 = cw[0:1, :], cw[1:2, :], cw[2:3, :]

    for lo, hi in zip(bounds[:-1], bounds[1:]):
        hw = _dot(h_ref[lo:hi, :], w_scr[...])
        hw_scr[8 + lo:8 + hi, :] = hw
        u = w2 * hw + w1 * hw_scr[7 + lo:7 + hi, :] + w0 * hw_scr[6 + lo:6 + hi, :] + cb
        gate, val = u[:, :tn], u[:, tn:]
        o_ref[lo:hi, :] = (gate * jax.nn.sigmoid(gate) * val).astype(o_ref.dtype)
    hw_scr[0:8, :] = hw_scr[tm:tm + 8, :]


def _ffn_up(hb, w_up, conv_w, conv_b, *, d_ff, tm, tn):
    s, d = hb.shape
    nj = d_ff // tn
    return pl.pallas_call(
        functools.partial(_ffn_up_kernel, tm=tm, tn=tn, bounds=(0, tm // 2, tm)),
        grid=(nj, s // tm),
        in_specs=[pl.BlockSpec((tm, d), lambda j, i: (i, 0)),
                  pl.BlockSpec((1, d, tn), lambda j, i: (0, 0, j)),
                  pl.BlockSpec((1, d, tn), lambda j, i: (0, 0, nj + j)),
                  pl.BlockSpec((1, CONV_W, tn), lambda j, i: (0, 0, j)),
                  pl.BlockSpec((1, CONV_W, tn), lambda j, i: (0, 0, nj + j)),
                  pl.BlockSpec((1, tn), lambda j, i: (0, j)),
                  pl.BlockSpec((1, tn), lambda j, i: (0, nj + j))],
        out_specs=pl.BlockSpec((tm, tn), lambda j, i: (i, j)),
        out_shape=jax.ShapeDtypeStruct((s, d_ff), BF16),
        scratch_shapes=[pltpu.VMEM((d, 2 * tn), BF16), pltpu.VMEM((8 + tm, 2 * tn), F32)],
        compiler_params=_params("parallel", "arbitrary"),
        name="ffn_up",
    )(hb, w_up, w_up, conv_w, conv_w, conv_b, conv_b)


def _layout(d_model):
    a_width = A_HEADS * A_HEAD_DIM
    g_width = d_model - a_width
    g_kwidth = g_width // 2
    names = ("qa", "ka", "va", "qi", "ki", "wi", "qg", "kg", "vg", "glr", "gg")
    sizes = (a_width, a_width, a_width, IDX_HEADS * IDX_DIM, IDX_DIM, IDX_HEADS,
             g_kwidth, g_kwidth, g_width, G_LOWRANK, g_width)
    offs = np.concatenate([[0], np.cumsum(sizes)])
    src = {n: (int(offs[i]), int(offs[i + 1])) for i, n in enumerate(names)}
    order = ("qi", "qa", "ka", "va", "qg", "kg", "vg", "gg", "ki")
    lay, blk = {}, 0
    for n in order:
        width = src[n][1] - src[n][0]
        assert width % LANES == 0
        lay[n] = (blk, width // LANES)
        blk += width // LANES
    return src, order, lay, blk


def kernel(x, w_in, w_gk2, b_gk, attn_out_g, gla_norm_g, w_o, ln1_g, ln1_b, w_up, conv_w, conv_b,
           w_down, ln2_g, ln2_b):
    assert x.shape[0] == 1 and w_in.shape[0] == DEPTH == 1
    _, s, d = x.shape
    x2 = x[0]
    src, order, lay, nb_main = _layout(d)
    g_width = d - A_HEADS * A_HEAD_DIM
    dv = g_width // G_HEADS
    dk = dv // 2
    d_ff = w_down.shape[1]
    topk = min(TOPK_MAX, s // 4)

    w_proj = _relayout_w_in(jnp.swapaxes(w_in, 1, 2), src, order)
    n_small = IDX_HEADS + G_LOWRANK
    w2 = jnp.zeros((LANES, G_HEADS * dk), F32).at[IDX_HEADS:n_small].set(w_gk2[0]).astype(BF16)
    bgk = b_gk[0].reshape(1, G_HEADS * dk)
    wd = w_down[0].astype(BF16)
    n_main = (nb_main + 1) * LANES
    q_lo, q_hi = lay["qa"][0] * LANES, (lay["qa"][0] + lay["qa"][1]) * LANES
    col = np.ones((1, n_main), np.float32)
    col[:, q_lo:q_hi] = A_HEAD_DIM ** -0.5 * LOG2E
    col_scale = jnp.asarray(col)

    tm = _tile(s, 1024)
    main, small = _proj_blocks(x2, w_proj, col_scale, tm=_tile(s, 512), tn=_tile(n_main, 1280))
    bias = _indexer(main, small, lay, s=s, tq=256, tk=256, topk=topk)
    ta = _tile(s, 512)
    o_a = _attention(main, bias, attn_out_g[0], lay, s=s, tq=ta, tk=ta)
    o_g = _gla(main, small, w2, bgk, gla_norm_g[0], lay, s=s, rows=_tile(s, 512), chunk=64, dk=dk, dv=dv)
    y1 = _w_o(o_a, o_g, w_o, x2, tm=tm, tn=_tile(d, 512), alpha=DN_ALPHA)
    h, hb = _layer_norm(y1, ln1_g[0], ln1_b[0], tr=_tile(s, 256, 8), also_bf16=True)

    act = _ffn_up(hb, w_up, conv_w, conv_b[0].reshape(1, 2 * d_ff), d_ff=d_ff, tm=tm, tn=_tile(d_ff, 256))
    y2 = _matmul(act, wd, tm=_tile(s, 512), tn=_tile(d, 512), tk=d_ff, out_dtype=F32, residual=h,
                 alpha=DN_ALPHA, name="w_down")
    (out,) = _layer_norm(y2, ln2_g[0], ln2_b[0], tr=_tile(s, 256, 8), also_bf16=False)
    return out[None]
```

```python
import functools

import numpy as np
import jax
import jax.numpy as jnp
from jax import lax
from jax.experimental import pallas as pl
from jax.experimental.pallas import tpu as pltpu

A_HEADS = 16
A_HEAD_DIM = 128
IDX_HEADS = 32
IDX_DIM = 128
TOPK_MAX = 256
G_HEADS = 4
G_LOWRANK = 16
G_TAU = 16.0
CONV_W = 3
LN_EPS = 1e-5
RMS_EPS = 1e-6
DEPTH = 1
DN_ALPHA = (2 * DEPTH) ** 0.25

LANES = 128
MIB = 1024 * 1024
VMEM_BUDGET = 56 * MIB

BF16 = jnp.bfloat16
F32 = jnp.float32
NEG_BIG = -1e30
LOG2E = 1.4426950408889634
INT_MIN = -2 ** 31
COUNT_UNROLL = (8, 2, 1)


def _nbytes(shape, dtype):
    n = jnp.dtype(dtype).itemsize
    for dim in shape:
        n *= 1 if dim is None else getattr(dim, "block_size", dim)
    return n


def _params(sem, blocks, scratch=(), temps=()):
    need = (2 * sum(_nbytes(*b) for b in blocks) + sum(_nbytes(*b) for b in scratch)
            + sum(_nbytes(*b) for b in temps))
    limit = -(-need // MIB) * MIB
    assert limit <= VMEM_BUDGET, (limit, VMEM_BUDGET)
    return pltpu.CompilerParams(dimension_semantics=sem, vmem_limit_bytes=limit)


def _tile(n, max_tile, quantum=LANES):
    best = None
    for t in range(quantum, min(n, max_tile) + 1, quantum):
        if n % t == 0:
            best = t
    assert best is not None, (n, max_tile, quantum)
    return best


def _dot(a, b):
    return jnp.dot(a, b, preferred_element_type=F32)


def _dot_nt(a, b):
    return lax.dot_general(a, b, (((1,), (1,)), ((), ())), preferred_element_type=F32)


def _dot_tn(a, b):
    return lax.dot_general(a, b, (((0,), (0,)), ((), ())), preferred_element_type=F32)


def _relayout_kernel(off_ref, kind_ref, a_ref, b_ref, o_ref, *, n_wi, n_small):
    del off_ref
    kind = kind_ref[pl.program_id(0)]

    @pl.when(kind == 0)
    def _():
        o_ref[...] = a_ref[0].astype(BF16)

    @pl.when(kind == 1)
    def _():
        o_ref[...] = jnp.zeros(o_ref.shape, BF16)

    @pl.when(kind == 2)
    def _():
        r = lax.broadcasted_iota(jnp.int32, o_ref.shape, 0)
        o_ref[...] = jnp.where(r < n_wi, a_ref[0], jnp.where(r < n_small, b_ref[0], 0.0)).astype(BF16)


def _relayout_w_in(w_t, src, order):
    _, n_in, d = w_t.shape
    starts = [src[n][0] + t * LANES for n in order for t in range((src[n][1] - src[n][0]) // LANES)]
    wi0, glr0 = src["wi"][0], src["glr"][0]
    n_wi = src["wi"][1] - wi0
    n_small = n_wi + src["glr"][1] - glr0
    glr_win = glr0 - n_wi
    assert n_small <= LANES and glr_win >= 0 and wi0 + LANES <= n_in and glr_win + LANES <= n_in
    assert all(o % 8 == 0 for o in starts + [wi0, glr_win])
    row_off8 = [o // 8 for o in starts + [0, wi0]]
    kind = [0] * len(starts) + [1, 2]
    nb = len(kind)
    tables = [jnp.asarray(np.asarray(t, np.int32)) for t in (row_off8, kind)]
    return pl.pallas_call(
        functools.partial(_relayout_kernel, n_wi=n_wi, n_small=n_small),
        grid_spec=pltpu.PrefetchScalarGridSpec(
            num_scalar_prefetch=2,
            grid=(nb,),
            in_specs=[pl.BlockSpec((pl.Element(1), pl.Element(LANES), pl.Element(d)),
                                   lambda b, off, kd: (0, off[b] * 8, 0)),
                      pl.BlockSpec((pl.Element(1), pl.Element(LANES), pl.Element(d)),
                                   lambda b, off, kd: (0, glr_win, 0))],
            out_specs=pl.BlockSpec((LANES, d), lambda b, off, kd: (b, 0))),
        out_shape=jax.ShapeDtypeStruct((nb * LANES, d), BF16),
        compiler_params=_params(("arbitrary",),
                                blocks=[((LANES, d), F32), ((LANES, d), F32), ((LANES, d), BF16)],
                                temps=[((LANES, d), F32)]),
        name="relayout_w_in",
    )(*tables, w_t, w_t)


def _proj_kernel(x_ref, w_ref, ws_ref, sc_ref, o_ref, os_ref, xb_scr):
    @pl.when(pl.program_id(1) == 0)
    def _():
        xb_scr[...] = x_ref[...].astype(BF16)
        os_ref[...] = _dot_nt(xb_scr[...], ws_ref[...])

    res = _dot_nt(xb_scr[...], w_ref[...]) * sc_ref[...]
    for c in range(o_ref.shape[0]):
        o_ref[c] = res[:, c * LANES:(c + 1) * LANES].astype(o_ref.dtype)


def _proj_blocks(x, wt, col_scale, *, tm, tn):
    s, k = x.shape
    n = wt.shape[0] - LANES
    return pl.pallas_call(
        _proj_kernel,
        grid=(s // tm, n // tn),
        in_specs=[pl.BlockSpec((tm, k), lambda i, j: (i, 0)),
                  pl.BlockSpec((tn, k), lambda i, j: (j, 0)),
                  pl.BlockSpec((LANES, k), lambda i, j: (n // LANES, 0)),
                  pl.BlockSpec((1, tn), lambda i, j: (0, j))],
        out_specs=[pl.BlockSpec((tn // LANES, tm, LANES), lambda i, j: (j, i, 0)),
                   pl.BlockSpec((tm, LANES), lambda i, j: (i, 0))],
        out_shape=[jax.ShapeDtypeStruct((n // LANES, s, LANES), BF16),
                   jax.ShapeDtypeStruct((s, LANES), F32)],
        scratch_shapes=[pltpu.VMEM((tm, k), BF16)],
        compiler_params=_params(("parallel", "arbitrary"),
                                blocks=[((tm, k), F32), ((tn, k), BF16), ((LANES, k), BF16), ((8, tn), F32),
                                        ((tm, tn), BF16), ((tm, LANES), F32)],
                                scratch=[((tm, k), BF16)],
                                temps=[((tm, tn), F32), ((tm, tn), F32)]),
        name="proj",
    )(x, wt, wt, col_scale)


def _mm_kernel(a_ref, b_ref, *rest, alpha, has_res, nk):
    if has_res:
        r_ref, o_ref = rest[0], rest[1]
        rest = rest[2:]
    else:
        r_ref, o_ref = None, rest[0]
        rest = rest[1:]

    def finish(acc):
        if has_res:
            acc = acc + alpha * r_ref[...]
        o_ref[...] = acc.astype(o_ref.dtype)

    if nk == 1:
        finish(_dot(a_ref[...], b_ref[...]))
        return
    acc_ref = rest[0]
    kk = pl.program_id(2)

    @pl.when(kk == 0)
    def _():
        acc_ref[...] = _dot(a_ref[...], b_ref[...])

    @pl.when(kk > 0)
    def _():
        acc_ref[...] += _dot(a_ref[...], b_ref[...])

    @pl.when(kk == nk - 1)
    def _():
        finish(acc_ref[...])


def _matmul(a, b, *, tm, tn, tk, out_dtype, residual=None, alpha=1.0, name="matmul"):
    m, k = a.shape
    n = b.shape[1]
    nk = k // tk
    in_specs = [pl.BlockSpec((tm, tk), lambda i, j, kk: (i, kk)),
                pl.BlockSpec((tk, tn), lambda i, j, kk: (kk, j))]
    args = [a, b]
    if residual is not None:
        in_specs.append(pl.BlockSpec((tm, tn), lambda i, j, kk: (i, j)))
        args.append(residual)
    scratch = [pltpu.VMEM((tm, tn), F32)] if nk > 1 else []
    return pl.pallas_call(
        functools.partial(_mm_kernel, alpha=alpha, has_res=residual is not None, nk=nk),
        grid=(m // tm, n // tn, nk),
        in_specs=in_specs,
        out_specs=pl.BlockSpec((tm, tn), lambda i, j, kk: (i, j)),
        out_shape=jax.ShapeDtypeStruct((m, n), out_dtype),
        scratch_shapes=scratch,
        compiler_params=_params(("parallel", "parallel", "arbitrary"),
                                blocks=[((tm, tk), a.dtype), ((tk, tn), b.dtype), ((tm, tn), out_dtype)]
                                + ([((tm, tn), F32)] if residual is not None else []),
                                scratch=[((tm, tn), F32)] if nk > 1 else [],
                                temps=[((tm, tn), F32)]),
        name=name,
    )(*args)


def _wo_kernel(oa_ref, og_ref, wt_ref, wb_ref, x_ref, o_ref, w_scr, *, ka, alpha):
    @pl.when(pl.program_id(1) == 0)
    def _():
        w_scr[:ka, :] = wt_ref[0].astype(BF16)
        w_scr[ka:, :] = wb_ref[0].astype(BF16)

    acc = _dot(oa_ref[...], w_scr[:ka, :]) + _dot(og_ref[...], w_scr[ka:, :])
    o_ref[...] = acc + alpha * x_ref[...]


def _w_o(o_a, o_g, w_o, x, *, tm, tn, alpha):
    s, ka = o_a.shape
    kg = o_g.shape[1]
    d = w_o.shape[2]
    assert ka % kg == 0
    return pl.pallas_call(
        functools.partial(_wo_kernel, ka=ka, alpha=alpha),
        grid=(d // tn, s // tm),
        in_specs=[pl.BlockSpec((tm, ka), lambda j, i: (i, 0)),
                  pl.BlockSpec((tm, kg), lambda j, i: (i, 0)),
                  pl.BlockSpec((1, ka, tn), lambda j, i: (0, 0, j)),
                  pl.BlockSpec((1, kg, tn), lambda j, i: (0, ka // kg, j)),
                  pl.BlockSpec((tm, tn), lambda j, i: (i, j))],
        out_specs=pl.BlockSpec((tm, tn), lambda j, i: (i, j)),
        out_shape=jax.ShapeDtypeStruct((s, d), F32),
        scratch_shapes=[pltpu.VMEM((ka + kg, tn), BF16)],
        compiler_params=_params(("parallel", "arbitrary"),
                                blocks=[((tm, ka), BF16), ((tm, kg), BF16), ((ka, tn), F32), ((kg, tn), F32),
                                        ((tm, tn), F32), ((tm, tn), F32)],
                                scratch=[((ka + kg, tn), BF16)],
                                temps=[((tm, tn), F32), ((tm, tn), F32)]),
        name="w_o",
    )(o_a, o_g, w_o, w_o, x)


def _ln_kernel(y_ref, g_ref, b_ref, *o_refs):
    y = y_ref[...]
    mu = jnp.mean(y, axis=1, keepdims=True)
    yc = y - mu
    var = jnp.mean(yc * yc, axis=1, keepdims=True)
    out = yc * lax.rsqrt(var + LN_EPS) * g_ref[...] + b_ref[...]
    for o_ref in o_refs:
        o_ref[...] = out.astype(o_ref.dtype)


def _layer_norm(y, g, b, *, tr, also_bf16):
    s, d = y.shape
    out_shape = [jax.ShapeDtypeStruct((s, d), F32)]
    out_specs = [pl.BlockSpec((tr, d), lambda i: (i, 0))]
    if also_bf16:
        out_shape.append(jax.ShapeDtypeStruct((s, d), BF16))
        out_specs.append(pl.BlockSpec((tr, d), lambda i: (i, 0)))
    return pl.pallas_call(
        _ln_kernel,
        grid=(s // tr,),
        in_specs=[pl.BlockSpec((tr, d), lambda i: (i, 0)),
                  pl.BlockSpec((1, d), lambda i: (0, 0)),
                  pl.BlockSpec((1, d), lambda i: (0, 0))],
        out_specs=out_specs,
        out_shape=out_shape,
        compiler_params=_params(("parallel",),
                                blocks=[((tr, d), F32), ((8, d), F32), ((8, d), F32)]
                                + [((tr, d), o.dtype) for o in out_shape],
                                temps=[((tr, d), F32), ((tr, d), F32)]),
        name="layer_norm",
    )(y, g.reshape(1, d), b.reshape(1, d))


def _idx_kernel(qi_ref, ki_ref, sm_ref, bias_ref, hi_scr, lo_scr, wb_scr, *, tq, tk, nkt, hi, group,
                topk, wscale):
    qb = pl.program_id(0)
    nk = ((qb + 1) * tq + tk - 1) // tk
    reps = tk // LANES
    i16 = jnp.int16
    low16 = -2 ** 15

    for h in range(hi):
        wb_scr[h] = jnp.broadcast_to(sm_ref[:, h:h + 1] * wscale, (tq, LANES))

    row = qb * tq + lax.broadcasted_iota(jnp.int32, (group, tk), 0)
    col = lax.broadcasted_iota(jnp.int32, (group, tk), 1)

    def lanes(a):
        return jnp.concatenate([a] * reps, axis=1) if reps > 1 else a

    def fold(a):
        out = a[:, :LANES]
        for r in range(1, reps):
            out = out + a[:, r * LANES:(r + 1) * LANES]
        return out

    def score_tile(kb, carry):
        kt = ki_ref[0, pl.ds(pl.multiple_of(kb * tk, tk), tk), :]
        for r in range(tq // group):
            rows = slice(r * group, (r + 1) * group)
            qs = qi_ref[:, rows, :].reshape(hi * group, IDX_DIM)
            d = _dot_nt(qs, kt)
            acc = jnp.zeros((group, tk), F32)
            for h in range(hi):
                acc = acc + jnp.maximum(d[h * group:(h + 1) * group], 0.0) * lanes(wb_scr[h, rows, :])
            bits = pltpu.bitcast(acc + 0.0, jnp.int32)
            key = jnp.where(kb * tk + col <= row + r * group,
                            bits ^ ((bits >> 31) & jnp.int32(0x7FFFFFFF)), INT_MIN)
            hi_scr[kb, rows, :] = (key >> 16).astype(i16)
            lo_scr[kb, rows, :] = ((key & 0xFFFF) + low16).astype(i16)
        return carry

    lax.fori_loop(0, nk, score_tile, 0)

    def count_ge(ref, cand):
        cand_t = lanes(cand.astype(i16))

        def step(kb, c):
            return c + fold(jnp.where(ref[kb] >= cand_t, i16(1), i16(0)))

        c = jnp.zeros((tq, LANES), i16)
        done = 0
        for width in COUNT_UNROLL:
            def step_many(q, c, width=width, done=done):
                for u in range(width):
                    c = step(done + q * width + u, c)
                return c

            n_groups = (nk - done) // width
            c = lax.fori_loop(0, n_groups, step_many, c)
            done = done + n_groups * width
        return jnp.sum(c.astype(F32), axis=1, keepdims=True)

    kf = float(topk)

    def bit_search(ref, base_cnt, cnt_start, stop_when_exact):
        def body(state):
            it, v, cnt_v = state
            cand = v + jnp.left_shift(jnp.int32(1), 15 - it)
            cnt = base_cnt + count_ge(ref, cand)
            ok = cnt >= kf
            return it + 1, jnp.where(ok, cand, v), jnp.where(ok, cnt, cnt_v)

        def unsettled(state):
            go = state[0] < 16
            if stop_when_exact:
                go = go & (jnp.max(jnp.where(state[2] == kf, 0.0, 1.0)) > 0.0)
            return go

        v0 = jnp.full((tq, LANES), low16, jnp.int32)
        _, v, cnt_v = lax.while_loop(unsettled, body, (jnp.int32(0), v0, cnt_start))
        return v, cnt_v

    n_all = jnp.zeros((tq, 1), F32) + (nk * tk).astype(F32)
    v_hi, n_from_hi = bit_search(hi_scr, 0.0, n_all, False)
    n_above = count_ge(hi_scr, v_hi + 1)
    v_hi_t = lanes(v_hi.astype(i16))

    def keep_equal(kb, carry):
        lo_scr[kb] = jnp.where(hi_scr[kb] == v_hi_t, lo_scr[kb], i16(low16))
        return carry

    lax.fori_loop(0, nk, keep_equal, 0)
    v_lo, n_sel = bit_search(lo_scr, n_above, n_from_hi, True)
    v_lo = jnp.where(v_hi == low16, jnp.maximum(v_lo, low16 + 1), v_lo)
    v_lo_t = lanes(v_lo.astype(i16))
    zero = jnp.zeros((tq, tk), bias_ref.dtype)
    neg = jnp.full((tq, tk), NEG_BIG, bias_ref.dtype)
    has_ties = jnp.max(jnp.where(n_sel > kf, 1.0, 0.0)) > 0.0

    @pl.when(jnp.logical_not(has_ties))
    def _():
        def emit(kb, carry):
            h16 = hi_scr[kb]
            sel = (h16 > v_hi_t) | ((h16 == v_hi_t) & (lo_scr[kb] >= v_lo_t))
            bias_ref[0, kb] = jnp.where(sel, zero, neg)
            return carry

        lax.fori_loop(0, nk, emit, 0)

    @pl.when(has_ties)
    def _():
        top16 = 2 ** 15 - 1
        n_gt = n_above + jnp.where(v_lo[:, :1] >= top16, 0.0, count_ge(lo_scr, jnp.minimum(v_lo + 1, top16)))
        need = kf - n_gt
        earlier = jnp.where(lax.broadcasted_iota(jnp.int32, (tk, tk), 0)
                            < lax.broadcasted_iota(jnp.int32, (tk, tk), 1), 1.0, 0.0).astype(BF16)
        one_b = jnp.ones((tq, tk), BF16)
        zero_b = jnp.zeros((tq, tk), BF16)

        def emit_ties(kb, seen):
            h16 = hi_scr[kb]
            l16 = lo_scr[kb]
            same_hi = h16 == v_hi_t
            larger = (h16 > v_hi_t) | (same_hi & (l16 > v_lo_t))
            tied = jnp.where(same_hi & (l16 == v_lo_t), one_b, zero_b)
            rank = seen + _dot(tied, earlier)
            keep = jnp.where(rank < need, 1.0, 0.0).astype(BF16)
            sel = larger | ((tied > 0.0) & (keep > 0.0))
            bias_ref[0, kb] = jnp.where(sel, zero, neg)
            return seen + jnp.sum(tied.astype(F32), axis=1, keepdims=True)

        lax.fori_loop(0, nk, emit_ties, jnp.zeros((tq, 1), F32))

    def fill(kb, carry):
        bias_ref[0, kb] = neg
        return carry

    lax.fori_loop(nk, nkt, fill, 0)


def _indexer(main, small, lay, *, s, tq, tk, topk):
    nkt = s // tk
    hi = IDX_HEADS
    group = min(tq, 64)
    qi0, ki0 = lay["qi"][0], lay["ki"][0]
    assert qi0 % hi == 0
    return pl.pallas_call(
        functools.partial(_idx_kernel, tq=tq, tk=tk, nkt=nkt, hi=hi, group=group, topk=topk,
                          wscale=IDX_HEADS ** -0.5 * IDX_DIM ** -0.5),
        grid=(s // tq,),
        in_specs=[pl.BlockSpec((hi, tq, LANES), lambda i: (qi0 // hi, i, 0)),
                  pl.BlockSpec((1, s, LANES), lambda i: (ki0, 0, 0)),
                  pl.BlockSpec((tq, LANES), lambda i: (i, 0))],
        out_specs=pl.BlockSpec((1, nkt, tq, tk), lambda i: (i, 0, 0, 0)),
        out_shape=jax.ShapeDtypeStruct((s // tq, nkt, tq, tk), BF16),
        scratch_shapes=[pltpu.VMEM((nkt, tq, tk), jnp.int16),
                        pltpu.VMEM((nkt, tq, tk), jnp.int16),
                        pltpu.VMEM((hi, tq, LANES), F32)],
        compiler_params=_params(("parallel",),
                                blocks=[((hi, tq, LANES), BF16), ((s, LANES), BF16), ((tq, LANES), F32),
                                        ((nkt, tq, tk), BF16)],
                                scratch=[((nkt, tq, tk), jnp.int16), ((nkt, tq, tk), jnp.int16),
                                         ((hi, tq, LANES), F32)],
                                temps=[((hi * group, tk), F32), ((hi * group, tk), F32)]),
        name="indexer",
    )(main, main, small)


def _attn_kernel(qi_ref, kb_ref, q_ref, k_ref, v_ref, b_ref, g_ref, o_ref, bias_scr, m_scr, acc_scr, *pipe,
                 nh, tq, tk):
    step = pl.program_id(0)
    qi = qi_ref[step]
    kb = kb_ref[step]

    @pl.when(kb == 0)
    def _():
        m_scr[...] = jnp.full(m_scr.shape, NEG_BIG, F32)
        acc_scr[...] = jnp.zeros(acc_scr.shape, F32)

    na, nb, sq, sk = b_ref.shape
    for a in range(na):
        for b in range(nb):
            bias_scr[a * sq:(a + 1) * sq, b * sk:(b + 1) * sk] = b_ref[a, b].astype(F32)

    ones = jnp.ones((tk, LANES), BF16)

    def qk(h, s_ref):
        s_ref[...] = _dot_nt(q_ref[h], k_ref[h])

    def soft(h, s_ref, p_ref, al_ref):
        s = s_ref[...] + bias_scr[...]
        m_prev = m_scr[h]
        m_new = jnp.maximum(m_prev, jnp.max(s, axis=1, keepdims=True))
        m_scr[h] = m_new
        p_ref[...] = jnp.exp2(s - m_new[:, :1]).astype(BF16)
        al_ref[...] = jnp.exp2(m_prev - m_new)

    def pv(h, p_ref, al_ref):
        al = al_ref[...]
        v_ext = jnp.concatenate([v_ref[h], ones], axis=1)
        acc_scr[h] = acc_scr[h] * jnp.concatenate([al, al], axis=1) + _dot(p_ref[...], v_ext)

    s_a, s_b, p_a, p_b, al_a, al_b = pipe
    qk(0, s_a)
    qk(1, s_b)
    soft(0, s_a, p_a, al_a)

    def pair(j, carry):
        qk(2 * j, s_a)
        soft(2 * j - 1, s_b, p_b, al_b)
        pv(2 * j - 2, p_a, al_a)
        qk(2 * j + 1, s_b)
        soft(2 * j, s_a, p_a, al_a)
        pv(2 * j - 1, p_b, al_b)
        return carry

    lax.fori_loop(1, nh // 2, pair, 0)
    soft(nh - 1, s_b, p_b, al_b)
    pv(nh - 2, p_a, al_a)
    pv(nh - 1, p_b, al_b)

    @pl.when(kb == (qi * tq + tq - 1) // tk)
    def _():
        ss = jnp.zeros((tq, LANES), F32)
        for h in range(nh):
            o_h = acc_scr[h, :, :LANES] / acc_scr[h, :, LANES:]
            acc_scr[h, :, :LANES] = o_h
            ss = ss + o_h * o_h
        ms = jnp.sum(ss, axis=1, keepdims=True) * (1.0 / (nh * LANES))
        r = lax.rsqrt(ms + RMS_EPS)
        for h in range(nh):
            sl = slice(h * LANES, (h + 1) * LANES)
            o_ref[:, sl] = (acc_scr[h, :, :LANES] * r * g_ref[:, sl]).astype(o_ref.dtype)


def _attention(main, bias, g, lay, *, s, tq, tk):
    nh = A_HEADS
    nqb, nkt, sq, sk = bias.shape
    q0, k0, v0 = lay["qa"][0], lay["ka"][0], lay["va"][0]
    assert q0 % nh == 0 and k0 % nh == 0 and v0 % nh == 0
    pairs = [(i, j) for i in range(s // tq) for j in range((i * tq + tq - 1) // tk + 1)]
    qi_tab = jnp.asarray(np.asarray([p[0] for p in pairs], np.int32))
    kb_tab = jnp.asarray(np.asarray([p[1] for p in pairs], np.int32))
    return pl.pallas_call(
        functools.partial(_attn_kernel, nh=nh, tq=tq, tk=tk),
        grid_spec=pltpu.PrefetchScalarGridSpec(
            num_scalar_prefetch=2,
            grid=(len(pairs),),
            in_specs=[pl.BlockSpec((nh, tq, LANES), lambda t, qi, kb: (q0 // nh, qi[t], 0)),
                      pl.BlockSpec((nh, tk, LANES), lambda t, qi, kb: (k0 // nh, kb[t], 0)),
                      pl.BlockSpec((nh, tk, LANES), lambda t, qi, kb: (v0 // nh, kb[t], 0)),
                      pl.BlockSpec((tq // sq, tk // sk, sq, sk), lambda t, qi, kb: (qi[t], kb[t], 0, 0)),
                      pl.BlockSpec((1, nh * LANES), lambda t, qi, kb: (0, 0))],
            out_specs=pl.BlockSpec((tq, nh * LANES), lambda t, qi, kb: (qi[t], 0)),
            scratch_shapes=[pltpu.VMEM((tq, tk), F32),
                            pltpu.VMEM((nh, tq, LANES), F32),
                            pltpu.VMEM((nh, tq, 2 * LANES), F32),
                            pltpu.VMEM((tq, tk), F32), pltpu.VMEM((tq, tk), F32),
                            pltpu.VMEM((tq, tk), BF16), pltpu.VMEM((tq, tk), BF16),
                            pltpu.VMEM((tq, LANES), F32), pltpu.VMEM((tq, LANES), F32)]),
        out_shape=jax.ShapeDtypeStruct((s, nh * LANES), BF16),
        compiler_params=_params(("arbitrary",),
                                blocks=[((nh, tq, LANES), BF16), ((nh, tk, LANES), BF16), ((nh, tk, LANES), BF16),
                                        ((tq, tk), BF16), ((8, nh * LANES), F32), ((tq, nh * LANES), BF16)],
                                scratch=[((tq, tk), F32), ((nh, tq, LANES), F32), ((nh, tq, 2 * LANES), F32),
                                         ((tq, tk), F32), ((tq, tk), F32), ((tq, tk), BF16), ((tq, tk), BF16),
                                         ((tq, LANES), F32), ((tq, LANES), F32)],
                                temps=[((tq, tk), F32), ((tq, tk), F32)]),
        name="attention",
    )(qi_tab, kb_tab, main, main, main, bias, g.reshape(1, nh * LANES))


def _gla_kernel(q_ref, k_ref, v_ref, gg_ref, sm_ref, w2_ref, bgk_ref, gn_ref, o_ref, st_scr, *,
                rows, chunk, nheads, dk, dv):
    @pl.when(pl.program_id(0) == 0)
    def _():
        st_scr[...] = jnp.zeros(st_scr.shape, F32)

    r_i = lax.broadcasted_iota(jnp.int32, (chunk, chunk), 0)
    c_i = lax.broadcasted_iota(jnp.int32, (chunk, chunk), 1)
    tri = jnp.where(r_i >= c_i, 1.0, 0.0).astype(BF16)
    nbk, nbv = dk // LANES, dv // LANES
    assert chunk >= 16 and chunk & (chunk - 1) == 0
    leaf = 2
    half_sizes = [chunk >> i for i in range(1, chunk.bit_length()) if chunk >> i >= leaf]
    quadrant = {m: (r_i // (2 * m) == c_i // (2 * m)) & (r_i % (2 * m) >= m) & (c_i % (2 * m) < m)
                for m in half_sizes}
    near = [(c_i == r_i - j) & (r_i % leaf >= j) for j in range(leaf)]
    sub = lax.broadcasted_iota(jnp.int32, (chunk // 8, 8, dk), 1)

    def reference_rows(b, m):
        if m >= 8:
            return jnp.concatenate(
                [jnp.broadcast_to(b[blk + m:blk + m + 1, :], (2 * m, dk)) for blk in range(0, chunk, 2 * m)],
                axis=0)
        b3 = b.reshape(chunk // 8, 8, dk)
        out = None
        for blk in range(0, 8, 2 * m):
            rows = jnp.broadcast_to(b3[:, blk + m:blk + m + 1, :], b3.shape)
            out = rows if out is None else jnp.where(sub >= blk, rows, out)
        return out.reshape(chunk, dk)

    def wide(ref, first, n, r0):
        return jnp.concatenate([ref[first + j, pl.ds(r0, chunk), :] for j in range(n)], axis=1)

    def step(c, carry):
        r0 = pl.multiple_of(c * chunk, chunk)
        z_all = _dot(sm_ref[pl.ds(r0, chunk), :].astype(BF16), w2_ref[...]) + bgk_ref[...]
        for h in range(nheads):
            q = wide(q_ref, h * nbk, nbk, r0).astype(F32) * (dk ** -0.5)
            k = wide(k_ref, h * nbk, nbk, r0).astype(F32)
            v = wide(v_ref, h * nbv, nbv, r0)
            z = z_all[:, h * dk:(h + 1) * dk]
            g = (jnp.minimum(z, 0.0) - jnp.log1p(jnp.exp(-jnp.abs(z)))) * (1.0 / G_TAU)
            g_hi = g.astype(BF16)
            rem = g - g_hi.astype(F32)
            g_mid = rem.astype(BF16)
            g_lo = (rem - g_mid.astype(F32)).astype(BF16)
            b = _dot(tri, g_hi) + _dot(tri, g_mid) + _dot(tri, g_lo)
            b_last = b[chunk - 1:chunk, :]
            att = jnp.zeros((chunk, chunk), F32)
            for m in half_sizes:
                e = jnp.exp(-jnp.abs(b - reference_rows(b, m)))
                att = att + jnp.where(quadrant[m], _dot_nt((q * e).astype(BF16), (k * e).astype(BF16)), 0.0)
            for j in range(leaf):
                kj = k if j == 0 else pltpu.roll(k, j, 0)
                bj = b if j == 0 else pltpu.roll(b, j, 0)
                pair = jnp.sum(q * kj * jnp.exp(jnp.minimum(b - bj, 0.0)), axis=1, keepdims=True)
                att = att + jnp.where(near[j], pair, 0.0)
            st = st_scr[h]
            o = _dot(att.astype(BF16), v) + _dot_nt((q * jnp.exp(b)).astype(BF16), st.astype(BF16))
            kd = (k * jnp.exp(b_last - b)).astype(BF16)
            st_scr[h] = st * jnp.exp(b_last) + _dot_tn(v, kd)
            ms = jnp.mean(o * o, axis=1, keepdims=True)
            gate = wide(gg_ref, h * nbv, nbv, r0).astype(F32)
            out = (o * lax.rsqrt(ms + RMS_EPS)) * gn_ref[...] * (gate * jax.nn.sigmoid(gate))
            o_ref[pl.ds(r0, chunk), h * dv:(h + 1) * dv] = out.astype(o_ref.dtype)
        return carry

    lax.fori_loop(0, rows // chunk, step, 0)


def _gla(main, small, w2, bgk, gn, lay, *, s, rows, chunk, dk, dv):
    nh = G_HEADS
    nqk, nv = nh * dk // LANES, nh * dv // LANES
    q0, k0, v0, g0 = lay["qg"][0], lay["kg"][0], lay["vg"][0], lay["gg"][0]
    assert q0 % nqk == 0 and k0 % nqk == 0 and v0 % nv == 0 and g0 % nv == 0
    return pl.pallas_call(
        functools.partial(_gla_kernel, rows=rows, chunk=chunk, nheads=nh, dk=dk, dv=dv),
        grid=(s // rows,),
        in_specs=[pl.BlockSpec((nqk, rows, LANES), lambda r: (q0 // nqk, r, 0)),
                  pl.BlockSpec((nqk, rows, LANES), lambda r: (k0 // nqk, r, 0)),
                  pl.BlockSpec((nv, rows, LANES), lambda r: (v0 // nv, r, 0)),
                  pl.BlockSpec((nv, rows, LANES), lambda r: (g0 // nv, r, 0)),
                  pl.BlockSpec((rows, LANES), lambda r: (r, 0)),
                  pl.BlockSpec((LANES, nh * dk), lambda r: (0, 0)),
                  pl.BlockSpec((1, nh * dk), lambda r: (0, 0)),
                  pl.BlockSpec((1, dv), lambda r: (0, 0))],
        out_specs=pl.BlockSpec((rows, nh * dv), lambda r: (r, 0)),
        out_shape=jax.ShapeDtypeStruct((s, nh * dv), BF16),
        scratch_shapes=[pltpu.VMEM((nh, dv, dk), F32)],
        compiler_params=_params(("arbitrary",),
                                blocks=[((nqk, rows, LANES), BF16), ((nqk, rows, LANES), BF16),
                                        ((nv, rows, LANES), BF16), ((nv, rows, LANES), BF16),
                                        ((rows, LANES), F32), ((LANES, nh * dk), BF16), ((8, nh * dk), F32),
                                        ((8, dv), F32), ((rows, nh * dv), BF16)],
                                scratch=[((nh, dv, dk), F32)],
                                temps=[((chunk, nh * dk), F32)] * 8 + [((dv, dk), F32)] * 2),
        name="gla",
    )(main, main, main, main, small, w2, bgk, gn.reshape(1, dv))


def _ffn_up_kernel(h_ref, wg_ref, wv_ref, cwg_ref, cwv_ref, cbg_ref, cbv_ref, o_ref, w_scr, hw_scr,
                   *, tm, tn, bounds):
    @pl.when(pl.program_id(1) == 0)
    def _():
        w_scr[:, :tn] = wg_ref[0].astype(BF16)
        w_scr[:, tn:] = wv_ref[0].astype(BF16)
        hw_scr[0:8, :] = jnp.zeros((8, 2 * tn), F32)

    cw = jnp.concatenate([cwg_ref[0], cwv_ref[0]], axis=1)
    cb = jnp.concatenate([cbg_ref[...], cbv_ref[...]], axis=1)
    w0, w1, w2 = cw[0:1, :], cw[1:2, :], cw[2:3, :]

    for lo, hi in zip(bounds[:-1], bounds[1:]):
        hw = _dot(h_ref[lo:hi, :], w_scr[...])
        hw_scr[8 + lo:8 + hi, :] = hw
        u = w2 * hw + w1 * hw_scr[7 + lo:7 + hi, :] + w0 * hw_scr[6 + lo:6 + hi, :] + cb
        gate, val = u[:, :tn], u[:, tn:]
        o_ref[lo:hi, :] = (gate * jax.nn.sigmoid(gate) * val).astype(o_ref.dtype)
    hw_scr[0:8, :] = hw_scr[tm:tm + 8, :]


def _ffn_up(hb, w_up, conv_w, conv_b, *, d_ff, tm, tn):
    s, d = hb.shape
    nj = d_ff // tn
    return pl.pallas_call(
        functools.partial(_ffn_up_kernel, tm=tm, tn=tn, bounds=(0, tm // 2, tm)),
        grid=(nj, s // tm),
        in_specs=[pl.BlockSpec((tm, d), lambda j, i: (i, 0)),
                  pl.BlockSpec((1, d, tn), lambda j, i: (0, 0, j)),
                  pl.BlockSpec((1, d, tn), lambda j, i: (0, 0, nj + j)),
                  pl.BlockSpec((1, CONV_W, tn), lambda j, i: (0, 0, j)),
                  pl.BlockSpec((1, CONV_W, tn), lambda j, i: (0, 0, nj + j)),
                  pl.BlockSpec((1, tn), lambda j, i: (0, j)),
                  pl.BlockSpec((1, tn), lambda j, i: (0, nj + j))],
        out_specs=pl.BlockSpec((tm, tn), lambda j, i: (i, j)),
        out_shape=jax.ShapeDtypeStruct((s, d_ff), BF16),
        scratch_shapes=[pltpu.VMEM((d, 2 * tn), BF16), pltpu.VMEM((8 + tm, 2 * tn), F32)],
        compiler_params=_params(("parallel", "arbitrary"),
                                blocks=[((tm, d), BF16), ((d, tn), F32), ((d, tn), F32), ((8, tn), F32),
                                        ((8, tn), F32), ((8, tn), F32), ((8, tn), F32), ((tm, tn), BF16)],
                                scratch=[((d, 2 * tn), BF16), ((8 + tm, 2 * tn), F32)],
                                temps=[((tm // 2, 2 * tn), F32)] * 3),
        name="ffn_up",
    )(hb, w_up, w_up, conv_w, conv_w, conv_b, conv_b)


def _layout(d_model):
    a_width = A_HEADS * A_HEAD_DIM
    g_width = d_model - a_width
    g_kwidth = g_width // 2
    names = ("qa", "ka", "va", "qi", "ki", "wi", "qg", "kg", "vg", "glr", "gg")
    sizes = (a_width, a_width, a_width, IDX_HEADS * IDX_DIM, IDX_DIM, IDX_HEADS,
             g_kwidth, g_kwidth, g_width, G_LOWRANK, g_width)
    offs = np.concatenate([[0], np.cumsum(sizes)])
    src = {n: (int(offs[i]), int(offs[i + 1])) for i, n in enumerate(names)}
    order = ("qi", "qa", "ka", "va", "qg", "kg", "vg", "gg", "ki")
    lay, blk = {}, 0
    for n in order:
        width = src[n][1] - src[n][0]
        assert width % LANES == 0
        lay[n] = (blk, width // LANES)
        blk += width // LANES
    return src, order, lay, blk


def kernel(x, w_in, w_gk2, b_gk, attn_out_g, gla_norm_g, w_o, ln1_g, ln1_b, w_up, conv_w, conv_b,
           w_down, ln2_g, ln2_b):
    assert x.shape[0] == 1 and w_in.shape[0] == DEPTH == 1
    _, s, d = x.shape
    x2 = x[0]
    src, order, lay, nb_main = _layout(d)
    g_width = d - A_HEADS * A_HEAD_DIM
    dv = g_width // G_HEADS
    dk = dv // 2
    d_ff = w_down.shape[1]
    topk = min(TOPK_MAX, s // 4)

    w_proj = _relayout_w_in(jnp.swapaxes(w_in, 1, 2), src, order)
    n_small = IDX_HEADS + G_LOWRANK
    w2 = jnp.zeros((LANES, G_HEADS * dk), F32).at[IDX_HEADS:n_small].set(w_gk2[0]).astype(BF16)
    bgk = b_gk[0].reshape(1, G_HEADS * dk)
    wd = w_down[0].astype(BF16)
    n_main = (nb_main + 1) * LANES
    q_lo, q_hi = lay["qa"][0] * LANES, (lay["qa"][0] + lay["qa"][1]) * LANES
    col = np.ones((1, n_main), np.float32)
    col[:, q_lo:q_hi] = A_HEAD_DIM ** -0.5 * LOG2E
    col_scale = jnp.asarray(col)

    tm = _tile(s, 1024)
    main, small = _proj_blocks(x2, w_proj, col_scale, tm=_tile(s, 512), tn=_tile(n_main, 1280))
    bias = _indexer(main, small, lay, s=s, tq=256, tk=256, topk=topk)
    ta = _tile(s, 512)
    o_a = _attention(main, bias, attn_out_g[0], lay, s=s, tq=ta, tk=ta)
    o_g = _gla(main, small, w2, bgk, gla_norm_g[0], lay, s=s, rows=_tile(s, 512), chunk=64, dk=dk, dv=dv)
    y1 = _w_o(o_a, o_g, w_o, x2, tm=tm, tn=_tile(d, 512), alpha=DN_ALPHA)
    h, hb = _layer_norm(y1, ln1_g[0], ln1_b[0], tr=_tile(s, 256, 8), also_bf16=True)

    act = _ffn_up(hb, w_up, conv_w, conv_b[0].reshape(1, 2 * d_ff), d_ff=d_ff, tm=tm, tn=_tile(d_ff, 256))
    y2 = _matmul(act, wd, tm=_tile(s, 512), tn=_tile(d, 512), tk=d_ff, out_dtype=F32, residual=h,
                 alpha=DN_ALPHA, name="w_down")
    (out,) = _layer_norm(y2, ln2_g[0], ln2_b[0], tr=_tile(s, 256, 8), also_bf16=False)
    return out[None]
```

```python
import functools

import numpy as np
import jax
import jax.numpy as jnp
from jax import lax
from jax.experimental import pallas as pl
from jax.experimental.pallas import tpu as pltpu

A_HEADS = 16
A_HEAD_DIM = 128
IDX_HEADS = 32
IDX_DIM = 128
TOPK_MAX = 256
G_HEADS = 4
G_LOWRANK = 16
G_TAU = 16.0
CONV_W = 3
LN_EPS = 1e-5
RMS_EPS = 1e-6
DEPTH = 1
DN_ALPHA = (2 * DEPTH) ** 0.25

LANES = 128
MIB = 1024 * 1024
VMEM_BUDGET = 56 * MIB

BF16 = jnp.bfloat16
F32 = jnp.float32
NEG_BIG = -1e30
LOG2E = 1.4426950408889634
INT_MIN = -2 ** 31
COUNT_UNROLL = (8, 2, 1)


def _nbytes(shape, dtype):
    n = jnp.dtype(dtype).itemsize
    for dim in shape:
        n *= 1 if dim is None else getattr(dim, "block_size", dim)
    return n


def _params(sem, blocks, scratch=(), temps=()):
    need = (2 * sum(_nbytes(*b) for b in blocks) + sum(_nbytes(*b) for b in scratch)
            + sum(_nbytes(*b) for b in temps))
    limit = -(-need // MIB) * MIB
    assert limit <= VMEM_BUDGET, (limit, VMEM_BUDGET)
    return pltpu.CompilerParams(dimension_semantics=sem, vmem_limit_bytes=limit)


def _tile(n, max_tile, quantum=LANES):
    best = None
    for t in range(quantum, min(n, max_tile) + 1, quantum):
        if n % t == 0:
            best = t
    assert best is not None, (n, max_tile, quantum)
    return best


def _dot(a, b):
    return jnp.dot(a, b, preferred_element_type=F32)


def _dot_nt(a, b):
    return lax.dot_general(a, b, (((1,), (1,)), ((), ())), preferred_element_type=F32)


def _dot_tn(a, b):
    return lax.dot_general(a, b, (((0,), (0,)), ((), ())), preferred_element_type=F32)


def _relayout_kernel(off_ref, kind_ref, a_ref, b_ref, o_ref, *, n_wi, n_small):
    del off_ref
    kind = kind_ref[pl.program_id(0)]

    @pl.when(kind == 0)
    def _():
        o_ref[...] = a_ref[0].astype(BF16)

    @pl.when(kind == 1)
    def _():
        o_ref[...] = jnp.zeros(o_ref.shape, BF16)

    @pl.when(kind == 2)
    def _():
        r = lax.broadcasted_iota(jnp.int32, o_ref.shape, 0)
        o_ref[...] = jnp.where(r < n_wi, a_ref[0], jnp.where(r < n_small, b_ref[0], 0.0)).astype(BF16)


def _relayout_w_in(w_t, src, order):
    _, n_in, d = w_t.shape
    starts = [src[n][0] + t * LANES for n in order for t in range((src[n][1] - src[n][0]) // LANES)]
    wi0, glr0 = src["wi"][0], src["glr"][0]
    n_wi = src["wi"][1] - wi0
    n_small = n_wi + src["glr"][1] - glr0
    glr_win = glr0 - n_wi
    assert n_small <= LANES and glr_win >= 0 and wi0 + LANES <= n_in and glr_win + LANES <= n_in
    assert all(o % 8 == 0 for o in starts + [wi0, glr_win])
    row_off8 = [o // 8 for o in starts + [0, wi0]]
    kind = [0] * len(starts) + [1, 2]
    nb = len(kind)
    tables = [jnp.asarray(np.asarray(t, np.int32)) for t in (row_off8, kind)]
    return pl.pallas_call(
        functools.partial(_relayout_kernel, n_wi=n_wi, n_small=n_small),
        grid_spec=pltpu.PrefetchScalarGridSpec(
            num_scalar_prefetch=2,
            grid=(nb,),
            in_specs=[pl.BlockSpec((pl.Element(1), pl.Element(LANES), pl.Element(d)),
                                   lambda b, off, kd: (0, off[b] * 8, 0)),
                      pl.BlockSpec((pl.Element(1), pl.Element(LANES), pl.Element(d)),
                                   lambda b, off, kd: (0, glr_win, 0))],
            out_specs=pl.BlockSpec((LANES, d), lambda b, off, kd: (b, 0))),
        out_shape=jax.ShapeDtypeStruct((nb * LANES, d), BF16),
        compiler_params=_params(("arbitrary",),
                                blocks=[((LANES, d), F32), ((LANES, d), F32), ((LANES, d), BF16)],
                                temps=[((LANES, d), F32)]),
        name="relayout_w_in",
    )(*tables, w_t, w_t)


def _proj_kernel(x_ref, w_ref, ws_ref, sc_ref, o_ref, os_ref, xb_scr):
    @pl.when(pl.program_id(1) == 0)
    def _():
        xb_scr[...] = x_ref[...].astype(BF16)
        os_ref[...] = _dot_nt(xb_scr[...], ws_ref[...])

    res = _dot_nt(xb_scr[...], w_ref[...]) * sc_ref[...]
    for c in range(o_ref.shape[0]):
        o_ref[c] = res[:, c * LANES:(c + 1) * LANES].astype(o_ref.dtype)


def _proj_blocks(x, wt, col_scale, *, tm, tn):
    s, k = x.shape
    n = wt.shape[0] - LANES
    return pl.pallas_call(
        _proj_kernel,
        grid=(s // tm, n // tn),
        in_specs=[pl.BlockSpec((tm, k), lambda i, j: (i, 0)),
                  pl.BlockSpec((tn, k), lambda i, j: (j, 0)),
                  pl.BlockSpec((LANES, k), lambda i, j: (n // LANES, 0)),
                  pl.BlockSpec((1, tn), lambda i, j: (0, j))],
        out_specs=[pl.BlockSpec((tn // LANES, tm, LANES), lambda i, j: (j, i, 0)),
                   pl.BlockSpec((tm, LANES), lambda i, j: (i, 0))],
        out_shape=[jax.ShapeDtypeStruct((n // LANES, s, LANES), BF16),
                   jax.ShapeDtypeStruct((s, LANES), F32)],
        scratch_shapes=[pltpu.VMEM((tm, k), BF16)],
        compiler_params=_params(("parallel", "arbitrary"),
                                blocks=[((tm, k), F32), ((tn, k), BF16), ((LANES, k), BF16), ((8, tn), F32),
                                        ((tm, tn), BF16), ((tm, LANES), F32)],
                                scratch=[((tm, k), BF16)],
                                temps=[((tm, tn), F32), ((tm, tn), F32)]),
        name="proj",
    )(x, wt, wt, col_scale)


def _mm_kernel(a_ref, b_ref, *rest, alpha, has_res, nk):
    if has_res:
        r_ref, o_ref = rest[0], rest[1]
        rest = rest[2:]
    else:
        r_ref, o_ref = None, rest[0]
        rest = rest[1:]

    def finish(acc):
        if has_res:
            acc = acc + alpha * r_ref[...]
        o_ref[...] = acc.astype(o_ref.dtype)

    if nk == 1:
        finish(_dot(a_ref[...], b_ref[...]))
        return
    acc_ref = rest[0]
    kk = pl.program_id(2)

    @pl.when(kk == 0)
    def _():
        acc_ref[...] = _dot(a_ref[...], b_ref[...])

    @pl.when(kk > 0)
    def _():
        acc_ref[...] += _dot(a_ref[...], b_ref[...])

    @pl.when(kk == nk - 1)
    def _():
        finish(acc_ref[...])


def _matmul(a, b, *, tm, tn, tk, out_dtype, residual=None, alpha=1.0, name="matmul"):
    m, k = a.shape
    n = b.shape[1]
    nk = k // tk
    in_specs = [pl.BlockSpec((tm, tk), lambda i, j, kk: (i, kk)),
                pl.BlockSpec((tk, tn), lambda i, j, kk: (kk, j))]
    args = [a, b]
    if residual is not None:
        in_specs.append(pl.BlockSpec((tm, tn), lambda i, j, kk: (i, j)))
        args.append(residual)
    scratch = [pltpu.VMEM((tm, tn), F32)] if nk > 1 else []
    return pl.pallas_call(
        functools.partial(_mm_kernel, alpha=alpha, has_res=residual is not None, nk=nk),
        grid=(m // tm, n // tn, nk),
        in_specs=in_specs,
        out_specs=pl.BlockSpec((tm, tn), lambda i, j, kk: (i, j)),
        out_shape=jax.ShapeDtypeStruct((m, n), out_dtype),
        scratch_shapes=scratch,
        compiler_params=_params(("parallel", "parallel", "arbitrary"),
                                blocks=[((tm, tk), a.dtype), ((tk, tn), b.dtype), ((tm, tn), out_dtype)]
                                + ([((tm, tn), F32)] if residual is not None else []),
                                scratch=[((tm, tn), F32)] if nk > 1 else [],
                                temps=[((tm, tn), F32)]),
        name=name,
    )(*args)


def _wo_kernel(oa_ref, og_ref, wt_ref, wb_ref, x_ref, o_ref, w_scr, *, ka, alpha):
    @pl.when(pl.program_id(1) == 0)
    def _():
        w_scr[:ka, :] = wt_ref[0].astype(BF16)
        w_scr[ka:, :] = wb_ref[0].astype(BF16)

    acc = _dot(oa_ref[...], w_scr[:ka, :]) + _dot(og_ref[...], w_scr[ka:, :])
    o_ref[...] = acc + alpha * x_ref[...]


def _w_o(o_a, o_g, w_o, x, *, tm, tn, alpha):
    s, ka = o_a.shape
    kg = o_g.shape[1]
    d = w_o.shape[2]
    assert ka % kg == 0
    return pl.pallas_call(
        functools.partial(_wo_kernel, ka=ka, alpha=alpha),
        grid=(d // tn, s // tm),
        in_specs=[pl.BlockSpec((tm, ka), lambda j, i: (i, 0)),
                  pl.BlockSpec((tm, kg), lambda j, i: (i, 0)),
                  pl.BlockSpec((1, ka, tn), lambda j, i: (0, 0, j)),
                  pl.BlockSpec((1, kg, tn), lambda j, i: (0, ka // kg, j)),
                  pl.BlockSpec((tm, tn), lambda j, i: (i, j))],
        out_specs=pl.BlockSpec((tm, tn), lambda j, i: (i, j)),
        out_shape=jax.ShapeDtypeStruct((s, d), F32),
        scratch_shapes=[pltpu.VMEM((ka + kg, tn), BF16)],
        compiler_params=_params(("parallel", "arbitrary"),
                                blocks=[((tm, ka), BF16), ((tm, kg), BF16), ((ka, tn), F32), ((kg, tn), F32),
                                        ((tm, tn), F32), ((tm, tn), F32)],
                                scratch=[((ka + kg, tn), BF16)],
                                temps=[((tm, tn), F32), ((tm, tn), F32)]),
        name="w_o",
    )(o_a, o_g, w_o, w_o, x)


def _ln_kernel(y_ref, g_ref, b_ref, *o_refs):
    y = y_ref[...]
    mu = jnp.mean(y, axis=1, keepdims=True)
    yc = y - mu
    var = jnp.mean(yc * yc, axis=1, keepdims=True)
    out = yc * lax.rsqrt(var + LN_EPS) * g_ref[...] + b_ref[...]
    for o_ref in o_refs:
        o_ref[...] = out.astype(o_ref.dtype)


def _layer_norm(y, g, b, *, tr, also_bf16):
    s, d = y.shape
    out_shape = [jax.ShapeDtypeStruct((s, d), F32)]
    out_specs = [pl.BlockSpec((tr, d), lambda i: (i, 0))]
    if also_bf16:
        out_shape.append(jax.ShapeDtypeStruct((s, d), BF16))
        out_specs.append(pl.BlockSpec((tr, d), lambda i: (i, 0)))
    return pl.pallas_call(
        _ln_kernel,
        grid=(s // tr,),
        in_specs=[pl.BlockSpec((tr, d), lambda i: (i, 0)),
                  pl.BlockSpec((1, d), lambda i: (0, 0)),
                  pl.BlockSpec((1, d), lambda i: (0, 0))],
        out_specs=out_specs,
        out_shape=out_shape,
        compiler_params=_params(("parallel",),
                                blocks=[((tr, d), F32), ((8, d), F32), ((8, d), F32)]
                                + [((tr, d), o.dtype) for o in out_shape],
                                temps=[((tr, d), F32), ((tr, d), F32)]),
        name="layer_norm",
    )(y, g.reshape(1, d), b.reshape(1, d))


def _idx_kernel(qi_ref, ki_ref, sm_ref, bias_ref, hi_scr, lo_scr, wb_scr, *, tq, tk, nkt, hi, group,
                topk, wscale):
    qb = pl.program_id(0)
    nk = ((qb + 1) * tq + tk - 1) // tk
    reps = tk // LANES
    i16 = jnp.int16
    low16 = -2 ** 15

    for h in range(hi):
        wb_scr[h] = jnp.broadcast_to(sm_ref[:, h:h + 1] * wscale, (tq, LANES))

    row = qb * tq + lax.broadcasted_iota(jnp.int32, (group, tk), 0)
    col = lax.broadcasted_iota(jnp.int32, (group, tk), 1)

    def lanes(a):
        return jnp.concatenate([a] * reps, axis=1) if reps > 1 else a

    def fold(a):
        out = a[:, :LANES]
        for r in range(1, reps):
            out = out + a[:, r * LANES:(r + 1) * LANES]
        return out

    def score_tile(kb, carry):
        kt = ki_ref[0, pl.ds(pl.multiple_of(kb * tk, tk), tk), :]
        for r in range(tq // group):
            rows = slice(r * group, (r + 1) * group)
            qs = qi_ref[:, rows, :].reshape(hi * group, IDX_DIM)
            d = _dot_nt(qs, kt)
            acc = jnp.zeros((group, tk), F32)
            for h in range(hi):
                acc = acc + jnp.maximum(d[h * group:(h + 1) * group], 0.0) * lanes(wb_scr[h, rows, :])
            bits = pltpu.bitcast(acc + 0.0, jnp.int32)
            key = jnp.where(kb * tk + col <= row + r * group,
                            bits ^ ((bits >> 31) & jnp.int32(0x7FFFFFFF)), INT_MIN)
            hi_scr[kb, rows, :] = (key >> 16).astype(i16)
            lo_scr[kb, rows, :] = ((key & 0xFFFF) + low16).astype(i16)
        return carry

    lax.fori_loop(0, nk, score_tile, 0)

    def count_ge(ref, cand):
        cand_t = lanes(cand.astype(i16))

        def step(kb, c):
            return c + fold(jnp.where(ref[kb] >= cand_t, i16(1), i16(0)))

        c = jnp.zeros((tq, LANES), i16)
        done = 0
        for width in COUNT_UNROLL:
            def step_many(q, c, width=width, done=done):
                for u in range(width):
                    c = step(done + q * width + u, c)
                return c

            n_groups = (nk - done) // width
            c = lax.fori_loop(0, n_groups, step_many, c)
            done = done + n_groups * width
        return jnp.sum(c.astype(F32), axis=1, keepdims=True)

    kf = float(topk)

    def bit_search(ref, base_cnt, cnt_start, stop_when_exact):
        def body(state):
            it, v, cnt_v = state
            cand = v + jnp.left_shift(jnp.int32(1), 15 - it)
            cnt = base_cnt + count_ge(ref, cand)
            ok = cnt >= kf
            return it + 1, jnp.where(ok, cand, v), jnp.where(ok, cnt, cnt_v)

        def unsettled(state):
            go = state[0] < 16
            if stop_when_exact:
                go = go & (jnp.max(jnp.where(state[2] == kf, 0.0, 1.0)) > 0.0)
            return go

        v0 = jnp.full((tq, LANES), low16, jnp.int32)
        _, v, cnt_v = lax.while_loop(unsettled, body, (jnp.int32(0), v0, cnt_start))
        return v, cnt_v

    n_all = jnp.zeros((tq, 1), F32) + (nk * tk).astype(F32)
    v_hi, n_from_hi = bit_search(hi_scr, 0.0, n_all, False)
    n_above = count_ge(hi_scr, v_hi + 1)
    v_hi_t = lanes(v_hi.astype(i16))

    def keep_equal(kb, carry):
        lo_scr[kb] = jnp.where(hi_scr[kb] == v_hi_t, lo_scr[kb], i16(low16))
        return carry

    lax.fori_loop(0, nk, keep_equal, 0)
    v_lo, n_sel = bit_search(lo_scr, n_above, n_from_hi, True)
    v_lo = jnp.where(v_hi == low16, jnp.maximum(v_lo, low16 + 1), v_lo)
    v_lo_t = lanes(v_lo.astype(i16))
    zero = jnp.zeros((tq, tk), bias_ref.dtype)
    neg = jnp.full((tq, tk), NEG_BIG, bias_ref.dtype)
    has_ties = jnp.max(jnp.where(n_sel > kf, 1.0, 0.0)) > 0.0

    @pl.when(jnp.logical_not(has_ties))
    def _():
        def emit(kb, carry):
            h16 = hi_scr[kb]
            sel = (h16 > v_hi_t) | ((h16 == v_hi_t) & (lo_scr[kb] >= v_lo_t))
            bias_ref[0, kb] = jnp.where(sel, zero, neg)
            return carry

        lax.fori_loop(0, nk, emit, 0)

    @pl.when(has_ties)
    def _():
        top16 = 2 ** 15 - 1
        n_gt = n_above + jnp.where(v_lo[:, :1] >= top16, 0.0, count_ge(lo_scr, jnp.minimum(v_lo + 1, top16)))
        need = kf - n_gt
        earlier = jnp.where(lax.broadcasted_iota(jnp.int32, (tk, tk), 0)
                            < lax.broadcasted_iota(jnp.int32, (tk, tk), 1), 1.0, 0.0).astype(BF16)
        one_b = jnp.ones((tq, tk), BF16)
        zero_b = jnp.zeros((tq, tk), BF16)

        def emit_ties(kb, seen):
            h16 = hi_scr[kb]
            l16 = lo_scr[kb]
            same_hi = h16 == v_hi_t
            larger = (h16 > v_hi_t) | (same_hi & (l16 > v_lo_t))
            tied = jnp.where(same_hi & (l16 == v_lo_t), one_b, zero_b)
            rank = seen + _dot(tied, earlier)
            keep = jnp.where(rank < need, 1.0, 0.0).astype(BF16)
            sel = larger | ((tied > 0.0) & (keep > 0.0))
            bias_ref[0, kb] = jnp.where(sel, zero, neg)
            return seen + jnp.sum(tied.astype(F32), axis=1, keepdims=True)

        lax.fori_loop(0, nk, emit_ties, jnp.zeros((tq, 1), F32))

    def fill(kb, carry):
        bias_ref[0, kb] = neg
        return carry

    lax.fori_loop(nk, nkt, fill, 0)


def _indexer(main, small, lay, *, s, tq, tk, topk):
    nkt = s // tk
    hi = IDX_HEADS
    group = min(tq, 64)
    qi0, ki0 = lay["qi"][0], lay["ki"][0]
    assert qi0 % hi == 0
    return pl.pallas_call(
        functools.partial(_idx_kernel, tq=tq, tk=tk, nkt=nkt, hi=hi, group=group, topk=topk,
                          wscale=IDX_HEADS ** -0.5 * IDX_DIM ** -0.5),
        grid=(s // tq,),
        in_specs=[pl.BlockSpec((hi, tq, LANES), lambda i: (qi0 // hi, i, 0)),
                  pl.BlockSpec((1, s, LANES), lambda i: (ki0, 0, 0)),
                  pl.BlockSpec((tq, LANES), lambda i: (i, 0))],
        out_specs=pl.BlockSpec((1, nkt, tq, tk), lambda i: (i, 0, 0, 0)),
        out_shape=jax.ShapeDtypeStruct((s // tq, nkt, tq, tk), BF16),
        scratch_shapes=[pltpu.VMEM((nkt, tq, tk), jnp.int16),
                        pltpu.VMEM((nkt, tq, tk), jnp.int16),
                        pltpu.VMEM((hi, tq, LANES), F32)],
        compiler_params=_params(("parallel",),
                                blocks=[((hi, tq, LANES), BF16), ((s, LANES), BF16), ((tq, LANES), F32),
                                        ((nkt, tq, tk), BF16)],
                                scratch=[((nkt, tq, tk), jnp.int16), ((nkt, tq, tk), jnp.int16),
                                         ((hi, tq, LANES), F32)],
                                temps=[((hi * group, tk), F32), ((hi * group, tk), F32)]),
        name="indexer",
    )(main, main, small)


def _attn_kernel(qi_ref, kb_ref, q_ref, k_ref, v_ref, b_ref, g_ref, o_ref, bias_scr, m_scr, acc_scr, *pipe,
                 nh, tq, tk):
    step = pl.program_id(0)
    qi = qi_ref[step]
    kb = kb_ref[step]

    @pl.when(kb == 0)
    def _():
        m_scr[...] = jnp.full(m_scr.shape, NEG_BIG, F32)
        acc_scr[...] = jnp.zeros(acc_scr.shape, F32)

    na, nb, sq, sk = b_ref.shape
    for a in range(na):
        for b in range(nb):
            bias_scr[a * sq:(a + 1) * sq, b * sk:(b + 1) * sk] = b_ref[a, b].astype(F32)

    ones = jnp.ones((tk, LANES), BF16)

    def qk(h, s_ref):
        s_ref[...] = _dot_nt(q_ref[h], k_ref[h])

    def soft(h, s_ref, p_ref, al_ref):
        s = s_ref[...] + bias_scr[...]
        m_prev = m_scr[h]
        m_new = jnp.maximum(m_prev, jnp.max(s, axis=1, keepdims=True))
        m_scr[h] = m_new
        p_ref[...] = jnp.exp2(s - m_new[:, :1]).astype(BF16)
        al_ref[...] = jnp.exp2(m_prev - m_new)

    def pv(h, p_ref, al_ref):
        al = al_ref[...]
        v_ext = jnp.concatenate([v_ref[h], ones], axis=1)
        acc_scr[h] = acc_scr[h] * jnp.concatenate([al, al], axis=1) + _dot(p_ref[...], v_ext)

    s_a, s_b, p_a, p_b, al_a, al_b = pipe
    qk(0, s_a)
    qk(1, s_b)
    soft(0, s_a, p_a, al_a)

    def pair(j, carry):
        qk(2 * j, s_a)
        soft(2 * j - 1, s_b, p_b, al_b)
        pv(2 * j - 2, p_a, al_a)
        qk(2 * j + 1, s_b)
        soft(2 * j, s_a, p_a, al_a)
        pv(2 * j - 1, p_b, al_b)
        return carry

    lax.fori_loop(1, nh // 2, pair, 0)
    soft(nh - 1, s_b, p_b, al_b)
    pv(nh - 2, p_a, al_a)
    pv(nh - 1, p_b, al_b)

    @pl.when(kb == (qi * tq + tq - 1) // tk)
    def _():
        ss = jnp.zeros((tq, LANES), F32)
        for h in range(nh):
            o_h = acc_scr[h, :, :LANES] / acc_scr[h, :, LANES:]
            acc_scr[h, :, :LANES] = o_h
            ss = ss + o_h * o_h
        ms = jnp.sum(ss, axis=1, keepdims=True) * (1.0 / (nh * LANES))
        r = lax.rsqrt(ms + RMS_EPS)
        for h in range(nh):
            sl = slice(h * LANES, (h + 1) * LANES)
            o_ref[:, sl] = (acc_scr[h, :, :LANES] * r * g_ref[:, sl]).astype(o_ref.dtype)


def _attention(main, bias, g, lay, *, s, tq, tk):
    nh = A_HEADS
    nqb, nkt, sq, sk = bias.shape
    q0, k0, v0 = lay["qa"][0], lay["ka"][0], lay["va"][0]
    assert q0 % nh == 0 and k0 % nh == 0 and v0 % nh == 0
    pairs = [(i, j) for i in range(s // tq) for j in range((i * tq + tq - 1) // tk + 1)]
    qi_tab = jnp.asarray(np.asarray([p[0] for p in pairs], np.int32))
    kb_tab = jnp.asarray(np.asarray([p[1] for p in pairs], np.int32))
    return pl.pallas_call(
        functools.partial(_attn_kernel, nh=nh, tq=tq, tk=tk),
        grid_spec=pltpu.PrefetchScalarGridSpec(
            num_scalar_prefetch=2,
            grid=(len(pairs),),
            in_specs=[pl.BlockSpec((nh, tq, LANES), lambda t, qi, kb: (q0 // nh, qi[t], 0)),
                      pl.BlockSpec((nh, tk, LANES), lambda t, qi, kb: (k0 // nh, kb[t], 0)),
                      pl.BlockSpec((nh, tk, LANES), lambda t, qi, kb: (v0 // nh, kb[t], 0)),
                      pl.BlockSpec((tq // sq, tk // sk, sq, sk), lambda t, qi, kb: (qi[t], kb[t], 0, 0)),
                      pl.BlockSpec((1, nh * LANES), lambda t, qi, kb: (0, 0))],
            out_specs=pl.BlockSpec((tq, nh * LANES), lambda t, qi, kb: (qi[t], 0)),
            scratch_shapes=[pltpu.VMEM((tq, tk), F32),
                            pltpu.VMEM((nh, tq, LANES), F32),
                            pltpu.VMEM((nh, tq, 2 * LANES), F32),
                            pltpu.VMEM((tq, tk), F32), pltpu.VMEM((tq, tk), F32),
                            pltpu.VMEM((tq, tk), BF16), pltpu.VMEM((tq, tk), BF16),
                            pltpu.VMEM((tq, LANES), F32), pltpu.VMEM((tq, LANES), F32)]),
        out_shape=jax.ShapeDtypeStruct((s, nh * LANES), BF16),
        compiler_params=_params(("arbitrary",),
                                blocks=[((nh, tq, LANES), BF16), ((nh, tk, LANES), BF16), ((nh, tk, LANES), BF16),
                                        ((tq, tk), BF16), ((8, nh * LANES), F32), ((tq, nh * LANES), BF16)],
                                scratch=[((tq, tk), F32), ((nh, tq, LANES), F32), ((nh, tq, 2 * LANES), F32),
                                         ((tq, tk), F32), ((tq, tk), F32), ((tq, tk), BF16), ((tq, tk), BF16),
                                         ((tq, LANES), F32), ((tq, LANES), F32)],
                                temps=[((tq, tk), F32), ((tq, tk), F32)]),
        name="attention",
    )(qi_tab, kb_tab, main, main, main, bias, g.reshape(1, nh * LANES))


def _gla_kernel(q_ref, k_ref, v_ref, gg_ref, sm_ref, w2_ref, bgk_ref, gn_ref, o_ref, st_scr, *,
                rows, chunk, nheads, dk, dv):
    @pl.when(pl.program_id(0) == 0)
    def _():
        st_scr[...] = jnp.zeros(st_scr.shape, F32)

    r_i = lax.broadcasted_iota(jnp.int32, (chunk, chunk), 0)
    c_i = lax.broadcasted_iota(jnp.int32, (chunk, chunk), 1)
    tri = jnp.where(r_i >= c_i, 1.0, 0.0).astype(BF16)
    nbk, nbv = dk // LANES, dv // LANES
    assert chunk >= 16 and chunk & (chunk - 1) == 0
    leaf = 2
    half_sizes = [chunk >> i for i in range(1, chunk.bit_length()) if chunk >> i >= leaf]
    quadrant = {m: (r_i // (2 * m) == c_i // (2 * m)) & (r_i % (2 * m) >= m) & (c_i % (2 * m) < m)
                for m in half_sizes}
    near = [(c_i == r_i - j) & (r_i % leaf >= j) for j in range(leaf)]
    sub = lax.broadcasted_iota(jnp.int32, (chunk // 8, 8, dk), 1)

    def reference_rows(b, m):
        if m >= 8:
            return jnp.concatenate(
                [jnp.broadcast_to(b[blk + m:blk + m + 1, :], (2 * m, dk)) for blk in range(0, chunk, 2 * m)],
                axis=0)
        b3 = b.reshape(chunk // 8, 8, dk)
        out = None
        for blk in range(0, 8, 2 * m):
            rows = jnp.broadcast_to(b3[:, blk + m:blk + m + 1, :], b3.shape)
            out = rows if out is None else jnp.where(sub >= blk, rows, out)
        return out.reshape(chunk, dk)

    def wide(ref, first, n, r0):
        return jnp.concatenate([ref[first + j, pl.ds(r0, chunk), :] for j in range(n)], axis=1)

    def step(c, carry):
        r0 = pl.multiple_of(c * chunk, chunk)
        z_all = _dot(sm_ref[pl.ds(r0, chunk), :].astype(BF16), w2_ref[...]) + bgk_ref[...]
        for h in range(nheads):
            q = wide(q_ref, h * nbk, nbk, r0).astype(F32) * (dk ** -0.5)
            k = wide(k_ref, h * nbk, nbk, r0).astype(F32)
            v = wide(v_ref, h * nbv, nbv, r0)
            z = z_all[:, h * dk:(h + 1) * dk]
            g = (jnp.minimum(z, 0.0) - jnp.log1p(jnp.exp(-jnp.abs(z)))) * (1.0 / G_TAU)
            g_hi = g.astype(BF16)
            rem = g - g_hi.astype(F32)
            g_mid = rem.astype(BF16)
            g_lo = (rem - g_mid.astype(F32)).astype(BF16)
            b = _dot(tri, g_hi) + _dot(tri, g_mid) + _dot(tri, g_lo)
            b_last = b[chunk - 1:chunk, :]
            att = jnp.zeros((chunk, chunk), F32)
            for m in half_sizes:
                e = jnp.exp(-jnp.abs(b - reference_rows(b, m)))
                att = att + jnp.where(quadrant[m], _dot_nt((q * e).astype(BF16), (k * e).astype(BF16)), 0.0)
            for j in range(leaf):
                kj = k if j == 0 else pltpu.roll(k, j, 0)
                bj = b if j == 0 else pltpu.roll(b, j, 0)
                pair = jnp.sum(q * kj * jnp.exp(jnp.minimum(b - bj, 0.0)), axis=1, keepdims=True)
                att = att + jnp.where(near[j], pair, 0.0)
            st = st_scr[h]
            o = _dot(att.astype(BF16), v) + _dot_nt((q * jnp.exp(b)).astype(BF16), st.astype(BF16))
            kd = (k * jnp.exp(b_last - b)).astype(BF16)
            st_scr[h] = st * jnp.exp(b_last) + _dot_tn(v, kd)
            ms = jnp.mean(o * o, axis=1, keepdims=True)
            gate = wide(gg_ref, h * nbv, nbv, r0).astype(F32)
            out = (o * lax.rsqrt(ms + RMS_EPS)) * gn_ref[...] * (gate * jax.nn.sigmoid(gate))
            o_ref[pl.ds(r0, chunk), h * dv:(h + 1) * dv] = out.astype(o_ref.dtype)
        return carry

    lax.fori_loop(0, rows // chunk, step, 0)


def _gla(main, small, w2, bgk, gn, lay, *, s, rows, chunk, dk, dv):
    nh = G_HEADS
    nqk, nv = nh * dk // LANES, nh * dv // LANES
    q0, k0, v0, g0 = lay["qg"][0], lay["kg"][0], lay["vg"][0], lay["gg"][0]
    assert q0 % nqk == 0 and k0 % nqk == 0 and v0 % nv == 0 and g0 % nv == 0
    return pl.pallas_call(
        functools.partial(_gla_kernel, rows=rows, chunk=chunk, nheads=nh, dk=dk, dv=dv),
        grid=(s // rows,),
        in_specs=[pl.BlockSpec((nqk, rows, LANES), lambda r: (q0 // nqk, r, 0)),
                  pl.BlockSpec((nqk, rows, LANES), lambda r: (k0 // nqk, r, 0)),
                  pl.BlockSpec((nv, rows, LANES), lambda r: (v0 // nv, r, 0)),
                  pl.BlockSpec((nv, rows, LANES), lambda r: (g0 // nv, r, 0)),
                  pl.BlockSpec((rows, LANES), lambda r: (r, 0)),
                  pl.BlockSpec((LANES, nh * dk), lambda r: (0, 0)),
                  pl.BlockSpec((1, nh * dk), lambda r: (0, 0)),
                  pl.BlockSpec((1, dv), lambda r: (0, 0))],
        out_specs=pl.BlockSpec((rows, nh * dv), lambda r: (r, 0)),
        out_shape=jax.ShapeDtypeStruct((s, nh * dv), BF16),
        scratch_shapes=[pltpu.VMEM((nh, dv, dk), F32)],
        compiler_params=_params(("arbitrary",),
                                blocks=[((nqk, rows, LANES), BF16), ((nqk, rows, LANES), BF16),
                                        ((nv, rows, LANES), BF16), ((nv, rows, LANES), BF16),
                                        ((rows, LANES), F32), ((LANES, nh * dk), BF16), ((8, nh * dk), F32),
                                        ((8, dv), F32), ((rows, nh * dv), BF16)],
                                scratch=[((nh, dv, dk), F32)],
                                temps=[((chunk, nh * dk), F32)] * 8 + [((dv, dk), F32)] * 2),
        name="gla",
    )(main, main, main, main, small, w2, bgk, gn.reshape(1, dv))


def _ffn_up_kernel(h_ref, wg_ref, wv_ref, cwg_ref, cwv_ref, cbg_ref, cbv_ref, o_ref, w_scr, hw_scr,
                   *, tm, tn, bounds):
    @pl.when(pl.program_id(1) == 0)
    def _():
        w_scr[:, :tn] = wg_ref[0].astype(BF16)
        w_scr[:, tn:] = wv_ref[0].astype(BF16)
        hw_scr[0:8, :] = jnp.zeros((8, 2 * tn), F32)

    cw = jnp.concatenate([cwg_ref[0], cwv_ref[0]], axis=1)
    cb = jnp.concatenate([cbg_ref[...], cbv_ref[...]], axis=1)
    w0, w1, w2 = cw[0:1, :], cw[1:2, :], cw[2:3, :]

    for lo, hi in zip(bounds[:-1], bounds[1:]):
        hw = _dot(h_ref[lo:hi, :], w_scr[...])
        hw_scr[8 + lo:8 + hi, :] = hw
        u = w2 * hw + w1 * hw_scr[7 + lo:7 + hi, :] + w0 * hw_scr[6 + lo:6 + hi, :] + cb
        gate, val = u[:, :tn], u[:, tn:]
        o_ref[lo:hi, :] = (gate * jax.nn.sigmoid(gate) * val).astype(o_ref.dtype)
    hw_scr[0:8, :] = hw_scr[tm:tm + 8, :]


def _ffn_up(hb, w_up, conv_w, conv_b, *, d_ff, tm, tn):
    s, d = hb.shape
    nj = d_ff // tn
    return pl.pallas_call(
        functools.partial(_ffn_up_kernel, tm=tm, tn=tn, bounds=(0, tm // 2, tm)),
        grid=(nj, s // tm),
        in_specs=[pl.BlockSpec((tm, d), lambda j, i: (i, 0)),
                  pl.BlockSpec((1, d, tn), lambda j, i: (0, 0, j)),
                  pl.BlockSpec((1, d, tn), lambda j, i: (0, 0, nj + j)),
                  pl.BlockSpec((1, CONV_W, tn), lambda j, i: (0, 0, j)),
                  pl.BlockSpec((1, CONV_W, tn), lambda j, i: (0, 0, nj + j)),
                  pl.BlockSpec((1, tn), lambda j, i: (0, j)),
                  pl.BlockSpec((1, tn), lambda j, i: (0, nj + j))],
        out_specs=pl.BlockSpec((tm, tn), lambda j, i: (i, j)),
        out_shape=jax.ShapeDtypeStruct((s, d_ff), BF16),
        scratch_shapes=[pltpu.VMEM((d, 2 * tn), BF16), pltpu.VMEM((8 + tm, 2 * tn), F32)],
        compiler_params=_params(("parallel", "arbitrary"),
                                blocks=[((tm, d), BF16), ((d, tn), F32), ((d, tn), F32), ((8, tn), F32),
                                        ((8, tn), F32), ((8, tn), F32), ((8, tn), F32), ((tm, tn), BF16)],
                                scratch=[((d, 2 * tn), BF16), ((8 + tm, 2 * tn), F32)],
                                temps=[((tm // 2, 2 * tn), F32)] * 3),
        name="ffn_up",
    )(hb, w_up, w_up, conv_w, conv_w, conv_b, conv_b)


def _layout(d_model):
    a_width = A_HEADS * A_HEAD_DIM
    g_width = d_model - a_width
    g_kwidth = g_width // 2
    names = ("qa", "ka", "va", "qi", "ki", "wi", "qg", "kg", "vg", "glr", "gg")
    sizes = (a_width, a_width, a_width, IDX_HEADS * IDX_DIM, IDX_DIM, IDX_HEADS,
             g_kwidth, g_kwidth, g_width, G_LOWRANK, g_width)
    offs = np.concatenate([[0], np.cumsum(sizes)])
    src = {n: (int(offs[i]), int(offs[i + 1])) for i, n in enumerate(names)}
    order = ("qi", "qa", "ka", "va", "qg", "kg", "vg", "gg", "ki")
    lay, blk = {}, 0
    for n in order:
        width = src[n][1] - src[n][0]
        assert width % LANES == 0
        lay[n] = (blk, width // LANES)
        blk += width // LANES
    return src, order, lay, blk


def kernel(x, w_in, w_gk2, b_gk, attn_out_g, gla_norm_g, w_o, ln1_g, ln1_b, w_up, conv_w, conv_b,
           w_down, ln2_g, ln2_b):
    assert x.shape[0] == 1 and w_in.shape[0] == DEPTH == 1
    _, s, d = x.shape
    x2 = x[0]
    src, order, lay, nb_main = _layout(d)
    g_width = d - A_HEADS * A_HEAD_DIM
    dv = g_width // G_HEADS
    dk = dv // 2
    d_ff = w_down.shape[1]
    topk = min(TOPK_MAX, s // 4)

    w_proj = _relayout_w_in(jnp.swapaxes(w_in, 1, 2), src, order)
    n_small = IDX_HEADS + G_LOWRANK
    w2 = jnp.zeros((LANES, G_HEADS * dk), F32).at[IDX_HEADS:n_small].set(w_gk2[0]).astype(BF16)
    bgk = b_gk[0].reshape(1, G_HEADS * dk)
    wd = w_down[0].astype(BF16)
    n_main = (nb_main + 1) * LANES
    q_lo, q_hi = lay["qa"][0] * LANES, (lay["qa"][0] + lay["qa"][1]) * LANES
    col = np.ones((1, n_main), np.float32)
    col[:, q_lo:q_hi] = A_HEAD_DIM ** -0.5 * LOG2E
    col_scale = jnp.asarray(col)

    tm = _tile(s, 1024)
    main, small = _proj_blocks(x2, w_proj, col_scale, tm=_tile(s, 512), tn=_tile(n_main, 1280))
    bias = _indexer(main, small, lay, s=s, tq=256, tk=256, topk=topk)
    ta = _tile(s, 512)
    o_a = _attention(main, bias, attn_out_g[0], lay, s=s, tq=ta, tk=ta)
    o_g = _gla(main, small, w2, bgk, gla_norm_g[0], lay, s=s, rows=_tile(s, 512), chunk=128, dk=dk, dv=dv)
    y1 = _w_o(o_a, o_g, w_o, x2, tm=tm, tn=_tile(d, 512), alpha=DN_ALPHA)
    h, hb = _layer_norm(y1, ln1_g[0], ln1_b[0], tr=_tile(s, 256, 8), also_bf16=True)

    act = _ffn_up(hb, w_up, conv_w, conv_b[0].reshape(1, 2 * d_ff), d_ff=d_ff, tm=tm, tn=_tile(d_ff, 256))
    y2 = _matmul(act, wd, tm=_tile(s, 512), tn=_tile(d, 512), tk=d_ff, out_dtype=F32, residual=h,
                 alpha=DN_ALPHA, name="w_down")
    (out,) = _layer_norm(y2, ln2_g[0], ln2_b[0], tr=_tile(s, 256, 8), also_bf16=False)
    return out[None]
```

```python
import functools

import numpy as np
import jax
import jax.numpy as jnp
from jax import lax
from jax.experimental import pallas as pl
from jax.experimental.pallas import tpu as pltpu

A_HEADS = 16
A_HEAD_DIM = 128
IDX_HEADS = 32
IDX_DIM = 128
TOPK_MAX = 256
G_HEADS = 4
G_LOWRANK = 16
G_TAU = 16.0
CONV_W = 3
LN_EPS = 1e-5
RMS_EPS = 1e-6
DEPTH = 1
DN_ALPHA = (2 * DEPTH) ** 0.25

LANES = 128
MIB = 1024 * 1024
VMEM_BUDGET = 56 * MIB

BF16 = jnp.bfloat16
F32 = jnp.float32
NEG_BIG = -1e30
LOG2E = 1.4426950408889634
INT_MIN = -2 ** 31
COUNT_UNROLL = (8, 2, 1)


def _nbytes(shape, dtype):
    n = jnp.dtype(dtype).itemsize
    for dim in shape:
        n *= 1 if dim is None else getattr(dim, "block_size", dim)
    return n


def _params(sem, blocks, scratch=(), temps=()):
    need = (2 * sum(_nbytes(*b) for b in blocks) + sum(_nbytes(*b) for b in scratch)
            + sum(_nbytes(*b) for b in temps))
    limit = -(-need // MIB) * MIB
    assert limit <= VMEM_BUDGET, (limit, VMEM_BUDGET)
    return pltpu.CompilerParams(dimension_semantics=sem, vmem_limit_bytes=limit)


def _tile(n, max_tile, quantum=LANES):
    best = None
    for t in range(quantum, min(n, max_tile) + 1, quantum):
        if n % t == 0:
            best = t
    assert best is not None, (n, max_tile, quantum)
    return best


def _dot(a, b):
    return jnp.dot(a, b, preferred_element_type=F32)


def _dot_nt(a, b):
    return lax.dot_general(a, b, (((1,), (1,)), ((), ())), preferred_element_type=F32)


def _dot_tn(a, b):
    return lax.dot_general(a, b, (((0,), (0,)), ((), ())), preferred_element_type=F32)


def _relayout_kernel(off_ref, kind_ref, a_ref, b_ref, o_ref, *, n_wi, n_small):
    del off_ref
    kind = kind_ref[pl.program_id(0)]

    @pl.when(kind == 0)
    def _():
        o_ref[...] = a_ref[0].astype(BF16)

    @pl.when(kind == 1)
    def _():
        o_ref[...] = jnp.zeros(o_ref.shape, BF16)

    @pl.when(kind == 2)
    def _():
        r = lax.broadcasted_iota(jnp.int32, o_ref.shape, 0)
        o_ref[...] = jnp.where(r < n_wi, a_ref[0], jnp.where(r < n_small, b_ref[0], 0.0)).astype(BF16)


def _relayout_w_in(w_t, src, order):
    _, n_in, d = w_t.shape
    starts = [src[n][0] + t * LANES for n in order for t in range((src[n][1] - src[n][0]) // LANES)]
    wi0, glr0 = src["wi"][0], src["glr"][0]
    n_wi = src["wi"][1] - wi0
    n_small = n_wi + src["glr"][1] - glr0
    glr_win = glr0 - n_wi
    assert n_small <= LANES and glr_win >= 0 and wi0 + LANES <= n_in and glr_win + LANES <= n_in
    assert all(o % 8 == 0 for o in starts + [wi0, glr_win])
    row_off8 = [o // 8 for o in starts + [0, wi0]]
    kind = [0] * len(starts) + [1, 2]
    nb = len(kind)
    tables = [jnp.asarray(np.asarray(t, np.int32)) for t in (row_off8, kind)]
    return pl.pallas_call(
        functools.partial(_relayout_kernel, n_wi=n_wi, n_small=n_small),
        grid_spec=pltpu.PrefetchScalarGridSpec(
            num_scalar_prefetch=2,
            grid=(nb,),
            in_specs=[pl.BlockSpec((pl.Element(1), pl.Element(LANES), pl.Element(d)),
                                   lambda b, off, kd: (0, off[b] * 8, 0)),
                      pl.BlockSpec((pl.Element(1), pl.Element(LANES), pl.Element(d)),
                                   lambda b, off, kd: (0, glr_win, 0))],
            out_specs=pl.BlockSpec((LANES, d), lambda b, off, kd: (b, 0))),
        out_shape=jax.ShapeDtypeStruct((nb * LANES, d), BF16),
        compiler_params=_params(("arbitrary",),
                                blocks=[((LANES, d), F32), ((LANES, d), F32), ((LANES, d), BF16)],
                                temps=[((LANES, d), F32)]),
        name="relayout_w_in",
    )(*tables, w_t, w_t)


def _proj_kernel(x_ref, w_ref, ws_ref, sc_ref, o_ref, os_ref, xb_scr):
    @pl.when(pl.program_id(1) == 0)
    def _():
        xb_scr[...] = x_ref[...].astype(BF16)
        os_ref[...] = _dot_nt(xb_scr[...], ws_ref[...])

    res = _dot_nt(xb_scr[...], w_ref[...]) * sc_ref[...]
    for c in range(o_ref.shape[0]):
        o_ref[c] = res[:, c * LANES:(c + 1) * LANES].astype(o_ref.dtype)


def _proj_blocks(x, wt, col_scale, *, tm, tn):
    s, k = x.shape
    n = wt.shape[0] - LANES
    return pl.pallas_call(
        _proj_kernel,
        grid=(s // tm, n // tn),
        in_specs=[pl.BlockSpec((tm, k), lambda i, j: (i, 0)),
                  pl.BlockSpec((tn, k), lambda i, j: (j, 0)),
                  pl.BlockSpec((LANES, k), lambda i, j: (n // LANES, 0)),
                  pl.BlockSpec((1, tn), lambda i, j: (0, j))],
        out_specs=[pl.BlockSpec((tn // LANES, tm, LANES), lambda i, j: (j, i, 0)),
                   pl.BlockSpec((tm, LANES), lambda i, j: (i, 0))],
        out_shape=[jax.ShapeDtypeStruct((n // LANES, s, LANES), BF16),
                   jax.ShapeDtypeStruct((s, LANES), F32)],
        scratch_shapes=[pltpu.VMEM((tm, k), BF16)],
        compiler_params=_params(("parallel", "arbitrary"),
                                blocks=[((tm, k), F32), ((tn, k), BF16), ((LANES, k), BF16), ((8, tn), F32),
                                        ((tm, tn), BF16), ((tm, LANES), F32)],
                                scratch=[((tm, k), BF16)],
                                temps=[((tm, tn), F32), ((tm, tn), F32)]),
        name="proj",
    )(x, wt, wt, col_scale)


def _mm_kernel(a_ref, b_ref, *rest, alpha, has_res, nk):
    if has_res:
        r_ref, o_ref = rest[0], rest[1]
        rest = rest[2:]
    else:
        r_ref, o_ref = None, rest[0]
        rest = rest[1:]

    def finish(acc):
        if has_res:
            acc = acc + alpha * r_ref[...]
        o_ref[...] = acc.astype(o_ref.dtype)

    if nk == 1:
        finish(_dot(a_ref[...], b_ref[...]))
        return
    acc_ref = rest[0]
    kk = pl.program_id(2)

    @pl.when(kk == 0)
    def _():
        acc_ref[...] = _dot(a_ref[...], b_ref[...])

    @pl.when(kk > 0)
    def _():
        acc_ref[...] += _dot(a_ref[...], b_ref[...])

    @pl.when(kk == nk - 1)
    def _():
        finish(acc_ref[...])


def _matmul(a, b, *, tm, tn, tk, out_dtype, residual=None, alpha=1.0, name="matmul"):
    m, k = a.shape
    n = b.shape[1]
    nk = k // tk
    in_specs = [pl.BlockSpec((tm, tk), lambda i, j, kk: (i, kk)),
                pl.BlockSpec((tk, tn), lambda i, j, kk: (kk, j))]
    args = [a, b]
    if residual is not None:
        in_specs.append(pl.BlockSpec((tm, tn), lambda i, j, kk: (i, j)))
        args.append(residual)
    scratch = [pltpu.VMEM((tm, tn), F32)] if nk > 1 else []
    return pl.pallas_call(
        functools.partial(_mm_kernel, alpha=alpha, has_res=residual is not None, nk=nk),
        grid=(m // tm, n // tn, nk),
        in_specs=in_specs,
        out_specs=pl.BlockSpec((tm, tn), lambda i, j, kk: (i, j)),
        out_shape=jax.ShapeDtypeStruct((m, n), out_dtype),
        scratch_shapes=scratch,
        compiler_params=_params(("parallel", "parallel", "arbitrary"),
                                blocks=[((tm, tk), a.dtype), ((tk, tn), b.dtype), ((tm, tn), out_dtype)]
                                + ([((tm, tn), F32)] if residual is not None else []),
                                scratch=[((tm, tn), F32)] if nk > 1 else [],
                                temps=[((tm, tn), F32)]),
        name=name,
    )(*args)


def _wo_kernel(oa_ref, og_ref, wt_ref, wb_ref, x_ref, o_ref, w_scr, *, ka, alpha):
    @pl.when(pl.program_id(1) == 0)
    def _():
        w_scr[:ka, :] = wt_ref[0].astype(BF16)
        w_scr[ka:, :] = wb_ref[0].astype(BF16)

    acc = _dot(oa_ref[...], w_scr[:ka, :]) + _dot(og_ref[...], w_scr[ka:, :])
    o_ref[...] = acc + alpha * x_ref[...]


def _w_o(o_a, o_g, w_o, x, *, tm, tn, alpha):
    s, ka = o_a.shape
    kg = o_g.shape[1]
    d = w_o.shape[2]
    assert ka % kg == 0
    return pl.pallas_call(
        functools.partial(_wo_kernel, ka=ka, alpha=alpha),
        grid=(d // tn, s // tm),
        in_specs=[pl.BlockSpec((tm, ka), lambda j, i: (i, 0)),
                  pl.BlockSpec((tm, kg), lambda j, i: (i, 0)),
                  pl.BlockSpec((1, ka, tn), lambda j, i: (0, 0, j)),
                  pl.BlockSpec((1, kg, tn), lambda j, i: (0, ka // kg, j)),
                  pl.BlockSpec((tm, tn), lambda j, i: (i, j))],
        out_specs=pl.BlockSpec((tm, tn), lambda j, i: (i, j)),
        out_shape=jax.ShapeDtypeStruct((s, d), F32),
        scratch_shapes=[pltpu.VMEM((ka + kg, tn), BF16)],
        compiler_params=_params(("parallel", "arbitrary"),
                                blocks=[((tm, ka), BF16), ((tm, kg), BF16), ((ka, tn), F32), ((kg, tn), F32),
                                        ((tm, tn), F32), ((tm, tn), F32)],
                                scratch=[((ka + kg, tn), BF16)],
                                temps=[((tm, tn), F32), ((tm, tn), F32)]),
        name="w_o",
    )(o_a, o_g, w_o, w_o, x)


def _ln_kernel(y_ref, g_ref, b_ref, *o_refs):
    y = y_ref[...]
    mu = jnp.mean(y, axis=1, keepdims=True)
    yc = y - mu
    var = jnp.mean(yc * yc, axis=1, keepdims=True)
    out = yc * lax.rsqrt(var + LN_EPS) * g_ref[...] + b_ref[...]
    for o_ref in o_refs:
        o_ref[...] = out.astype(o_ref.dtype)


def _layer_norm(y, g, b, *, tr, also_bf16):
    s, d = y.shape
    out_shape = [jax.ShapeDtypeStruct((s, d), F32)]
    out_specs = [pl.BlockSpec((tr, d), lambda i: (i, 0))]
    if also_bf16:
        out_shape.append(jax.ShapeDtypeStruct((s, d), BF16))
        out_specs.append(pl.BlockSpec((tr, d), lambda i: (i, 0)))
    return pl.pallas_call(
        _ln_kernel,
        grid=(s // tr,),
        in_specs=[pl.BlockSpec((tr, d), lambda i: (i, 0)),
                  pl.BlockSpec((1, d), lambda i: (0, 0)),
                  pl.BlockSpec((1, d), lambda i: (0, 0))],
        out_specs=out_specs,
        out_shape=out_shape,
        compiler_params=_params(("parallel",),
                                blocks=[((tr, d), F32), ((8, d), F32), ((8, d), F32)]
                                + [((tr, d), o.dtype) for o in out_shape],
                                temps=[((tr, d), F32), ((tr, d), F32)]),
        name="layer_norm",
    )(y, g.reshape(1, d), b.reshape(1, d))


def _idx_kernel(qi_ref, ki_ref, sm_ref, bias_ref, hi_scr, lo_scr, wb_scr, *, tq, tk, nkt, hi, group,
                topk, wscale):
    qb = pl.program_id(0)
    nk = ((qb + 1) * tq + tk - 1) // tk
    reps = tk // LANES
    i16 = jnp.int16
    low16 = -2 ** 15

    for h in range(hi):
        wb_scr[h] = jnp.broadcast_to(sm_ref[:, h:h + 1] * wscale, (tq, LANES))

    row = qb * tq + lax.broadcasted_iota(jnp.int32, (group, tk), 0)
    col = lax.broadcasted_iota(jnp.int32, (group, tk), 1)

    def lanes(a):
        return jnp.concatenate([a] * reps, axis=1) if reps > 1 else a

    def fold(a):
        out = a[:, :LANES]
        for r in range(1, reps):
            out = out + a[:, r * LANES:(r + 1) * LANES]
        return out

    def score_tile(kb, carry):
        kt = ki_ref[0, pl.ds(pl.multiple_of(kb * tk, tk), tk), :]
        for r in range(tq // group):
            rows = slice(r * group, (r + 1) * group)
            qs = qi_ref[:, rows, :].reshape(hi * group, IDX_DIM)
            d = _dot_nt(qs, kt)
            acc = jnp.zeros((group, tk), F32)
            for h in range(hi):
                acc = acc + jnp.maximum(d[h * group:(h + 1) * group], 0.0) * lanes(wb_scr[h, rows, :])
            bits = pltpu.bitcast(acc + 0.0, jnp.int32)
            key = jnp.where(kb * tk + col <= row + r * group,
                            bits ^ ((bits >> 31) & jnp.int32(0x7FFFFFFF)), INT_MIN)
            hi_scr[kb, rows, :] = (key >> 16).astype(i16)
            lo_scr[kb, rows, :] = ((key & 0xFFFF) + low16).astype(i16)
        return carry

    lax.fori_loop(0, nk, score_tile, 0)

    def count_ge(ref, cand):
        cand_t = lanes(cand.astype(i16))

        def step(kb, c):
            return c + fold(jnp.where(ref[kb] >= cand_t, i16(1), i16(0)))

        c = jnp.zeros((tq, LANES), i16)
        done = 0
        for width in COUNT_UNROLL:
            def step_many(q, c, width=width, done=done):
                for u in range(width):
                    c = step(done + q * width + u, c)
                return c

            n_groups = (nk - done) // width
            c = lax.fori_loop(0, n_groups, step_many, c)
            done = done + n_groups * width
        return jnp.sum(c.astype(F32), axis=1, keepdims=True)

    kf = float(topk)

    def bit_search(ref, base_cnt, cnt_start, stop_when_exact):
        def body(state):
            it, v, cnt_v = state
            cand = v + jnp.left_shift(jnp.int32(1), 15 - it)
            cnt = base_cnt + count_ge(ref, cand)
            ok = cnt >= kf
            return it + 1, jnp.where(ok, cand, v), jnp.where(ok, cnt, cnt_v)

        def unsettled(state):
            go = state[0] < 16
            if stop_when_exact:
                go = go & (jnp.max(jnp.where(state[2] == kf, 0.0, 1.0)) > 0.0)
            return go

        v0 = jnp.full((tq, LANES), low16, jnp.int32)
        _, v, cnt_v = lax.while_loop(unsettled, body, (jnp.int32(0), v0, cnt_start))
        return v, cnt_v

    n_all = jnp.zeros((tq, 1), F32) + (nk * tk).astype(F32)
    v_hi, n_from_hi = bit_search(hi_scr, 0.0, n_all, False)
    n_above = count_ge(hi_scr, v_hi + 1)
    v_hi_t = lanes(v_hi.astype(i16))

    def keep_equal(kb, carry):
        lo_scr[kb] = jnp.where(hi_scr[kb] == v_hi_t, lo_scr[kb], i16(low16))
        return carry

    lax.fori_loop(0, nk, keep_equal, 0)
    v_lo, n_sel = bit_search(lo_scr, n_above, n_from_hi, True)
    v_lo = jnp.where(v_hi == low16, jnp.maximum(v_lo, low16 + 1), v_lo)
    v_lo_t = lanes(v_lo.astype(i16))
    zero = jnp.zeros((tq, tk), bias_ref.dtype)
    neg = jnp.full((tq, tk), NEG_BIG, bias_ref.dtype)
    has_ties = jnp.max(jnp.where(n_sel > kf, 1.0, 0.0)) > 0.0

    @pl.when(jnp.logical_not(has_ties))
    def _():
        def emit(kb, carry):
            h16 = hi_scr[kb]
            sel = (h16 > v_hi_t) | ((h16 == v_hi_t) & (lo_scr[kb] >= v_lo_t))
            bias_ref[0, kb] = jnp.where(sel, zero, neg)
            return carry

        lax.fori_loop(0, nk, emit, 0)

    @pl.when(has_ties)
    def _():
        top16 = 2 ** 15 - 1
        n_gt = n_above + jnp.where(v_lo[:, :1] >= top16, 0.0, count_ge(lo_scr, jnp.minimum(v_lo + 1, top16)))
        need = kf - n_gt
        earlier = jnp.where(lax.broadcasted_iota(jnp.int32, (tk, tk), 0)
                            < lax.broadcasted_iota(jnp.int32, (tk, tk), 1), 1.0, 0.0).astype(BF16)
        one_b = jnp.ones((tq, tk), BF16)
        zero_b = jnp.zeros((tq, tk), BF16)

        def emit_ties(kb, seen):
            h16 = hi_scr[kb]
            l16 = lo_scr[kb]
            same_hi = h16 == v_hi_t
            larger = (h16 > v_hi_t) | (same_hi & (l16 > v_lo_t))
            tied = jnp.where(same_hi & (l16 == v_lo_t), one_b, zero_b)
            rank = seen + _dot(tied, earlier)
            keep = jnp.where(rank < need, 1.0, 0.0).astype(BF16)
            sel = larger | ((tied > 0.0) & (keep > 0.0))
            bias_ref[0, kb] = jnp.where(sel, zero, neg)
            return seen + jnp.sum(tied.astype(F32), axis=1, keepdims=True)

        lax.fori_loop(0, nk, emit_ties, jnp.zeros((tq, 1), F32))

    def fill(kb, carry):
        bias_ref[0, kb] = neg
        return carry

    lax.fori_loop(nk, nkt, fill, 0)


def _indexer(main, small, lay, *, s, tq, tk, topk):
    nkt = s // tk
    hi = IDX_HEADS
    group = min(tq, 64)
    qi0, ki0 = lay["qi"][0], lay["ki"][0]
    assert qi0 % hi == 0
    return pl.pallas_call(
        functools.partial(_idx_kernel, tq=tq, tk=tk, nkt=nkt, hi=hi, group=group, topk=topk,
                          wscale=IDX_HEADS ** -0.5 * IDX_DIM ** -0.5),
        grid=(s // tq,),
        in_specs=[pl.BlockSpec((hi, tq, LANES), lambda i: (qi0 // hi, i, 0)),
                  pl.BlockSpec((1, s, LANES), lambda i: (ki0, 0, 0)),
                  pl.BlockSpec((tq, LANES), lambda i: (i, 0))],
        out_specs=pl.BlockSpec((1, nkt, tq, tk), lambda i: (i, 0, 0, 0)),
        out_shape=jax.ShapeDtypeStruct((s // tq, nkt, tq, tk), BF16),
        scratch_shapes=[pltpu.VMEM((nkt, tq, tk), jnp.int16),
                        pltpu.VMEM((nkt, tq, tk), jnp.int16),
                        pltpu.VMEM((hi, tq, LANES), F32)],
        compiler_params=_params(("parallel",),
                                blocks=[((hi, tq, LANES), BF16), ((s, LANES), BF16), ((tq, LANES), F32),
                                        ((nkt, tq, tk), BF16)],
                                scratch=[((nkt, tq, tk), jnp.int16), ((nkt, tq, tk), jnp.int16),
                                         ((hi, tq, LANES), F32)],
                                temps=[((hi * group, tk), F32), ((hi * group, tk), F32)]),
        name="indexer",
    )(main, main, small)


def _attn_kernel(qi_ref, kb_ref, q_ref, k_ref, v_ref, b_ref, g_ref, o_ref, bias_scr, m_scr, acc_scr, *pipe,
                 nh, tq, tk):
    step = pl.program_id(0)
    qi = qi_ref[step]
    kb = kb_ref[step]

    @pl.when(kb == 0)
    def _():
        m_scr[...] = jnp.full(m_scr.shape, NEG_BIG, F32)
        acc_scr[...] = jnp.zeros(acc_scr.shape, F32)

    na, nb, sq, sk = b_ref.shape
    for a in range(na):
        for b in range(nb):
            bias_scr[a * sq:(a + 1) * sq, b * sk:(b + 1) * sk] = b_ref[a, b].astype(F32)

    ones = jnp.ones((tk, LANES), BF16)

    def qk(h, s_ref):
        s_ref[...] = _dot_nt(q_ref[h], k_ref[h])

    def soft(h, s_ref, p_ref, al_ref):
        s = s_ref[...] + bias_scr[...]
        m_prev = m_scr[h]
        m_new = jnp.maximum(m_prev, jnp.max(s, axis=1, keepdims=True))
        m_scr[h] = m_new
        p_ref[...] = jnp.exp2(s - m_new[:, :1]).astype(BF16)
        al_ref[...] = jnp.exp2(m_prev - m_new)

    def pv(h, p_ref, al_ref):
        al = al_ref[...]
        v_ext = jnp.concatenate([v_ref[h], ones], axis=1)
        acc_scr[h] = acc_scr[h] * jnp.concatenate([al, al], axis=1) + _dot(p_ref[...], v_ext)

    s_a, s_b, p_a, p_b, al_a, al_b = pipe
    qk(0, s_a)
    qk(1, s_b)
    soft(0, s_a, p_a, al_a)

    def pair(j, carry):
        qk(2 * j, s_a)
        soft(2 * j - 1, s_b, p_b, al_b)
        pv(2 * j - 2, p_a, al_a)
        qk(2 * j + 1, s_b)
        soft(2 * j, s_a, p_a, al_a)
        pv(2 * j - 1, p_b, al_b)
        return carry

    lax.fori_loop(1, nh // 2, pair, 0)
    soft(nh - 1, s_b, p_b, al_b)
    pv(nh - 2, p_a, al_a)
    pv(nh - 1, p_b, al_b)

    @pl.when(kb == (qi * tq + tq - 1) // tk)
    def _():
        ss = jnp.zeros((tq, LANES), F32)
        for h in range(nh):
            o_h = acc_scr[h, :, :LANES] / acc_scr[h, :, LANES:]
            acc_scr[h, :, :LANES] = o_h
            ss = ss + o_h * o_h
        ms = jnp.sum(ss, axis=1, keepdims=True) * (1.0 / (nh * LANES))
        r = lax.rsqrt(ms + RMS_EPS)
        for h in range(nh):
            sl = slice(h * LANES, (h + 1) * LANES)
            o_ref[:, sl] = (acc_scr[h, :, :LANES] * r * g_ref[:, sl]).astype(o_ref.dtype)


def _attention(main, bias, g, lay, *, s, tq, tk):
    nh = A_HEADS
    nqb, nkt, sq, sk = bias.shape
    q0, k0, v0 = lay["qa"][0], lay["ka"][0], lay["va"][0]
    assert q0 % nh == 0 and k0 % nh == 0 and v0 % nh == 0
    pairs = [(i, j) for i in range(s // tq) for j in range((i * tq + tq - 1) // tk + 1)]
    qi_tab = jnp.asarray(np.asarray([p[0] for p in pairs], np.int32))
    kb_tab = jnp.asarray(np.asarray([p[1] for p in pairs], np.int32))
    return pl.pallas_call(
        functools.partial(_attn_kernel, nh=nh, tq=tq, tk=tk),
        grid_spec=pltpu.PrefetchScalarGridSpec(
            num_scalar_prefetch=2,
            grid=(len(pairs),),
            in_specs=[pl.BlockSpec((nh, tq, LANES), lambda t, qi, kb: (q0 // nh, qi[t], 0)),
                      pl.BlockSpec((nh, tk, LANES), lambda t, qi, kb: (k0 // nh, kb[t], 0)),
                      pl.BlockSpec((nh, tk, LANES), lambda t, qi, kb: (v0 // nh, kb[t], 0)),
                      pl.BlockSpec((tq // sq, tk // sk, sq, sk), lambda t, qi, kb: (qi[t], kb[t], 0, 0)),
                      pl.BlockSpec((1, nh * LANES), lambda t, qi, kb: (0, 0))],
            out_specs=pl.BlockSpec((tq, nh * LANES), lambda t, qi, kb: (qi[t], 0)),
            scratch_shapes=[pltpu.VMEM((tq, tk), F32),
                            pltpu.VMEM((nh, tq, LANES), F32),
                            pltpu.VMEM((nh, tq, 2 * LANES), F32),
                            pltpu.VMEM((tq, tk), F32), pltpu.VMEM((tq, tk), F32),
                            pltpu.VMEM((tq, tk), BF16), pltpu.VMEM((tq, tk), BF16),
                            pltpu.VMEM((tq, LANES), F32), pltpu.VMEM((tq, LANES), F32)]),
        out_shape=jax.ShapeDtypeStruct((s, nh * LANES), BF16),
        compiler_params=_params(("arbitrary",),
                                blocks=[((nh, tq, LANES), BF16), ((nh, tk, LANES), BF16), ((nh, tk, LANES), BF16),
                                        ((tq, tk), BF16), ((8, nh * LANES), F32), ((tq, nh * LANES), BF16)],
                                scratch=[((tq, tk), F32), ((nh, tq, LANES), F32), ((nh, tq, 2 * LANES), F32),
                                         ((tq, tk), F32), ((tq, tk), F32), ((tq, tk), BF16), ((tq, tk), BF16),
                                         ((tq, LANES), F32), ((tq, LANES), F32)],
                                temps=[((tq, tk), F32), ((tq, tk), F32)]),
        name="attention",
    )(qi_tab, kb_tab, main, main, main, bias, g.reshape(1, nh * LANES))


def _gla_kernel(q_ref, k_ref, v_ref, gg_ref, sm_ref, w2_ref, bgk_ref, gn_ref, o_ref, st_scr, *,
                rows, chunk, nheads, dk, dv):
    @pl.when(pl.program_id(0) == 0)
    def _():
        st_scr[...] = jnp.zeros(st_scr.shape, F32)

    r_i = lax.broadcasted_iota(jnp.int32, (chunk, chunk), 0)
    c_i = lax.broadcasted_iota(jnp.int32, (chunk, chunk), 1)
    tri = jnp.where(r_i >= c_i, 1.0, 0.0).astype(BF16)
    nbk, nbv = dk // LANES, dv // LANES
    assert chunk >= 16 and chunk & (chunk - 1) == 0
    leaf = 2
    half_sizes = [chunk >> i for i in range(1, chunk.bit_length()) if chunk >> i >= leaf]
    quadrant = {m: (r_i // (2 * m) == c_i // (2 * m)) & (r_i % (2 * m) >= m) & (c_i % (2 * m) < m)
                for m in half_sizes}
    near = [(c_i == r_i - j) & (r_i % leaf >= j) for j in range(leaf)]
    sub = lax.broadcasted_iota(jnp.int32, (chunk // 8, 8, dk), 1)

    def reference_rows(b, m):
        if m >= 8:
            return jnp.concatenate(
                [jnp.broadcast_to(b[blk + m:blk + m + 1, :], (2 * m, dk)) for blk in range(0, chunk, 2 * m)],
                axis=0)
        b3 = b.reshape(chunk // 8, 8, dk)
        out = None
        for blk in range(0, 8, 2 * m):
            rows = jnp.broadcast_to(b3[:, blk + m:blk + m + 1, :], b3.shape)
            out = rows if out is None else jnp.where(sub >= blk, rows, out)
        return out.reshape(chunk, dk)

    def wide(ref, first, n, r0):
        return jnp.concatenate([ref[first + j, pl.ds(r0, chunk), :] for j in range(n)], axis=1)

    def step(c, carry):
        r0 = pl.multiple_of(c * chunk, chunk)
        z_all = _dot(sm_ref[pl.ds(r0, chunk), :].astype(BF16), w2_ref[...]) + bgk_ref[...]
        for h in range(nheads):
            q = wide(q_ref, h * nbk, nbk, r0).astype(F32) * (dk ** -0.5)
            k = wide(k_ref, h * nbk, nbk, r0).astype(F32)
            v = wide(v_ref, h * nbv, nbv, r0)
            z = z_all[:, h * dk:(h + 1) * dk]
            g = (jnp.minimum(z, 0.0) - jnp.log1p(jnp.exp(-jnp.abs(z)))) * (1.0 / G_TAU)
            g_hi = g.astype(BF16)
            rem = g - g_hi.astype(F32)
            g_mid = rem.astype(BF16)
            g_lo = (rem - g_mid.astype(F32)).astype(BF16)
            b = _dot(tri, g_hi) + _dot(tri, g_mid) + _dot(tri, g_lo)
            b_last = b[chunk - 1:chunk, :]
            att = jnp.zeros((chunk, chunk), F32)
            for m in half_sizes:
                e = jnp.exp(-jnp.abs(b - reference_rows(b, m)))
                att = att + jnp.where(quadrant[m], _dot_nt((q * e).astype(BF16), (k * e).astype(BF16)), 0.0)
            for j in range(leaf):
                kj = k if j == 0 else pltpu.roll(k, j, 0)
                bj = b if j == 0 else pltpu.roll(b, j, 0)
                pair = jnp.sum(q * kj * jnp.exp(jnp.minimum(b - bj, 0.0)), axis=1, keepdims=True)
                att = att + jnp.where(near[j], pair, 0.0)
            st = st_scr[h]
            o = _dot(att.astype(BF16), v) + _dot_nt((q * jnp.exp(b)).astype(BF16), st.astype(BF16))
            kd = (k * jnp.exp(b_last - b)).astype(BF16)
            st_scr[h] = st * jnp.exp(b_last) + _dot_tn(v, kd)
            ms = jnp.mean(o * o, axis=1, keepdims=True)
            gate = wide(gg_ref, h * nbv, nbv, r0).astype(F32)
            out = (o * lax.rsqrt(ms + RMS_EPS)) * gn_ref[...] * (gate * jax.nn.sigmoid(gate))
            o_ref[pl.ds(r0, chunk), h * dv:(h + 1) * dv] = out.astype(o_ref.dtype)
        return carry

    lax.fori_loop(0, rows // chunk, step, 0)


def _gla(main, small, w2, bgk, gn, lay, *, s, rows, chunk, dk, dv):
    nh = G_HEADS
    nqk, nv = nh * dk // LANES, nh * dv // LANES
    q0, k0, v0, g0 = lay["qg"][0], lay["kg"][0], lay["vg"][0], lay["gg"][0]
    assert q0 % nqk == 0 and k0 % nqk == 0 and v0 % nv == 0 and g0 % nv == 0
    return pl.pallas_call(
        functools.partial(_gla_kernel, rows=rows, chunk=chunk, nheads=nh, dk=dk, dv=dv),
        grid=(s // rows,),
        in_specs=[pl.BlockSpec((nqk, rows, LANES), lambda r: (q0 // nqk, r, 0)),
                  pl.BlockSpec((nqk, rows, LANES), lambda r: (k0 // nqk, r, 0)),
                  pl.BlockSpec((nv, rows, LANES), lambda r: (v0 // nv, r, 0)),
                  pl.BlockSpec((nv, rows, LANES), lambda r: (g0 // nv, r, 0)),
                  pl.BlockSpec((rows, LANES), lambda r: (r, 0)),
                  pl.BlockSpec((LANES, nh * dk), lambda r: (0, 0)),
                  pl.BlockSpec((1, nh * dk), lambda r: (0, 0)),
                  pl.BlockSpec((1, dv), lambda r: (0, 0))],
        out_specs=pl.BlockSpec((rows, nh * dv), lambda r: (r, 0)),
        out_shape=jax.ShapeDtypeStruct((s, nh * dv), BF16),
        scratch_shapes=[pltpu.VMEM((nh, dv, dk), F32)],
        compiler_params=_params(("arbitrary",),
                                blocks=[((nqk, rows, LANES), BF16), ((nqk, rows, LANES), BF16),
                                        ((nv, rows, LANES), BF16), ((nv, rows, LANES), BF16),
                                        ((rows, LANES), F32), ((LANES, nh * dk), BF16), ((8, nh * dk), F32),
                                        ((8, dv), F32), ((rows, nh * dv), BF16)],
                                scratch=[((nh, dv, dk), F32)],
                                temps=[((chunk, nh * dk), F32)] * 8 + [((dv, dk), F32)] * 2),
        name="gla",
    )(main, main, main, main, small, w2, bgk, gn.reshape(1, dv))


def _ffn_up_kernel(h_ref, wg_ref, wv_ref, cwg_ref, cwv_ref, cbg_ref, cbv_ref, o_ref, w_scr, hw_scr,
                   *, tm, tn, bounds):
    @pl.when(pl.program_id(1) == 0)
    def _():
        w_scr[:, :tn] = wg_ref[0].astype(BF16)
        w_scr[:, tn:] = wv_ref[0].astype(BF16)
        hw_scr[0:8, :] = jnp.zeros((8, 2 * tn), F32)

    cw = jnp.concatenate([cwg_ref[0], cwv_ref[0]], axis=1)
    cb = jnp.concatenate([cbg_ref[...], cbv_ref[...]], axis=1)
    w0, w1, w2 = cw[0:1, :], cw[1:2, :], cw[2:3, :]

    for lo, hi in zip(bounds[:-1], bounds[1:]):
        hw = _dot(h_ref[lo:hi, :], w_scr[...])
        hw_scr[8 + lo:8 + hi, :] = hw
        u = w2 * hw + w1 * hw_scr[7 + lo:7 + hi, :] + w0 * hw_scr[6 + lo:6 + hi, :] + cb
        gate, val = u[:, :tn], u[:, tn:]
        o_ref[lo:hi, :] = (gate * jax.nn.sigmoid(gate) * val).astype(o_ref.dtype)
    hw_scr[0:8, :] = hw_scr[tm:tm + 8, :]


def _ffn_up(hb, w_up, conv_w, conv_b, *, d_ff, tm, tn):
    s, d = hb.shape
    nj = d_ff // tn
    return pl.pallas_call(
        functools.partial(_ffn_up_kernel, tm=tm, tn=tn, bounds=(0, tm // 2, tm)),
        grid=(nj, s // tm),
        in_specs=[pl.BlockSpec((tm, d), lambda j, i: (i, 0)),
                  pl.BlockSpec((1, d, tn), lambda j, i: (0, 0, j)),
                  pl.BlockSpec((1, d, tn), lambda j, i: (0, 0, nj + j)),
                  pl.BlockSpec((1, CONV_W, tn), lambda j, i: (0, 0, j)),
                  pl.BlockSpec((1, CONV_W, tn), lambda j, i: (0, 0, nj + j)),
                  pl.BlockSpec((1, tn), lambda j, i: (0, j)),
                  pl.BlockSpec((1, tn), lambda j, i: (0, nj + j))],
        out_specs=pl.BlockSpec((tm, tn), lambda j, i: (i, j)),
        out_shape=jax.ShapeDtypeStruct((s, d_ff), BF16),
        scratch_shapes=[pltpu.VMEM((d, 2 * tn), BF16), pltpu.VMEM((8 + tm, 2 * tn), F32)],
        compiler_params=_params(("parallel", "arbitrary"),
                                blocks=[((tm, d), BF16), ((d, tn), F32), ((d, tn), F32), ((8, tn), F32),
                                        ((8, tn), F32), ((8, tn), F32), ((8, tn), F32), ((tm, tn), BF16)],
                                scratch=[((d, 2 * tn), BF16), ((8 + tm, 2 * tn), F32)],
                                temps=[((tm // 2, 2 * tn), F32)] * 3),
        name="ffn_up",
    )(hb, w_up, w_up, conv_w, conv_w, conv_b, conv_b)


def _layout(d_model):
    a_width = A_HEADS * A_HEAD_DIM
    g_width = d_model - a_width
    g_kwidth = g_width // 2
    names = ("qa", "ka", "va", "qi", "ki", "wi", "qg", "kg", "vg", "glr", "gg")
    sizes = (a_width, a_width, a_width, IDX_HEADS * IDX_DIM, IDX_DIM, IDX_HEADS,
             g_kwidth, g_kwidth, g_width, G_LOWRANK, g_width)
    offs = np.concatenate([[0], np.cumsum(sizes)])
    src = {n: (int(offs[i]), int(offs[i + 1])) for i, n in enumerate(names)}
    order = ("qi", "qa", "ka", "va", "qg", "kg", "vg", "gg", "ki")
    lay, blk = {}, 0
    for n in order:
        width = src[n][1] - src[n][0]
        assert width % LANES == 0
        lay[n] = (blk, width // LANES)
        blk += width // LANES
    return src, order, lay, blk


def kernel(x, w_in, w_gk2, b_gk, attn_out_g, gla_norm_g, w_o, ln1_g, ln1_b, w_up, conv_w, conv_b,
           w_down, ln2_g, ln2_b):
    assert x.shape[0] == 1 and w_in.shape[0] == DEPTH == 1
    _, s, d = x.shape
    x2 = x[0]
    src, order, lay, nb_main = _layout(d)
    g_width = d - A_HEADS * A_HEAD_DIM
    dv = g_width // G_HEADS
    dk = dv // 2
    d_ff = w_down.shape[1]
    topk = min(TOPK_MAX, s // 4)

    w_proj = _relayout_w_in(jnp.swapaxes(w_in, 1, 2), src, order)
    n_small = IDX_HEADS + G_LOWRANK
    w2 = jnp.zeros((LANES, G_HEADS * dk), F32).at[IDX_HEADS:n_small].set(w_gk2[0]).astype(BF16)
    bgk = b_gk[0].reshape(1, G_HEADS * dk)
    wd = w_down[0].astype(BF16)
    n_main = (nb_main + 1) * LANES
    q_lo, q_hi = lay["qa"][0] * LANES, (lay["qa"][0] + lay["qa"][1]) * LANES
    col = np.ones((1, n_main), np.float32)
    col[:, q_lo:q_hi] = A_HEAD_DIM ** -0.5 * LOG2E
    col_scale = jnp.asarray(col)

    tm = _tile(s, 1024)
    main, small = _proj_blocks(x2, w_proj, col_scale, tm=_tile(s, 512), tn=_tile(n_main, 1280))
    bias = _indexer(main, small, lay, s=s, tq=256, tk=256, topk=topk)
    ta = _tile(s, 512)
    o_a = _attention(main, bias, attn_out_g[0], lay, s=s, tq=ta, tk=ta)
    o_g = _gla(main, small, w2, bgk, gla_norm_g[0], lay, s=s, rows=_tile(s, 512), chunk=256, dk=dk, dv=dv)
    y1 = _w_o(o_a, o_g, w_o, x2, tm=tm, tn=_tile(d, 512), alpha=DN_ALPHA)
    h, hb = _layer_norm(y1, ln1_g[0], ln1_b[0], tr=_tile(s, 256, 8), also_bf16=True)

    act = _ffn_up(hb, w_up, conv_w, conv_b[0].reshape(1, 2 * d_ff), d_ff=d_ff, tm=tm, tn=_tile(d_ff, 256))
    y2 = _matmul(act, wd, tm=_tile(s, 512), tn=_tile(d, 512), tk=d_ff, out_dtype=F32, residual=h,
                 alpha=DN_ALPHA, name="w_down")
    (out,) = _layer_norm(y2, ln2_g[0], ln2_b[0], tr=_tile(s, 256, 8), also_bf16=False)
    return out[None]
```

```python
import functools

import numpy as np
import jax
import jax.numpy as jnp
from jax import lax
from jax.experimental import pallas as pl
from jax.experimental.pallas import tpu as pltpu

A_HEADS = 16
A_HEAD_DIM = 128
IDX_HEADS = 32
IDX_DIM = 128
TOPK_MAX = 256
G_HEADS = 4
G_LOWRANK = 16
G_TAU = 16.0
CONV_W = 3
LN_EPS = 1e-5
RMS_EPS = 1e-6
DEPTH = 1
DN_ALPHA = (2 * DEPTH) ** 0.25

LANES = 128
MIB = 1024 * 1024
VMEM_BUDGET = 56 * MIB

BF16 = jnp.bfloat16
F32 = jnp.float32
NEG_BIG = -1e30
LOG2E = 1.4426950408889634
INT_MIN = -2 ** 31
COUNT_UNROLL = (8, 2, 1)


def _nbytes(shape, dtype):
    n = jnp.dtype(dtype).itemsize
    for dim in shape:
        n *= 1 if dim is None else getattr(dim, "block_size", dim)
    return n


def _params(sem, blocks, scratch=(), temps=()):
    need = (2 * sum(_nbytes(*b) for b in blocks) + sum(_nbytes(*b) for b in scratch)
            + sum(_nbytes(*b) for b in temps))
    limit = -(-need // MIB) * MIB
    assert limit <= VMEM_BUDGET, (limit, VMEM_BUDGET)
    return pltpu.CompilerParams(dimension_semantics=sem, vmem_limit_bytes=limit)


def _tile(n, max_tile, quantum=LANES):
    best = None
    for t in range(quantum, min(n, max_tile) + 1, quantum):
        if n % t == 0:
            best = t
    assert best is not None, (n, max_tile, quantum)
    return best


def _dot(a, b):
    return jnp.dot(a, b, preferred_element_type=F32)


def _dot_nt(a, b):
    return lax.dot_general(a, b, (((1,), (1,)), ((), ())), preferred_element_type=F32)


def _dot_tn(a, b):
    return lax.dot_general(a, b, (((0,), (0,)), ((), ())), preferred_element_type=F32)


def _relayout_kernel(off_ref, kind_ref, a_ref, b_ref, o_ref, *, n_wi, n_small):
    del off_ref
    kind = kind_ref[pl.program_id(0)]

    @pl.when(kind == 0)
    def _():
        o_ref[...] = a_ref[0].astype(BF16)

    @pl.when(kind == 1)
    def _():
        o_ref[...] = jnp.zeros(o_ref.shape, BF16)

    @pl.when(kind == 2)
    def _():
        r = lax.broadcasted_iota(jnp.int32, o_ref.shape, 0)
        o_ref[...] = jnp.where(r < n_wi, a_ref[0], jnp.where(r < n_small, b_ref[0], 0.0)).astype(BF16)


def _relayout_w_in(w_t, src, order):
    _, n_in, d = w_t.shape
    starts = [src[n][0] + t * LANES for n in order for t in range((src[n][1] - src[n][0]) // LANES)]
    wi0, glr0 = src["wi"][0], src["glr"][0]
    n_wi = src["wi"][1] - wi0
    n_small = n_wi + src["glr"][1] - glr0
    glr_win = glr0 - n_wi
    assert n_small <= LANES and glr_win >= 0 and wi0 + LANES <= n_in and glr_win + LANES <= n_in
    assert all(o % 8 == 0 for o in starts + [wi0, glr_win])
    row_off8 = [o // 8 for o in starts + [0, wi0]]
    kind = [0] * len(starts) + [1, 2]
    nb = len(kind)
    tables = [jnp.asarray(np.asarray(t, np.int32)) for t in (row_off8, kind)]
    return pl.pallas_call(
        functools.partial(_relayout_kernel, n_wi=n_wi, n_small=n_small),
        grid_spec=pltpu.PrefetchScalarGridSpec(
            num_scalar_prefetch=2,
            grid=(nb,),
            in_specs=[pl.BlockSpec((pl.Element(1), pl.Element(LANES), pl.Element(d)),
                                   lambda b, off, kd: (0, off[b] * 8, 0)),
                      pl.BlockSpec((pl.Element(1), pl.Element(LANES), pl.Element(d)),
                                   lambda b, off, kd: (0, glr_win, 0))],
            out_specs=pl.BlockSpec((LANES, d), lambda b, off, kd: (b, 0))),
        out_shape=jax.ShapeDtypeStruct((nb * LANES, d), BF16),
        compiler_params=_params(("arbitrary",),
                                blocks=[((LANES, d), F32), ((LANES, d), F32), ((LANES, d), BF16)],
                                temps=[((LANES, d), F32)]),
        name="relayout_w_in",
    )(*tables, w_t, w_t)


def _proj_kernel(x_ref, w_ref, ws_ref, sc_ref, o_ref, os_ref, xb_scr):
    @pl.when(pl.program_id(1) == 0)
    def _():
        xb_scr[...] = x_ref[...].astype(BF16)
        os_ref[...] = _dot_nt(xb_scr[...], ws_ref[...])

    res = _dot_nt(xb_scr[...], w_ref[...]) * sc_ref[...]
    for c in range(o_ref.shape[0]):
        o_ref[c] = res[:, c * LANES:(c + 1) * LANES].astype(o_ref.dtype)


def _proj_blocks(x, wt, col_scale, *, tm, tn):
    s, k = x.shape
    n = wt.shape[0] - LANES
    return pl.pallas_call(
        _proj_kernel,
        grid=(s // tm, n // tn),
        in_specs=[pl.BlockSpec((tm, k), lambda i, j: (i, 0)),
                  pl.BlockSpec((tn, k), lambda i, j: (j, 0)),
                  pl.BlockSpec((LANES, k), lambda i, j: (n // LANES, 0)),
                  pl.BlockSpec((1, tn), lambda i, j: (0, j))],
        out_specs=[pl.BlockSpec((tn // LANES, tm, LANES), lambda i, j: (j, i, 0)),
                   pl.BlockSpec((tm, LANES), lambda i, j: (i, 0))],
        out_shape=[jax.ShapeDtypeStruct((n // LANES, s, LANES), BF16),
                   jax.ShapeDtypeStruct((s, LANES), F32)],
        scratch_shapes=[pltpu.VMEM((tm, k), BF16)],
        compiler_params=_params(("parallel", "arbitrary"),
                                blocks=[((tm, k), F32), ((tn, k), BF16), ((LANES, k), BF16), ((8, tn), F32),
                                        ((tm, tn), BF16), ((tm, LANES), F32)],
                                scratch=[((tm, k), BF16)],
                                temps=[((tm, tn), F32), ((tm, tn), F32)]),
        name="proj",
    )(x, wt, wt, col_scale)


def _mm_kernel(a_ref, b_ref, *rest, alpha, has_res, nk):
    if has_res:
        r_ref, o_ref = rest[0], rest[1]
        rest = rest[2:]
    else:
        r_ref, o_ref = None, rest[0]
        rest = rest[1:]

    def finish(acc):
        if has_res:
            acc = acc + alpha * r_ref[...]
        o_ref[...] = acc.astype(o_ref.dtype)

    if nk == 1:
        finish(_dot(a_ref[...], b_ref[...]))
        return
    acc_ref = rest[0]
    kk = pl.program_id(2)

    @pl.when(kk == 0)
    def _():
        acc_ref[...] = _dot(a_ref[...], b_ref[...])

    @pl.when(kk > 0)
    def _():
        acc_ref[...] += _dot(a_ref[...], b_ref[...])

    @pl.when(kk == nk - 1)
    def _():
        finish(acc_ref[...])


def _matmul(a, b, *, tm, tn, tk, out_dtype, residual=None, alpha=1.0, name="matmul"):
    m, k = a.shape
    n = b.shape[1]
    nk = k // tk
    in_specs = [pl.BlockSpec((tm, tk), lambda i, j, kk: (i, kk)),
                pl.BlockSpec((tk, tn), lambda i, j, kk: (kk, j))]
    args = [a, b]
    if residual is not None:
        in_specs.append(pl.BlockSpec((tm, tn), lambda i, j, kk: (i, j)))
        args.append(residual)
    scratch = [pltpu.VMEM((tm, tn), F32)] if nk > 1 else []
    return pl.pallas_call(
        functools.partial(_mm_kernel, alpha=alpha, has_res=residual is not None, nk=nk),
        grid=(m // tm, n // tn, nk),
        in_specs=in_specs,
        out_specs=pl.BlockSpec((tm, tn), lambda i, j, kk: (i, j)),
        out_shape=jax.ShapeDtypeStruct((m, n), out_dtype),
        scratch_shapes=scratch,
        compiler_params=_params(("parallel", "parallel", "arbitrary"),
                                blocks=[((tm, tk), a.dtype), ((tk, tn), b.dtype), ((tm, tn), out_dtype)]
                                + ([((tm, tn), F32)] if residual is not None else []),
                                scratch=[((tm, tn), F32)] if nk > 1 else [],
                                temps=[((tm, tn), F32)]),
        name=name,
    )(*args)


def _wo_kernel(oa_ref, og_ref, wt_ref, wb_ref, x_ref, o_ref, w_scr, *, ka, alpha):
    @pl.when(pl.program_id(1) == 0)
    def _():
        w_scr[:ka, :] = wt_ref[0].astype(BF16)
        w_scr[ka:, :] = wb_ref[0].astype(BF16)

    acc = _dot(oa_ref[...], w_scr[:ka, :]) + _dot(og_ref[...], w_scr[ka:, :])
    o_ref[...] = acc + alpha * x_ref[...]


def _w_o(o_a, o_g, w_o, x, *, tm, tn, alpha):
    s, ka = o_a.shape
    kg = o_g.shape[1]
    d = w_o.shape[2]
    assert ka % kg == 0
    return pl.pallas_call(
        functools.partial(_wo_kernel, ka=ka, alpha=alpha),
        grid=(d // tn, s // tm),
        in_specs=[pl.BlockSpec((tm, ka), lambda j, i: (i, 0)),
                  pl.BlockSpec((tm, kg), lambda j, i: (i, 0)),
                  pl.BlockSpec((1, ka, tn), lambda j, i: (0, 0, j)),
                  pl.BlockSpec((1, kg, tn), lambda j, i: (0, ka // kg, j)),
                  pl.BlockSpec((tm, tn), lambda j, i: (i, j))],
        out_specs=pl.BlockSpec((tm, tn), lambda j, i: (i, j)),
        out_shape=jax.ShapeDtypeStruct((s, d), F32),
        scratch_shapes=[pltpu.VMEM((ka + kg, tn), BF16)],
        compiler_params=_params(("parallel", "arbitrary"),
                                blocks=[((tm, ka), BF16), ((tm, kg), BF16), ((ka, tn), F32), ((kg, tn), F32),
                                        ((tm, tn), F32), ((tm, tn), F32)],
                                scratch=[((ka + kg, tn), BF16)],
                                temps=[((tm, tn), F32), ((tm, tn), F32)]),
        name="w_o",
    )(o_a, o_g, w_o, w_o, x)


def _ln_kernel(y_ref, g_ref, b_ref, *o_refs):
    y = y_ref[...]
    mu = jnp.mean(y, axis=1, keepdims=True)
    yc = y - mu
    var = jnp.mean(yc * yc, axis=1, keepdims=True)
    out = yc * lax.rsqrt(var + LN_EPS) * g_ref[...] + b_ref[...]
    for o_ref in o_refs:
        o_ref[...] = out.astype(o_ref.dtype)


def _layer_norm(y, g, b, *, tr, also_bf16):
    s, d = y.shape
    out_shape = [jax.ShapeDtypeStruct((s, d), F32)]
    out_specs = [pl.BlockSpec((tr, d), lambda i: (i, 0))]
    if also_bf16:
        out_shape.append(jax.ShapeDtypeStruct((s, d), BF16))
        out_specs.append(pl.BlockSpec((tr, d), lambda i: (i, 0)))
    return pl.pallas_call(
        _ln_kernel,
        grid=(s // tr,),
        in_specs=[pl.BlockSpec((tr, d), lambda i: (i, 0)),
                  pl.BlockSpec((1, d), lambda i: (0, 0)),
                  pl.BlockSpec((1, d), lambda i: (0, 0))],
        out_specs=out_specs,
        out_shape=out_shape,
        compiler_params=_params(("parallel",),
                                blocks=[((tr, d), F32), ((8, d), F32), ((8, d), F32)]
                                + [((tr, d), o.dtype) for o in out_shape],
                                temps=[((tr, d), F32), ((tr, d), F32)]),
        name="layer_norm",
    )(y, g.reshape(1, d), b.reshape(1, d))


def _idx_kernel(qi_ref, ki_ref, sm_ref, bias_ref, hi_scr, lo_scr, wb_scr, *, tq, tk, nkt, hi, group,
                topk, wscale):
    qb = pl.program_id(0)
    nk = ((qb + 1) * tq + tk - 1) // tk
    reps = tk // LANES
    i16 = jnp.int16
    low16 = -2 ** 15

    for h in range(hi):
        wb_scr[h] = jnp.broadcast_to(sm_ref[:, h:h + 1] * wscale, (tq, LANES))

    row = qb * tq + lax.broadcasted_iota(jnp.int32, (group, tk), 0)
    col = lax.broadcasted_iota(jnp.int32, (group, tk), 1)

    def lanes(a):
        return jnp.concatenate([a] * reps, axis=1) if reps > 1 else a

    def fold(a):
        out = a[:, :LANES]
        for r in range(1, reps):
            out = out + a[:, r * LANES:(r + 1) * LANES]
        return out

    def score_tile(kb, carry):
        kt = ki_ref[0, pl.ds(pl.multiple_of(kb * tk, tk), tk), :]
        for r in range(tq // group):
            rows = slice(r * group, (r + 1) * group)
            qs = qi_ref[:, rows, :].reshape(hi * group, IDX_DIM)
            d = _dot_nt(qs, kt)
            acc = jnp.zeros((group, tk), F32)
            for h in range(hi):
                acc = acc + jnp.maximum(d[h * group:(h + 1) * group], 0.0) * lanes(wb_scr[h, rows, :])
            bits = pltpu.bitcast(acc + 0.0, jnp.int32)
            key = jnp.where(kb * tk + col <= row + r * group,
                            bits ^ ((bits >> 31) & jnp.int32(0x7FFFFFFF)), INT_MIN)
            hi_scr[kb, rows, :] = (key >> 16).astype(i16)
            lo_scr[kb, rows, :] = ((key & 0xFFFF) + low16).astype(i16)
        return carry

    lax.fori_loop(0, nk, score_tile, 0)

    def count_ge(ref, cand):
        cand_t = lanes(cand.astype(i16))

        def step(kb, c):
            return c + fold(jnp.where(ref[kb] >= cand_t, i16(1), i16(0)))

        c = jnp.zeros((tq, LANES), i16)
        done = 0
        for width in COUNT_UNROLL:
            def step_many(q, c, width=width, done=done):
                for u in range(width):
                    c = step(done + q * width + u, c)
                return c

            n_groups = (nk - done) // width
            c = lax.fori_loop(0, n_groups, step_many, c)
            done = done + n_groups * width
        return jnp.sum(c.astype(F32), axis=1, keepdims=True)

    kf = float(topk)

    def bit_search(ref, base_cnt, cnt_start, stop_when_exact):
        def body(state):
            it, v, cnt_v = state
            cand = v + jnp.left_shift(jnp.int32(1), 15 - it)
            cnt = base_cnt + count_ge(ref, cand)
            ok = cnt >= kf
            return it + 1, jnp.where(ok, cand, v), jnp.where(ok, cnt, cnt_v)

        def unsettled(state):
            go = state[0] < 16
            if stop_when_exact:
                go = go & (jnp.max(jnp.where(state[2] == kf, 0.0, 1.0)) > 0.0)
            return go

        v0 = jnp.full((tq, LANES), low16, jnp.int32)
        _, v, cnt_v = lax.while_loop(unsettled, body, (jnp.int32(0), v0, cnt_start))
        return v, cnt_v

    n_all = jnp.zeros((tq, 1), F32) + (nk * tk).astype(F32)
    v_hi, n_from_hi = bit_search(hi_scr, 0.0, n_all, False)
    n_above = count_ge(hi_scr, v_hi + 1)
    v_hi_t = lanes(v_hi.astype(i16))

    def keep_equal(kb, carry):
        lo_scr[kb] = jnp.where(hi_scr[kb] == v_hi_t, lo_scr[kb], i16(low16))
        return carry

    lax.fori_loop(0, nk, keep_equal, 0)
    v_lo, n_sel = bit_search(lo_scr, n_above, n_from_hi, True)
    v_lo = jnp.where(v_hi == low16, jnp.maximum(v_lo, low16 + 1), v_lo)
    v_lo_t = lanes(v_lo.astype(i16))
    zero = jnp.zeros((tq, tk), bias_ref.dtype)
    neg = jnp.full((tq, tk), NEG_BIG, bias_ref.dtype)
    has_ties = jnp.max(jnp.where(n_sel > kf, 1.0, 0.0)) > 0.0

    @pl.when(jnp.logical_not(has_ties))
    def _():
        def emit(kb, carry):
            h16 = hi_scr[kb]
            sel = (h16 > v_hi_t) | ((h16 == v_hi_t) & (lo_scr[kb] >= v_lo_t))
            bias_ref[0, kb] = jnp.where(sel, zero, neg)
            return carry

        lax.fori_loop(0, nk, emit, 0)

    @pl.when(has_ties)
    def _():
        top16 = 2 ** 15 - 1
        n_gt = n_above + jnp.where(v_lo[:, :1] >= top16, 0.0, count_ge(lo_scr, jnp.minimum(v_lo + 1, top16)))
        need = kf - n_gt
        earlier = jnp.where(lax.broadcasted_iota(jnp.int32, (tk, tk), 0)
                            < lax.broadcasted_iota(jnp.int32, (tk, tk), 1), 1.0, 0.0).astype(BF16)
        one_b = jnp.ones((tq, tk), BF16)
        zero_b = jnp.zeros((tq, tk), BF16)

        def emit_ties(kb, seen):
            h16 = hi_scr[kb]
            l16 = lo_scr[kb]
            same_hi = h16 == v_hi_t
            larger = (h16 > v_hi_t) | (same_hi & (l16 > v_lo_t))
            tied = jnp.where(same_hi & (l16 == v_lo_t), one_b, zero_b)
            rank = seen + _dot(tied, earlier)
            keep = jnp.where(rank < need, 1.0, 0.0).astype(BF16)
            sel = larger | ((tied > 0.0) & (keep > 0.0))
            bias_ref[0, kb] = jnp.where(sel, zero, neg)
            return seen + jnp.sum(tied.astype(F32), axis=1, keepdims=True)

        lax.fori_loop(0, nk, emit_ties, jnp.zeros((tq, 1), F32))

    def fill(kb, carry):
        bias_ref[0, kb] = neg
        return carry

    lax.fori_loop(nk, nkt, fill, 0)


def _indexer(main, small, lay, *, s, tq, tk, topk):
    nkt = s // tk
    hi = IDX_HEADS
    group = min(tq, 64)
    qi0, ki0 = lay["qi"][0], lay["ki"][0]
    assert qi0 % hi == 0
    return pl.pallas_call(
        functools.partial(_idx_kernel, tq=tq, tk=tk, nkt=nkt, hi=hi, group=group, topk=topk,
                          wscale=IDX_HEADS ** -0.5 * IDX_DIM ** -0.5),
        grid=(s // tq,),
        in_specs=[pl.BlockSpec((hi, tq, LANES), lambda i: (qi0 // hi, i, 0)),
                  pl.BlockSpec((1, s, LANES), lambda i: (ki0, 0, 0)),
                  pl.BlockSpec((tq, LANES), lambda i: (i, 0))],
        out_specs=pl.BlockSpec((1, nkt, tq, tk), lambda i: (i, 0, 0, 0)),
        out_shape=jax.ShapeDtypeStruct((s // tq, nkt, tq, tk), BF16),
        scratch_shapes=[pltpu.VMEM((nkt, tq, tk), jnp.int16),
                        pltpu.VMEM((nkt, tq, tk), jnp.int16),
                        pltpu.VMEM((hi, tq, LANES), F32)],
        compiler_params=_params(("parallel",),
                                blocks=[((hi, tq, LANES), BF16), ((s, LANES), BF16), ((tq, LANES), F32),
                                        ((nkt, tq, tk), BF16)],
                                scratch=[((nkt, tq, tk), jnp.int16), ((nkt, tq, tk), jnp.int16),
                                         ((hi, tq, LANES), F32)],
                                temps=[((hi * group, tk), F32), ((hi * group, tk), F32)]),
        name="indexer",
    )(main, main, small)


def _attn_kernel(qi_ref, kb_ref, q_ref, k_ref, v_ref, b_ref, g_ref, o_ref, bias_scr, m_scr, acc_scr, *pipe,
                 nh, tq, tk):
    step = pl.program_id(0)
    qi = qi_ref[step]
    kb = kb_ref[step]

    @pl.when(kb == 0)
    def _():
        m_scr[...] = jnp.full(m_scr.shape, NEG_BIG, F32)
        acc_scr[...] = jnp.zeros(acc_scr.shape, F32)

    na, nb, sq, sk = b_ref.shape
    for a in range(na):
        for b in range(nb):
            bias_scr[a * sq:(a + 1) * sq, b * sk:(b + 1) * sk] = b_ref[a, b].astype(F32)

    ones = jnp.ones((tk, LANES), BF16)

    def qk(h, s_ref):
        s_ref[...] = _dot_nt(q_ref[h], k_ref[h])

    def soft(h, s_ref, p_ref, al_ref):
        s = s_ref[...] + bias_scr[...]
        m_prev = m_scr[h]
        m_new = jnp.maximum(m_prev, jnp.max(s, axis=1, keepdims=True))
        m_scr[h] = m_new
        p_ref[...] = jnp.exp2(s - m_new[:, :1]).astype(BF16)
        al_ref[...] = jnp.exp2(m_prev - m_new)

    def pv(h, p_ref, al_ref):
        al = al_ref[...]
        v_ext = jnp.concatenate([v_ref[h], ones], axis=1)
        acc_scr[h] = acc_scr[h] * jnp.concatenate([al, al], axis=1) + _dot(p_ref[...], v_ext)

    s_a, s_b, p_a, p_b, al_a, al_b = pipe
    qk(0, s_a)
    qk(1, s_b)
    soft(0, s_a, p_a, al_a)

    def pair(j, carry):
        qk(2 * j, s_a)
        soft(2 * j - 1, s_b, p_b, al_b)
        pv(2 * j - 2, p_a, al_a)
        qk(2 * j + 1, s_b)
        soft(2 * j, s_a, p_a, al_a)
        pv(2 * j - 1, p_b, al_b)
        return carry

    lax.fori_loop(1, nh // 2, pair, 0)
    soft(nh - 1, s_b, p_b, al_b)
    pv(nh - 2, p_a, al_a)
    pv(nh - 1, p_b, al_b)

    @pl.when(kb == (qi * tq + tq - 1) // tk)
    def _():
        ss = jnp.zeros((tq, LANES), F32)
        for h in range(nh):
            o_h = acc_scr[h, :, :LANES] / acc_scr[h, :, LANES:]
            acc_scr[h, :, :LANES] = o_h
            ss = ss + o_h * o_h
        ms = jnp.sum(ss, axis=1, keepdims=True) * (1.0 / (nh * LANES))
        r = lax.rsqrt(ms + RMS_EPS)
        for h in range(nh):
            sl = slice(h * LANES, (h + 1) * LANES)
            o_ref[:, sl] = (acc_scr[h, :, :LANES] * r * g_ref[:, sl]).astype(o_ref.dtype)


def _attention(main, bias, g, lay, *, s, tq, tk):
    nh = A_HEADS
    nqb, nkt, sq, sk = bias.shape
    q0, k0, v0 = lay["qa"][0], lay["ka"][0], lay["va"][0]
    assert q0 % nh == 0 and k0 % nh == 0 and v0 % nh == 0
    pairs = [(i, j) for i in range(s // tq) for j in range((i * tq + tq - 1) // tk + 1)]
    qi_tab = jnp.asarray(np.asarray([p[0] for p in pairs], np.int32))
    kb_tab = jnp.asarray(np.asarray([p[1] for p in pairs], np.int32))
    return pl.pallas_call(
        functools.partial(_attn_kernel, nh=nh, tq=tq, tk=tk),
        grid_spec=pltpu.PrefetchScalarGridSpec(
            num_scalar_prefetch=2,
            grid=(len(pairs),),
            in_specs=[pl.BlockSpec((nh, tq, LANES), lambda t, qi, kb: (q0 // nh, qi[t], 0)),
                      pl.BlockSpec((nh, tk, LANES), lambda t, qi, kb: (k0 // nh, kb[t], 0)),
                      pl.BlockSpec((nh, tk, LANES), lambda t, qi, kb: (v0 // nh, kb[t], 0)),
                      pl.BlockSpec((tq // sq, tk // sk, sq, sk), lambda t, qi, kb: (qi[t], kb[t], 0, 0)),
                      pl.BlockSpec((1, nh * LANES), lambda t, qi, kb: (0, 0))],
            out_specs=pl.BlockSpec((tq, nh * LANES), lambda t, qi, kb: (qi[t], 0)),
            scratch_shapes=[pltpu.VMEM((tq, tk), F32),
                            pltpu.VMEM((nh, tq, LANES), F32),
                            pltpu.VMEM((nh, tq, 2 * LANES), F32),
                            pltpu.VMEM((tq, tk), F32), pltpu.VMEM((tq, tk), F32),
                            pltpu.VMEM((tq, tk), BF16), pltpu.VMEM((tq, tk), BF16),
                            pltpu.VMEM((tq, LANES), F32), pltpu.VMEM((tq, LANES), F32)]),
        out_shape=jax.ShapeDtypeStruct((s, nh * LANES), BF16),
        compiler_params=_params(("arbitrary",),
                                blocks=[((nh, tq, LANES), BF16), ((nh, tk, LANES), BF16), ((nh, tk, LANES), BF16),
                                        ((tq, tk), BF16), ((8, nh * LANES), F32), ((tq, nh * LANES), BF16)],
                                scratch=[((tq, tk), F32), ((nh, tq, LANES), F32), ((nh, tq, 2 * LANES), F32),
                                         ((tq, tk), F32), ((tq, tk), F32), ((tq, tk), BF16), ((tq, tk), BF16),
                                         ((tq, LANES), F32), ((tq, LANES), F32)],
                                temps=[((tq, tk), F32), ((tq, tk), F32)]),
        name="attention",
    )(qi_tab, kb_tab, main, main, main, bias, g.reshape(1, nh * LANES))


def _gla_kernel(q_ref, k_ref, v_ref, gg_ref, sm_ref, w2_ref, bgk_ref, gn_ref, o_ref, st_scr, *,
                rows, chunk, nheads, dk, dv):
    @pl.when(pl.program_id(0) == 0)
    def _():
        st_scr[...] = jnp.zeros(st_scr.shape, F32)

    r_i = lax.broadcasted_iota(jnp.int32, (chunk, chunk), 0)
    c_i = lax.broadcasted_iota(jnp.int32, (chunk, chunk), 1)
    tri = jnp.where(r_i >= c_i, 1.0, 0.0).astype(BF16)
    nbk, nbv = dk // LANES, dv // LANES
    assert chunk >= 16 and chunk & (chunk - 1) == 0
    leaf = 2
    half_sizes = [chunk >> i for i in range(1, chunk.bit_length()) if chunk >> i >= leaf]
    quadrant = {m: (r_i // (2 * m) == c_i // (2 * m)) & (r_i % (2 * m) >= m) & (c_i % (2 * m) < m)
                for m in half_sizes}
    near = [(c_i == r_i - j) & (r_i % leaf >= j) for j in range(leaf)]
    sub = lax.broadcasted_iota(jnp.int32, (chunk // 8, 8, dk), 1)

    def reference_rows(b, m):
        if m >= 8:
            return jnp.concatenate(
                [jnp.broadcast_to(b[blk + m:blk + m + 1, :], (2 * m, dk)) for blk in range(0, chunk, 2 * m)],
                axis=0)
        b3 = b.reshape(chunk // 8, 8, dk)
        out = None
        for blk in range(0, 8, 2 * m):
            rows = jnp.broadcast_to(b3[:, blk + m:blk + m + 1, :], b3.shape)
            out = rows if out is None else jnp.where(sub >= blk, rows, out)
        return out.reshape(chunk, dk)

    def wide(ref, first, n, r0):
        return jnp.concatenate([ref[first + j, pl.ds(r0, chunk), :] for j in range(n)], axis=1)

    def step(c, carry):
        r0 = pl.multiple_of(c * chunk, chunk)
        z_all = _dot(sm_ref[pl.ds(r0, chunk), :].astype(BF16), w2_ref[...]) + bgk_ref[...]
        for h in range(nheads):
            q = wide(q_ref, h * nbk, nbk, r0).astype(F32) * (dk ** -0.5)
            k = wide(k_ref, h * nbk, nbk, r0).astype(F32)
            v = wide(v_ref, h * nbv, nbv, r0)
            z = z_all[:, h * dk:(h + 1) * dk]
            g = (jnp.minimum(z, 0.0) - jnp.log1p(jnp.exp(-jnp.abs(z)))) * (LOG2E / G_TAU)
            g_hi = g.astype(BF16)
            rem = g - g_hi.astype(F32)
            g_mid = rem.astype(BF16)
            g_lo = (rem - g_mid.astype(F32)).astype(BF16)
            b = _dot(tri, g_hi) + _dot(tri, g_mid) + _dot(tri, g_lo)
            b_last = b[chunk - 1:chunk, :]
            att = jnp.zeros((chunk, chunk), F32)
            for m in half_sizes:
                e = jnp.exp2(-jnp.abs(b - reference_rows(b, m)))
                att = att + jnp.where(quadrant[m], _dot_nt((q * e).astype(BF16), (k * e).astype(BF16)), 0.0)
            for j in range(leaf):
                kj = k if j == 0 else pltpu.roll(k, j, 0)
                bj = b if j == 0 else pltpu.roll(b, j, 0)
                pair = jnp.sum(q * kj * jnp.exp2(jnp.minimum(b - bj, 0.0)), axis=1, keepdims=True)
                att = att + jnp.where(near[j], pair, 0.0)
            st = st_scr[h]
            o = _dot(att.astype(BF16), v) + _dot_nt((q * jnp.exp2(b)).astype(BF16), st.astype(BF16))
            kd = (k * jnp.exp2(b_last - b)).astype(BF16)
            st_scr[h] = st * jnp.exp2(b_last) + _dot_tn(v, kd)
            ms = jnp.mean(o * o, axis=1, keepdims=True)
            gate = wide(gg_ref, h * nbv, nbv, r0).astype(F32)
            out = (o * lax.rsqrt(ms + RMS_EPS)) * gn_ref[...] * (gate * jax.nn.sigmoid(gate))
            o_ref[pl.ds(r0, chunk), h * dv:(h + 1) * dv] = out.astype(o_ref.dtype)
        return carry

    lax.fori_loop(0, rows // chunk, step, 0)


def _gla(main, small, w2, bgk, gn, lay, *, s, rows, chunk, dk, dv):
    nh = G_HEADS
    nqk, nv = nh * dk // LANES, nh * dv // LANES
    q0, k0, v0, g0 = lay["qg"][0], lay["kg"][0], lay["vg"][0], lay["gg"][0]
    assert q0 % nqk == 0 and k0 % nqk == 0 and v0 % nv == 0 and g0 % nv == 0
    return pl.pallas_call(
        functools.partial(_gla_kernel, rows=rows, chunk=chunk, nheads=nh, dk=dk, dv=dv),
        grid=(s // rows,),
        in_specs=[pl.BlockSpec((nqk, rows, LANES), lambda r: (q0 // nqk, r, 0)),
                  pl.BlockSpec((nqk, rows, LANES), lambda r: (k0 // nqk, r, 0)),
                  pl.BlockSpec((nv, rows, LANES), lambda r: (v0 // nv, r, 0)),
                  pl.BlockSpec((nv, rows, LANES), lambda r: (g0 // nv, r, 0)),
                  pl.BlockSpec((rows, LANES), lambda r: (r, 0)),
                  pl.BlockSpec((LANES, nh * dk), lambda r: (0, 0)),
                  pl.BlockSpec((1, nh * dk), lambda r: (0, 0)),
                  pl.BlockSpec((1, dv), lambda r: (0, 0))],
        out_specs=pl.BlockSpec((rows, nh * dv), lambda r: (r, 0)),
        out_shape=jax.ShapeDtypeStruct((s, nh * dv), BF16),
        scratch_shapes=[pltpu.VMEM((nh, dv, dk), F32)],
        compiler_params=_params(("arbitrary",),
                                blocks=[((nqk, rows, LANES), BF16), ((nqk, rows, LANES), BF16),
                                        ((nv, rows, LANES), BF16), ((nv, rows, LANES), BF16),
                                        ((rows, LANES), F32), ((LANES, nh * dk), BF16), ((8, nh * dk), F32),
                                        ((8, dv), F32), ((rows, nh * dv), BF16)],
                                scratch=[((nh, dv, dk), F32)],
                                temps=[((chunk, nh * dk), F32)] * 8 + [((dv, dk), F32)] * 2),
        name="gla",
    )(main, main, main, main, small, w2, bgk, gn.reshape(1, dv))


def _ffn_up_kernel(h_ref, wg_ref, wv_ref, cwg_ref, cwv_ref, cbg_ref, cbv_ref, o_ref, w_scr, hw_scr,
                   *, tm, tn, bounds):
    @pl.when(pl.program_id(1) == 0)
    def _():
        w_scr[:, :tn] = wg_ref[0].astype(BF16)
        w_scr[:, tn:] = wv_ref[0].astype(BF16)
        hw_scr[0:8, :] = jnp.zeros((8, 2 * tn), F32)

    cw = jnp.concatenate([cwg_ref[0], cwv_ref[0]], axis=1)
    cb = jnp.concatenate([cbg_ref[...], cbv_ref[...]], axis=1)
    w0, w1, w2 = cw[0:1, :], cw[1:2, :], cw[2:3, :]

    for lo, hi in zip(bounds[:-1], bounds[1:]):
        hw = _dot(h_ref[lo:hi, :], w_scr[...])
        hw_scr[8 + lo:8 + hi, :] = hw
        u = w2 * hw + w1 * hw_scr[7 + lo:7 + hi, :] + w0 * hw_scr[6 + lo:6 + hi, :] + cb
        gate, val = u[:, :tn], u[:, tn:]
        o_ref[lo:hi, :] = (gate * jax.nn.sigmoid(gate) * val).astype(o_ref.dtype)
    hw_scr[0:8, :] = hw_scr[tm:tm + 8, :]


def _ffn_up(hb, w_up, conv_w, conv_b, *, d_ff, tm, tn):
    s, d = hb.shape
    nj = d_ff // tn
    return pl.pallas_call(
        functools.partial(_ffn_up_kernel, tm=tm, tn=tn, bounds=(0, tm // 2, tm)),
        grid=(nj, s // tm),
        in_specs=[pl.BlockSpec((tm, d), lambda j, i: (i, 0)),
                  pl.BlockSpec((1, d, tn), lambda j, i: (0, 0, j)),
                  pl.BlockSpec((1, d, tn), lambda j, i: (0, 0, nj + j)),
                  pl.BlockSpec((1, CONV_W, tn), lambda j, i: (0, 0, j)),
                  pl.BlockSpec((1, CONV_W, tn), lambda j, i: (0, 0, nj + j)),
                  pl.BlockSpec((1, tn), lambda j, i: (0, j)),
                  pl.BlockSpec((1, tn), lambda j, i: (0, nj + j))],
        out_specs=pl.BlockSpec((tm, tn), lambda j, i: (i, j)),
        out_shape=jax.ShapeDtypeStruct((s, d_ff), BF16),
        scratch_shapes=[pltpu.VMEM((d, 2 * tn), BF16), pltpu.VMEM((8 + tm, 2 * tn), F32)],
        compiler_params=_params(("parallel", "arbitrary"),
                                blocks=[((tm, d), BF16), ((d, tn), F32), ((d, tn), F32), ((8, tn), F32),
                                        ((8, tn), F32), ((8, tn), F32), ((8, tn), F32), ((tm, tn), BF16)],
                                scratch=[((d, 2 * tn), BF16), ((8 + tm, 2 * tn), F32)],
                                temps=[((tm // 2, 2 * tn), F32)] * 3),
        name="ffn_up",
    )(hb, w_up, w_up, conv_w, conv_w, conv_b, conv_b)


def _layout(d_model):
    a_width = A_HEADS * A_HEAD_DIM
    g_width = d_model - a_width
    g_kwidth = g_width // 2
    names = ("qa", "ka", "va", "qi", "ki", "wi", "qg", "kg", "vg", "glr", "gg")
    sizes = (a_width, a_width, a_width, IDX_HEADS * IDX_DIM, IDX_DIM, IDX_HEADS,
             g_kwidth, g_kwidth, g_width, G_LOWRANK, g_width)
    offs = np.concatenate([[0], np.cumsum(sizes)])
    src = {n: (int(offs[i]), int(offs[i + 1])) for i, n in enumerate(names)}
    order = ("qi", "qa", "ka", "va", "qg", "kg", "vg", "gg", "ki")
    lay, blk = {}, 0
    for n in order:
        width = src[n][1] - src[n][0]
        assert width % LANES == 0
        lay[n] = (blk, width // LANES)
        blk += width // LANES
    return src, order, lay, blk


def kernel(x, w_in, w_gk2, b_gk, attn_out_g, gla_norm_g, w_o, ln1_g, ln1_b, w_up, conv_w, conv_b,
           w_down, ln2_g, ln2_b):
    assert x.shape[0] == 1 and w_in.shape[0] == DEPTH == 1
    _, s, d = x.shape
    x2 = x[0]
    src, order, lay, nb_main = _layout(d)
    g_width = d - A_HEADS * A_HEAD_DIM
    dv = g_width // G_HEADS
    dk = dv // 2
    d_ff = w_down.shape[1]
    topk = min(TOPK_MAX, s // 4)

    w_proj = _relayout_w_in(jnp.swapaxes(w_in, 1, 2), src, order)
    n_small = IDX_HEADS + G_LOWRANK
    w2 = jnp.zeros((LANES, G_HEADS * dk), F32).at[IDX_HEADS:n_small].set(w_gk2[0]).astype(BF16)
    bgk = b_gk[0].reshape(1, G_HEADS * dk)
    wd = w_down[0].astype(BF16)
    n_main = (nb_main + 1) * LANES
    q_lo, q_hi = lay["qa"][0] * LANES, (lay["qa"][0] + lay["qa"][1]) * LANES
    col = np.ones((1, n_main), np.float32)
    col[:, q_lo:q_hi] = A_HEAD_DIM ** -0.5 * LOG2E
    col_scale = jnp.asarray(col)

    tm = _tile(s, 1024)
    main, small = _proj_blocks(x2, w_proj, col_scale, tm=_tile(s, 512), tn=_tile(n_main, 1280))
    bias = _indexer(main, small, lay, s=s, tq=256, tk=256, topk=topk)
    ta = _tile(s, 512)
    o_a = _attention(main, bias, attn_out_g[0], lay, s=s, tq=ta, tk=ta)
    o_g = _gla(main, small, w2, bgk, gla_norm_g[0], lay, s=s, rows=_tile(s, 512), chunk=256, dk=dk, dv=dv)
    y1 = _w_o(o_a, o_g, w_o, x2, tm=tm, tn=_tile(d, 512), alpha=DN_ALPHA)
    h, hb = _layer_norm(y1, ln1_g[0], ln1_b[0], tr=_tile(s, 256, 8), also_bf16=True)

    act = _ffn_up(hb, w_up, conv_w, conv_b[0].reshape(1, 2 * d_ff), d_ff=d_ff, tm=tm, tn=_tile(d_ff, 256))
    y2 = _matmul(act, wd, tm=_tile(s, 512), tn=_tile(d, 512), tk=d_ff, out_dtype=F32, residual=h,
                 alpha=DN_ALPHA, name="w_down")
    (out,) = _layer_norm(y2, ln2_g[0], ln2_b[0], tr=_tile(s, 256, 8), also_bf16=False)
    return out[None]
```

```python
import functools

import numpy as np
import jax
import jax.numpy as jnp
from jax import lax
from jax.experimental import pallas as pl
from jax.experimental.pallas import tpu as pltpu

A_HEADS = 16
A_HEAD_DIM = 128
IDX_HEADS = 32
IDX_DIM = 128
TOPK_MAX = 256
G_HEADS = 4
G_LOWRANK = 16
G_TAU = 16.0
CONV_W = 3
LN_EPS = 1e-5
RMS_EPS = 1e-6
DEPTH = 1
DN_ALPHA = (2 * DEPTH) ** 0.25

LANES = 128
MIB = 1024 * 1024
VMEM_BUDGET = 56 * MIB

BF16 = jnp.bfloat16
F32 = jnp.float32
NEG_BIG = -1e30
LOG2E = 1.4426950408889634
INT_MIN = -2 ** 31
COUNT_UNROLL = (8, 2, 1)


def _nbytes(shape, dtype):
    n = jnp.dtype(dtype).itemsize
    for dim in shape:
        n *= 1 if dim is None else getattr(dim, "block_size", dim)
    return n


def _params(sem, blocks, scratch=(), temps=()):
    need = (2 * sum(_nbytes(*b) for b in blocks) + sum(_nbytes(*b) for b in scratch)
            + sum(_nbytes(*b) for b in temps))
    limit = -(-need // MIB) * MIB
    assert limit <= VMEM_BUDGET, (limit, VMEM_BUDGET)
    return pltpu.CompilerParams(dimension_semantics=sem, vmem_limit_bytes=limit)


def _tile(n, max_tile, quantum=LANES):
    best = None
    for t in range(quantum, min(n, max_tile) + 1, quantum):
        if n % t == 0:
            best = t
    assert best is not None, (n, max_tile, quantum)
    return best


def _dot(a, b):
    return jnp.dot(a, b, preferred_element_type=F32)


def _dot_nt(a, b):
    return lax.dot_general(a, b, (((1,), (1,)), ((), ())), preferred_element_type=F32)


def _dot_tn(a, b):
    return lax.dot_general(a, b, (((0,), (0,)), ((), ())), preferred_element_type=F32)


def _relayout_kernel(off_ref, kind_ref, a_ref, b_ref, o_ref, *, n_wi, n_small):
    del off_ref
    kind = kind_ref[pl.program_id(0)]

    @pl.when(kind == 0)
    def _():
        o_ref[...] = a_ref[0].astype(BF16)

    @pl.when(kind == 1)
    def _():
        o_ref[...] = jnp.zeros(o_ref.shape, BF16)

    @pl.when(kind == 2)
    def _():
        r = lax.broadcasted_iota(jnp.int32, o_ref.shape, 0)
        o_ref[...] = jnp.where(r < n_wi, a_ref[0], jnp.where(r < n_small, b_ref[0], 0.0)).astype(BF16)


def _relayout_w_in(w_t, src, order):
    _, n_in, d = w_t.shape
    starts = [src[n][0] + t * LANES for n in order for t in range((src[n][1] - src[n][0]) // LANES)]
    wi0, glr0 = src["wi"][0], src["glr"][0]
    n_wi = src["wi"][1] - wi0
    n_small = n_wi + src["glr"][1] - glr0
    glr_win = glr0 - n_wi
    assert n_small <= LANES and glr_win >= 0 and wi0 + LANES <= n_in and glr_win + LANES <= n_in
    assert all(o % 8 == 0 for o in starts + [wi0, glr_win])
    row_off8 = [o // 8 for o in starts + [0, wi0]]
    kind = [0] * len(starts) + [1, 2]
    nb = len(kind)
    tables = [jnp.asarray(np.asarray(t, np.int32)) for t in (row_off8, kind)]
    return pl.pallas_call(
        functools.partial(_relayout_kernel, n_wi=n_wi, n_small=n_small),
        grid_spec=pltpu.PrefetchScalarGridSpec(
            num_scalar_prefetch=2,
            grid=(nb,),
            in_specs=[pl.BlockSpec((pl.Element(1), pl.Element(LANES), pl.Element(d)),
                                   lambda b, off, kd: (0, off[b] * 8, 0)),
                      pl.BlockSpec((pl.Element(1), pl.Element(LANES), pl.Element(d)),
                                   lambda b, off, kd: (0, glr_win, 0))],
            out_specs=pl.BlockSpec((LANES, d), lambda b, off, kd: (b, 0))),
        out_shape=jax.ShapeDtypeStruct((nb * LANES, d), BF16),
        compiler_params=_params(("arbitrary",),
                                blocks=[((LANES, d), F32), ((LANES, d), F32), ((LANES, d), BF16)],
                                temps=[((LANES, d), F32)]),
        name="relayout_w_in",
    )(*tables, w_t, w_t)


def _proj_kernel(x_ref, w_ref, ws_ref, sc_ref, o_ref, os_ref, xb_scr):
    @pl.when(pl.program_id(1) == 0)
    def _():
        xb_scr[...] = x_ref[...].astype(BF16)
        os_ref[...] = _dot_nt(xb_scr[...], ws_ref[...])

    res = _dot_nt(xb_scr[...], w_ref[...]) * sc_ref[...]
    for c in range(o_ref.shape[0]):
        o_ref[c] = res[:, c * LANES:(c + 1) * LANES].astype(o_ref.dtype)


def _proj_blocks(x, wt, col_scale, *, tm, tn):
    s, k = x.shape
    n = wt.shape[0] - LANES
    return pl.pallas_call(
        _proj_kernel,
        grid=(s // tm, n // tn),
        in_specs=[pl.BlockSpec((tm, k), lambda i, j: (i, 0)),
                  pl.BlockSpec((tn, k), lambda i, j: (j, 0)),
                  pl.BlockSpec((LANES, k), lambda i, j: (n // LANES, 0)),
                  pl.BlockSpec((1, tn), lambda i, j: (0, j))],
        out_specs=[pl.BlockSpec((tn // LANES, tm, LANES), lambda i, j: (j, i, 0)),
                   pl.BlockSpec((tm, LANES), lambda i, j: (i, 0))],
        out_shape=[jax.ShapeDtypeStruct((n // LANES, s, LANES), BF16),
                   jax.ShapeDtypeStruct((s, LANES), F32)],
        scratch_shapes=[pltpu.VMEM((tm, k), BF16)],
        compiler_params=_params(("parallel", "arbitrary"),
                                blocks=[((tm, k), F32), ((tn, k), BF16), ((LANES, k), BF16), ((8, tn), F32),
                                        ((tm, tn), BF16), ((tm, LANES), F32)],
                                scratch=[((tm, k), BF16)],
                                temps=[((tm, tn), F32), ((tm, tn), F32)]),
        name="proj",
    )(x, wt, wt, col_scale)


def _mm_kernel(a_ref, b_ref, *rest, alpha, has_res, nk):
    if has_res:
        r_ref, o_ref = rest[0], rest[1]
        rest = rest[2:]
    else:
        r_ref, o_ref = None, rest[0]
        rest = rest[1:]

    def finish(acc):
        if has_res:
            acc = acc + alpha * r_ref[...]
        o_ref[...] = acc.astype(o_ref.dtype)

    if nk == 1:
        finish(_dot(a_ref[...], b_ref[...]))
        return
    acc_ref = rest[0]
    kk = pl.program_id(2)

    @pl.when(kk == 0)
    def _():
        acc_ref[...] = _dot(a_ref[...], b_ref[...])

    @pl.when(kk > 0)
    def _():
        acc_ref[...] += _dot(a_ref[...], b_ref[...])

    @pl.when(kk == nk - 1)
    def _():
        finish(acc_ref[...])


def _matmul(a, b, *, tm, tn, tk, out_dtype, residual=None, alpha=1.0, name="matmul"):
    m, k = a.shape
    n = b.shape[1]
    nk = k // tk
    in_specs = [pl.BlockSpec((tm, tk), lambda i, j, kk: (i, kk)),
                pl.BlockSpec((tk, tn), lambda i, j, kk: (kk, j))]
    args = [a, b]
    if residual is not None:
        in_specs.append(pl.BlockSpec((tm, tn), lambda i, j, kk: (i, j)))
        args.append(residual)
    scratch = [pltpu.VMEM((tm, tn), F32)] if nk > 1 else []
    return pl.pallas_call(
        functools.partial(_mm_kernel, alpha=alpha, has_res=residual is not None, nk=nk),
        grid=(m // tm, n // tn, nk),
        in_specs=in_specs,
        out_specs=pl.BlockSpec((tm, tn), lambda i, j, kk: (i, j)),
        out_shape=jax.ShapeDtypeStruct((m, n), out_dtype),
        scratch_shapes=scratch,
        compiler_params=_params(("parallel", "parallel", "arbitrary"),
                                blocks=[((tm, tk), a.dtype), ((tk, tn), b.dtype), ((tm, tn), out_dtype)]
                                + ([((tm, tn), F32)] if residual is not None else []),
                                scratch=[((tm, tn), F32)] if nk > 1 else [],
                                temps=[((tm, tn), F32)]),
        name=name,
    )(*args)


def _wo_kernel(oa_ref, og_ref, wt_ref, wb_ref, x_ref, o_ref, w_scr, *, ka, alpha):
    @pl.when(pl.program_id(1) == 0)
    def _():
        w_scr[:ka, :] = wt_ref[0].astype(BF16)
        w_scr[ka:, :] = wb_ref[0].astype(BF16)

    acc = _dot(oa_ref[...], w_scr[:ka, :]) + _dot(og_ref[...], w_scr[ka:, :])
    o_ref[...] = acc + alpha * x_ref[...]


def _w_o(o_a, o_g, w_o, x, *, tm, tn, alpha):
    s, ka = o_a.shape
    kg = o_g.shape[1]
    d = w_o.shape[2]
    assert ka % kg == 0
    return pl.pallas_call(
        functools.partial(_wo_kernel, ka=ka, alpha=alpha),
        grid=(d // tn, s // tm),
        in_specs=[pl.BlockSpec((tm, ka), lambda j, i: (i, 0)),
                  pl.BlockSpec((tm, kg), lambda j, i: (i, 0)),
                  pl.BlockSpec((1, ka, tn), lambda j, i: (0, 0, j)),
                  pl.BlockSpec((1, kg, tn), lambda j, i: (0, ka // kg, j)),
                  pl.BlockSpec((tm, tn), lambda j, i: (i, j))],
        out_specs=pl.BlockSpec((tm, tn), lambda j, i: (i, j)),
        out_shape=jax.ShapeDtypeStruct((s, d), F32),
        scratch_shapes=[pltpu.VMEM((ka + kg, tn), BF16)],
        compiler_params=_params(("parallel", "arbitrary"),
                                blocks=[((tm, ka), BF16), ((tm, kg), BF16), ((ka, tn), F32), ((kg, tn), F32),
                                        ((tm, tn), F32), ((tm, tn), F32)],
                                scratch=[((ka + kg, tn), BF16)],
                                temps=[((tm, tn), F32), ((tm, tn), F32)]),
        name="w_o",
    )(o_a, o_g, w_o, w_o, x)


def _ln_kernel(y_ref, g_ref, b_ref, *o_refs):
    y = y_ref[...]
    mu = jnp.mean(y, axis=1, keepdims=True)
    yc = y - mu
    var = jnp.mean(yc * yc, axis=1, keepdims=True)
    out = yc * lax.rsqrt(var + LN_EPS) * g_ref[...] + b_ref[...]
    for o_ref in o_refs:
        o_ref[...] = out.astype(o_ref.dtype)


def _layer_norm(y, g, b, *, tr, also_bf16):
    s, d = y.shape
    out_shape = [jax.ShapeDtypeStruct((s, d), F32)]
    out_specs = [pl.BlockSpec((tr, d), lambda i: (i, 0))]
    if also_bf16:
        out_shape.append(jax.ShapeDtypeStruct((s, d), BF16))
        out_specs.append(pl.BlockSpec((tr, d), lambda i: (i, 0)))
    return pl.pallas_call(
        _ln_kernel,
        grid=(s // tr,),
        in_specs=[pl.BlockSpec((tr, d), lambda i: (i, 0)),
                  pl.BlockSpec((1, d), lambda i: (0, 0)),
                  pl.BlockSpec((1, d), lambda i: (0, 0))],
        out_specs=out_specs,
        out_shape=out_shape,
        compiler_params=_params(("parallel",),
                                blocks=[((tr, d), F32), ((8, d), F32), ((8, d), F32)]
                                + [((tr, d), o.dtype) for o in out_shape],
                                temps=[((tr, d), F32), ((tr, d), F32)]),
        name="layer_norm",
    )(y, g.reshape(1, d), b.reshape(1, d))


def _idx_kernel(qi_ref, ki_ref, sm_ref, bias_ref, hi_scr, lo_scr, wb_scr, *, tq, tk, nkt, hi, group,
                topk, wscale):
    qb = pl.program_id(0)
    nk = ((qb + 1) * tq + tk - 1) // tk
    reps = tk // LANES
    i16 = jnp.int16
    low16 = -2 ** 15

    for h in range(hi):
        wb_scr[h] = jnp.broadcast_to(sm_ref[:, h:h + 1] * wscale, (tq, LANES))

    row = qb * tq + lax.broadcasted_iota(jnp.int32, (group, tk), 0)
    col = lax.broadcasted_iota(jnp.int32, (group, tk), 1)

    def lanes(a):
        return jnp.concatenate([a] * reps, axis=1) if reps > 1 else a

    def fold(a):
        out = a[:, :LANES]
        for r in range(1, reps):
            out = out + a[:, r * LANES:(r + 1) * LANES]
        return out

    def score_tile(kb, carry):
        kt = ki_ref[0, pl.ds(pl.multiple_of(kb * tk, tk), tk), :]
        for r in range(tq // group):
            rows = slice(r * group, (r + 1) * group)
            qs = qi_ref[:, rows, :].reshape(hi * group, IDX_DIM)
            d = _dot_nt(qs, kt)
            acc = jnp.zeros((group, tk), F32)
            for h in range(hi):
                acc = acc + jnp.maximum(d[h * group:(h + 1) * group], 0.0) * lanes(wb_scr[h, rows, :])
            bits = pltpu.bitcast(acc + 0.0, jnp.int32)
            key = jnp.where(kb * tk + col <= row + r * group,
                            bits ^ ((bits >> 31) & jnp.int32(0x7FFFFFFF)), INT_MIN)
            hi_scr[kb, rows, :] = (key >> 16).astype(i16)
            lo_scr[kb, rows, :] = ((key & 0xFFFF) + low16).astype(i16)
        return carry

    lax.fori_loop(0, nk, score_tile, 0)

    def count_ge(ref, cand):
        cand_t = lanes(cand.astype(i16))

        def step(kb, c):
            return c + fold(jnp.where(ref[kb] >= cand_t, i16(1), i16(0)))

        c = jnp.zeros((tq, LANES), i16)
        done = 0
        for width in COUNT_UNROLL:
            def step_many(q, c, width=width, done=done):
                for u in range(width):
                    c = step(done + q * width + u, c)
                return c

            n_groups = (nk - done) // width
            c = lax.fori_loop(0, n_groups, step_many, c)
            done = done + n_groups * width
        return jnp.sum(c.astype(F32), axis=1, keepdims=True)

    kf = float(topk)

    def bit_search(ref, base_cnt, cnt_start, stop_when_exact):
        def body(state):
            it, v, cnt_v = state
            cand = v + jnp.left_shift(jnp.int32(1), 15 - it)
            cnt = base_cnt + count_ge(ref, cand)
            ok = cnt >= kf
            return it + 1, jnp.where(ok, cand, v), jnp.where(ok, cnt, cnt_v)

        def unsettled(state):
            go = state[0] < 16
            if stop_when_exact:
                go = go & (jnp.max(jnp.where(state[2] == kf, 0.0, 1.0)) > 0.0)
            return go

        v0 = jnp.full((tq, LANES), low16, jnp.int32)
        _, v, cnt_v = lax.while_loop(unsettled, body, (jnp.int32(0), v0, cnt_start))
        return v, cnt_v

    n_all = jnp.zeros((tq, 1), F32) + (nk * tk).astype(F32)
    v_hi, n_from_hi = bit_search(hi_scr, 0.0, n_all, False)
    n_above = count_ge(hi_scr, v_hi + 1)
    v_hi_t = lanes(v_hi.astype(i16))

    def keep_equal(kb, carry):
        lo_scr[kb] = jnp.where(hi_scr[kb] == v_hi_t, lo_scr[kb], i16(low16))
        return carry

    lax.fori_loop(0, nk, keep_equal, 0)
    v_lo, n_sel = bit_search(lo_scr, n_above, n_from_hi, True)
    v_lo = jnp.where(v_hi == low16, jnp.maximum(v_lo, low16 + 1), v_lo)
    v_lo_t = lanes(v_lo.astype(i16))
    zero = jnp.zeros((tq, tk), bias_ref.dtype)
    neg = jnp.full((tq, tk), NEG_BIG, bias_ref.dtype)
    has_ties = jnp.max(jnp.where(n_sel > kf, 1.0, 0.0)) > 0.0

    @pl.when(jnp.logical_not(has_ties))
    def _():
        def emit(kb, carry):
            h16 = hi_scr[kb]
            sel = (h16 > v_hi_t) | ((h16 == v_hi_t) & (lo_scr[kb] >= v_lo_t))
            bias_ref[0, kb] = jnp.where(sel, zero, neg)
            return carry

        lax.fori_loop(0, nk, emit, 0)

    @pl.when(has_ties)
    def _():
        top16 = 2 ** 15 - 1
        n_gt = n_above + jnp.where(v_lo[:, :1] >= top16, 0.0, count_ge(lo_scr, jnp.minimum(v_lo + 1, top16)))
        need = kf - n_gt
        earlier = jnp.where(lax.broadcasted_iota(jnp.int32, (tk, tk), 0)
                            < lax.broadcasted_iota(jnp.int32, (tk, tk), 1), 1.0, 0.0).astype(BF16)
        one_b = jnp.ones((tq, tk), BF16)
        zero_b = jnp.zeros((tq, tk), BF16)

        def emit_ties(kb, seen):
            h16 = hi_scr[kb]
            l16 = lo_scr[kb]
            same_hi = h16 == v_hi_t
            larger = (h16 > v_hi_t) | (same_hi & (l16 > v_lo_t))
            tied = jnp.where(same_hi & (l16 == v_lo_t), one_b, zero_b)
            rank = seen + _dot(tied, earlier)
            keep = jnp.where(rank < need, 1.0, 0.0).astype(BF16)
            sel = larger | ((tied > 0.0) & (keep > 0.0))
            bias_ref[0, kb] = jnp.where(sel, zero, neg)
            return seen + jnp.sum(tied.astype(F32), axis=1, keepdims=True)

        lax.fori_loop(0, nk, emit_ties, jnp.zeros((tq, 1), F32))

    def fill(kb, carry):
        bias_ref[0, kb] = neg
        return carry

    lax.fori_loop(nk, nkt, fill, 0)


def _indexer(main, small, lay, *, s, tq, tk, topk):
    nkt = s // tk
    hi = IDX_HEADS
    group = min(tq, 64)
    qi0, ki0 = lay["qi"][0], lay["ki"][0]
    assert qi0 % hi == 0
    return pl.pallas_call(
        functools.partial(_idx_kernel, tq=tq, tk=tk, nkt=nkt, hi=hi, group=group, topk=topk,
                          wscale=IDX_HEADS ** -0.5 * IDX_DIM ** -0.5),
        grid=(s // tq,),
        in_specs=[pl.BlockSpec((hi, tq, LANES), lambda i: (qi0 // hi, i, 0)),
                  pl.BlockSpec((1, s, LANES), lambda i: (ki0, 0, 0)),
                  pl.BlockSpec((tq, LANES), lambda i: (i, 0))],
        out_specs=pl.BlockSpec((1, nkt, tq, tk), lambda i: (i, 0, 0, 0)),
        out_shape=jax.ShapeDtypeStruct((s // tq, nkt, tq, tk), BF16),
        scratch_shapes=[pltpu.VMEM((nkt, tq, tk), jnp.int16),
                        pltpu.VMEM((nkt, tq, tk), jnp.int16),
                        pltpu.VMEM((hi, tq, LANES), F32)],
        compiler_params=_params(("parallel",),
                                blocks=[((hi, tq, LANES), BF16), ((s, LANES), BF16), ((tq, LANES), F32),
                                        ((nkt, tq, tk), BF16)],
                                scratch=[((nkt, tq, tk), jnp.int16), ((nkt, tq, tk), jnp.int16),
                                         ((hi, tq, LANES), F32)],
                                temps=[((hi * group, tk), F32), ((hi * group, tk), F32)]),
        name="indexer",
    )(main, main, small)


def _attn_kernel(qi_ref, kb_ref, q_ref, k_ref, v_ref, b_ref, g_ref, o_ref, bias_scr, m_scr, acc_scr, *pipe,
                 nh, tq, tk):
    step = pl.program_id(0)
    qi = qi_ref[step]
    kb = kb_ref[step]

    @pl.when(kb == 0)
    def _():
        m_scr[...] = jnp.full(m_scr.shape, NEG_BIG, F32)
        acc_scr[...] = jnp.zeros(acc_scr.shape, F32)

    na, nb, sq, sk = b_ref.shape
    for a in range(na):
        for b in range(nb):
            bias_scr[a * sq:(a + 1) * sq, b * sk:(b + 1) * sk] = b_ref[a, b].astype(F32)

    ones = jnp.ones((tk, LANES), BF16)

    def qk(h, s_ref):
        s_ref[...] = _dot_nt(q_ref[h], k_ref[h])

    def soft(h, s_ref, p_ref, al_ref):
        s = s_ref[...] + bias_scr[...]
        m_prev = m_scr[h]
        m_new = jnp.maximum(m_prev, jnp.max(s, axis=1, keepdims=True))
        m_scr[h] = m_new
        p_ref[...] = jnp.exp2(s - m_new[:, :1]).astype(BF16)
        al_ref[...] = jnp.exp2(m_prev - m_new)

    def pv(h, p_ref, al_ref):
        al = al_ref[...]
        v_ext = jnp.concatenate([v_ref[h], ones], axis=1)
        acc_scr[h] = acc_scr[h] * jnp.concatenate([al, al], axis=1) + _dot(p_ref[...], v_ext)

    s_a, s_b, p_a, p_b, al_a, al_b = pipe
    qk(0, s_a)
    qk(1, s_b)
    soft(0, s_a, p_a, al_a)

    def pair(j, carry):
        qk(2 * j, s_a)
        soft(2 * j - 1, s_b, p_b, al_b)
        pv(2 * j - 2, p_a, al_a)
        qk(2 * j + 1, s_b)
        soft(2 * j, s_a, p_a, al_a)
        pv(2 * j - 1, p_b, al_b)
        return carry

    lax.fori_loop(1, nh // 2, pair, 0)
    soft(nh - 1, s_b, p_b, al_b)
    pv(nh - 2, p_a, al_a)
    pv(nh - 1, p_b, al_b)

    @pl.when(kb == (qi * tq + tq - 1) // tk)
    def _():
        ss = jnp.zeros((tq, LANES), F32)
        for h in range(nh):
            o_h = acc_scr[h, :, :LANES] / acc_scr[h, :, LANES:]
            acc_scr[h, :, :LANES] = o_h
            ss = ss + o_h * o_h
        ms = jnp.sum(ss, axis=1, keepdims=True) * (1.0 / (nh * LANES))
        r = lax.rsqrt(ms + RMS_EPS)
        for h in range(nh):
            sl = slice(h * LANES, (h + 1) * LANES)
            o_ref[:, sl] = (acc_scr[h, :, :LANES] * r * g_ref[:, sl]).astype(o_ref.dtype)


def _attention(main, bias, g, lay, *, s, tq, tk):
    nh = A_HEADS
    nqb, nkt, sq, sk = bias.shape
    q0, k0, v0 = lay["qa"][0], lay["ka"][0], lay["va"][0]
    assert q0 % nh == 0 and k0 % nh == 0 and v0 % nh == 0
    pairs = [(i, j) for i in range(s // tq) for j in range((i * tq + tq - 1) // tk + 1)]
    qi_tab = jnp.asarray(np.asarray([p[0] for p in pairs], np.int32))
    kb_tab = jnp.asarray(np.asarray([p[1] for p in pairs], np.int32))
    return pl.pallas_call(
        functools.partial(_attn_kernel, nh=nh, tq=tq, tk=tk),
        grid_spec=pltpu.PrefetchScalarGridSpec(
            num_scalar_prefetch=2,
            grid=(len(pairs),),
            in_specs=[pl.BlockSpec((nh, tq, LANES), lambda t, qi, kb: (q0 // nh, qi[t], 0)),
                      pl.BlockSpec((nh, tk, LANES), lambda t, qi, kb: (k0 // nh, kb[t], 0)),
                      pl.BlockSpec((nh, tk, LANES), lambda t, qi, kb: (v0 // nh, kb[t], 0)),
                      pl.BlockSpec((tq // sq, tk // sk, sq, sk), lambda t, qi, kb: (qi[t], kb[t], 0, 0)),
                      pl.BlockSpec((1, nh * LANES), lambda t, qi, kb: (0, 0))],
            out_specs=pl.BlockSpec((tq, nh * LANES), lambda t, qi, kb: (qi[t], 0)),
            scratch_shapes=[pltpu.VMEM((tq, tk), F32),
                            pltpu.VMEM((nh, tq, LANES), F32),
                            pltpu.VMEM((nh, tq, 2 * LANES), F32),
                            pltpu.VMEM((tq, tk), F32), pltpu.VMEM((tq, tk), F32),
                            pltpu.VMEM((tq, tk), BF16), pltpu.VMEM((tq, tk), BF16),
                            pltpu.VMEM((tq, LANES), F32), pltpu.VMEM((tq, LANES), F32)]),
        out_shape=jax.ShapeDtypeStruct((s, nh * LANES), BF16),
        compiler_params=_params(("arbitrary",),
                                blocks=[((nh, tq, LANES), BF16), ((nh, tk, LANES), BF16), ((nh, tk, LANES), BF16),
                                        ((tq, tk), BF16), ((8, nh * LANES), F32), ((tq, nh * LANES), BF16)],
                                scratch=[((tq, tk), F32), ((nh, tq, LANES), F32), ((nh, tq, 2 * LANES), F32),
                                         ((tq, tk), F32), ((tq, tk), F32), ((tq, tk), BF16), ((tq, tk), BF16),
                                         ((tq, LANES), F32), ((tq, LANES), F32)],
                                temps=[((tq, tk), F32), ((tq, tk), F32)]),
        name="attention",
    )(qi_tab, kb_tab, main, main, main, bias, g.reshape(1, nh * LANES))


def _gla_kernel(q_ref, k_ref, v_ref, gg_ref, sm_ref, w2_ref, bgk_ref, gn_ref, o_ref, st_scr, *,
                rows, chunk, nheads, dk, dv):
    @pl.when(pl.program_id(0) == 0)
    def _():
        st_scr[...] = jnp.zeros(st_scr.shape, F32)

    r_i = lax.broadcasted_iota(jnp.int32, (chunk, chunk), 0)
    c_i = lax.broadcasted_iota(jnp.int32, (chunk, chunk), 1)
    tri = jnp.where(r_i >= c_i, 1.0, 0.0).astype(BF16)
    nbk, nbv = dk // LANES, dv // LANES
    assert chunk >= 16 and chunk & (chunk - 1) == 0
    leaf = 2
    half_sizes = [chunk >> i for i in range(1, chunk.bit_length()) if chunk >> i >= leaf]
    quadrant = {m: (r_i // (2 * m) == c_i // (2 * m)) & (r_i % (2 * m) >= m) & (c_i % (2 * m) < m)
                for m in half_sizes}
    near = [(c_i == r_i - j) & (r_i % leaf >= j) for j in range(leaf)]
    sub = lax.broadcasted_iota(jnp.int32, (chunk // 8, 8, dk), 1)

    def reference_rows(b, m):
        if m >= 8:
            return jnp.concatenate(
                [jnp.broadcast_to(b[blk + m:blk + m + 1, :], (2 * m, dk)) for blk in range(0, chunk, 2 * m)],
                axis=0)
        b3 = b.reshape(chunk // 8, 8, dk)
        out = None
        for blk in range(0, 8, 2 * m):
            rows = jnp.broadcast_to(b3[:, blk + m:blk + m + 1, :], b3.shape)
            out = rows if out is None else jnp.where(sub >= blk, rows, out)
        return out.reshape(chunk, dk)

    def wide(ref, first, n, r0):
        return jnp.concatenate([ref[first + j, pl.ds(r0, chunk), :] for j in range(n)], axis=1)

    def step(c, carry):
        r0 = pl.multiple_of(c * chunk, chunk)
        z_all = _dot(sm_ref[pl.ds(r0, chunk), :].astype(BF16), w2_ref[...]) + bgk_ref[...]
        for h in range(nheads):
            q = wide(q_ref, h * nbk, nbk, r0).astype(F32) * (dk ** -0.5)
            k = wide(k_ref, h * nbk, nbk, r0).astype(F32)
            v = wide(v_ref, h * nbv, nbv, r0)
            z = z_all[:, h * dk:(h + 1) * dk]
            g = (jnp.minimum(z, 0.0) - jnp.log1p(jnp.exp(-jnp.abs(z)))) * (LOG2E / G_TAU)
            g_hi = g.astype(BF16)
            rem = g - g_hi.astype(F32)
            g_mid = rem.astype(BF16)
            g_lo = (rem - g_mid.astype(F32)).astype(BF16)
            b = _dot(tri, g_hi) + _dot(tri, g_mid) + _dot(tri, g_lo)
            b_last = b[chunk - 1:chunk, :]
            att = jnp.zeros((chunk, chunk), F32)
            for m in half_sizes:
                e = jnp.exp2(-jnp.abs(b - reference_rows(b, m)))
                att = att + jnp.where(quadrant[m], _dot_nt((q * e).astype(BF16), (k * e).astype(BF16)), 0.0)
            for j in range(leaf):
                kj = k if j == 0 else pltpu.roll(k, j, 0)
                bj = b if j == 0 else pltpu.roll(b, j, 0)
                pair = jnp.sum(q * kj * jnp.exp2(jnp.minimum(b - bj, 0.0)), axis=1, keepdims=True)
                att = att + jnp.where(near[j], pair, 0.0)
            st = st_scr[h]
            o = _dot(att.astype(BF16), v) + _dot_nt((q * jnp.exp2(b)).astype(BF16), st.astype(BF16))
            kd = (k * jnp.exp2(b_last - b)).astype(BF16)
            st_scr[h] = st * jnp.exp2(b_last) + _dot_tn(v, kd)
            ms = jnp.mean(o * o, axis=1, keepdims=True)
            gate = wide(gg_ref, h * nbv, nbv, r0).astype(F32)
            out = (o * lax.rsqrt(ms + RMS_EPS)) * gn_ref[...] * (gate * jax.nn.sigmoid(gate))
            o_ref[pl.ds(r0, chunk), h * dv:(h + 1) * dv] = out.astype(o_ref.dtype)
        return carry

    lax.fori_loop(0, rows // chunk, step, 0)


def _gla(main, small, w2, bgk, gn, lay, *, s, rows, chunk, dk, dv):
    nh = G_HEADS
    nqk, nv = nh * dk // LANES, nh * dv // LANES
    q0, k0, v0, g0 = lay["qg"][0], lay["kg"][0], lay["vg"][0], lay["gg"][0]
    assert q0 % nqk == 0 and k0 % nqk == 0 and v0 % nv == 0 and g0 % nv == 0
    return pl.pallas_call(
        functools.partial(_gla_kernel, rows=rows, chunk=chunk, nheads=nh, dk=dk, dv=dv),
        grid=(s // rows,),
        in_specs=[pl.BlockSpec((nqk, rows, LANES), lambda r: (q0 // nqk, r, 0)),
                  pl.BlockSpec((nqk, rows, LANES), lambda r: (k0 // nqk, r, 0)),
                  pl.BlockSpec((nv, rows, LANES), lambda r: (v0 // nv, r, 0)),
                  pl.BlockSpec((nv, rows, LANES), lambda r: (g0 // nv, r, 0)),
                  pl.BlockSpec((rows, LANES), lambda r: (r, 0)),
                  pl.BlockSpec((LANES, nh * dk), lambda r: (0, 0)),
                  pl.BlockSpec((1, nh * dk), lambda r: (0, 0)),
                  pl.BlockSpec((1, dv), lambda r: (0, 0))],
        out_specs=pl.BlockSpec((rows, nh * dv), lambda r: (r, 0)),
        out_shape=jax.ShapeDtypeStruct((s, nh * dv), BF16),
        scratch_shapes=[pltpu.VMEM((nh, dv, dk), F32)],
        compiler_params=_params(("arbitrary",),
                                blocks=[((nqk, rows, LANES), BF16), ((nqk, rows, LANES), BF16),
                                        ((nv, rows, LANES), BF16), ((nv, rows, LANES), BF16),
                                        ((rows, LANES), F32), ((LANES, nh * dk), BF16), ((8, nh * dk), F32),
                                        ((8, dv), F32), ((rows, nh * dv), BF16)],
                                scratch=[((nh, dv, dk), F32)],
                                temps=[((chunk, nh * dk), F32)] * 8 + [((dv, dk), F32)] * 2),
        name="gla",
    )(main, main, main, main, small, w2, bgk, gn.reshape(1, dv))


def _ffn_up_kernel(h_ref, wg_ref, wv_ref, cwg_ref, cwv_ref, cbg_ref, cbv_ref, wd_ref, o_ref, wdb_ref, w_scr,
                   hw_scr, *, tm, tn, bounds):
    wdb_ref[...] = wd_ref[0].astype(BF16)

    @pl.when(pl.program_id(1) == 0)
    def _():
        w_scr[:, :tn] = wg_ref[0].astype(BF16)
        w_scr[:, tn:] = wv_ref[0].astype(BF16)
        hw_scr[0:8, :] = jnp.zeros((8, 2 * tn), F32)

    cw = jnp.concatenate([cwg_ref[0], cwv_ref[0]], axis=1)
    cb = jnp.concatenate([cbg_ref[...], cbv_ref[...]], axis=1)
    w0, w1, w2 = cw[0:1, :], cw[1:2, :], cw[2:3, :]

    for lo, hi in zip(bounds[:-1], bounds[1:]):
        hw = _dot(h_ref[lo:hi, :], w_scr[...])
        hw_scr[8 + lo:8 + hi, :] = hw
        u = w2 * hw + w1 * hw_scr[7 + lo:7 + hi, :] + w0 * hw_scr[6 + lo:6 + hi, :] + cb
        gate, val = u[:, :tn], u[:, tn:]
        o_ref[lo:hi, :] = (gate * jax.nn.sigmoid(gate) * val).astype(o_ref.dtype)
    hw_scr[0:8, :] = hw_scr[tm:tm + 8, :]


def _ffn_up(hb, w_up, conv_w, conv_b, w_down, *, d_ff, tm, tn):
    s, d = hb.shape
    nj, ni = d_ff // tn, s // tm
    d_out = w_down.shape[2]
    slab = d_ff // (nj * ni)
    assert slab * nj * ni == d_ff and slab % 16 == 0
    return pl.pallas_call(
        functools.partial(_ffn_up_kernel, tm=tm, tn=tn, bounds=(0, tm // 2, tm)),
        grid=(nj, ni),
        in_specs=[pl.BlockSpec((tm, d), lambda j, i: (i, 0)),
                  pl.BlockSpec((1, d, tn), lambda j, i: (0, 0, j)),
                  pl.BlockSpec((1, d, tn), lambda j, i: (0, 0, nj + j)),
                  pl.BlockSpec((1, CONV_W, tn), lambda j, i: (0, 0, j)),
                  pl.BlockSpec((1, CONV_W, tn), lambda j, i: (0, 0, nj + j)),
                  pl.BlockSpec((1, tn), lambda j, i: (0, j)),
                  pl.BlockSpec((1, tn), lambda j, i: (0, nj + j)),
                  pl.BlockSpec((1, slab, d_out), lambda j, i: (0, j * ni + i, 0))],
        out_specs=[pl.BlockSpec((tm, tn), lambda j, i: (i, j)),
                   pl.BlockSpec((slab, d_out), lambda j, i: (j * ni + i, 0))],
        out_shape=[jax.ShapeDtypeStruct((s, d_ff), BF16), jax.ShapeDtypeStruct((d_ff, d_out), BF16)],
        scratch_shapes=[pltpu.VMEM((d, 2 * tn), BF16), pltpu.VMEM((8 + tm, 2 * tn), F32)],
        compiler_params=_params(("parallel", "arbitrary"),
                                blocks=[((tm, d), BF16), ((d, tn), F32), ((d, tn), F32), ((8, tn), F32),
                                        ((8, tn), F32), ((8, tn), F32), ((8, tn), F32), ((tm, tn), BF16),
                                        ((slab, d_out), F32), ((slab, d_out), BF16)],
                                scratch=[((d, 2 * tn), BF16), ((8 + tm, 2 * tn), F32)],
                                temps=[((tm // 2, 2 * tn), F32)] * 3),
        name="ffn_up",
    )(hb, w_up, w_up, conv_w, conv_w, conv_b, conv_b, w_down)


def _layout(d_model):
    a_width = A_HEADS * A_HEAD_DIM
    g_width = d_model - a_width
    g_kwidth = g_width // 2
    names = ("qa", "ka", "va", "qi", "ki", "wi", "qg", "kg", "vg", "glr", "gg")
    sizes = (a_width, a_width, a_width, IDX_HEADS * IDX_DIM, IDX_DIM, IDX_HEADS,
             g_kwidth, g_kwidth, g_width, G_LOWRANK, g_width)
    offs = np.concatenate([[0], np.cumsum(sizes)])
    src = {n: (int(offs[i]), int(offs[i + 1])) for i, n in enumerate(names)}
    order = ("qi", "qa", "ka", "va", "qg", "kg", "vg", "gg", "ki")
    lay, blk = {}, 0
    for n in order:
        width = src[n][1] - src[n][0]
        assert width % LANES == 0
        lay[n] = (blk, width // LANES)
        blk += width // LANES
    return src, order, lay, blk


def kernel(x, w_in, w_gk2, b_gk, attn_out_g, gla_norm_g, w_o, ln1_g, ln1_b, w_up, conv_w, conv_b,
           w_down, ln2_g, ln2_b):
    assert x.shape[0] == 1 and w_in.shape[0] == DEPTH == 1
    _, s, d = x.shape
    x2 = x[0]
    src, order, lay, nb_main = _layout(d)
    g_width = d - A_HEADS * A_HEAD_DIM
    dv = g_width // G_HEADS
    dk = dv // 2
    d_ff = w_down.shape[1]
    topk = min(TOPK_MAX, s // 4)

    w_proj = _relayout_w_in(jnp.swapaxes(w_in, 1, 2), src, order)
    n_small = IDX_HEADS + G_LOWRANK
    w2 = jnp.zeros((LANES, G_HEADS * dk), F32).at[IDX_HEADS:n_small].set(w_gk2[0]).astype(BF16)
    bgk = b_gk[0].reshape(1, G_HEADS * dk)
    n_main = (nb_main + 1) * LANES
    q_lo, q_hi = lay["qa"][0] * LANES, (lay["qa"][0] + lay["qa"][1]) * LANES
    col = np.ones((1, n_main), np.float32)
    col[:, q_lo:q_hi] = A_HEAD_DIM ** -0.5 * LOG2E
    col_scale = jnp.asarray(col)

    tm = _tile(s, 1024)
    main, small = _proj_blocks(x2, w_proj, col_scale, tm=_tile(s, 512), tn=_tile(n_main, 1280))
    bias = _indexer(main, small, lay, s=s, tq=256, tk=256, topk=topk)
    ta = _tile(s, 512)
    o_a = _attention(main, bias, attn_out_g[0], lay, s=s, tq=ta, tk=ta)
    o_g = _gla(main, small, w2, bgk, gla_norm_g[0], lay, s=s, rows=_tile(s, 512), chunk=256, dk=dk, dv=dv)
    y1 = _w_o(o_a, o_g, w_o, x2, tm=tm, tn=_tile(d, 512), alpha=DN_ALPHA)
    h, hb = _layer_norm(y1, ln1_g[0], ln1_b[0], tr=_tile(s, 256, 8), also_bf16=True)

    act, wd = _ffn_up(hb, w_up, conv_w, conv_b[0].reshape(1, 2 * d_ff), w_down, d_ff=d_ff, tm=tm,
                      tn=_tile(d_ff, 256))
    y2 = _matmul(act, wd, tm=_tile(s, 512), tn=_tile(d, 512), tk=d_ff, out_dtype=F32, residual=h,
                 alpha=DN_ALPHA, name="w_down")
    (out,) = _layer_norm(y2, ln2_g[0], ln2_b[0], tr=_tile(s, 256, 8), also_bf16=False)
    return out[None]
```

```python
import functools

import numpy as np
import jax
import jax.numpy as jnp
from jax import lax
from jax.experimental import pallas as pl
from jax.experimental.pallas import tpu as pltpu

A_HEADS = 16
A_HEAD_DIM = 128
IDX_HEADS = 32
IDX_DIM = 128
TOPK_MAX = 256
G_HEADS = 4
G_LOWRANK = 16
G_TAU = 16.0
CONV_W = 3
LN_EPS = 1e-5
RMS_EPS = 1e-6
DEPTH = 1
DN_ALPHA = (2 * DEPTH) ** 0.25

LANES = 128
MIB = 1024 * 1024
VMEM_BUDGET = 56 * MIB

BF16 = jnp.bfloat16
F32 = jnp.float32
NEG_BIG = -1e30
LOG2E = 1.4426950408889634
INT_MIN = -2 ** 31
COUNT_UNROLL = (8, 2, 1)


def _nbytes(shape, dtype):
    n = jnp.dtype(dtype).itemsize
    for dim in shape:
        n *= 1 if dim is None else getattr(dim, "block_size", dim)
    return n


def _params(sem, blocks, scratch=(), temps=()):
    need = (2 * sum(_nbytes(*b) for b in blocks) + sum(_nbytes(*b) for b in scratch)
            + sum(_nbytes(*b) for b in temps))
    limit = -(-need // MIB) * MIB
    assert limit <= VMEM_BUDGET, (limit, VMEM_BUDGET)
    return pltpu.CompilerParams(dimension_semantics=sem, vmem_limit_bytes=limit)


def _tile(n, max_tile, quantum=LANES):
    best = None
    for t in range(quantum, min(n, max_tile) + 1, quantum):
        if n % t == 0:
            best = t
    assert best is not None, (n, max_tile, quantum)
    return best


def _dot(a, b):
    return jnp.dot(a, b, preferred_element_type=F32)


def _dot_nt(a, b):
    return lax.dot_general(a, b, (((1,), (1,)), ((), ())), preferred_element_type=F32)


def _dot_tn(a, b):
    return lax.dot_general(a, b, (((0,), (0,)), ((), ())), preferred_element_type=F32)


def _relayout_kernel(off_ref, kind_ref, a_ref, b_ref, o_ref, *, n_wi, n_small):
    del off_ref
    kind = kind_ref[pl.program_id(0)]

    @pl.when(kind == 0)
    def _():
        o_ref[...] = a_ref[0].astype(BF16)

    @pl.when(kind == 1)
    def _():
        o_ref[...] = jnp.zeros(o_ref.shape, BF16)

    @pl.when(kind == 2)
    def _():
        r = lax.broadcasted_iota(jnp.int32, o_ref.shape, 0)
        o_ref[...] = jnp.where(r < n_wi, a_ref[0], jnp.where(r < n_small, b_ref[0], 0.0)).astype(BF16)


def _relayout_w_in(w_t, src, order):
    _, n_in, d = w_t.shape
    starts = [src[n][0] + t * LANES for n in order for t in range((src[n][1] - src[n][0]) // LANES)]
    wi0, glr0 = src["wi"][0], src["glr"][0]
    n_wi = src["wi"][1] - wi0
    n_small = n_wi + src["glr"][1] - glr0
    glr_win = glr0 - n_wi
    assert n_small <= LANES and glr_win >= 0 and wi0 + LANES <= n_in and glr_win + LANES <= n_in
    assert all(o % 8 == 0 for o in starts + [wi0, glr_win])
    row_off8 = [o // 8 for o in starts + [0, wi0]]
    kind = [0] * len(starts) + [1, 2]
    nb = len(kind)
    tables = [jnp.asarray(np.asarray(t, np.int32)) for t in (row_off8, kind)]
    return pl.pallas_call(
        functools.partial(_relayout_kernel, n_wi=n_wi, n_small=n_small),
        grid_spec=pltpu.PrefetchScalarGridSpec(
            num_scalar_prefetch=2,
            grid=(nb,),
            in_specs=[pl.BlockSpec((pl.Element(1), pl.Element(LANES), pl.Element(d)),
                                   lambda b, off, kd: (0, off[b] * 8, 0)),
                      pl.BlockSpec((pl.Element(1), pl.Element(LANES), pl.Element(d)),
                                   lambda b, off, kd: (0, glr_win, 0))],
            out_specs=pl.BlockSpec((LANES, d), lambda b, off, kd: (b, 0))),
        out_shape=jax.ShapeDtypeStruct((nb * LANES, d), BF16),
        compiler_params=_params(("arbitrary",),
                                blocks=[((LANES, d), F32), ((LANES, d), F32), ((LANES, d), BF16)],
                                temps=[((LANES, d), F32)]),
        name="relayout_w_in",
    )(*tables, w_t, w_t)


def _proj_kernel(x_ref, w_ref, ws_ref, sc_ref, o_ref, os_ref, xb_scr):
    @pl.when(pl.program_id(1) == 0)
    def _():
        xb_scr[...] = x_ref[...].astype(BF16)
        os_ref[...] = _dot_nt(xb_scr[...], ws_ref[...])

    res = _dot_nt(xb_scr[...], w_ref[...]) * sc_ref[...]
    for c in range(o_ref.shape[0]):
        o_ref[c] = res[:, c * LANES:(c + 1) * LANES].astype(o_ref.dtype)


def _proj_blocks(x, wt, col_scale, *, tm, tn):
    s, k = x.shape
    n = wt.shape[0] - LANES
    return pl.pallas_call(
        _proj_kernel,
        grid=(s // tm, n // tn),
        in_specs=[pl.BlockSpec((tm, k), lambda i, j: (i, 0)),
                  pl.BlockSpec((tn, k), lambda i, j: (j, 0)),
                  pl.BlockSpec((LANES, k), lambda i, j: (n // LANES, 0)),
                  pl.BlockSpec((1, tn), lambda i, j: (0, j))],
        out_specs=[pl.BlockSpec((tn // LANES, tm, LANES), lambda i, j: (j, i, 0)),
                   pl.BlockSpec((tm, LANES), lambda i, j: (i, 0))],
        out_shape=[jax.ShapeDtypeStruct((n // LANES, s, LANES), BF16),
                   jax.ShapeDtypeStruct((s, LANES), F32)],
        scratch_shapes=[pltpu.VMEM((tm, k), BF16)],
        compiler_params=_params(("parallel", "arbitrary"),
                                blocks=[((tm, k), F32), ((tn, k), BF16), ((LANES, k), BF16), ((8, tn), F32),
                                        ((tm, tn), BF16), ((tm, LANES), F32)],
                                scratch=[((tm, k), BF16)],
                                temps=[((tm, tn), F32), ((tm, tn), F32)]),
        name="proj",
    )(x, wt, wt, col_scale)


def _mm_kernel(a_ref, b_ref, *rest, alpha, has_res, nk):
    if has_res:
        r_ref, o_ref = rest[0], rest[1]
        rest = rest[2:]
    else:
        r_ref, o_ref = None, rest[0]
        rest = rest[1:]

    def finish(acc):
        if has_res:
            acc = acc + alpha * r_ref[...]
        o_ref[...] = acc.astype(o_ref.dtype)

    if nk == 1:
        finish(_dot(a_ref[...], b_ref[...]))
        return
    acc_ref = rest[0]
    kk = pl.program_id(2)

    @pl.when(kk == 0)
    def _():
        acc_ref[...] = _dot(a_ref[...], b_ref[...])

    @pl.when(kk > 0)
    def _():
        acc_ref[...] += _dot(a_ref[...], b_ref[...])

    @pl.when(kk == nk - 1)
    def _():
        finish(acc_ref[...])


def _matmul(a, b, *, tm, tn, tk, out_dtype, residual=None, alpha=1.0, name="matmul"):
    m, k = a.shape
    n = b.shape[1]
    nk = k // tk
    in_specs = [pl.BlockSpec((tm, tk), lambda i, j, kk: (i, kk)),
                pl.BlockSpec((tk, tn), lambda i, j, kk: (kk, j))]
    args = [a, b]
    if residual is not None:
        in_specs.append(pl.BlockSpec((tm, tn), lambda i, j, kk: (i, j)))
        args.append(residual)
    scratch = [pltpu.VMEM((tm, tn), F32)] if nk > 1 else []
    return pl.pallas_call(
        functools.partial(_mm_kernel, alpha=alpha, has_res=residual is not None, nk=nk),
        grid=(m // tm, n // tn, nk),
        in_specs=in_specs,
        out_specs=pl.BlockSpec((tm, tn), lambda i, j, kk: (i, j)),
        out_shape=jax.ShapeDtypeStruct((m, n), out_dtype),
        scratch_shapes=scratch,
        compiler_params=_params(("parallel", "parallel", "arbitrary"),
                                blocks=[((tm, tk), a.dtype), ((tk, tn), b.dtype), ((tm, tn), out_dtype)]
                                + ([((tm, tn), F32)] if residual is not None else []),
                                scratch=[((tm, tn), F32)] if nk > 1 else [],
                                temps=[((tm, tn), F32)]),
        name=name,
    )(*args)


def _wo_kernel(oa_ref, og_ref, wt_ref, wb_ref, x_ref, o_ref, w_scr, *, ka, alpha):
    @pl.when(pl.program_id(1) == 0)
    def _():
        w_scr[:ka, :] = wt_ref[0].astype(BF16)
        w_scr[ka:, :] = wb_ref[0].astype(BF16)

    acc = _dot(oa_ref[...], w_scr[:ka, :]) + _dot(og_ref[...], w_scr[ka:, :])
    o_ref[...] = acc + alpha * x_ref[...]


def _w_o(o_a, o_g, w_o, x, *, tm, tn, alpha):
    s, ka = o_a.shape
    kg = o_g.shape[1]
    d = w_o.shape[2]
    assert ka % kg == 0
    return pl.pallas_call(
        functools.partial(_wo_kernel, ka=ka, alpha=alpha),
        grid=(d // tn, s // tm),
        in_specs=[pl.BlockSpec((tm, ka), lambda j, i: (i, 0)),
                  pl.BlockSpec((tm, kg), lambda j, i: (i, 0)),
                  pl.BlockSpec((1, ka, tn), lambda j, i: (0, 0, j)),
                  pl.BlockSpec((1, kg, tn), lambda j, i: (0, ka // kg, j)),
                  pl.BlockSpec((tm, tn), lambda j, i: (i, j))],
        out_specs=pl.BlockSpec((tm, tn), lambda j, i: (i, j)),
        out_shape=jax.ShapeDtypeStruct((s, d), F32),
        scratch_shapes=[pltpu.VMEM((ka + kg, tn), BF16)],
        compiler_params=_params(("parallel", "arbitrary"),
                                blocks=[((tm, ka), BF16), ((tm, kg), BF16), ((ka, tn), F32), ((kg, tn), F32),
                                        ((tm, tn), F32), ((tm, tn), F32)],
                                scratch=[((ka + kg, tn), BF16)],
                                temps=[((tm, tn), F32), ((tm, tn), F32)]),
        name="w_o",
    )(o_a, o_g, w_o, w_o, x)


def _ln_kernel(y_ref, g_ref, b_ref, *o_refs):
    y = y_ref[...]
    mu = jnp.mean(y, axis=1, keepdims=True)
    yc = y - mu
    var = jnp.mean(yc * yc, axis=1, keepdims=True)
    out = yc * lax.rsqrt(var + LN_EPS) * g_ref[...] + b_ref[...]
    for o_ref in o_refs:
        o_ref[...] = out.astype(o_ref.dtype)


def _layer_norm(y, g, b, *, tr, also_bf16):
    s, d = y.shape
    out_shape = [jax.ShapeDtypeStruct((s, d), F32)]
    out_specs = [pl.BlockSpec((tr, d), lambda i: (i, 0))]
    if also_bf16:
        out_shape.append(jax.ShapeDtypeStruct((s, d), BF16))
        out_specs.append(pl.BlockSpec((tr, d), lambda i: (i, 0)))
    return pl.pallas_call(
        _ln_kernel,
        grid=(s // tr,),
        in_specs=[pl.BlockSpec((tr, d), lambda i: (i, 0)),
                  pl.BlockSpec((1, d), lambda i: (0, 0)),
                  pl.BlockSpec((1, d), lambda i: (0, 0))],
        out_specs=out_specs,
        out_shape=out_shape,
        compiler_params=_params(("parallel",),
                                blocks=[((tr, d), F32), ((8, d), F32), ((8, d), F32)]
                                + [((tr, d), o.dtype) for o in out_shape],
                                temps=[((tr, d), F32), ((tr, d), F32)]),
        name="layer_norm",
    )(y, g.reshape(1, d), b.reshape(1, d))


def _idx_kernel(qi_ref, ki_ref, sm_ref, bias_ref, hi_scr, lo_scr, wb_scr, *, tq, tk, nkt, hi, group,
                topk, wscale):
    qb = pl.program_id(0)
    nk = ((qb + 1) * tq + tk - 1) // tk
    reps = tk // LANES
    i16 = jnp.int16
    low16 = -2 ** 15

    for h in range(hi):
        wb_scr[h] = jnp.broadcast_to(sm_ref[:, h:h + 1] * wscale, (tq, LANES))

    row = qb * tq + lax.broadcasted_iota(jnp.int32, (group, tk), 0)
    col = lax.broadcasted_iota(jnp.int32, (group, tk), 1)

    def lanes(a):
        return jnp.concatenate([a] * reps, axis=1) if reps > 1 else a

    def fold(a):
        out = a[:, :LANES]
        for r in range(1, reps):
            out = out + a[:, r * LANES:(r + 1) * LANES]
        return out

    def score_tile(kb, carry):
        kt = ki_ref[0, pl.ds(pl.multiple_of(kb * tk, tk), tk), :]
        for r in range(tq // group):
            rows = slice(r * group, (r + 1) * group)
            qs = qi_ref[:, rows, :].reshape(hi * group, IDX_DIM)
            d = _dot_nt(qs, kt)
            acc = jnp.zeros((group, tk), F32)
            for h in range(hi):
                acc = acc + jnp.maximum(d[h * group:(h + 1) * group], 0.0) * lanes(wb_scr[h, rows, :])
            bits = pltpu.bitcast(acc + 0.0, jnp.int32)
            key = jnp.where(kb * tk + col <= row + r * group,
                            bits ^ ((bits >> 31) & jnp.int32(0x7FFFFFFF)), INT_MIN)
            hi_scr[kb, rows, :] = (key >> 16).astype(i16)
            lo_scr[kb, rows, :] = ((key & 0xFFFF) + low16).astype(i16)
        return carry

    lax.fori_loop(0, nk, score_tile, 0)

    def count_ge(ref, cand):
        cand_t = lanes(cand.astype(i16))

        def step(kb, c):
            return c + fold(jnp.where(ref[kb] >= cand_t, i16(1), i16(0)))

        c = jnp.zeros((tq, LANES), i16)
        done = 0
        for width in COUNT_UNROLL:
            def step_many(q, c, width=width, done=done):
                for u in range(width):
                    c = step(done + q * width + u, c)
                return c

            n_groups = (nk - done) // width
            c = lax.fori_loop(0, n_groups, step_many, c)
            done = done + n_groups * width
        return jnp.sum(c.astype(F32), axis=1, keepdims=True)

    kf = float(topk)

    def bit_search(ref, base_cnt, cnt_start, stop_when_exact):
        def body(state):
            it, v, cnt_v = state
            cand = v + jnp.left_shift(jnp.int32(1), 15 - it)
            cnt = base_cnt + count_ge(ref, cand)
            ok = cnt >= kf
            return it + 1, jnp.where(ok, cand, v), jnp.where(ok, cnt, cnt_v)

        def unsettled(state):
            go = state[0] < 16
            if stop_when_exact:
                go = go & (jnp.max(jnp.where(state[2] == kf, 0.0, 1.0)) > 0.0)
            return go

        v0 = jnp.full((tq, LANES), low16, jnp.int32)
        _, v, cnt_v = lax.while_loop(unsettled, body, (jnp.int32(0), v0, cnt_start))
        return v, cnt_v

    n_all = jnp.zeros((tq, 1), F32) + (nk * tk).astype(F32)
    v_hi, n_from_hi = bit_search(hi_scr, 0.0, n_all, False)
    n_above = count_ge(hi_scr, v_hi + 1)
    v_hi_t = lanes(v_hi.astype(i16))

    def keep_equal(kb, carry):
        lo_scr[kb] = jnp.where(hi_scr[kb] == v_hi_t, lo_scr[kb], i16(low16))
        return carry

    lax.fori_loop(0, nk, keep_equal, 0)
    v_lo, n_sel = bit_search(lo_scr, n_above, n_from_hi, True)
    v_lo = jnp.where(v_hi == low16, jnp.maximum(v_lo, low16 + 1), v_lo)
    v_lo_t = lanes(v_lo.astype(i16))
    zero = jnp.zeros((tq, tk), bias_ref.dtype)
    neg = jnp.full((tq, tk), NEG_BIG, bias_ref.dtype)
    has_ties = jnp.max(jnp.where(n_sel > kf, 1.0, 0.0)) > 0.0

    @pl.when(jnp.logical_not(has_ties))
    def _():
        def emit(kb, carry):
            h16 = hi_scr[kb]
            sel = (h16 > v_hi_t) | ((h16 == v_hi_t) & (lo_scr[kb] >= v_lo_t))
            bias_ref[0, kb] = jnp.where(sel, zero, neg)
            return carry

        lax.fori_loop(0, nk, emit, 0)

    @pl.when(has_ties)
    def _():
        top16 = 2 ** 15 - 1
        n_gt = n_above + jnp.where(v_lo[:, :1] >= top16, 0.0, count_ge(lo_scr, jnp.minimum(v_lo + 1, top16)))
        need = kf - n_gt
        earlier = jnp.where(lax.broadcasted_iota(jnp.int32, (tk, tk), 0)
                            < lax.broadcasted_iota(jnp.int32, (tk, tk), 1), 1.0, 0.0).astype(BF16)
        one_b = jnp.ones((tq, tk), BF16)
        zero_b = jnp.zeros((tq, tk), BF16)

        def emit_ties(kb, seen):
            h16 = hi_scr[kb]
            l16 = lo_scr[kb]
            same_hi = h16 == v_hi_t
            larger = (h16 > v_hi_t) | (same_hi & (l16 > v_lo_t))
            tied = jnp.where(same_hi & (l16 == v_lo_t), one_b, zero_b)
            rank = seen + _dot(tied, earlier)
            keep = jnp.where(rank < need, 1.0, 0.0).astype(BF16)
            sel = larger | ((tied > 0.0) & (keep > 0.0))
            bias_ref[0, kb] = jnp.where(sel, zero, neg)
            return seen + jnp.sum(tied.astype(F32), axis=1, keepdims=True)

        lax.fori_loop(0, nk, emit_ties, jnp.zeros((tq, 1), F32))

    def fill(kb, carry):
        bias_ref[0, kb] = neg
        return carry

    lax.fori_loop(nk, nkt, fill, 0)


def _indexer(main, small, lay, *, s, tq, tk, topk):
    nkt = s // tk
    hi = IDX_HEADS
    group = min(tq, 64)
    qi0, ki0 = lay["qi"][0], lay["ki"][0]
    assert qi0 % hi == 0
    return pl.pallas_call(
        functools.partial(_idx_kernel, tq=tq, tk=tk, nkt=nkt, hi=hi, group=group, topk=topk,
                          wscale=IDX_HEADS ** -0.5 * IDX_DIM ** -0.5),
        grid=(s // tq,),
        in_specs=[pl.BlockSpec((hi, tq, LANES), lambda i: (qi0 // hi, i, 0)),
                  pl.BlockSpec((1, s, LANES), lambda i: (ki0, 0, 0)),
                  pl.BlockSpec((tq, LANES), lambda i: (i, 0))],
        out_specs=pl.BlockSpec((1, nkt, tq, tk), lambda i: (i, 0, 0, 0)),
        out_shape=jax.ShapeDtypeStruct((s // tq, nkt, tq, tk), BF16),
        scratch_shapes=[pltpu.VMEM((nkt, tq, tk), jnp.int16),
                        pltpu.VMEM((nkt, tq, tk), jnp.int16),
                        pltpu.VMEM((hi, tq, LANES), F32)],
        compiler_params=_params(("parallel",),
                                blocks=[((hi, tq, LANES), BF16), ((s, LANES), BF16), ((tq, LANES), F32),
                                        ((nkt, tq, tk), BF16)],
                                scratch=[((nkt, tq, tk), jnp.int16), ((nkt, tq, tk), jnp.int16),
                                         ((hi, tq, LANES), F32)],
                                temps=[((hi * group, tk), F32), ((hi * group, tk), F32)]),
        name="indexer",
    )(main, main, small)


def _attn_kernel(qi_ref, kb_ref, q_ref, k_ref, v_ref, b_ref, g_ref, o_ref, bias_scr, m_scr, acc_scr, *pipe,
                 nh, tq, tk):
    step = pl.program_id(0)
    qi = qi_ref[step]
    kb = kb_ref[step]

    @pl.when(kb == 0)
    def _():
        m_scr[...] = jnp.full(m_scr.shape, NEG_BIG, F32)
        acc_scr[...] = jnp.zeros(acc_scr.shape, F32)

    na, nb, sq, sk = b_ref.shape
    for a in range(na):
        for b in range(nb):
            bias_scr[a * sq:(a + 1) * sq, b * sk:(b + 1) * sk] = b_ref[a, b].astype(F32)

    ones = jnp.ones((tk, LANES), BF16)

    def qk(h, s_ref):
        s_ref[...] = _dot_nt(q_ref[h], k_ref[h])

    def soft(h, s_ref, p_ref, al_ref):
        s = s_ref[...] + bias_scr[...]
        m_prev = m_scr[h]
        m_new = jnp.maximum(m_prev, jnp.max(s, axis=1, keepdims=True))
        m_scr[h] = m_new
        p_ref[...] = jnp.exp2(s - jnp.concatenate([m_new] * (tk // LANES), axis=1)).astype(BF16)
        al_ref[...] = jnp.exp2(m_prev - m_new)

    def pv(h, p_ref, al_ref):
        al = al_ref[...]
        v_ext = jnp.concatenate([v_ref[h], ones], axis=1)
        acc_scr[h] = acc_scr[h] * jnp.concatenate([al, al], axis=1) + _dot(p_ref[...], v_ext)

    s_a, s_b, p_a, p_b, al_a, al_b = pipe
    qk(0, s_a)
    qk(1, s_b)
    soft(0, s_a, p_a, al_a)

    def pair(j, carry):
        qk(2 * j, s_a)
        soft(2 * j - 1, s_b, p_b, al_b)
        pv(2 * j - 2, p_a, al_a)
        qk(2 * j + 1, s_b)
        soft(2 * j, s_a, p_a, al_a)
        pv(2 * j - 1, p_b, al_b)
        return carry

    lax.fori_loop(1, nh // 2, pair, 0)
    soft(nh - 1, s_b, p_b, al_b)
    pv(nh - 2, p_a, al_a)
    pv(nh - 1, p_b, al_b)

    @pl.when(kb == (qi * tq + tq - 1) // tk)
    def _():
        ss = jnp.zeros((tq, LANES), F32)
        for h in range(nh):
            o_h = acc_scr[h, :, :LANES] / acc_scr[h, :, LANES:]
            acc_scr[h, :, :LANES] = o_h
            ss = ss + o_h * o_h
        ms = jnp.sum(ss, axis=1, keepdims=True) * (1.0 / (nh * LANES))
        r = lax.rsqrt(ms + RMS_EPS)
        for h in range(nh):
            sl = slice(h * LANES, (h + 1) * LANES)
            o_ref[:, sl] = (acc_scr[h, :, :LANES] * r * g_ref[:, sl]).astype(o_ref.dtype)


def _attention(main, bias, g, lay, *, s, tq, tk):
    nh = A_HEADS
    nqb, nkt, sq, sk = bias.shape
    q0, k0, v0 = lay["qa"][0], lay["ka"][0], lay["va"][0]
    assert q0 % nh == 0 and k0 % nh == 0 and v0 % nh == 0
    pairs = [(i, j) for i in range(s // tq) for j in range((i * tq + tq - 1) // tk + 1)]
    qi_tab = jnp.asarray(np.asarray([p[0] for p in pairs], np.int32))
    kb_tab = jnp.asarray(np.asarray([p[1] for p in pairs], np.int32))
    return pl.pallas_call(
        functools.partial(_attn_kernel, nh=nh, tq=tq, tk=tk),
        grid_spec=pltpu.PrefetchScalarGridSpec(
            num_scalar_prefetch=2,
            grid=(len(pairs),),
            in_specs=[pl.BlockSpec((nh, tq, LANES), lambda t, qi, kb: (q0 // nh, qi[t], 0)),
                      pl.BlockSpec((nh, tk, LANES), lambda t, qi, kb: (k0 // nh, kb[t], 0)),
                      pl.BlockSpec((nh, tk, LANES), lambda t, qi, kb: (v0 // nh, kb[t], 0)),
                      pl.BlockSpec((tq // sq, tk // sk, sq, sk), lambda t, qi, kb: (qi[t], kb[t], 0, 0)),
                      pl.BlockSpec((1, nh * LANES), lambda t, qi, kb: (0, 0))],
            out_specs=pl.BlockSpec((tq, nh * LANES), lambda t, qi, kb: (qi[t], 0)),
            scratch_shapes=[pltpu.VMEM((tq, tk), F32),
                            pltpu.VMEM((nh, tq, LANES), F32),
                            pltpu.VMEM((nh, tq, 2 * LANES), F32),
                            pltpu.VMEM((tq, tk), F32), pltpu.VMEM((tq, tk), F32),
                            pltpu.VMEM((tq, tk), BF16), pltpu.VMEM((tq, tk), BF16),
                            pltpu.VMEM((tq, LANES), F32), pltpu.VMEM((tq, LANES), F32)]),
        out_shape=jax.ShapeDtypeStruct((s, nh * LANES), BF16),
        compiler_params=_params(("arbitrary",),
                                blocks=[((nh, tq, LANES), BF16), ((nh, tk, LANES), BF16), ((nh, tk, LANES), BF16),
                                        ((tq, tk), BF16), ((8, nh * LANES), F32), ((tq, nh * LANES), BF16)],
                                scratch=[((tq, tk), F32), ((nh, tq, LANES), F32), ((nh, tq, 2 * LANES), F32),
                                         ((tq, tk), F32), ((tq, tk), F32), ((tq, tk), BF16), ((tq, tk), BF16),
                                         ((tq, LANES), F32), ((tq, LANES), F32)],
                                temps=[((tq, tk), F32), ((tq, tk), F32)]),
        name="attention",
    )(qi_tab, kb_tab, main, main, main, bias, g.reshape(1, nh * LANES))


def _gla_kernel(q_ref, k_ref, v_ref, gg_ref, sm_ref, w2_ref, bgk_ref, gn_ref, o_ref, st_scr, *,
                rows, chunk, nheads, dk, dv):
    @pl.when(pl.program_id(0) == 0)
    def _():
        st_scr[...] = jnp.zeros(st_scr.shape, F32)

    r_i = lax.broadcasted_iota(jnp.int32, (chunk, chunk), 0)
    c_i = lax.broadcasted_iota(jnp.int32, (chunk, chunk), 1)
    tri = jnp.where(r_i >= c_i, 1.0, 0.0).astype(BF16)
    nbk, nbv = dk // LANES, dv // LANES
    assert chunk >= 16 and chunk & (chunk - 1) == 0
    leaf = 2
    half_sizes = [chunk >> i for i in range(1, chunk.bit_length()) if chunk >> i >= leaf]
    quadrant = {m: (r_i // (2 * m) == c_i // (2 * m)) & (r_i % (2 * m) >= m) & (c_i % (2 * m) < m)
                for m in half_sizes}
    near = [(c_i == r_i - j) & (r_i % leaf >= j) for j in range(leaf)]
    sub = lax.broadcasted_iota(jnp.int32, (chunk // 8, 8, dk), 1)

    def reference_rows(b, m):
        if m >= 8:
            return jnp.concatenate(
                [jnp.broadcast_to(b[blk + m:blk + m + 1, :], (2 * m, dk)) for blk in range(0, chunk, 2 * m)],
                axis=0)
        b3 = b.reshape(chunk // 8, 8, dk)
        out = None
        for blk in range(0, 8, 2 * m):
            rows = jnp.broadcast_to(b3[:, blk + m:blk + m + 1, :], b3.shape)
            out = rows if out is None else jnp.where(sub >= blk, rows, out)
        return out.reshape(chunk, dk)

    def wide(ref, first, n, r0):
        return jnp.concatenate([ref[first + j, pl.ds(r0, chunk), :] for j in range(n)], axis=1)

    def step(c, carry):
        r0 = pl.multiple_of(c * chunk, chunk)
        z_all = _dot(sm_ref[pl.ds(r0, chunk), :].astype(BF16), w2_ref[...]) + bgk_ref[...]
        for h in range(nheads):
            q = wide(q_ref, h * nbk, nbk, r0).astype(F32) * (dk ** -0.5)
            k = wide(k_ref, h * nbk, nbk, r0).astype(F32)
            v = wide(v_ref, h * nbv, nbv, r0)
            z = z_all[:, h * dk:(h + 1) * dk]
            g = (jnp.minimum(z, 0.0) - jnp.log1p(jnp.exp(-jnp.abs(z)))) * (LOG2E / G_TAU)
            g_hi = g.astype(BF16)
            rem = g - g_hi.astype(F32)
            g_mid = rem.astype(BF16)
            g_lo = (rem - g_mid.astype(F32)).astype(BF16)
            b = _dot(tri, g_hi) + _dot(tri, g_mid) + _dot(tri, g_lo)
            b_last = b[chunk - 1:chunk, :]
            att = jnp.zeros((chunk, chunk), F32)
            for m in half_sizes:
                e = jnp.exp2(-jnp.abs(b - reference_rows(b, m)))
                att = att + jnp.where(quadrant[m], _dot_nt((q * e).astype(BF16), (k * e).astype(BF16)), 0.0)
            for j in range(leaf):
                kj = k if j == 0 else pltpu.roll(k, j, 0)
                bj = b if j == 0 else pltpu.roll(b, j, 0)
                pair = jnp.sum(q * kj * jnp.exp2(jnp.minimum(b - bj, 0.0)), axis=1, keepdims=True)
                att = att + jnp.where(near[j], pair, 0.0)
            st = st_scr[h]
            o = _dot(att.astype(BF16), v) + _dot_nt((q * jnp.exp2(b)).astype(BF16), st.astype(BF16))
            kd = (k * jnp.exp2(b_last - b)).astype(BF16)
            st_scr[h] = st * jnp.exp2(b_last) + _dot_tn(v, kd)
            ms = jnp.mean(o * o, axis=1, keepdims=True)
            gate = wide(gg_ref, h * nbv, nbv, r0).astype(F32)
            out = (o * lax.rsqrt(ms + RMS_EPS)) * gn_ref[...] * (gate * jax.nn.sigmoid(gate))
            o_ref[pl.ds(r0, chunk), h * dv:(h + 1) * dv] = out.astype(o_ref.dtype)
        return carry

    lax.fori_loop(0, rows // chunk, step, 0)


def _gla(main, small, w2, bgk, gn, lay, *, s, rows, chunk, dk, dv):
    nh = G_HEADS
    nqk, nv = nh * dk // LANES, nh * dv // LANES
    q0, k0, v0, g0 = lay["qg"][0], lay["kg"][0], lay["vg"][0], lay["gg"][0]
    assert q0 % nqk == 0 and k0 % nqk == 0 and v0 % nv == 0 and g0 % nv == 0
    return pl.pallas_call(
        functools.partial(_gla_kernel, rows=rows, chunk=chunk, nheads=nh, dk=dk, dv=dv),
        grid=(s // rows,),
        in_specs=[pl.BlockSpec((nqk, rows, LANES), lambda r: (q0 // nqk, r, 0)),
                  pl.BlockSpec((nqk, rows, LANES), lambda r: (k0 // nqk, r, 0)),
                  pl.BlockSpec((nv, rows, LANES), lambda r: (v0 // nv, r, 0)),
                  pl.BlockSpec((nv, rows, LANES), lambda r: (g0 // nv, r, 0)),
                  pl.BlockSpec((rows, LANES), lambda r: (r, 0)),
                  pl.BlockSpec((LANES, nh * dk), lambda r: (0, 0)),
                  pl.BlockSpec((1, nh * dk), lambda r: (0, 0)),
                  pl.BlockSpec((1, dv), lambda r: (0, 0))],
        out_specs=pl.BlockSpec((rows, nh * dv), lambda r: (r, 0)),
        out_shape=jax.ShapeDtypeStruct((s, nh * dv), BF16),
        scratch_shapes=[pltpu.VMEM((nh, dv, dk), F32)],
        compiler_params=_params(("arbitrary",),
                                blocks=[((nqk, rows, LANES), BF16), ((nqk, rows, LANES), BF16),
                                        ((nv, rows, LANES), BF16), ((nv, rows, LANES), BF16),
                                        ((rows, LANES), F32), ((LANES, nh * dk), BF16), ((8, nh * dk), F32),
                                        ((8, dv), F32), ((rows, nh * dv), BF16)],
                                scratch=[((nh, dv, dk), F32)],
                                temps=[((chunk, nh * dk), F32)] * 8 + [((dv, dk), F32)] * 2),
        name="gla",
    )(main, main, main, main, small, w2, bgk, gn.reshape(1, dv))


def _ffn_up_kernel(h_ref, wg_ref, wv_ref, cwg_ref, cwv_ref, cbg_ref, cbv_ref, wd_ref, o_ref, wdb_ref, w_scr,
                   hw_scr, *, tm, tn, bounds):
    wdb_ref[...] = wd_ref[0].astype(BF16)

    @pl.when(pl.program_id(1) == 0)
    def _():
        w_scr[:, :tn] = wg_ref[0].astype(BF16)
        w_scr[:, tn:] = wv_ref[0].astype(BF16)
        hw_scr[0:8, :] = jnp.zeros((8, 2 * tn), F32)

    cw = jnp.concatenate([cwg_ref[0], cwv_ref[0]], axis=1)
    cb = jnp.concatenate([cbg_ref[...], cbv_ref[...]], axis=1)
    w0, w1, w2 = cw[0:1, :], cw[1:2, :], cw[2:3, :]

    for lo, hi in zip(bounds[:-1], bounds[1:]):
        hw = _dot(h_ref[lo:hi, :], w_scr[...])
        hw_scr[8 + lo:8 + hi, :] = hw
        u = w2 * hw + w1 * hw_scr[7 + lo:7 + hi, :] + w0 * hw_scr[6 + lo:6 + hi, :] + cb
        gate, val = u[:, :tn], u[:, tn:]
        o_ref[lo:hi, :] = (gate * jax.nn.sigmoid(gate) * val).astype(o_ref.dtype)
    hw_scr[0:8, :] = hw_scr[tm:tm + 8, :]


def _ffn_up(hb, w_up, conv_w, conv_b, w_down, *, d_ff, tm, tn):
    s, d = hb.shape
    nj, ni = d_ff // tn, s // tm
    d_out = w_down.shape[2]
    slab = d_ff // (nj * ni)
    assert slab * nj * ni == d_ff and slab % 16 == 0
    return pl.pallas_call(
        functools.partial(_ffn_up_kernel, tm=tm, tn=tn, bounds=(0, tm // 2, tm)),
        grid=(nj, ni),
        in_specs=[pl.BlockSpec((tm, d), lambda j, i: (i, 0)),
                  pl.BlockSpec((1, d, tn), lambda j, i: (0, 0, j)),
                  pl.BlockSpec((1, d, tn), lambda j, i: (0, 0, nj + j)),
                  pl.BlockSpec((1, CONV_W, tn), lambda j, i: (0, 0, j)),
                  pl.BlockSpec((1, CONV_W, tn), lambda j, i: (0, 0, nj + j)),
                  pl.BlockSpec((1, tn), lambda j, i: (0, j)),
                  pl.BlockSpec((1, tn), lambda j, i: (0, nj + j)),
                  pl.BlockSpec((1, slab, d_out), lambda j, i: (0, j * ni + i, 0))],
        out_specs=[pl.BlockSpec((tm, tn), lambda j, i: (i, j)),
                   pl.BlockSpec((slab, d_out), lambda j, i: (j * ni + i, 0))],
        out_shape=[jax.ShapeDtypeStruct((s, d_ff), BF16), jax.ShapeDtypeStruct((d_ff, d_out), BF16)],
        scratch_shapes=[pltpu.VMEM((d, 2 * tn), BF16), pltpu.VMEM((8 + tm, 2 * tn), F32)],
        compiler_params=_params(("parallel", "arbitrary"),
                                blocks=[((tm, d), BF16), ((d, tn), F32), ((d, tn), F32), ((8, tn), F32),
                                        ((8, tn), F32), ((8, tn), F32), ((8, tn), F32), ((tm, tn), BF16),
                                        ((slab, d_out), F32), ((slab, d_out), BF16)],
                                scratch=[((d, 2 * tn), BF16), ((8 + tm, 2 * tn), F32)],
                                temps=[((tm // 2, 2 * tn), F32)] * 3),
        name="ffn_up",
    )(hb, w_up, w_up, conv_w, conv_w, conv_b, conv_b, w_down)


def _layout(d_model):
    a_width = A_HEADS * A_HEAD_DIM
    g_width = d_model - a_width
    g_kwidth = g_width // 2
    names = ("qa", "ka", "va", "qi", "ki", "wi", "qg", "kg", "vg", "glr", "gg")
    sizes = (a_width, a_width, a_width, IDX_HEADS * IDX_DIM, IDX_DIM, IDX_HEADS,
             g_kwidth, g_kwidth, g_width, G_LOWRANK, g_width)
    offs = np.concatenate([[0], np.cumsum(sizes)])
    src = {n: (int(offs[i]), int(offs[i + 1])) for i, n in enumerate(names)}
    order = ("qi", "qa", "ka", "va", "qg", "kg", "vg", "gg", "ki")
    lay, blk = {}, 0
    for n in order:
        width = src[n][1] - src[n][0]
        assert width % LANES == 0
        lay[n] = (blk, width // LANES)
        blk += width // LANES
    return src, order, lay, blk


def kernel(x, w_in, w_gk2, b_gk, attn_out_g, gla_norm_g, w_o, ln1_g, ln1_b, w_up, conv_w, conv_b,
           w_down, ln2_g, ln2_b):
    assert x.shape[0] == 1 and w_in.shape[0] == DEPTH == 1
    _, s, d = x.shape
    x2 = x[0]
    src, order, lay, nb_main = _layout(d)
    g_width = d - A_HEADS * A_HEAD_DIM
    dv = g_width // G_HEADS
    dk = dv // 2
    d_ff = w_down.shape[1]
    topk = min(TOPK_MAX, s // 4)

    w_proj = _relayout_w_in(jnp.swapaxes(w_in, 1, 2), src, order)
    n_small = IDX_HEADS + G_LOWRANK
    w2 = jnp.zeros((LANES, G_HEADS * dk), F32).at[IDX_HEADS:n_small].set(w_gk2[0]).astype(BF16)
    bgk = b_gk[0].reshape(1, G_HEADS * dk)
    n_main = (nb_main + 1) * LANES
    q_lo, q_hi = lay["qa"][0] * LANES, (lay["qa"][0] + lay["qa"][1]) * LANES
    col = np.ones((1, n_main), np.float32)
    col[:, q_lo:q_hi] = A_HEAD_DIM ** -0.5 * LOG2E
    col_scale = jnp.asarray(col)

    tm = _tile(s, 1024)
    main, small = _proj_blocks(x2, w_proj, col_scale, tm=_tile(s, 512), tn=_tile(n_main, 1280))
    bias = _indexer(main, small, lay, s=s, tq=256, tk=256, topk=topk)
    ta = _tile(s, 512)
    o_a = _attention(main, bias, attn_out_g[0], lay, s=s, tq=ta, tk=ta)
    o_g = _gla(main, small, w2, bgk, gla_norm_g[0], lay, s=s, rows=_tile(s, 512), chunk=256, dk=dk, dv=dv)
    y1 = _w_o(o_a, o_g, w_o, x2, tm=tm, tn=_tile(d, 512), alpha=DN_ALPHA)
    h, hb = _layer_norm(y1, ln1_g[0], ln1_b[0], tr=_tile(s, 256, 8), also_bf16=True)

    act, wd = _ffn_up(hb, w_up, conv_w, conv_b[0].reshape(1, 2 * d_ff), w_down, d_ff=d_ff, tm=tm,
                      tn=_tile(d_ff, 256))
    y2 = _matmul(act, wd, tm=_tile(s, 512), tn=_tile(d, 512), tk=d_ff, out_dtype=F32, residual=h,
                 alpha=DN_ALPHA, name="w_down")
    (out,) = _layer_norm(y2, ln2_g[0], ln2_b[0], tr=_tile(s, 256, 8), also_bf16=False)
    return out[None]
```

```python
import functools

import numpy as np
import jax
import jax.numpy as jnp
from jax import lax
from jax.experimental import pallas as pl
from jax.experimental.pallas import tpu as pltpu

A_HEADS = 16
A_HEAD_DIM = 128
IDX_HEADS = 32
IDX_DIM = 128
TOPK_MAX = 256
G_HEADS = 4
G_LOWRANK = 16
G_TAU = 16.0
CONV_W = 3
LN_EPS = 1e-5
RMS_EPS = 1e-6
DEPTH = 1
DN_ALPHA = (2 * DEPTH) ** 0.25

LANES = 128
MIB = 1024 * 1024
VMEM_BUDGET = 56 * MIB

BF16 = jnp.bfloat16
F32 = jnp.float32
NEG_BIG = -1e30
LOG2E = 1.4426950408889634
INT_MIN = -2 ** 31
COUNT_UNROLL = (8, 2, 1)


def _nbytes(shape, dtype):
    n = jnp.dtype(dtype).itemsize
    for dim in shape:
        n *= 1 if dim is None else getattr(dim, "block_size", dim)
    return n


def _params(sem, blocks, scratch=(), temps=()):
    need = (2 * sum(_nbytes(*b) for b in blocks) + sum(_nbytes(*b) for b in scratch)
            + sum(_nbytes(*b) for b in temps))
    limit = -(-need // MIB) * MIB
    assert limit <= VMEM_BUDGET, (limit, VMEM_BUDGET)
    return pltpu.CompilerParams(dimension_semantics=sem, vmem_limit_bytes=limit)


def _tile(n, max_tile, quantum=LANES):
    best = None
    for t in range(quantum, min(n, max_tile) + 1, quantum):
        if n % t == 0:
            best = t
    assert best is not None, (n, max_tile, quantum)
    return best


def _dot(a, b):
    return jnp.dot(a, b, preferred_element_type=F32)


def _dot_nt(a, b):
    return lax.dot_general(a, b, (((1,), (1,)), ((), ())), preferred_element_type=F32)


def _dot_tn(a, b):
    return lax.dot_general(a, b, (((0,), (0,)), ((), ())), preferred_element_type=F32)


def _relayout_kernel(off_ref, kind_ref, a_ref, b_ref, o_ref, *, n_wi, n_small):
    del off_ref
    kind = kind_ref[pl.program_id(0)]

    @pl.when(kind == 0)
    def _():
        o_ref[...] = a_ref[0].astype(BF16)

    @pl.when(kind == 1)
    def _():
        o_ref[...] = jnp.zeros(o_ref.shape, BF16)

    @pl.when(kind == 2)
    def _():
        r = lax.broadcasted_iota(jnp.int32, o_ref.shape, 0)
        o_ref[...] = jnp.where(r < n_wi, a_ref[0], jnp.where(r < n_small, b_ref[0], 0.0)).astype(BF16)


def _relayout_w_in(w_t, src, order):
    _, n_in, d = w_t.shape
    starts = [src[n][0] + t * LANES for n in order for t in range((src[n][1] - src[n][0]) // LANES)]
    wi0, glr0 = src["wi"][0], src["glr"][0]
    n_wi = src["wi"][1] - wi0
    n_small = n_wi + src["glr"][1] - glr0
    glr_win = glr0 - n_wi
    assert n_small <= LANES and glr_win >= 0 and wi0 + LANES <= n_in and glr_win + LANES <= n_in
    assert all(o % 8 == 0 for o in starts + [wi0, glr_win])
    row_off8 = [o // 8 for o in starts + [0, wi0]]
    kind = [0] * len(starts) + [1, 2]
    nb = len(kind)
    tables = [jnp.asarray(np.asarray(t, np.int32)) for t in (row_off8, kind)]
    return pl.pallas_call(
        functools.partial(_relayout_kernel, n_wi=n_wi, n_small=n_small),
        grid_spec=pltpu.PrefetchScalarGridSpec(
            num_scalar_prefetch=2,
            grid=(nb,),
            in_specs=[pl.BlockSpec((pl.Element(1), pl.Element(LANES), pl.Element(d)),
                                   lambda b, off, kd: (0, off[b] * 8, 0)),
                      pl.BlockSpec((pl.Element(1), pl.Element(LANES), pl.Element(d)),
                                   lambda b, off, kd: (0, glr_win, 0))],
            out_specs=pl.BlockSpec((LANES, d), lambda b, off, kd: (b, 0))),
        out_shape=jax.ShapeDtypeStruct((nb * LANES, d), BF16),
        compiler_params=_params(("arbitrary",),
                                blocks=[((LANES, d), F32), ((LANES, d), F32), ((LANES, d), BF16)],
                                temps=[((LANES, d), F32)]),
        name="relayout_w_in",
    )(*tables, w_t, w_t)


def _proj_kernel(x_ref, w_ref, ws_ref, sc_ref, o_ref, os_ref, xb_scr):
    @pl.when(pl.program_id(1) == 0)
    def _():
        xb_scr[...] = x_ref[...].astype(BF16)
        os_ref[...] = _dot_nt(xb_scr[...], ws_ref[...])

    res = _dot_nt(xb_scr[...], w_ref[...]) * sc_ref[...]
    for c in range(o_ref.shape[0]):
        o_ref[c] = res[:, c * LANES:(c + 1) * LANES].astype(o_ref.dtype)


def _proj_blocks(x, wt, col_scale, *, tm, tn):
    s, k = x.shape
    n = wt.shape[0] - LANES
    return pl.pallas_call(
        _proj_kernel,
        grid=(s // tm, n // tn),
        in_specs=[pl.BlockSpec((tm, k), lambda i, j: (i, 0)),
                  pl.BlockSpec((tn, k), lambda i, j: (j, 0)),
                  pl.BlockSpec((LANES, k), lambda i, j: (n // LANES, 0)),
                  pl.BlockSpec((1, tn), lambda i, j: (0, j))],
        out_specs=[pl.BlockSpec((tn // LANES, tm, LANES), lambda i, j: (j, i, 0)),
                   pl.BlockSpec((tm, LANES), lambda i, j: (i, 0))],
        out_shape=[jax.ShapeDtypeStruct((n // LANES, s, LANES), BF16),
                   jax.ShapeDtypeStruct((s, LANES), F32)],
        scratch_shapes=[pltpu.VMEM((tm, k), BF16)],
        compiler_params=_params(("parallel", "arbitrary"),
                                blocks=[((tm, k), F32), ((tn, k), BF16), ((LANES, k), BF16), ((8, tn), F32),
                                        ((tm, tn), BF16), ((tm, LANES), F32)],
                                scratch=[((tm, k), BF16)],
                                temps=[((tm, tn), F32), ((tm, tn), F32)]),
        name="proj",
    )(x, wt, wt, col_scale)


def _mm_kernel(a_ref, b_ref, *rest, alpha, has_res, nk):
    if has_res:
        r_ref, o_ref = rest[0], rest[1]
        rest = rest[2:]
    else:
        r_ref, o_ref = None, rest[0]
        rest = rest[1:]

    def finish(acc):
        if has_res:
            acc = acc + alpha * r_ref[...]
        o_ref[...] = acc.astype(o_ref.dtype)

    if nk == 1:
        finish(_dot(a_ref[...], b_ref[...]))
        return
    acc_ref = rest[0]
    kk = pl.program_id(2)

    @pl.when(kk == 0)
    def _():
        acc_ref[...] = _dot(a_ref[...], b_ref[...])

    @pl.when(kk > 0)
    def _():
        acc_ref[...] += _dot(a_ref[...], b_ref[...])

    @pl.when(kk == nk - 1)
    def _():
        finish(acc_ref[...])


def _matmul(a, b, *, tm, tn, tk, out_dtype, residual=None, alpha=1.0, name="matmul"):
    m, k = a.shape
    n = b.shape[1]
    nk = k // tk
    in_specs = [pl.BlockSpec((tm, tk), lambda i, j, kk: (i, kk)),
                pl.BlockSpec((tk, tn), lambda i, j, kk: (kk, j))]
    args = [a, b]
    if residual is not None:
        in_specs.append(pl.BlockSpec((tm, tn), lambda i, j, kk: (i, j)))
        args.append(residual)
    scratch = [pltpu.VMEM((tm, tn), F32)] if nk > 1 else []
    return pl.pallas_call(
        functools.partial(_mm_kernel, alpha=alpha, has_res=residual is not None, nk=nk),
        grid=(m // tm, n // tn, nk),
        in_specs=in_specs,
        out_specs=pl.BlockSpec((tm, tn), lambda i, j, kk: (i, j)),
        out_shape=jax.ShapeDtypeStruct((m, n), out_dtype),
        scratch_shapes=scratch,
        compiler_params=_params(("parallel", "parallel", "arbitrary"),
                                blocks=[((tm, tk), a.dtype), ((tk, tn), b.dtype), ((tm, tn), out_dtype)]
                                + ([((tm, tn), F32)] if residual is not None else []),
                                scratch=[((tm, tn), F32)] if nk > 1 else [],
                                temps=[((tm, tn), F32)]),
        name=name,
    )(*args)


def _wo_ln_kernel(oa_ref, og_ref, wt_ref, wb_ref, x_ref, g_ref, b_ref, h_ref, hb_ref, *, nj, tn, alpha):
    j = pl.program_id(1)

    for c in range(nj):
        @pl.when(j == c)
        def _(c=c):
            h_ref[:, c * tn:(c + 1) * tn] = (_dot(oa_ref[...], wt_ref[...]) + _dot(og_ref[...], wb_ref[...])
                                             + alpha * x_ref[...])

    @pl.when(j == nj)
    def _():
        d = nj * tn
        cols = [slice(c * tn, (c + 1) * tn) for c in range(nj)]
        total = h_ref[:, cols[0]].sum(axis=1, keepdims=True)
        for c in range(1, nj):
            total = total + h_ref[:, cols[c]].sum(axis=1, keepdims=True)
        mu = total * (1.0 / d)
        sq = jnp.zeros_like(mu)
        for c in range(nj):
            yc = h_ref[:, cols[c]] - mu
            sq = sq + (yc * yc).sum(axis=1, keepdims=True)
        r = lax.rsqrt(sq * (1.0 / d) + LN_EPS)
        for c in range(nj):
            out = (h_ref[:, cols[c]] - mu) * r * g_ref[:, cols[c]] + b_ref[:, cols[c]]
            h_ref[:, cols[c]] = out
            hb_ref[:, cols[c]] = out.astype(BF16)


def _w_o_ln(o_a, o_g, wo_b, x, g, b, *, tm, tn, alpha):
    s, ka = o_a.shape
    kg = o_g.shape[1]
    d = wo_b.shape[1]
    nj = d // tn
    assert ka % kg == 0

    def col(j):
        return jnp.minimum(j, nj - 1)

    return pl.pallas_call(
        functools.partial(_wo_ln_kernel, nj=nj, tn=tn, alpha=alpha),
        grid=(s // tm, nj + 1),
        in_specs=[pl.BlockSpec((tm, ka), lambda i, j: (i, 0)),
                  pl.BlockSpec((tm, kg), lambda i, j: (i, 0)),
                  pl.BlockSpec((ka, tn), lambda i, j: (0, col(j))),
                  pl.BlockSpec((kg, tn), lambda i, j: (ka // kg, col(j))),
                  pl.BlockSpec((tm, tn), lambda i, j: (i, col(j))),
                  pl.BlockSpec((1, d), lambda i, j: (0, 0)),
                  pl.BlockSpec((1, d), lambda i, j: (0, 0))],
        out_specs=[pl.BlockSpec((tm, d), lambda i, j: (i, 0)),
                   pl.BlockSpec((tm, d), lambda i, j: (i, 0))],
        out_shape=[jax.ShapeDtypeStruct((s, d), F32), jax.ShapeDtypeStruct((s, d), BF16)],
        compiler_params=_params(("parallel", "arbitrary"),
                                blocks=[((tm, ka), BF16), ((tm, kg), BF16), ((ka, tn), BF16), ((kg, tn), BF16),
                                        ((tm, tn), F32), ((8, d), F32), ((8, d), F32), ((tm, d), F32),
                                        ((tm, d), BF16)],
                                temps=[((tm, tn), F32)] * 6),
        name="w_o_ln",
    )(o_a, o_g, wo_b, wo_b, x, g.reshape(1, d), b.reshape(1, d))


def _ln_kernel(y_ref, g_ref, b_ref, *o_refs):
    y = y_ref[...]
    mu = jnp.mean(y, axis=1, keepdims=True)
    yc = y - mu
    var = jnp.mean(yc * yc, axis=1, keepdims=True)
    out = yc * lax.rsqrt(var + LN_EPS) * g_ref[...] + b_ref[...]
    for o_ref in o_refs:
        o_ref[...] = out.astype(o_ref.dtype)


def _layer_norm(y, g, b, *, tr, also_bf16):
    s, d = y.shape
    out_shape = [jax.ShapeDtypeStruct((s, d), F32)]
    out_specs = [pl.BlockSpec((tr, d), lambda i: (i, 0))]
    if also_bf16:
        out_shape.append(jax.ShapeDtypeStruct((s, d), BF16))
        out_specs.append(pl.BlockSpec((tr, d), lambda i: (i, 0)))
    return pl.pallas_call(
        _ln_kernel,
        grid=(s // tr,),
        in_specs=[pl.BlockSpec((tr, d), lambda i: (i, 0)),
                  pl.BlockSpec((1, d), lambda i: (0, 0)),
                  pl.BlockSpec((1, d), lambda i: (0, 0))],
        out_specs=out_specs,
        out_shape=out_shape,
        compiler_params=_params(("parallel",),
                                blocks=[((tr, d), F32), ((8, d), F32), ((8, d), F32)]
                                + [((tr, d), o.dtype) for o in out_shape],
                                temps=[((tr, d), F32), ((tr, d), F32)]),
        name="layer_norm",
    )(y, g.reshape(1, d), b.reshape(1, d))


def _idx_kernel(qi_ref, ki_ref, sm_ref, bias_ref, hi_scr, lo_scr, wb_scr, *, tq, tk, nkt, hi, group,
                topk, wscale):
    qb = pl.program_id(0)
    nk = ((qb + 1) * tq + tk - 1) // tk
    reps = tk // LANES
    i16 = jnp.int16
    low16 = -2 ** 15

    for h in range(hi):
        wb_scr[h] = jnp.broadcast_to(sm_ref[:, h:h + 1] * wscale, (tq, LANES))

    row = qb * tq + lax.broadcasted_iota(jnp.int32, (group, tk), 0)
    col = lax.broadcasted_iota(jnp.int32, (group, tk), 1)

    def lanes(a):
        return jnp.concatenate([a] * reps, axis=1) if reps > 1 else a

    def fold(a):
        out = a[:, :LANES]
        for r in range(1, reps):
            out = out + a[:, r * LANES:(r + 1) * LANES]
        return out

    def score_tile(kb, carry):
        kt = ki_ref[0, pl.ds(pl.multiple_of(kb * tk, tk), tk), :]
        for r in range(tq // group):
            rows = slice(r * group, (r + 1) * group)
            qs = qi_ref[:, rows, :].reshape(hi * group, IDX_DIM)
            d = _dot_nt(qs, kt)
            acc = jnp.zeros((group, tk), F32)
            for h in range(hi):
                acc = acc + jnp.maximum(d[h * group:(h + 1) * group], 0.0) * lanes(wb_scr[h, rows, :])
            bits = pltpu.bitcast(acc + 0.0, jnp.int32)
            key = jnp.where(kb * tk + col <= row + r * group,
                            bits ^ ((bits >> 31) & jnp.int32(0x7FFFFFFF)), INT_MIN)
            hi_scr[kb, rows, :] = (key >> 16).astype(i16)
            lo_scr[kb, rows, :] = ((key & 0xFFFF) + low16).astype(i16)
        return carry

    lax.fori_loop(0, nk, score_tile, 0)

    def count_ge(ref, cand):
        cand_t = lanes(cand.astype(i16))

        def step(kb, c):
            return c + fold(jnp.where(ref[kb] >= cand_t, i16(1), i16(0)))

        c = jnp.zeros((tq, LANES), i16)
        done = 0
        for width in COUNT_UNROLL:
            def step_many(q, c, width=width, done=done):
                for u in range(width):
                    c = step(done + q * width + u, c)
                return c

            n_groups = (nk - done) // width
            c = lax.fori_loop(0, n_groups, step_many, c)
            done = done + n_groups * width
        return jnp.sum(c.astype(F32), axis=1, keepdims=True)

    kf = float(topk)

    def bit_search(ref, base_cnt, cnt_start, stop_when_exact):
        def body(state):
            it, v, cnt_v = state
            cand = v + jnp.left_shift(jnp.int32(1), 15 - it)
            cnt = base_cnt + count_ge(ref, cand)
            ok = cnt >= kf
            return it + 1, jnp.where(ok, cand, v), jnp.where(ok, cnt, cnt_v)

        def unsettled(state):
            go = state[0] < 16
            if stop_when_exact:
                go = go & (jnp.max(jnp.where(state[2] == kf, 0.0, 1.0)) > 0.0)
            return go

        v0 = jnp.full((tq, LANES), low16, jnp.int32)
        _, v, cnt_v = lax.while_loop(unsettled, body, (jnp.int32(0), v0, cnt_start))
        return v, cnt_v

    n_all = jnp.zeros((tq, 1), F32) + (nk * tk).astype(F32)
    v_hi, n_from_hi = bit_search(hi_scr, 0.0, n_all, False)
    n_above = count_ge(hi_scr, v_hi + 1)
    v_hi_t = lanes(v_hi.astype(i16))

    def keep_equal(kb, carry):
        lo_scr[kb] = jnp.where(hi_scr[kb] == v_hi_t, lo_scr[kb], i16(low16))
        return carry

    lax.fori_loop(0, nk, keep_equal, 0)
    v_lo, n_sel = bit_search(lo_scr, n_above, n_from_hi, True)
    v_lo = jnp.where(v_hi == low16, jnp.maximum(v_lo, low16 + 1), v_lo)
    v_lo_t = lanes(v_lo.astype(i16))
    zero = jnp.zeros((tq, tk), bias_ref.dtype)
    neg = jnp.full((tq, tk), NEG_BIG, bias_ref.dtype)
    has_ties = jnp.max(jnp.where(n_sel > kf, 1.0, 0.0)) > 0.0

    @pl.when(jnp.logical_not(has_ties))
    def _():
        def emit(kb, carry):
            h16 = hi_scr[kb]
            sel = (h16 > v_hi_t) | ((h16 == v_hi_t) & (lo_scr[kb] >= v_lo_t))
            bias_ref[0, kb] = jnp.where(sel, zero, neg)
            return carry

        lax.fori_loop(0, nk, emit, 0)

    @pl.when(has_ties)
    def _():
        top16 = 2 ** 15 - 1
        n_gt = n_above + jnp.where(v_lo[:, :1] >= top16, 0.0, count_ge(lo_scr, jnp.minimum(v_lo + 1, top16)))
        need = kf - n_gt
        earlier = jnp.where(lax.broadcasted_iota(jnp.int32, (tk, tk), 0)
                            < lax.broadcasted_iota(jnp.int32, (tk, tk), 1), 1.0, 0.0).astype(BF16)
        one_b = jnp.ones((tq, tk), BF16)
        zero_b = jnp.zeros((tq, tk), BF16)

        def emit_ties(kb, seen):
            h16 = hi_scr[kb]
            l16 = lo_scr[kb]
            same_hi = h16 == v_hi_t
            larger = (h16 > v_hi_t) | (same_hi & (l16 > v_lo_t))
            tied = jnp.where(same_hi & (l16 == v_lo_t), one_b, zero_b)
            rank = seen + _dot(tied, earlier)
            keep = jnp.where(rank < need, 1.0, 0.0).astype(BF16)
            sel = larger | ((tied > 0.0) & (keep > 0.0))
            bias_ref[0, kb] = jnp.where(sel, zero, neg)
            return seen + jnp.sum(tied.astype(F32), axis=1, keepdims=True)

        lax.fori_loop(0, nk, emit_ties, jnp.zeros((tq, 1), F32))

    def fill(kb, carry):
        bias_ref[0, kb] = neg
        return carry

    lax.fori_loop(nk, nkt, fill, 0)


def _indexer(main, small, lay, *, s, tq, tk, topk):
    nkt = s // tk
    hi = IDX_HEADS
    group = min(tq, 64)
    qi0, ki0 = lay["qi"][0], lay["ki"][0]
    assert qi0 % hi == 0
    return pl.pallas_call(
        functools.partial(_idx_kernel, tq=tq, tk=tk, nkt=nkt, hi=hi, group=group, topk=topk,
                          wscale=IDX_HEADS ** -0.5 * IDX_DIM ** -0.5),
        grid=(s // tq,),
        in_specs=[pl.BlockSpec((hi, tq, LANES), lambda i: (qi0 // hi, i, 0)),
                  pl.BlockSpec((1, s, LANES), lambda i: (ki0, 0, 0)),
                  pl.BlockSpec((tq, LANES), lambda i: (i, 0))],
        out_specs=pl.BlockSpec((1, nkt, tq, tk), lambda i: (i, 0, 0, 0)),
        out_shape=jax.ShapeDtypeStruct((s // tq, nkt, tq, tk), BF16),
        scratch_shapes=[pltpu.VMEM((nkt, tq, tk), jnp.int16),
                        pltpu.VMEM((nkt, tq, tk), jnp.int16),
                        pltpu.VMEM((hi, tq, LANES), F32)],
        compiler_params=_params(("parallel",),
                                blocks=[((hi, tq, LANES), BF16), ((s, LANES), BF16), ((tq, LANES), F32),
                                        ((nkt, tq, tk), BF16)],
                                scratch=[((nkt, tq, tk), jnp.int16), ((nkt, tq, tk), jnp.int16),
                                         ((hi, tq, LANES), F32)],
                                temps=[((hi * group, tk), F32), ((hi * group, tk), F32)]),
        name="indexer",
    )(main, main, small)


def _attn_kernel(qi_ref, kb_ref, q_ref, k_ref, v_ref, b_ref, g_ref, o_ref, bias_scr, m_scr, acc_scr, *pipe,
                 nh, tq, tk):
    step = pl.program_id(0)
    qi = qi_ref[step]
    kb = kb_ref[step]

    @pl.when(kb == 0)
    def _():
        m_scr[...] = jnp.full(m_scr.shape, NEG_BIG, F32)
        acc_scr[...] = jnp.zeros(acc_scr.shape, F32)

    na, nb, sq, sk = b_ref.shape
    for a in range(na):
        for b in range(nb):
            bias_scr[a * sq:(a + 1) * sq, b * sk:(b + 1) * sk] = b_ref[a, b].astype(F32)

    ones = jnp.ones((tk, LANES), BF16)

    def qk(h, s_ref):
        s_ref[...] = _dot_nt(q_ref[h], k_ref[h])

    def soft(h, s_ref, p_ref, al_ref):
        s = s_ref[...] + bias_scr[...]
        m_prev = m_scr[h]
        m_new = jnp.maximum(m_prev, jnp.max(s, axis=1, keepdims=True))
        m_scr[h] = m_new
        p_ref[...] = jnp.exp2(s - jnp.concatenate([m_new] * (tk // LANES), axis=1)).astype(BF16)
        al_ref[...] = jnp.exp2(m_prev - m_new)

    def pv(h, p_ref, al_ref):
        al = al_ref[...]
        v_ext = jnp.concatenate([v_ref[h], ones], axis=1)
        acc_scr[h] = acc_scr[h] * jnp.concatenate([al, al], axis=1) + _dot(p_ref[...], v_ext)

    s_a, s_b, p_a, p_b, al_a, al_b = pipe
    qk(0, s_a)
    qk(1, s_b)
    soft(0, s_a, p_a, al_a)

    def pair(j, carry):
        qk(2 * j, s_a)
        soft(2 * j - 1, s_b, p_b, al_b)
        pv(2 * j - 2, p_a, al_a)
        qk(2 * j + 1, s_b)
        soft(2 * j, s_a, p_a, al_a)
        pv(2 * j - 1, p_b, al_b)
        return carry

    lax.fori_loop(1, nh // 2, pair, 0)
    soft(nh - 1, s_b, p_b, al_b)
    pv(nh - 2, p_a, al_a)
    pv(nh - 1, p_b, al_b)

    @pl.when(kb == (qi * tq + tq - 1) // tk)
    def _():
        ss = jnp.zeros((tq, LANES), F32)
        for h in range(nh):
            o_h = acc_scr[h, :, :LANES] / acc_scr[h, :, LANES:]
            acc_scr[h, :, :LANES] = o_h
            ss = ss + o_h * o_h
        ms = jnp.sum(ss, axis=1, keepdims=True) * (1.0 / (nh * LANES))
        r = lax.rsqrt(ms + RMS_EPS)
        for h in range(nh):
            sl = slice(h * LANES, (h + 1) * LANES)
            o_ref[:, sl] = (acc_scr[h, :, :LANES] * r * g_ref[:, sl]).astype(o_ref.dtype)


def _attention(main, bias, g, lay, *, s, tq, tk):
    nh = A_HEADS
    nqb, nkt, sq, sk = bias.shape
    q0, k0, v0 = lay["qa"][0], lay["ka"][0], lay["va"][0]
    assert q0 % nh == 0 and k0 % nh == 0 and v0 % nh == 0
    pairs = [(i, j) for i in range(s // tq) for j in range((i * tq + tq - 1) // tk + 1)]
    qi_tab = jnp.asarray(np.asarray([p[0] for p in pairs], np.int32))
    kb_tab = jnp.asarray(np.asarray([p[1] for p in pairs], np.int32))
    return pl.pallas_call(
        functools.partial(_attn_kernel, nh=nh, tq=tq, tk=tk),
        grid_spec=pltpu.PrefetchScalarGridSpec(
            num_scalar_prefetch=2,
            grid=(len(pairs),),
            in_specs=[pl.BlockSpec((nh, tq, LANES), lambda t, qi, kb: (q0 // nh, qi[t], 0)),
                      pl.BlockSpec((nh, tk, LANES), lambda t, qi, kb: (k0 // nh, kb[t], 0)),
                      pl.BlockSpec((nh, tk, LANES), lambda t, qi, kb: (v0 // nh, kb[t], 0)),
                      pl.BlockSpec((tq // sq, tk // sk, sq, sk), lambda t, qi, kb: (qi[t], kb[t], 0, 0)),
                      pl.BlockSpec((1, nh * LANES), lambda t, qi, kb: (0, 0))],
            out_specs=pl.BlockSpec((tq, nh * LANES), lambda t, qi, kb: (qi[t], 0)),
            scratch_shapes=[pltpu.VMEM((tq, tk), F32),
                            pltpu.VMEM((nh, tq, LANES), F32),
                            pltpu.VMEM((nh, tq, 2 * LANES), F32),
                            pltpu.VMEM((tq, tk), F32), pltpu.VMEM((tq, tk), F32),
                            pltpu.VMEM((tq, tk), BF16), pltpu.VMEM((tq, tk), BF16),
                            pltpu.VMEM((tq, LANES), F32), pltpu.VMEM((tq, LANES), F32)]),
        out_shape=jax.ShapeDtypeStruct((s, nh * LANES), BF16),
        compiler_params=_params(("arbitrary",),
                                blocks=[((nh, tq, LANES), BF16), ((nh, tk, LANES), BF16), ((nh, tk, LANES), BF16),
                                        ((tq, tk), BF16), ((8, nh * LANES), F32), ((tq, nh * LANES), BF16)],
                                scratch=[((tq, tk), F32), ((nh, tq, LANES), F32), ((nh, tq, 2 * LANES), F32),
                                         ((tq, tk), F32), ((tq, tk), F32), ((tq, tk), BF16), ((tq, tk), BF16),
                                         ((tq, LANES), F32), ((tq, LANES), F32)],
                                temps=[((tq, tk), F32), ((tq, tk), F32)]),
        name="attention",
    )(qi_tab, kb_tab, main, main, main, bias, g.reshape(1, nh * LANES))


def _gla_kernel(q_ref, k_ref, v_ref, gg_ref, sm_ref, w2_ref, bgk_ref, gn_ref, o_ref, st_scr, *,
                rows, chunk, nheads, dk, dv):
    @pl.when(pl.program_id(0) == 0)
    def _():
        st_scr[...] = jnp.zeros(st_scr.shape, F32)

    r_i = lax.broadcasted_iota(jnp.int32, (chunk, chunk), 0)
    c_i = lax.broadcasted_iota(jnp.int32, (chunk, chunk), 1)
    tri = jnp.where(r_i >= c_i, 1.0, 0.0).astype(BF16)
    nbk, nbv = dk // LANES, dv // LANES
    assert chunk >= 16 and chunk & (chunk - 1) == 0
    leaf = 2
    half_sizes = [chunk >> i for i in range(1, chunk.bit_length()) if chunk >> i >= leaf]
    quadrant = {m: (r_i // (2 * m) == c_i // (2 * m)) & (r_i % (2 * m) >= m) & (c_i % (2 * m) < m)
                for m in half_sizes}
    near = [(c_i == r_i - j) & (r_i % leaf >= j) for j in range(leaf)]
    sub = lax.broadcasted_iota(jnp.int32, (chunk // 8, 8, dk), 1)

    def reference_rows(b, m):
        if m >= 8:
            return jnp.concatenate(
                [jnp.broadcast_to(b[blk + m:blk + m + 1, :], (2 * m, dk)) for blk in range(0, chunk, 2 * m)],
                axis=0)
        b3 = b.reshape(chunk // 8, 8, dk)
        out = None
        for blk in range(0, 8, 2 * m):
            rows = jnp.broadcast_to(b3[:, blk + m:blk + m + 1, :], b3.shape)
            out = rows if out is None else jnp.where(sub >= blk, rows, out)
        return out.reshape(chunk, dk)

    def wide(ref, first, n, r0):
        return jnp.concatenate([ref[first + j, pl.ds(r0, chunk), :] for j in range(n)], axis=1)

    def step(c, carry):
        r0 = pl.multiple_of(c * chunk, chunk)
        z_all = _dot(sm_ref[pl.ds(r0, chunk), :].astype(BF16), w2_ref[...]) + bgk_ref[...]
        for h in range(nheads):
            q = wide(q_ref, h * nbk, nbk, r0).astype(F32) * (dk ** -0.5)
            k = wide(k_ref, h * nbk, nbk, r0).astype(F32)
            v = wide(v_ref, h * nbv, nbv, r0)
            z = z_all[:, h * dk:(h + 1) * dk]
            g = (jnp.minimum(z, 0.0) - jnp.log1p(jnp.exp(-jnp.abs(z)))) * (LOG2E / G_TAU)
            g_hi = g.astype(BF16)
            rem = g - g_hi.astype(F32)
            g_mid = rem.astype(BF16)
            g_lo = (rem - g_mid.astype(F32)).astype(BF16)
            b = _dot(tri, g_hi) + _dot(tri, g_mid) + _dot(tri, g_lo)
            b_last = b[chunk - 1:chunk, :]
            att = jnp.zeros((chunk, chunk), F32)
            for m in half_sizes:
                e = jnp.exp2(-jnp.abs(b - reference_rows(b, m)))
                att = att + jnp.where(quadrant[m], _dot_nt((q * e).astype(BF16), (k * e).astype(BF16)), 0.0)
            for j in range(leaf):
                kj = k if j == 0 else pltpu.roll(k, j, 0)
                bj = b if j == 0 else pltpu.roll(b, j, 0)
                pair = jnp.sum(q * kj * jnp.exp2(jnp.minimum(b - bj, 0.0)), axis=1, keepdims=True)
                att = att + jnp.where(near[j], pair, 0.0)
            st = st_scr[h]
            o = _dot(att.astype(BF16), v) + _dot_nt((q * jnp.exp2(b)).astype(BF16), st.astype(BF16))
            kd = (k * jnp.exp2(b_last - b)).astype(BF16)
            st_scr[h] = st * jnp.exp2(b_last) + _dot_tn(v, kd)
            ms = jnp.mean(o * o, axis=1, keepdims=True)
            gate = wide(gg_ref, h * nbv, nbv, r0).astype(F32)
            out = (o * lax.rsqrt(ms + RMS_EPS)) * gn_ref[...] * (gate * jax.nn.sigmoid(gate))
            o_ref[pl.ds(r0, chunk), h * dv:(h + 1) * dv] = out.astype(o_ref.dtype)
        return carry

    lax.fori_loop(0, rows // chunk, step, 0)


def _gla(main, small, w2, bgk, gn, lay, *, s, rows, chunk, dk, dv):
    nh = G_HEADS
    nqk, nv = nh * dk // LANES, nh * dv // LANES
    q0, k0, v0, g0 = lay["qg"][0], lay["kg"][0], lay["vg"][0], lay["gg"][0]
    assert q0 % nqk == 0 and k0 % nqk == 0 and v0 % nv == 0 and g0 % nv == 0
    return pl.pallas_call(
        functools.partial(_gla_kernel, rows=rows, chunk=chunk, nheads=nh, dk=dk, dv=dv),
        grid=(s // rows,),
        in_specs=[pl.BlockSpec((nqk, rows, LANES), lambda r: (q0 // nqk, r, 0)),
                  pl.BlockSpec((nqk, rows, LANES), lambda r: (k0 // nqk, r, 0)),
                  pl.BlockSpec((nv, rows, LANES), lambda r: (v0 // nv, r, 0)),
                  pl.BlockSpec((nv, rows, LANES), lambda r: (g0 // nv, r, 0)),
                  pl.BlockSpec((rows, LANES), lambda r: (r, 0)),
                  pl.BlockSpec((LANES, nh * dk), lambda r: (0, 0)),
                  pl.BlockSpec((1, nh * dk), lambda r: (0, 0)),
                  pl.BlockSpec((1, dv), lambda r: (0, 0))],
        out_specs=pl.BlockSpec((rows, nh * dv), lambda r: (r, 0)),
        out_shape=jax.ShapeDtypeStruct((s, nh * dv), BF16),
        scratch_shapes=[pltpu.VMEM((nh, dv, dk), F32)],
        compiler_params=_params(("arbitrary",),
                                blocks=[((nqk, rows, LANES), BF16), ((nqk, rows, LANES), BF16),
                                        ((nv, rows, LANES), BF16), ((nv, rows, LANES), BF16),
                                        ((rows, LANES), F32), ((LANES, nh * dk), BF16), ((8, nh * dk), F32),
                                        ((8, dv), F32), ((rows, nh * dv), BF16)],
                                scratch=[((nh, dv, dk), F32)],
                                temps=[((chunk, nh * dk), F32)] * 8 + [((dv, dk), F32)] * 2),
        name="gla",
    )(main, main, main, main, small, w2, bgk, gn.reshape(1, dv))


def _ffn_up_kernel(h_ref, wg_ref, wv_ref, cwg_ref, cwv_ref, cbg_ref, cbv_ref, wd_ref, o_ref, wdb_ref, w_scr,
                   hw_scr, *, tm, tn, bounds):
    wdb_ref[...] = wd_ref[0].astype(BF16)

    @pl.when(pl.program_id(1) == 0)
    def _():
        w_scr[:, :tn] = wg_ref[0].astype(BF16)
        w_scr[:, tn:] = wv_ref[0].astype(BF16)
        hw_scr[0:8, :] = jnp.zeros((8, 2 * tn), F32)

    cw = jnp.concatenate([cwg_ref[0], cwv_ref[0]], axis=1)
    cb = jnp.concatenate([cbg_ref[...], cbv_ref[...]], axis=1)
    w0, w1, w2 = cw[0:1, :], cw[1:2, :], cw[2:3, :]

    for lo, hi in zip(bounds[:-1], bounds[1:]):
        hw = _dot(h_ref[lo:hi, :], w_scr[...])
        hw_scr[8 + lo:8 + hi, :] = hw
        u = w2 * hw + w1 * hw_scr[7 + lo:7 + hi, :] + w0 * hw_scr[6 + lo:6 + hi, :] + cb
        gate, val = u[:, :tn], u[:, tn:]
        o_ref[lo:hi, :] = (gate * jax.nn.sigmoid(gate) * val).astype(o_ref.dtype)
    hw_scr[0:8, :] = hw_scr[tm:tm + 8, :]


def _ffn_up(hb, w_up, conv_w, conv_b, w_down, *, d_ff, tm, tn):
    s, d = hb.shape
    nj, ni = d_ff // tn, s // tm
    d_out = w_down.shape[2]
    slab = d_ff // (nj * ni)
    assert slab * nj * ni == d_ff and slab % 16 == 0
    return pl.pallas_call(
        functools.partial(_ffn_up_kernel, tm=tm, tn=tn, bounds=(0, tm // 2, tm)),
        grid=(nj, ni),
        in_specs=[pl.BlockSpec((tm, d), lambda j, i: (i, 0)),
                  pl.BlockSpec((1, d, tn), lambda j, i: (0, 0, j)),
                  pl.BlockSpec((1, d, tn), lambda j, i: (0, 0, nj + j)),
                  pl.BlockSpec((1, CONV_W, tn), lambda j, i: (0, 0, j)),
                  pl.BlockSpec((1, CONV_W, tn), lambda j, i: (0, 0, nj + j)),
                  pl.BlockSpec((1, tn), lambda j, i: (0, j)),
                  pl.BlockSpec((1, tn), lambda j, i: (0, nj + j)),
                  pl.BlockSpec((1, slab, d_out), lambda j, i: (0, j * ni + i, 0))],
        out_specs=[pl.BlockSpec((tm, tn), lambda j, i: (i, j)),
                   pl.BlockSpec((slab, d_out), lambda j, i: (j * ni + i, 0))],
        out_shape=[jax.ShapeDtypeStruct((s, d_ff), BF16), jax.ShapeDtypeStruct((d_ff, d_out), BF16)],
        scratch_shapes=[pltpu.VMEM((d, 2 * tn), BF16), pltpu.VMEM((8 + tm, 2 * tn), F32)],
        compiler_params=_params(("parallel", "arbitrary"),
                                blocks=[((tm, d), BF16), ((d, tn), F32), ((d, tn), F32), ((8, tn), F32),
                                        ((8, tn), F32), ((8, tn), F32), ((8, tn), F32), ((tm, tn), BF16),
                                        ((slab, d_out), F32), ((slab, d_out), BF16)],
                                scratch=[((d, 2 * tn), BF16), ((8 + tm, 2 * tn), F32)],
                                temps=[((tm // 2, 2 * tn), F32)] * 3),
        name="ffn_up",
    )(hb, w_up, w_up, conv_w, conv_w, conv_b, conv_b, w_down)


def _layout(d_model):
    a_width = A_HEADS * A_HEAD_DIM
    g_width = d_model - a_width
    g_kwidth = g_width // 2
    names = ("qa", "ka", "va", "qi", "ki", "wi", "qg", "kg", "vg", "glr", "gg")
    sizes = (a_width, a_width, a_width, IDX_HEADS * IDX_DIM, IDX_DIM, IDX_HEADS,
             g_kwidth, g_kwidth, g_width, G_LOWRANK, g_width)
    offs = np.concatenate([[0], np.cumsum(sizes)])
    src = {n: (int(offs[i]), int(offs[i + 1])) for i, n in enumerate(names)}
    order = ("qi", "qa", "ka", "va", "qg", "kg", "vg", "gg", "ki")
    lay, blk = {}, 0
    for n in order:
        width = src[n][1] - src[n][0]
        assert width % LANES == 0
        lay[n] = (blk, width // LANES)
        blk += width // LANES
    return src, order, lay, blk


def kernel(x, w_in, w_gk2, b_gk, attn_out_g, gla_norm_g, w_o, ln1_g, ln1_b, w_up, conv_w, conv_b,
           w_down, ln2_g, ln2_b):
    assert x.shape[0] == 1 and w_in.shape[0] == DEPTH == 1
    _, s, d = x.shape
    x2 = x[0]
    src, order, lay, nb_main = _layout(d)
    g_width = d - A_HEADS * A_HEAD_DIM
    dv = g_width // G_HEADS
    dk = dv // 2
    d_ff = w_down.shape[1]
    topk = min(TOPK_MAX, s // 4)

    w_proj = _relayout_w_in(jnp.swapaxes(w_in, 1, 2), src, order)
    n_small = IDX_HEADS + G_LOWRANK
    w2 = jnp.zeros((LANES, G_HEADS * dk), F32).at[IDX_HEADS:n_small].set(w_gk2[0]).astype(BF16)
    bgk = b_gk[0].reshape(1, G_HEADS * dk)
    n_main = (nb_main + 1) * LANES
    q_lo, q_hi = lay["qa"][0] * LANES, (lay["qa"][0] + lay["qa"][1]) * LANES
    col = np.ones((1, n_main), np.float32)
    col[:, q_lo:q_hi] = A_HEAD_DIM ** -0.5 * LOG2E
    col_scale = jnp.asarray(col)

    tm = _tile(s, 1024)
    main, small = _proj_blocks(x2, w_proj, col_scale, tm=_tile(s, 512), tn=_tile(n_main, 1280))
    bias = _indexer(main, small, lay, s=s, tq=256, tk=256, topk=topk)
    ta = _tile(s, 512)
    o_a = _attention(main, bias, attn_out_g[0], lay, s=s, tq=ta, tk=ta)
    o_g = _gla(main, small, w2, bgk, gla_norm_g[0], lay, s=s, rows=_tile(s, 512), chunk=256, dk=dk, dv=dv)
    h, hb = _w_o_ln(o_a, o_g, w_o[0].astype(BF16), x2, ln1_g[0], ln1_b[0], tm=_tile(s, 512), tn=_tile(d, 512),
                    alpha=DN_ALPHA)

    act, wd = _ffn_up(hb, w_up, conv_w, conv_b[0].reshape(1, 2 * d_ff), w_down, d_ff=d_ff, tm=tm,
                      tn=_tile(d_ff, 256))
    y2 = _matmul(act, wd, tm=_tile(s, 512), tn=_tile(d, 512), tk=d_ff, out_dtype=F32, residual=h,
                 alpha=DN_ALPHA, name="w_down")
    (out,) = _layer_norm(y2, ln2_g[0], ln2_b[0], tr=_tile(s, 256, 8), also_bf16=False)
    return out[None]
```

```python
import functools

import numpy as np
import jax
import jax.numpy as jnp
from jax import lax
from jax.experimental import pallas as pl
from jax.experimental.pallas import tpu as pltpu

A_HEADS = 16
A_HEAD_DIM = 128
IDX_HEADS = 32
IDX_DIM = 128
TOPK_MAX = 256
G_HEADS = 4
G_LOWRANK = 16
G_TAU = 16.0
CONV_W = 3
LN_EPS = 1e-5
RMS_EPS = 1e-6
DEPTH = 1
DN_ALPHA = (2 * DEPTH) ** 0.25

LANES = 128
MIB = 1024 * 1024
VMEM_BUDGET = 56 * MIB

BF16 = jnp.bfloat16
F32 = jnp.float32
NEG_BIG = -1e30
LOG2E = 1.4426950408889634
INT_MIN = -2 ** 31
COUNT_UNROLL = (8, 2, 1)


def _nbytes(shape, dtype):
    n = jnp.dtype(dtype).itemsize
    for dim in shape:
        n *= 1 if dim is None else getattr(dim, "block_size", dim)
    return n


def _params(sem, blocks, scratch=(), temps=()):
    need = (2 * sum(_nbytes(*b) for b in blocks) + sum(_nbytes(*b) for b in scratch)
            + sum(_nbytes(*b) for b in temps))
    limit = -(-need // MIB) * MIB
    assert limit <= VMEM_BUDGET, (limit, VMEM_BUDGET)
    return pltpu.CompilerParams(dimension_semantics=sem, vmem_limit_bytes=limit)


def _tile(n, max_tile, quantum=LANES):
    best = None
    for t in range(quantum, min(n, max_tile) + 1, quantum):
        if n % t == 0:
            best = t
    assert best is not None, (n, max_tile, quantum)
    return best


def _dot(a, b):
    return jnp.dot(a, b, preferred_element_type=F32)


def _dot_nt(a, b):
    return lax.dot_general(a, b, (((1,), (1,)), ((), ())), preferred_element_type=F32)


def _dot_tn(a, b):
    return lax.dot_general(a, b, (((0,), (0,)), ((), ())), preferred_element_type=F32)


def _relayout_kernel(off_ref, kind_ref, a_ref, b_ref, o_ref, *, n_wi, n_small):
    del off_ref
    kind = kind_ref[pl.program_id(0)]

    @pl.when(kind == 0)
    def _():
        o_ref[...] = a_ref[0].astype(BF16)

    @pl.when(kind == 1)
    def _():
        o_ref[...] = jnp.zeros(o_ref.shape, BF16)

    @pl.when(kind == 2)
    def _():
        r = lax.broadcasted_iota(jnp.int32, o_ref.shape, 0)
        o_ref[...] = jnp.where(r < n_wi, a_ref[0], jnp.where(r < n_small, b_ref[0], 0.0)).astype(BF16)


def _relayout_w_in(w_t, src, order):
    _, n_in, d = w_t.shape
    starts = [src[n][0] + t * LANES for n in order for t in range((src[n][1] - src[n][0]) // LANES)]
    wi0, glr0 = src["wi"][0], src["glr"][0]
    n_wi = src["wi"][1] - wi0
    n_small = n_wi + src["glr"][1] - glr0
    glr_win = glr0 - n_wi
    assert n_small <= LANES and glr_win >= 0 and wi0 + LANES <= n_in and glr_win + LANES <= n_in
    assert all(o % 8 == 0 for o in starts + [wi0, glr_win])
    row_off8 = [o // 8 for o in starts + [0, wi0]]
    kind = [0] * len(starts) + [1, 2]
    nb = len(kind)
    tables = [jnp.asarray(np.asarray(t, np.int32)) for t in (row_off8, kind)]
    return pl.pallas_call(
        functools.partial(_relayout_kernel, n_wi=n_wi, n_small=n_small),
        grid_spec=pltpu.PrefetchScalarGridSpec(
            num_scalar_prefetch=2,
            grid=(nb,),
            in_specs=[pl.BlockSpec((pl.Element(1), pl.Element(LANES), pl.Element(d)),
                                   lambda b, off, kd: (0, off[b] * 8, 0)),
                      pl.BlockSpec((pl.Element(1), pl.Element(LANES), pl.Element(d)),
                                   lambda b, off, kd: (0, glr_win, 0))],
            out_specs=pl.BlockSpec((LANES, d), lambda b, off, kd: (b, 0))),
        out_shape=jax.ShapeDtypeStruct((nb * LANES, d), BF16),
        compiler_params=_params(("arbitrary",),
                                blocks=[((LANES, d), F32), ((LANES, d), F32), ((LANES, d), BF16)],
                                temps=[((LANES, d), F32)]),
        name="relayout_w_in",
    )(*tables, w_t, w_t)


def _proj_kernel(x_ref, w_ref, ws_ref, sc_ref, wd_ref, o_ref, os_ref, wdb_ref, xb_scr):
    wdb_ref[...] = wd_ref[0].astype(BF16)

    @pl.when(pl.program_id(1) == 0)
    def _():
        xb_scr[...] = x_ref[...].astype(BF16)
        os_ref[...] = _dot_nt(xb_scr[...], ws_ref[...])

    res = _dot_nt(xb_scr[...], w_ref[...]) * sc_ref[...]
    for c in range(o_ref.shape[0]):
        o_ref[c] = res[:, c * LANES:(c + 1) * LANES].astype(o_ref.dtype)


def _proj_blocks(x, wt, col_scale, w_down, *, tm, tn):
    s, k = x.shape
    n = wt.shape[0] - LANES
    ni, nj = s // tm, n // tn
    _, f, d_out = w_down.shape
    n_slab = max(c for c in range(1, ni * nj + 1) if f % c == 0 and (f // c) % 16 == 0)
    slab = f // n_slab

    def slab_of(i, j):
        return jnp.minimum(i * nj + j, n_slab - 1)

    return pl.pallas_call(
        _proj_kernel,
        grid=(ni, nj),
        in_specs=[pl.BlockSpec((tm, k), lambda i, j: (i, 0)),
                  pl.BlockSpec((tn, k), lambda i, j: (j, 0)),
                  pl.BlockSpec((LANES, k), lambda i, j: (n // LANES, 0)),
                  pl.BlockSpec((1, tn), lambda i, j: (0, j)),
                  pl.BlockSpec((1, slab, d_out), lambda i, j: (0, slab_of(i, j), 0))],
        out_specs=[pl.BlockSpec((tn // LANES, tm, LANES), lambda i, j: (j, i, 0)),
                   pl.BlockSpec((tm, LANES), lambda i, j: (i, 0)),
                   pl.BlockSpec((slab, d_out), lambda i, j: (slab_of(i, j), 0))],
        out_shape=[jax.ShapeDtypeStruct((n // LANES, s, LANES), BF16),
                   jax.ShapeDtypeStruct((s, LANES), F32),
                   jax.ShapeDtypeStruct((f, d_out), BF16)],
        scratch_shapes=[pltpu.VMEM((tm, k), BF16)],
        compiler_params=_params(("arbitrary", "arbitrary"),
                                blocks=[((tm, k), F32), ((tn, k), BF16), ((LANES, k), BF16), ((8, tn), F32),
                                        ((tm, tn), BF16), ((tm, LANES), F32), ((slab, d_out), F32),
                                        ((slab, d_out), BF16)],
                                scratch=[((tm, k), BF16)],
                                temps=[((tm, tn), F32), ((tm, tn), F32)]),
        name="proj",
    )(x, wt, wt, col_scale, w_down)


def _mm_kernel(a_ref, b_ref, *rest, alpha, has_res, nk):
    if has_res:
        r_ref, o_ref = rest[0], rest[1]
        rest = rest[2:]
    else:
        r_ref, o_ref = None, rest[0]
        rest = rest[1:]

    def finish(acc):
        if has_res:
            acc = acc + alpha * r_ref[...]
        o_ref[...] = acc.astype(o_ref.dtype)

    if nk == 1:
        finish(_dot(a_ref[...], b_ref[...]))
        return
    acc_ref = rest[0]
    kk = pl.program_id(2)

    @pl.when(kk == 0)
    def _():
        acc_ref[...] = _dot(a_ref[...], b_ref[...])

    @pl.when(kk > 0)
    def _():
        acc_ref[...] += _dot(a_ref[...], b_ref[...])

    @pl.when(kk == nk - 1)
    def _():
        finish(acc_ref[...])


def _matmul(a, b, *, tm, tn, tk, out_dtype, residual=None, alpha=1.0, name="matmul"):
    m, k = a.shape
    n = b.shape[1]
    nk = k // tk
    in_specs = [pl.BlockSpec((tm, tk), lambda i, j, kk: (i, kk)),
                pl.BlockSpec((tk, tn), lambda i, j, kk: (kk, j))]
    args = [a, b]
    if residual is not None:
        in_specs.append(pl.BlockSpec((tm, tn), lambda i, j, kk: (i, j)))
        args.append(residual)
    scratch = [pltpu.VMEM((tm, tn), F32)] if nk > 1 else []
    return pl.pallas_call(
        functools.partial(_mm_kernel, alpha=alpha, has_res=residual is not None, nk=nk),
        grid=(m // tm, n // tn, nk),
        in_specs=in_specs,
        out_specs=pl.BlockSpec((tm, tn), lambda i, j, kk: (i, j)),
        out_shape=jax.ShapeDtypeStruct((m, n), out_dtype),
        scratch_shapes=scratch,
        compiler_params=_params(("parallel", "parallel", "arbitrary"),
                                blocks=[((tm, tk), a.dtype), ((tk, tn), b.dtype), ((tm, tn), out_dtype)]
                                + ([((tm, tn), F32)] if residual is not None else []),
                                scratch=[((tm, tn), F32)] if nk > 1 else [],
                                temps=[((tm, tn), F32)]),
        name=name,
    )(*args)


def _wo_kernel(oa_ref, og_ref, wt_ref, wb_ref, x_ref, o_ref, w_scr, *, ka, alpha):
    @pl.when(pl.program_id(1) == 0)
    def _():
        w_scr[:ka, :] = wt_ref[0].astype(BF16)
        w_scr[ka:, :] = wb_ref[0].astype(BF16)

    acc = _dot(oa_ref[...], w_scr[:ka, :]) + _dot(og_ref[...], w_scr[ka:, :])
    o_ref[...] = acc + alpha * x_ref[...]


def _w_o(o_a, o_g, w_o, x, *, tm, tn, alpha):
    s, ka = o_a.shape
    kg = o_g.shape[1]
    d = w_o.shape[2]
    assert ka % kg == 0
    return pl.pallas_call(
        functools.partial(_wo_kernel, ka=ka, alpha=alpha),
        grid=(d // tn, s // tm),
        in_specs=[pl.BlockSpec((tm, ka), lambda j, i: (i, 0)),
                  pl.BlockSpec((tm, kg), lambda j, i: (i, 0)),
                  pl.BlockSpec((1, ka, tn), lambda j, i: (0, 0, j)),
                  pl.BlockSpec((1, kg, tn), lambda j, i: (0, ka // kg, j)),
                  pl.BlockSpec((tm, tn), lambda j, i: (i, j))],
        out_specs=pl.BlockSpec((tm, tn), lambda j, i: (i, j)),
        out_shape=jax.ShapeDtypeStruct((s, d), F32),
        scratch_shapes=[pltpu.VMEM((ka + kg, tn), BF16)],
        compiler_params=_params(("parallel", "arbitrary"),
                                blocks=[((tm, ka), BF16), ((tm, kg), BF16), ((ka, tn), F32), ((kg, tn), F32),
                                        ((tm, tn), F32), ((tm, tn), F32)],
                                scratch=[((ka + kg, tn), BF16)],
                                temps=[((tm, tn), F32), ((tm, tn), F32)]),
        name="w_o",
    )(o_a, o_g, w_o, w_o, x)


def _ln_kernel(y_ref, g_ref, b_ref, *o_refs):
    y = y_ref[...]
    mu = jnp.mean(y, axis=1, keepdims=True)
    yc = y - mu
    var = jnp.mean(yc * yc, axis=1, keepdims=True)
    out = yc * lax.rsqrt(var + LN_EPS) * g_ref[...] + b_ref[...]
    for o_ref in o_refs:
        o_ref[...] = out.astype(o_ref.dtype)


def _layer_norm(y, g, b, *, tr, also_bf16):
    s, d = y.shape
    out_shape = [jax.ShapeDtypeStruct((s, d), F32)]
    out_specs = [pl.BlockSpec((tr, d), lambda i: (i, 0))]
    if also_bf16:
        out_shape.append(jax.ShapeDtypeStruct((s, d), BF16))
        out_specs.append(pl.BlockSpec((tr, d), lambda i: (i, 0)))
    return pl.pallas_call(
        _ln_kernel,
        grid=(s // tr,),
        in_specs=[pl.BlockSpec((tr, d), lambda i: (i, 0)),
                  pl.BlockSpec((1, d), lambda i: (0, 0)),
                  pl.BlockSpec((1, d), lambda i: (0, 0))],
        out_specs=out_specs,
        out_shape=out_shape,
        compiler_params=_params(("parallel",),
                                blocks=[((tr, d), F32), ((8, d), F32), ((8, d), F32)]
                                + [((tr, d), o.dtype) for o in out_shape],
                                temps=[((tr, d), F32), ((tr, d), F32)]),
        name="layer_norm",
    )(y, g.reshape(1, d), b.reshape(1, d))


def _idx_kernel(qi_ref, ki_ref, sm_ref, bias_ref, hi_scr, lo_scr, wb_scr, *, tq, tk, nkt, hi, group,
                topk, wscale):
    qb = pl.program_id(0)
    nk = ((qb + 1) * tq + tk - 1) // tk
    reps = tk // LANES
    i16 = jnp.int16
    low16 = -2 ** 15

    for h in range(hi):
        wb_scr[h] = jnp.broadcast_to(sm_ref[:, h:h + 1] * wscale, (tq, LANES))

    row = qb * tq + lax.broadcasted_iota(jnp.int32, (group, tk), 0)
    col = lax.broadcasted_iota(jnp.int32, (group, tk), 1)

    def lanes(a):
        return jnp.concatenate([a] * reps, axis=1) if reps > 1 else a

    def fold(a):
        out = a[:, :LANES]
        for r in range(1, reps):
            out = out + a[:, r * LANES:(r + 1) * LANES]
        return out

    def score_tile(kb, carry):
        kt = ki_ref[0, pl.ds(pl.multiple_of(kb * tk, tk), tk), :]
        for r in range(tq // group):
            rows = slice(r * group, (r + 1) * group)
            qs = qi_ref[:, rows, :].reshape(hi * group, IDX_DIM)
            d = _dot_nt(qs, kt)
            acc = jnp.zeros((group, tk), F32)
            for h in range(hi):
                acc = acc + jnp.maximum(d[h * group:(h + 1) * group], 0.0) * lanes(wb_scr[h, rows, :])
            bits = pltpu.bitcast(acc + 0.0, jnp.int32)
            key = jnp.where(kb * tk + col <= row + r * group,
                            bits ^ ((bits >> 31) & jnp.int32(0x7FFFFFFF)), INT_MIN)
            hi_scr[kb, rows, :] = (key >> 16).astype(i16)
            lo_scr[kb, rows, :] = ((key & 0xFFFF) + low16).astype(i16)
        return carry

    lax.fori_loop(0, nk, score_tile, 0)

    def count_ge(ref, cand):
        cand_t = lanes(cand.astype(i16))

        def step(kb, c):
            return c + fold(jnp.where(ref[kb] >= cand_t, i16(1), i16(0)))

        c = jnp.zeros((tq, LANES), i16)
        done = 0
        for width in COUNT_UNROLL:
            def step_many(q, c, width=width, done=done):
                for u in range(width):
                    c = step(done + q * width + u, c)
                return c

            n_groups = (nk - done) // width
            c = lax.fori_loop(0, n_groups, step_many, c)
            done = done + n_groups * width
        return jnp.sum(c.astype(F32), axis=1, keepdims=True)

    kf = float(topk)

    def bit_search(ref, base_cnt, cnt_start, stop_when_exact):
        def body(state):
            it, v, cnt_v = state
            cand = v + jnp.left_shift(jnp.int32(1), 15 - it)
            cnt = base_cnt + count_ge(ref, cand)
            ok = cnt >= kf
            return it + 1, jnp.where(ok, cand, v), jnp.where(ok, cnt, cnt_v)

        def unsettled(state):
            go = state[0] < 16
            if stop_when_exact:
                go = go & (jnp.max(jnp.where(state[2] == kf, 0.0, 1.0)) > 0.0)
            return go

        v0 = jnp.full((tq, LANES), low16, jnp.int32)
        _, v, cnt_v = lax.while_loop(unsettled, body, (jnp.int32(0), v0, cnt_start))
        return v, cnt_v

    n_all = jnp.zeros((tq, 1), F32) + (nk * tk).astype(F32)
    v_hi, n_from_hi = bit_search(hi_scr, 0.0, n_all, False)
    n_above = count_ge(hi_scr, v_hi + 1)
    v_hi_t = lanes(v_hi.astype(i16))

    def keep_equal(kb, carry):
        lo_scr[kb] = jnp.where(hi_scr[kb] == v_hi_t, lo_scr[kb], i16(low16))
        return carry

    lax.fori_loop(0, nk, keep_equal, 0)
    v_lo, n_sel = bit_search(lo_scr, n_above, n_from_hi, True)
    v_lo = jnp.where(v_hi == low16, jnp.maximum(v_lo, low16 + 1), v_lo)
    v_lo_t = lanes(v_lo.astype(i16))
    zero = jnp.zeros((tq, tk), bias_ref.dtype)
    neg = jnp.full((tq, tk), NEG_BIG, bias_ref.dtype)
    has_ties = jnp.max(jnp.where(n_sel > kf, 1.0, 0.0)) > 0.0

    @pl.when(jnp.logical_not(has_ties))
    def _():
        def emit(kb, carry):
            h16 = hi_scr[kb]
            sel = (h16 > v_hi_t) | ((h16 == v_hi_t) & (lo_scr[kb] >= v_lo_t))
            bias_ref[0, kb] = jnp.where(sel, zero, neg)
            return carry

        lax.fori_loop(0, nk, emit, 0)

    @pl.when(has_ties)
    def _():
        top16 = 2 ** 15 - 1
        n_gt = n_above + jnp.where(v_lo[:, :1] >= top16, 0.0, count_ge(lo_scr, jnp.minimum(v_lo + 1, top16)))
        need = kf - n_gt
        earlier = jnp.where(lax.broadcasted_iota(jnp.int32, (tk, tk), 0)
                            < lax.broadcasted_iota(jnp.int32, (tk, tk), 1), 1.0, 0.0).astype(BF16)
        one_b = jnp.ones((tq, tk), BF16)
        zero_b = jnp.zeros((tq, tk), BF16)

        def emit_ties(kb, seen):
            h16 = hi_scr[kb]
            l16 = lo_scr[kb]
            same_hi = h16 == v_hi_t
            larger = (h16 > v_hi_t) | (same_hi & (l16 > v_lo_t))
            tied = jnp.where(same_hi & (l16 == v_lo_t), one_b, zero_b)
            rank = seen + _dot(tied, earlier)
            keep = jnp.where(rank < need, 1.0, 0.0).astype(BF16)
            sel = larger | ((tied > 0.0) & (keep > 0.0))
            bias_ref[0, kb] = jnp.where(sel, zero, neg)
            return seen + jnp.sum(tied.astype(F32), axis=1, keepdims=True)

        lax.fori_loop(0, nk, emit_ties, jnp.zeros((tq, 1), F32))

    def fill(kb, carry):
        bias_ref[0, kb] = neg
        return carry

    lax.fori_loop(nk, nkt, fill, 0)


def _indexer(main, small, lay, *, s, tq, tk, topk):
    nkt = s // tk
    hi = IDX_HEADS
    group = min(tq, 64)
    qi0, ki0 = lay["qi"][0], lay["ki"][0]
    assert qi0 % hi == 0
    return pl.pallas_call(
        functools.partial(_idx_kernel, tq=tq, tk=tk, nkt=nkt, hi=hi, group=group, topk=topk,
                          wscale=IDX_HEADS ** -0.5 * IDX_DIM ** -0.5),
        grid=(s // tq,),
        in_specs=[pl.BlockSpec((hi, tq, LANES), lambda i: (qi0 // hi, i, 0)),
                  pl.BlockSpec((1, s, LANES), lambda i: (ki0, 0, 0)),
                  pl.BlockSpec((tq, LANES), lambda i: (i, 0))],
        out_specs=pl.BlockSpec((1, nkt, tq, tk), lambda i: (i, 0, 0, 0)),
        out_shape=jax.ShapeDtypeStruct((s // tq, nkt, tq, tk), BF16),
        scratch_shapes=[pltpu.VMEM((nkt, tq, tk), jnp.int16),
                        pltpu.VMEM((nkt, tq, tk), jnp.int16),
                        pltpu.VMEM((hi, tq, LANES), F32)],
        compiler_params=_params(("parallel",),
                                blocks=[((hi, tq, LANES), BF16), ((s, LANES), BF16), ((tq, LANES), F32),
                                        ((nkt, tq, tk), BF16)],
                                scratch=[((nkt, tq, tk), jnp.int16), ((nkt, tq, tk), jnp.int16),
                                         ((hi, tq, LANES), F32)],
                                temps=[((hi * group, tk), F32), ((hi * group, tk), F32)]),
        name="indexer",
    )(main, main, small)


def _attn_kernel(qi_ref, kb_ref, q_ref, k_ref, v_ref, b_ref, g_ref, o_ref, bias_scr, m_scr, acc_scr, *pipe,
                 nh, tq, tk):
    step = pl.program_id(0)
    qi = qi_ref[step]
    kb = kb_ref[step]

    @pl.when(kb == 0)
    def _():
        m_scr[...] = jnp.full(m_scr.shape, NEG_BIG, F32)
        acc_scr[...] = jnp.zeros(acc_scr.shape, F32)

    na, nb, sq, sk = b_ref.shape
    for a in range(na):
        for b in range(nb):
            bias_scr[a * sq:(a + 1) * sq, b * sk:(b + 1) * sk] = b_ref[a, b].astype(F32)

    ones = jnp.ones((tk, LANES), BF16)

    def qk(h, s_ref):
        s_ref[...] = _dot_nt(q_ref[h], k_ref[h])

    def soft(h, s_ref, p_ref, al_ref):
        s = s_ref[...] + bias_scr[...]
        m_prev = m_scr[h]
        m_new = jnp.maximum(m_prev, jnp.max(s, axis=1, keepdims=True))
        m_scr[h] = m_new
        p_ref[...] = jnp.exp2(s - jnp.concatenate([m_new] * (tk // LANES), axis=1)).astype(BF16)
        al_ref[...] = jnp.exp2(m_prev - m_new)

    def pv(h, p_ref, al_ref):
        al = al_ref[...]
        v_ext = jnp.concatenate([v_ref[h], ones], axis=1)
        acc_scr[h] = acc_scr[h] * jnp.concatenate([al, al], axis=1) + _dot(p_ref[...], v_ext)

    s_a, s_b, p_a, p_b, al_a, al_b = pipe
    qk(0, s_a)
    qk(1, s_b)
    soft(0, s_a, p_a, al_a)

    def pair(j, carry):
        qk(2 * j, s_a)
        soft(2 * j - 1, s_b, p_b, al_b)
        pv(2 * j - 2, p_a, al_a)
        qk(2 * j + 1, s_b)
        soft(2 * j, s_a, p_a, al_a)
        pv(2 * j - 1, p_b, al_b)
        return carry

    lax.fori_loop(1, nh // 2, pair, 0)
    soft(nh - 1, s_b, p_b, al_b)
    pv(nh - 2, p_a, al_a)
    pv(nh - 1, p_b, al_b)

    @pl.when(kb == (qi * tq + tq - 1) // tk)
    def _():
        ss = jnp.zeros((tq, LANES), F32)
        for h in range(nh):
            o_h = acc_scr[h, :, :LANES] / acc_scr[h, :, LANES:]
            acc_scr[h, :, :LANES] = o_h
            ss = ss + o_h * o_h
        ms = jnp.sum(ss, axis=1, keepdims=True) * (1.0 / (nh * LANES))
        r = lax.rsqrt(ms + RMS_EPS)
        for h in range(nh):
            sl = slice(h * LANES, (h + 1) * LANES)
            o_ref[:, sl] = (acc_scr[h, :, :LANES] * r * g_ref[:, sl]).astype(o_ref.dtype)


def _attention(main, bias, g, lay, *, s, tq, tk):
    nh = A_HEADS
    nqb, nkt, sq, sk = bias.shape
    q0, k0, v0 = lay["qa"][0], lay["ka"][0], lay["va"][0]
    assert q0 % nh == 0 and k0 % nh == 0 and v0 % nh == 0
    pairs = [(i, j) for i in range(s // tq) for j in range((i * tq + tq - 1) // tk + 1)]
    qi_tab = jnp.asarray(np.asarray([p[0] for p in pairs], np.int32))
    kb_tab = jnp.asarray(np.asarray([p[1] for p in pairs], np.int32))
    return pl.pallas_call(
        functools.partial(_attn_kernel, nh=nh, tq=tq, tk=tk),
        grid_spec=pltpu.PrefetchScalarGridSpec(
            num_scalar_prefetch=2,
            grid=(len(pairs),),
            in_specs=[pl.BlockSpec((nh, tq, LANES), lambda t, qi, kb: (q0 // nh, qi[t], 0)),
                      pl.BlockSpec((nh, tk, LANES), lambda t, qi, kb: (k0 // nh, kb[t], 0)),
                      pl.BlockSpec((nh, tk, LANES), lambda t, qi, kb: (v0 // nh, kb[t], 0)),
                      pl.BlockSpec((tq // sq, tk // sk, sq, sk), lambda t, qi, kb: (qi[t], kb[t], 0, 0)),
                      pl.BlockSpec((1, nh * LANES), lambda t, qi, kb: (0, 0))],
            out_specs=pl.BlockSpec((tq, nh * LANES), lambda t, qi, kb: (qi[t], 0)),
            scratch_shapes=[pltpu.VMEM((tq, tk), F32),
                            pltpu.VMEM((nh, tq, LANES), F32),
                            pltpu.VMEM((nh, tq, 2 * LANES), F32),
                            pltpu.VMEM((tq, tk), F32), pltpu.VMEM((tq, tk), F32),
                            pltpu.VMEM((tq, tk), BF16), pltpu.VMEM((tq, tk), BF16),
                            pltpu.VMEM((tq, LANES), F32), pltpu.VMEM((tq, LANES), F32)]),
        out_shape=jax.ShapeDtypeStruct((s, nh * LANES), BF16),
        compiler_params=_params(("arbitrary",),
                                blocks=[((nh, tq, LANES), BF16), ((nh, tk, LANES), BF16), ((nh, tk, LANES), BF16),
                                        ((tq, tk), BF16), ((8, nh * LANES), F32), ((tq, nh * LANES), BF16)],
                                scratch=[((tq, tk), F32), ((nh, tq, LANES), F32), ((nh, tq, 2 * LANES), F32),
                                         ((tq, tk), F32), ((tq, tk), F32), ((tq, tk), BF16), ((tq, tk), BF16),
                                         ((tq, LANES), F32), ((tq, LANES), F32)],
                                temps=[((tq, tk), F32), ((tq, tk), F32)]),
        name="attention",
    )(qi_tab, kb_tab, main, main, main, bias, g.reshape(1, nh * LANES))


def _gla_kernel(q_ref, k_ref, v_ref, gg_ref, sm_ref, w2_ref, bgk_ref, gn_ref, o_ref, st_scr, *,
                rows, chunk, nheads, dk, dv):
    @pl.when(pl.program_id(0) == 0)
    def _():
        st_scr[...] = jnp.zeros(st_scr.shape, F32)

    r_i = lax.broadcasted_iota(jnp.int32, (chunk, chunk), 0)
    c_i = lax.broadcasted_iota(jnp.int32, (chunk, chunk), 1)
    tri = jnp.where(r_i >= c_i, 1.0, 0.0).astype(BF16)
    nbk, nbv = dk // LANES, dv // LANES
    assert chunk >= 16 and chunk & (chunk - 1) == 0
    leaf = 2
    half_sizes = [chunk >> i for i in range(1, chunk.bit_length()) if chunk >> i >= leaf]
    quadrant = {m: (r_i // (2 * m) == c_i // (2 * m)) & (r_i % (2 * m) >= m) & (c_i % (2 * m) < m)
                for m in half_sizes}
    near = [(c_i == r_i - j) & (r_i % leaf >= j) for j in range(leaf)]
    sub = lax.broadcasted_iota(jnp.int32, (chunk // 8, 8, dk), 1)

    def reference_rows(b, m):
        if m >= 8:
            return jnp.concatenate(
                [jnp.broadcast_to(b[blk + m:blk + m + 1, :], (2 * m, dk)) for blk in range(0, chunk, 2 * m)],
                axis=0)
        b3 = b.reshape(chunk // 8, 8, dk)
        out = None
        for blk in range(0, 8, 2 * m):
            rows = jnp.broadcast_to(b3[:, blk + m:blk + m + 1, :], b3.shape)
            out = rows if out is None else jnp.where(sub >= blk, rows, out)
        return out.reshape(chunk, dk)

    def wide(ref, first, n, r0):
        return jnp.concatenate([ref[first + j, pl.ds(r0, chunk), :] for j in range(n)], axis=1)

    def step(c, carry):
        r0 = pl.multiple_of(c * chunk, chunk)
        z_all = _dot(sm_ref[pl.ds(r0, chunk), :].astype(BF16), w2_ref[...]) + bgk_ref[...]
        for h in range(nheads):
            q = wide(q_ref, h * nbk, nbk, r0).astype(F32) * (dk ** -0.5)
            k = wide(k_ref, h * nbk, nbk, r0).astype(F32)
            v = wide(v_ref, h * nbv, nbv, r0)
            z = z_all[:, h * dk:(h + 1) * dk]
            g = (jnp.minimum(z, 0.0) - jnp.log1p(jnp.exp(-jnp.abs(z)))) * (LOG2E / G_TAU)
            g_hi = g.astype(BF16)
            rem = g - g_hi.astype(F32)
            g_mid = rem.astype(BF16)
            g_lo = (rem - g_mid.astype(F32)).astype(BF16)
            b = _dot(tri, g_hi) + _dot(tri, g_mid) + _dot(tri, g_lo)
            b_last = b[chunk - 1:chunk, :]
            att = jnp.zeros((chunk, chunk), F32)
            for m in half_sizes:
                e = jnp.exp2(-jnp.abs(b - reference_rows(b, m)))
                att = att + jnp.where(quadrant[m], _dot_nt((q * e).astype(BF16), (k * e).astype(BF16)), 0.0)
            for j in range(leaf):
                kj = k if j == 0 else pltpu.roll(k, j, 0)
                bj = b if j == 0 else pltpu.roll(b, j, 0)
                pair = jnp.sum(q * kj * jnp.exp2(jnp.minimum(b - bj, 0.0)), axis=1, keepdims=True)
                att = att + jnp.where(near[j], pair, 0.0)
            st = st_scr[h]
            o = _dot(att.astype(BF16), v) + _dot_nt((q * jnp.exp2(b)).astype(BF16), st.astype(BF16))
            kd = (k * jnp.exp2(b_last - b)).astype(BF16)
            st_scr[h] = st * jnp.exp2(b_last) + _dot_tn(v, kd)
            ms = jnp.mean(o * o, axis=1, keepdims=True)
            gate = wide(gg_ref, h * nbv, nbv, r0).astype(F32)
            out = (o * lax.rsqrt(ms + RMS_EPS)) * gn_ref[...] * (gate * jax.nn.sigmoid(gate))
            o_ref[pl.ds(r0, chunk), h * dv:(h + 1) * dv] = out.astype(o_ref.dtype)
        return carry

    lax.fori_loop(0, rows // chunk, step, 0)


def _gla(main, small, w2, bgk, gn, lay, *, s, rows, chunk, dk, dv):
    nh = G_HEADS
    nqk, nv = nh * dk // LANES, nh * dv // LANES
    q0, k0, v0, g0 = lay["qg"][0], lay["kg"][0], lay["vg"][0], lay["gg"][0]
    assert q0 % nqk == 0 and k0 % nqk == 0 and v0 % nv == 0 and g0 % nv == 0
    return pl.pallas_call(
        functools.partial(_gla_kernel, rows=rows, chunk=chunk, nheads=nh, dk=dk, dv=dv),
        grid=(s // rows,),
        in_specs=[pl.BlockSpec((nqk, rows, LANES), lambda r: (q0 // nqk, r, 0)),
                  pl.BlockSpec((nqk, rows, LANES), lambda r: (k0 // nqk, r, 0)),
                  pl.BlockSpec((nv, rows, LANES), lambda r: (v0 // nv, r, 0)),
                  pl.BlockSpec((nv, rows, LANES), lambda r: (g0 // nv, r, 0)),
                  pl.BlockSpec((rows, LANES), lambda r: (r, 0)),
                  pl.BlockSpec((LANES, nh * dk), lambda r: (0, 0)),
                  pl.BlockSpec((1, nh * dk), lambda r: (0, 0)),
                  pl.BlockSpec((1, dv), lambda r: (0, 0))],
        out_specs=pl.BlockSpec((rows, nh * dv), lambda r: (r, 0)),
        out_shape=jax.ShapeDtypeStruct((s, nh * dv), BF16),
        scratch_shapes=[pltpu.VMEM((nh, dv, dk), F32)],
        compiler_params=_params(("arbitrary",),
                                blocks=[((nqk, rows, LANES), BF16), ((nqk, rows, LANES), BF16),
                                        ((nv, rows, LANES), BF16), ((nv, rows, LANES), BF16),
                                        ((rows, LANES), F32), ((LANES, nh * dk), BF16), ((8, nh * dk), F32),
                                        ((8, dv), F32), ((rows, nh * dv), BF16)],
                                scratch=[((nh, dv, dk), F32)],
                                temps=[((chunk, nh * dk), F32)] * 8 + [((dv, dk), F32)] * 2),
        name="gla",
    )(main, main, main, main, small, w2, bgk, gn.reshape(1, dv))


def _ffn_up_kernel(h_ref, wg_ref, wv_ref, cwg_ref, cwv_ref, cbg_ref, cbv_ref, o_ref, w_scr, hw_scr,
                   *, tm, tn, bounds):
    @pl.when(pl.program_id(1) == 0)
    def _():
        w_scr[:, :tn] = wg_ref[0].astype(BF16)
        w_scr[:, tn:] = wv_ref[0].astype(BF16)
        hw_scr[0:8, :] = jnp.zeros((8, 2 * tn), F32)

    cw = jnp.concatenate([cwg_ref[0], cwv_ref[0]], axis=1)
    cb = jnp.concatenate([cbg_ref[...], cbv_ref[...]], axis=1)
    w0, w1, w2 = cw[0:1, :], cw[1:2, :], cw[2:3, :]

    for lo, hi in zip(bounds[:-1], bounds[1:]):
        hw = _dot(h_ref[lo:hi, :], w_scr[...])
        hw_scr[8 + lo:8 + hi, :] = hw
        u = w2 * hw + w1 * hw_scr[7 + lo:7 + hi, :] + w0 * hw_scr[6 + lo:6 + hi, :] + cb
        gate, val = u[:, :tn], u[:, tn:]
        o_ref[lo:hi, :] = (gate * jax.nn.sigmoid(gate) * val).astype(o_ref.dtype)
    hw_scr[0:8, :] = hw_scr[tm:tm + 8, :]


def _ffn_up(hb, w_up, conv_w, conv_b, *, d_ff, tm, tn):
    s, d = hb.shape
    nj = d_ff // tn
    return pl.pallas_call(
        functools.partial(_ffn_up_kernel, tm=tm, tn=tn, bounds=(0, tm // 2, tm)),
        grid=(nj, s // tm),
        in_specs=[pl.BlockSpec((tm, d), lambda j, i: (i, 0)),
                  pl.BlockSpec((1, d, tn), lambda j, i: (0, 0, j)),
                  pl.BlockSpec((1, d, tn), lambda j, i: (0, 0, nj + j)),
                  pl.BlockSpec((1, CONV_W, tn), lambda j, i: (0, 0, j)),
                  pl.BlockSpec((1, CONV_W, tn), lambda j, i: (0, 0, nj + j)),
                  pl.BlockSpec((1, tn), lambda j, i: (0, j)),
                  pl.BlockSpec((1, tn), lambda j, i: (0, nj + j))],
        out_specs=pl.BlockSpec((tm, tn), lambda j, i: (i, j)),
        out_shape=jax.ShapeDtypeStruct((s, d_ff), BF16),
        scratch_shapes=[pltpu.VMEM((d, 2 * tn), BF16), pltpu.VMEM((8 + tm, 2 * tn), F32)],
        compiler_params=_params(("parallel", "arbitrary"),
                                blocks=[((tm, d), BF16), ((d, tn), F32), ((d, tn), F32), ((8, tn), F32),
                                        ((8, tn), F32), ((8, tn), F32), ((8, tn), F32), ((tm, tn), BF16)],
                                scratch=[((d, 2 * tn), BF16), ((8 + tm, 2 * tn), F32)],
                                temps=[((tm // 2, 2 * tn), F32)] * 3),
        name="ffn_up",
    )(hb, w_up, w_up, conv_w, conv_w, conv_b, conv_b)


def _layout(d_model):
    a_width = A_HEADS * A_HEAD_DIM
    g_width = d_model - a_width
    g_kwidth = g_width // 2
    names = ("qa", "ka", "va", "qi", "ki", "wi", "qg", "kg", "vg", "glr", "gg")
    sizes = (a_width, a_width, a_width, IDX_HEADS * IDX_DIM, IDX_DIM, IDX_HEADS,
             g_kwidth, g_kwidth, g_width, G_LOWRANK, g_width)
    offs = np.concatenate([[0], np.cumsum(sizes)])
    src = {n: (int(offs[i]), int(offs[i + 1])) for i, n in enumerate(names)}
    order = ("qi", "qa", "ka", "va", "qg", "kg", "vg", "gg", "ki")
    lay, blk = {}, 0
    for n in order:
        width = src[n][1] - src[n][0]
        assert width % LANES == 0
        lay[n] = (blk, width // LANES)
        blk += width // LANES
    return src, order, lay, blk


def kernel(x, w_in, w_gk2, b_gk, attn_out_g, gla_norm_g, w_o, ln1_g, ln1_b, w_up, conv_w, conv_b,
           w_down, ln2_g, ln2_b):
    assert x.shape[0] == 1 and w_in.shape[0] == DEPTH == 1
    _, s, d = x.shape
    x2 = x[0]
    src, order, lay, nb_main = _layout(d)
    g_width = d - A_HEADS * A_HEAD_DIM
    dv = g_width // G_HEADS
    dk = dv // 2
    d_ff = w_down.shape[1]
    topk = min(TOPK_MAX, s // 4)

    w_proj = _relayout_w_in(jnp.swapaxes(w_in, 1, 2), src, order)
    n_small = IDX_HEADS + G_LOWRANK
    w2 = jnp.zeros((LANES, G_HEADS * dk), F32).at[IDX_HEADS:n_small].set(w_gk2[0]).astype(BF16)
    bgk = b_gk[0].reshape(1, G_HEADS * dk)
    n_main = (nb_main + 1) * LANES
    q_lo, q_hi = lay["qa"][0] * LANES, (lay["qa"][0] + lay["qa"][1]) * LANES
    col = np.ones((1, n_main), np.float32)
    col[:, q_lo:q_hi] = A_HEAD_DIM ** -0.5 * LOG2E
    col_scale = jnp.asarray(col)

    tm = _tile(s, 1024)
    main, small, wd = _proj_blocks(x2, w_proj, col_scale, w_down, tm=_tile(s, 512), tn=_tile(n_main, 1280))
    bias = _indexer(main, small, lay, s=s, tq=256, tk=256, topk=topk)
    ta = _tile(s, 512)
    o_a = _attention(main, bias, attn_out_g[0], lay, s=s, tq=ta, tk=ta)
    o_g = _gla(main, small, w2, bgk, gla_norm_g[0], lay, s=s, rows=_tile(s, 512), chunk=256, dk=dk, dv=dv)
    y1 = _w_o(o_a, o_g, w_o, x2, tm=tm, tn=_tile(d, 512), alpha=DN_ALPHA)
    h, hb = _layer_norm(y1, ln1_g[0], ln1_b[0], tr=_tile(s, 256, 8), also_bf16=True)

    act = _ffn_up(hb, w_up, conv_w, conv_b[0].reshape(1, 2 * d_ff), d_ff=d_ff, tm=tm, tn=_tile(d_ff, 256))
    y2 = _matmul(act, wd, tm=_tile(s, 512), tn=_tile(d, 512), tk=d_ff, out_dtype=F32, residual=h,
                 alpha=DN_ALPHA, name="w_down")
    (out,) = _layer_norm(y2, ln2_g[0], ln2_b[0], tr=_tile(s, 256, 8), also_bf16=False)
    return out[None]
```

```python
import functools

import numpy as np
import jax
import jax.numpy as jnp
from jax import lax
from jax.experimental import pallas as pl
from jax.experimental.pallas import tpu as pltpu

A_HEADS = 16
A_HEAD_DIM = 128
IDX_HEADS = 32
IDX_DIM = 128
TOPK_MAX = 256
G_HEADS = 4
G_LOWRANK = 16
G_TAU = 16.0
CONV_W = 3
LN_EPS = 1e-5
RMS_EPS = 1e-6
DEPTH = 1
DN_ALPHA = (2 * DEPTH) ** 0.25

LANES = 128
MIB = 1024 * 1024
VMEM_BUDGET = 56 * MIB

BF16 = jnp.bfloat16
F32 = jnp.float32
NEG_BIG = -1e30
LOG2E = 1.4426950408889634
INT_MIN = -2 ** 31
COUNT_UNROLL = (8, 2, 1)


def _nbytes(shape, dtype):
    n = jnp.dtype(dtype).itemsize
    for dim in shape:
        n *= 1 if dim is None else getattr(dim, "block_size", dim)
    return n


def _params(sem, blocks, scratch=(), temps=()):
    need = (2 * sum(_nbytes(*b) for b in blocks) + sum(_nbytes(*b) for b in scratch)
            + sum(_nbytes(*b) for b in temps))
    limit = -(-need // MIB) * MIB
    assert limit <= VMEM_BUDGET, (limit, VMEM_BUDGET)
    return pltpu.CompilerParams(dimension_semantics=sem, vmem_limit_bytes=limit)


def _tile(n, max_tile, quantum=LANES):
    best = None
    for t in range(quantum, min(n, max_tile) + 1, quantum):
        if n % t == 0:
            best = t
    assert best is not None, (n, max_tile, quantum)
    return best


def _dot(a, b):
    return jnp.dot(a, b, preferred_element_type=F32)


def _dot_nt(a, b):
    return lax.dot_general(a, b, (((1,), (1,)), ((), ())), preferred_element_type=F32)


def _dot_tn(a, b):
    return lax.dot_general(a, b, (((0,), (0,)), ((), ())), preferred_element_type=F32)


def _relayout_kernel(off_ref, kind_ref, a_ref, b_ref, o_ref, *, n_wi, n_small):
    del off_ref
    kind = kind_ref[pl.program_id(0)]

    @pl.when(kind == 0)
    def _():
        o_ref[...] = a_ref[0].astype(BF16)

    @pl.when(kind == 1)
    def _():
        o_ref[...] = jnp.zeros(o_ref.shape, BF16)

    @pl.when(kind == 2)
    def _():
        r = lax.broadcasted_iota(jnp.int32, o_ref.shape, 0)
        o_ref[...] = jnp.where(r < n_wi, a_ref[0], jnp.where(r < n_small, b_ref[0], 0.0)).astype(BF16)


def _relayout_w_in(w_t, src, order):
    _, n_in, d = w_t.shape
    starts = [src[n][0] + t * LANES for n in order for t in range((src[n][1] - src[n][0]) // LANES)]
    wi0, glr0 = src["wi"][0], src["glr"][0]
    n_wi = src["wi"][1] - wi0
    n_small = n_wi + src["glr"][1] - glr0
    glr_win = glr0 - n_wi
    assert n_small <= LANES and glr_win >= 0 and wi0 + LANES <= n_in and glr_win + LANES <= n_in
    assert all(o % 8 == 0 for o in starts + [wi0, glr_win])
    row_off8 = [o // 8 for o in starts + [0, wi0]]
    kind = [0] * len(starts) + [1, 2]
    nb = len(kind)
    tables = [jnp.asarray(np.asarray(t, np.int32)) for t in (row_off8, kind)]
    return pl.pallas_call(
        functools.partial(_relayout_kernel, n_wi=n_wi, n_small=n_small),
        grid_spec=pltpu.PrefetchScalarGridSpec(
            num_scalar_prefetch=2,
            grid=(nb,),
            in_specs=[pl.BlockSpec((pl.Element(1), pl.Element(LANES), pl.Element(d)),
                                   lambda b, off, kd: (0, off[b] * 8, 0)),
                      pl.BlockSpec((pl.Element(1), pl.Element(LANES), pl.Element(d)),
                                   lambda b, off, kd: (0, glr_win, 0))],
            out_specs=pl.BlockSpec((LANES, d), lambda b, off, kd: (b, 0))),
        out_shape=jax.ShapeDtypeStruct((nb * LANES, d), BF16),
        compiler_params=_params(("arbitrary",),
                                blocks=[((LANES, d), F32), ((LANES, d), F32), ((LANES, d), BF16)],
                                temps=[((LANES, d), F32)]),
        name="relayout_w_in",
    )(*tables, w_t, w_t)


def _proj_kernel(x_ref, w_ref, ws_ref, sc_ref, wd_ref, o_ref, os_ref, wdb_ref, xb_scr):
    wdb_ref[...] = wd_ref[0].astype(BF16)

    @pl.when(pl.program_id(1) == 0)
    def _():
        xb_scr[...] = x_ref[...].astype(BF16)
        os_ref[...] = _dot_nt(xb_scr[...], ws_ref[...])

    res = _dot_nt(xb_scr[...], w_ref[...]) * sc_ref[...]
    for c in range(o_ref.shape[0]):
        o_ref[c] = res[:, c * LANES:(c + 1) * LANES].astype(o_ref.dtype)


def _proj_blocks(x, wt, col_scale, w_down, *, tm, tn):
    s, k = x.shape
    n = wt.shape[0] - LANES
    ni, nj = s // tm, n // tn
    _, f, d_out = w_down.shape
    n_slab = max(c for c in range(1, ni * nj + 1) if f % c == 0 and (f // c) % 16 == 0)
    slab = f // n_slab

    def slab_of(i, j):
        return jnp.minimum(i * nj + j, n_slab - 1)

    return pl.pallas_call(
        _proj_kernel,
        grid=(ni, nj),
        in_specs=[pl.BlockSpec((tm, k), lambda i, j: (i, 0)),
                  pl.BlockSpec((tn, k), lambda i, j: (j, 0)),
                  pl.BlockSpec((LANES, k), lambda i, j: (n // LANES, 0)),
                  pl.BlockSpec((1, tn), lambda i, j: (0, j)),
                  pl.BlockSpec((1, slab, d_out), lambda i, j: (0, slab_of(i, j), 0))],
        out_specs=[pl.BlockSpec((tn // LANES, tm, LANES), lambda i, j: (j, i, 0)),
                   pl.BlockSpec((tm, LANES), lambda i, j: (i, 0)),
                   pl.BlockSpec((slab, d_out), lambda i, j: (slab_of(i, j), 0))],
        out_shape=[jax.ShapeDtypeStruct((n // LANES, s, LANES), BF16),
                   jax.ShapeDtypeStruct((s, LANES), F32),
                   jax.ShapeDtypeStruct((f, d_out), BF16)],
        scratch_shapes=[pltpu.VMEM((tm, k), BF16)],
        compiler_params=_params(("arbitrary", "arbitrary"),
                                blocks=[((tm, k), F32), ((tn, k), BF16), ((LANES, k), BF16), ((8, tn), F32),
                                        ((tm, tn), BF16), ((tm, LANES), F32), ((slab, d_out), F32),
                                        ((slab, d_out), BF16)],
                                scratch=[((tm, k), BF16)],
                                temps=[((tm, tn), F32), ((tm, tn), F32)]),
        name="proj",
    )(x, wt, wt, col_scale, w_down)


def _mm_kernel(a_ref, b_ref, *rest, alpha, has_res, nk):
    if has_res:
        r_ref, o_ref = rest[0], rest[1]
        rest = rest[2:]
    else:
        r_ref, o_ref = None, rest[0]
        rest = rest[1:]

    def finish(acc):
        if has_res:
            acc = acc + alpha * r_ref[...]
        o_ref[...] = acc.astype(o_ref.dtype)

    if nk == 1:
        finish(_dot(a_ref[...], b_ref[...]))
        return
    acc_ref = rest[0]
    kk = pl.program_id(2)

    @pl.when(kk == 0)
    def _():
        acc_ref[...] = _dot(a_ref[...], b_ref[...])

    @pl.when(kk > 0)
    def _():
        acc_ref[...] += _dot(a_ref[...], b_ref[...])

    @pl.when(kk == nk - 1)
    def _():
        finish(acc_ref[...])


def _matmul(a, b, *, tm, tn, tk, out_dtype, residual=None, alpha=1.0, name="matmul"):
    m, k = a.shape
    n = b.shape[1]
    nk = k // tk
    in_specs = [pl.BlockSpec((tm, tk), lambda i, j, kk: (i, kk)),
                pl.BlockSpec((tk, tn), lambda i, j, kk: (kk, j))]
    args = [a, b]
    if residual is not None:
        in_specs.append(pl.BlockSpec((tm, tn), lambda i, j, kk: (i, j)))
        args.append(residual)
    scratch = [pltpu.VMEM((tm, tn), F32)] if nk > 1 else []
    return pl.pallas_call(
        functools.partial(_mm_kernel, alpha=alpha, has_res=residual is not None, nk=nk),
        grid=(m // tm, n // tn, nk),
        in_specs=in_specs,
        out_specs=pl.BlockSpec((tm, tn), lambda i, j, kk: (i, j)),
        out_shape=jax.ShapeDtypeStruct((m, n), out_dtype),
        scratch_shapes=scratch,
        compiler_params=_params(("parallel", "parallel", "arbitrary"),
                                blocks=[((tm, tk), a.dtype), ((tk, tn), b.dtype), ((tm, tn), out_dtype)]
                                + ([((tm, tn), F32)] if residual is not None else []),
                                scratch=[((tm, tn), F32)] if nk > 1 else [],
                                temps=[((tm, tn), F32)]),
        name=name,
    )(*args)


def _wo_kernel(oa_ref, og_ref, wt_ref, wb_ref, x_ref, o_ref, w_scr, *, ka, alpha):
    @pl.when(pl.program_id(1) == 0)
    def _():
        w_scr[:ka, :] = wt_ref[0].astype(BF16)
        w_scr[ka:, :] = wb_ref[0].astype(BF16)

    acc = _dot(oa_ref[...], w_scr[:ka, :]) + _dot(og_ref[...], w_scr[ka:, :])
    o_ref[...] = acc + alpha * x_ref[...]


def _w_o(o_a, o_g, w_o, x, *, tm, tn, alpha):
    s, ka = o_a.shape
    kg = o_g.shape[1]
    d = w_o.shape[2]
    assert ka % kg == 0
    return pl.pallas_call(
        functools.partial(_wo_kernel, ka=ka, alpha=alpha),
        grid=(d // tn, s // tm),
        in_specs=[pl.BlockSpec((tm, ka), lambda j, i: (i, 0)),
                  pl.BlockSpec((tm, kg), lambda j, i: (i, 0)),
                  pl.BlockSpec((1, ka, tn), lambda j, i: (0, 0, j)),
                  pl.BlockSpec((1, kg, tn), lambda j, i: (0, ka // kg, j)),
                  pl.BlockSpec((tm, tn), lambda j, i: (i, j))],
        out_specs=pl.BlockSpec((tm, tn), lambda j, i: (i, j)),
        out_shape=jax.ShapeDtypeStruct((s, d), F32),
        scratch_shapes=[pltpu.VMEM((ka + kg, tn), BF16)],
        compiler_params=_params(("parallel", "arbitrary"),
                                blocks=[((tm, ka), BF16), ((tm, kg), BF16), ((ka, tn), F32), ((kg, tn), F32),
                                        ((tm, tn), F32), ((tm, tn), F32)],
                                scratch=[((ka + kg, tn), BF16)],
                                temps=[((tm, tn), F32), ((tm, tn), F32)]),
        name="w_o",
    )(o_a, o_g, w_o, w_o, x)


def _ln_kernel(y_ref, g_ref, b_ref, *o_refs):
    y = y_ref[...]
    mu = jnp.mean(y, axis=1, keepdims=True)
    yc = y - mu
    var = jnp.mean(yc * yc, axis=1, keepdims=True)
    out = yc * lax.rsqrt(var + LN_EPS) * g_ref[...] + b_ref[...]
    for o_ref in o_refs:
        o_ref[...] = out.astype(o_ref.dtype)


def _layer_norm(y, g, b, *, tr, also_bf16):
    s, d = y.shape
    out_shape = [jax.ShapeDtypeStruct((s, d), F32)]
    out_specs = [pl.BlockSpec((tr, d), lambda i: (i, 0))]
    if also_bf16:
        out_shape.append(jax.ShapeDtypeStruct((s, d), BF16))
        out_specs.append(pl.BlockSpec((tr, d), lambda i: (i, 0)))
    return pl.pallas_call(
        _ln_kernel,
        grid=(s // tr,),
        in_specs=[pl.BlockSpec((tr, d), lambda i: (i, 0)),
                  pl.BlockSpec((1, d), lambda i: (0, 0)),
                  pl.BlockSpec((1, d), lambda i: (0, 0))],
        out_specs=out_specs,
        out_shape=out_shape,
        compiler_params=_params(("parallel",),
                                blocks=[((tr, d), F32), ((8, d), F32), ((8, d), F32)]
                                + [((tr, d), o.dtype) for o in out_shape],
                                temps=[((tr, d), F32), ((tr, d), F32)]),
        name="layer_norm",
    )(y, g.reshape(1, d), b.reshape(1, d))


def _idx_kernel(qi_ref, ki_ref, sm_ref, bias_ref, hi_scr, lo_scr, wb_scr, *, tq, tk, nkt, hi, group,
                topk, wscale):
    qb = pl.program_id(0)
    nk = ((qb + 1) * tq + tk - 1) // tk
    reps = tk // LANES
    i16 = jnp.int16
    low16 = -2 ** 15

    for h in range(hi):
        wb_scr[h] = jnp.broadcast_to(sm_ref[:, h:h + 1] * wscale, (tq, LANES))

    row = qb * tq + lax.broadcasted_iota(jnp.int32, (group, tk), 0)
    col = lax.broadcasted_iota(jnp.int32, (group, tk), 1)

    def lanes(a):
        return jnp.concatenate([a] * reps, axis=1) if reps > 1 else a

    def fold(a):
        out = a[:, :LANES]
        for r in range(1, reps):
            out = out + a[:, r * LANES:(r + 1) * LANES]
        return out

    def score_tile(kb, carry):
        kt = ki_ref[0, pl.ds(pl.multiple_of(kb * tk, tk), tk), :]
        for r in range(tq // group):
            rows = slice(r * group, (r + 1) * group)
            qs = qi_ref[:, rows, :].reshape(hi * group, IDX_DIM)
            d = _dot_nt(qs, kt)
            acc = jnp.zeros((group, tk), F32)
            for h in range(hi):
                acc = acc + jnp.maximum(d[h * group:(h + 1) * group], 0.0) * lanes(wb_scr[h, rows, :])
            bits = pltpu.bitcast(acc + 0.0, jnp.int32)
            key = jnp.where(kb * tk + col <= row + r * group,
                            bits ^ ((bits >> 31) & jnp.int32(0x7FFFFFFF)), INT_MIN)
            hi_scr[kb, rows, :] = (key >> 16).astype(i16)
            lo_scr[kb, rows, :] = ((key & 0xFFFF) + low16).astype(i16)
        return carry

    lax.fori_loop(0, nk, score_tile, 0)

    def for_each_tile(body):
        done = 0
        for width in COUNT_UNROLL:
            def group(q, carry, width=width, done=done):
                for u in range(width):
                    body(done + q * width + u)
                return carry

            n_groups = (nk - done) // width
            lax.fori_loop(0, n_groups, group, 0)
            done = done + n_groups * width

    def count_ge(ref, cand):
        cand_t = lanes(cand.astype(i16))

        def step(kb, c):
            return c + fold(jnp.where(ref[kb] >= cand_t, i16(1), i16(0)))

        c = jnp.zeros((tq, LANES), i16)
        done = 0
        for width in COUNT_UNROLL:
            def step_many(q, c, width=width, done=done):
                for u in range(width):
                    c = step(done + q * width + u, c)
                return c

            n_groups = (nk - done) // width
            c = lax.fori_loop(0, n_groups, step_many, c)
            done = done + n_groups * width
        return jnp.sum(c.astype(F32), axis=1, keepdims=True)

    kf = float(topk)

    def bit_search(ref, base_cnt, cnt_start, stop_when_exact):
        def body(state):
            it, v, cnt_v = state
            cand = v + jnp.left_shift(jnp.int32(1), 15 - it)
            cnt = base_cnt + count_ge(ref, cand)
            ok = cnt >= kf
            return it + 1, jnp.where(ok, cand, v), jnp.where(ok, cnt, cnt_v)

        def unsettled(state):
            go = state[0] < 16
            if stop_when_exact:
                go = go & (jnp.max(jnp.where(state[2] == kf, 0.0, 1.0)) > 0.0)
            return go

        v0 = jnp.full((tq, LANES), low16, jnp.int32)
        _, v, cnt_v = lax.while_loop(unsettled, body, (jnp.int32(0), v0, cnt_start))
        return v, cnt_v

    n_all = jnp.zeros((tq, 1), F32) + (nk * tk).astype(F32)
    v_hi, n_from_hi = bit_search(hi_scr, 0.0, n_all, False)
    n_above = count_ge(hi_scr, v_hi + 1)
    v_hi_t = lanes(v_hi.astype(i16))

    def keep_equal(kb):
        lo_scr[kb] = jnp.where(hi_scr[kb] == v_hi_t, lo_scr[kb], i16(low16))

    for_each_tile(keep_equal)
    v_lo, n_sel = bit_search(lo_scr, n_above, n_from_hi, True)
    v_lo = jnp.where(v_hi == low16, jnp.maximum(v_lo, low16 + 1), v_lo)
    v_lo_t = lanes(v_lo.astype(i16))
    zero = jnp.zeros((tq, tk), bias_ref.dtype)
    neg = jnp.full((tq, tk), NEG_BIG, bias_ref.dtype)
    has_ties = jnp.max(jnp.where(n_sel > kf, 1.0, 0.0)) > 0.0

    @pl.when(jnp.logical_not(has_ties))
    def _():
        def emit(kb):
            h16 = hi_scr[kb]
            sel = (h16 > v_hi_t) | ((h16 == v_hi_t) & (lo_scr[kb] >= v_lo_t))
            bias_ref[0, kb] = jnp.where(sel, zero, neg)

        for_each_tile(emit)

    @pl.when(has_ties)
    def _():
        top16 = 2 ** 15 - 1
        n_gt = n_above + jnp.where(v_lo[:, :1] >= top16, 0.0, count_ge(lo_scr, jnp.minimum(v_lo + 1, top16)))
        need = kf - n_gt
        earlier = jnp.where(lax.broadcasted_iota(jnp.int32, (tk, tk), 0)
                            < lax.broadcasted_iota(jnp.int32, (tk, tk), 1), 1.0, 0.0).astype(BF16)
        one_b = jnp.ones((tq, tk), BF16)
        zero_b = jnp.zeros((tq, tk), BF16)

        def emit_ties(kb, seen):
            h16 = hi_scr[kb]
            l16 = lo_scr[kb]
            same_hi = h16 == v_hi_t
            larger = (h16 > v_hi_t) | (same_hi & (l16 > v_lo_t))
            tied = jnp.where(same_hi & (l16 == v_lo_t), one_b, zero_b)
            rank = seen + _dot(tied, earlier)
            keep = jnp.where(rank < need, 1.0, 0.0).astype(BF16)
            sel = larger | ((tied > 0.0) & (keep > 0.0))
            bias_ref[0, kb] = jnp.where(sel, zero, neg)
            return seen + jnp.sum(tied.astype(F32), axis=1, keepdims=True)

        lax.fori_loop(0, nk, emit_ties, jnp.zeros((tq, 1), F32))

    def fill(kb, carry):
        bias_ref[0, kb] = neg
        return carry

    lax.fori_loop(nk, nkt, fill, 0)


def _indexer(main, small, lay, *, s, tq, tk, topk):
    nkt = s // tk
    hi = IDX_HEADS
    group = min(tq, 64)
    qi0, ki0 = lay["qi"][0], lay["ki"][0]
    assert qi0 % hi == 0
    return pl.pallas_call(
        functools.partial(_idx_kernel, tq=tq, tk=tk, nkt=nkt, hi=hi, group=group, topk=topk,
                          wscale=IDX_HEADS ** -0.5 * IDX_DIM ** -0.5),
        grid=(s // tq,),
        in_specs=[pl.BlockSpec((hi, tq, LANES), lambda i: (qi0 // hi, i, 0)),
                  pl.BlockSpec((1, s, LANES), lambda i: (ki0, 0, 0)),
                  pl.BlockSpec((tq, LANES), lambda i: (i, 0))],
        out_specs=pl.BlockSpec((1, nkt, tq, tk), lambda i: (i, 0, 0, 0)),
        out_shape=jax.ShapeDtypeStruct((s // tq, nkt, tq, tk), BF16),
        scratch_shapes=[pltpu.VMEM((nkt, tq, tk), jnp.int16),
                        pltpu.VMEM((nkt, tq, tk), jnp.int16),
                        pltpu.VMEM((hi, tq, LANES), F32)],
        compiler_params=_params(("parallel",),
                                blocks=[((hi, tq, LANES), BF16), ((s, LANES), BF16), ((tq, LANES), F32),
                                        ((nkt, tq, tk), BF16)],
                                scratch=[((nkt, tq, tk), jnp.int16), ((nkt, tq, tk), jnp.int16),
                                         ((hi, tq, LANES), F32)],
                                temps=[((hi * group, tk), F32), ((hi * group, tk), F32)]),
        name="indexer",
    )(main, main, small)


def _attn_kernel(qi_ref, kb_ref, q_ref, k_ref, v_ref, b_ref, g_ref, o_ref, bias_scr, m_scr, acc_scr, *pipe,
                 nh, tq, tk):
    step = pl.program_id(0)
    qi = qi_ref[step]
    kb = kb_ref[step]

    @pl.when(kb == 0)
    def _():
        m_scr[...] = jnp.full(m_scr.shape, NEG_BIG, F32)
        acc_scr[...] = jnp.zeros(acc_scr.shape, F32)

    na, nb, sq, sk = b_ref.shape
    for a in range(na):
        for b in range(nb):
            bias_scr[a * sq:(a + 1) * sq, b * sk:(b + 1) * sk] = b_ref[a, b].astype(F32)

    ones = jnp.ones((tk, LANES), BF16)

    def qk(h, s_ref):
        s_ref[...] = _dot_nt(q_ref[h], k_ref[h])

    def soft(h, s_ref, p_ref, al_ref):
        s = s_ref[...] + bias_scr[...]
        m_prev = m_scr[h]
        m_new = jnp.maximum(m_prev, jnp.max(s, axis=1, keepdims=True))
        m_scr[h] = m_new
        p_ref[...] = jnp.exp2(s - jnp.concatenate([m_new] * (tk // LANES), axis=1)).astype(BF16)
        al_ref[...] = jnp.exp2(m_prev - m_new)

    def pv(h, p_ref, al_ref):
        al = al_ref[...]
        v_ext = jnp.concatenate([v_ref[h], ones], axis=1)
        acc_scr[h] = acc_scr[h] * jnp.concatenate([al, al], axis=1) + _dot(p_ref[...], v_ext)

    s_a, s_b, p_a, p_b, al_a, al_b = pipe
    qk(0, s_a)
    qk(1, s_b)
    soft(0, s_a, p_a, al_a)

    def pair(j, carry):
        qk(2 * j, s_a)
        soft(2 * j - 1, s_b, p_b, al_b)
        pv(2 * j - 2, p_a, al_a)
        qk(2 * j + 1, s_b)
        soft(2 * j, s_a, p_a, al_a)
        pv(2 * j - 1, p_b, al_b)
        return carry

    lax.fori_loop(1, nh // 2, pair, 0)
    soft(nh - 1, s_b, p_b, al_b)
    pv(nh - 2, p_a, al_a)
    pv(nh - 1, p_b, al_b)

    @pl.when(kb == (qi * tq + tq - 1) // tk)
    def _():
        ss = jnp.zeros((tq, LANES), F32)
        for h in range(nh):
            o_h = acc_scr[h, :, :LANES] / acc_scr[h, :, LANES:]
            acc_scr[h, :, :LANES] = o_h
            ss = ss + o_h * o_h
        ms = jnp.sum(ss, axis=1, keepdims=True) * (1.0 / (nh * LANES))
        r = lax.rsqrt(ms + RMS_EPS)
        for h in range(nh):
            sl = slice(h * LANES, (h + 1) * LANES)
            o_ref[:, sl] = (acc_scr[h, :, :LANES] * r * g_ref[:, sl]).astype(o_ref.dtype)


def _attention(main, bias, g, lay, *, s, tq, tk):
    nh = A_HEADS
    nqb, nkt, sq, sk = bias.shape
    q0, k0, v0 = lay["qa"][0], lay["ka"][0], lay["va"][0]
    assert q0 % nh == 0 and k0 % nh == 0 and v0 % nh == 0
    pairs = [(i, j) for i in range(s // tq) for j in range((i * tq + tq - 1) // tk + 1)]
    qi_tab = jnp.asarray(np.asarray([p[0] for p in pairs], np.int32))
    kb_tab = jnp.asarray(np.asarray([p[1] for p in pairs], np.int32))
    return pl.pallas_call(
        functools.partial(_attn_kernel, nh=nh, tq=tq, tk=tk),
        grid_spec=pltpu.PrefetchScalarGridSpec(
            num_scalar_prefetch=2,
            grid=(len(pairs),),
            in_specs=[pl.BlockSpec((nh, tq, LANES), lambda t, qi, kb: (q0 // nh, qi[t], 0)),
                      pl.BlockSpec((nh, tk, LANES), lambda t, qi, kb: (k0 // nh, kb[t], 0)),
                      pl.BlockSpec((nh, tk, LANES), lambda t, qi, kb: (v0 // nh, kb[t], 0)),
                      pl.BlockSpec((tq // sq, tk // sk, sq, sk), lambda t, qi, kb: (qi[t], kb[t], 0, 0)),
                      pl.BlockSpec((1, nh * LANES), lambda t, qi, kb: (0, 0))],
            out_specs=pl.BlockSpec((tq, nh * LANES), lambda t, qi, kb: (qi[t], 0)),
            scratch_shapes=[pltpu.VMEM((tq, tk), F32),
                            pltpu.VMEM((nh, tq, LANES), F32),
                            pltpu.VMEM((nh, tq, 2 * LANES), F32),
                            pltpu.VMEM((tq, tk), F32), pltpu.VMEM((tq, tk), F32),
                            pltpu.VMEM((tq, tk), BF16), pltpu.VMEM((tq, tk), BF16),
                            pltpu.VMEM((tq, LANES), F32), pltpu.VMEM((tq, LANES), F32)]),
        out_shape=jax.ShapeDtypeStruct((s, nh * LANES), BF16),
        compiler_params=_params(("arbitrary",),
                                blocks=[((nh, tq, LANES), BF16), ((nh, tk, LANES), BF16), ((nh, tk, LANES), BF16),
                                        ((tq, tk), BF16), ((8, nh * LANES), F32), ((tq, nh * LANES), BF16)],
                                scratch=[((tq, tk), F32), ((nh, tq, LANES), F32), ((nh, tq, 2 * LANES), F32),
                                         ((tq, tk), F32), ((tq, tk), F32), ((tq, tk), BF16), ((tq, tk), BF16),
                                         ((tq, LANES), F32), ((tq, LANES), F32)],
                                temps=[((tq, tk), F32), ((tq, tk), F32)]),
        name="attention",
    )(qi_tab, kb_tab, main, main, main, bias, g.reshape(1, nh * LANES))


def _gla_kernel(q_ref, k_ref, v_ref, gg_ref, sm_ref, w2_ref, bgk_ref, gn_ref, o_ref, st_scr, *,
                rows, chunk, nheads, dk, dv):
    @pl.when(pl.program_id(0) == 0)
    def _():
        st_scr[...] = jnp.zeros(st_scr.shape, F32)

    r_i = lax.broadcasted_iota(jnp.int32, (chunk, chunk), 0)
    c_i = lax.broadcasted_iota(jnp.int32, (chunk, chunk), 1)
    tri = jnp.where(r_i >= c_i, 1.0, 0.0).astype(BF16)
    nbk, nbv = dk // LANES, dv // LANES
    assert chunk >= 16 and chunk & (chunk - 1) == 0
    leaf = 2
    half_sizes = [chunk >> i for i in range(1, chunk.bit_length()) if chunk >> i >= leaf]
    quadrant = {m: (r_i // (2 * m) == c_i // (2 * m)) & (r_i % (2 * m) >= m) & (c_i % (2 * m) < m)
                for m in half_sizes}
    near = [(c_i == r_i - j) & (r_i % leaf >= j) for j in range(leaf)]
    sub = lax.broadcasted_iota(jnp.int32, (chunk // 8, 8, dk), 1)

    def reference_rows(b, m):
        if m >= 8:
            return jnp.concatenate(
                [jnp.broadcast_to(b[blk + m:blk + m + 1, :], (2 * m, dk)) for blk in range(0, chunk, 2 * m)],
                axis=0)
        b3 = b.reshape(chunk // 8, 8, dk)
        out = None
        for blk in range(0, 8, 2 * m):
            rows = jnp.broadcast_to(b3[:, blk + m:blk + m + 1, :], b3.shape)
            out = rows if out is None else jnp.where(sub >= blk, rows, out)
        return out.reshape(chunk, dk)

    def wide(ref, first, n, r0):
        return jnp.concatenate([ref[first + j, pl.ds(r0, chunk), :] for j in range(n)], axis=1)

    def step(c, carry):
        r0 = pl.multiple_of(c * chunk, chunk)
        z_all = _dot(sm_ref[pl.ds(r0, chunk), :].astype(BF16), w2_ref[...]) + bgk_ref[...]
        for h in range(nheads):
            q = wide(q_ref, h * nbk, nbk, r0).astype(F32) * (dk ** -0.5)
            k = wide(k_ref, h * nbk, nbk, r0).astype(F32)
            v = wide(v_ref, h * nbv, nbv, r0)
            z = z_all[:, h * dk:(h + 1) * dk]
            g = (jnp.minimum(z, 0.0) - jnp.log1p(jnp.exp(-jnp.abs(z)))) * (LOG2E / G_TAU)
            g_hi = g.astype(BF16)
            rem = g - g_hi.astype(F32)
            g_mid = rem.astype(BF16)
            g_lo = (rem - g_mid.astype(F32)).astype(BF16)
            b = _dot(tri, g_hi) + _dot(tri, g_mid) + _dot(tri, g_lo)
            b_last = b[chunk - 1:chunk, :]
            att = jnp.zeros((chunk, chunk), F32)
            for m in half_sizes:
                e = jnp.exp2(-jnp.abs(b - reference_rows(b, m)))
                att = att + jnp.where(quadrant[m], _dot_nt((q * e).astype(BF16), (k * e).astype(BF16)), 0.0)
            for j in range(leaf):
                kj = k if j == 0 else pltpu.roll(k, j, 0)
                bj = b if j == 0 else pltpu.roll(b, j, 0)
                pair = jnp.sum(q * kj * jnp.exp2(jnp.minimum(b - bj, 0.0)), axis=1, keepdims=True)
                att = att + jnp.where(near[j], pair, 0.0)
            st = st_scr[h]
            o = _dot(att.astype(BF16), v) + _dot_nt((q * jnp.exp2(b)).astype(BF16), st.astype(BF16))
            kd = (k * jnp.exp2(b_last - b)).astype(BF16)
            st_scr[h] = st * jnp.exp2(b_last) + _dot_tn(v, kd)
            ms = jnp.mean(o * o, axis=1, keepdims=True)
            gate = wide(gg_ref, h * nbv, nbv, r0).astype(F32)
            out = (o * lax.rsqrt(ms + RMS_EPS)) * gn_ref[...] * (gate * jax.nn.sigmoid(gate))
            o_ref[pl.ds(r0, chunk), h * dv:(h + 1) * dv] = out.astype(o_ref.dtype)
        return carry

    lax.fori_loop(0, rows // chunk, step, 0)


def _gla(main, small, w2, bgk, gn, lay, *, s, rows, chunk, dk, dv):
    nh = G_HEADS
    nqk, nv = nh * dk // LANES, nh * dv // LANES
    q0, k0, v0, g0 = lay["qg"][0], lay["kg"][0], lay["vg"][0], lay["gg"][0]
    assert q0 % nqk == 0 and k0 % nqk == 0 and v0 % nv == 0 and g0 % nv == 0
    return pl.pallas_call(
        functools.partial(_gla_kernel, rows=rows, chunk=chunk, nheads=nh, dk=dk, dv=dv),
        grid=(s // rows,),
        in_specs=[pl.BlockSpec((nqk, rows, LANES), lambda r: (q0 // nqk, r, 0)),
                  pl.BlockSpec((nqk, rows, LANES), lambda r: (k0 // nqk, r, 0)),
                  pl.BlockSpec((nv, rows, LANES), lambda r: (v0 // nv, r, 0)),
                  pl.BlockSpec((nv, rows, LANES), lambda r: (g0 // nv, r, 0)),
                  pl.BlockSpec((rows, LANES), lambda r: (r, 0)),
                  pl.BlockSpec((LANES, nh * dk), lambda r: (0, 0)),
                  pl.BlockSpec((1, nh * dk), lambda r: (0, 0)),
                  pl.BlockSpec((1, dv), lambda r: (0, 0))],
        out_specs=pl.BlockSpec((rows, nh * dv), lambda r: (r, 0)),
        out_shape=jax.ShapeDtypeStruct((s, nh * dv), BF16),
        scratch_shapes=[pltpu.VMEM((nh, dv, dk), F32)],
        compiler_params=_params(("arbitrary",),
                                blocks=[((nqk, rows, LANES), BF16), ((nqk, rows, LANES), BF16),
                                        ((nv, rows, LANES), BF16), ((nv, rows, LANES), BF16),
                                        ((rows, LANES), F32), ((LANES, nh * dk), BF16), ((8, nh * dk), F32),
                                        ((8, dv), F32), ((rows, nh * dv), BF16)],
                                scratch=[((nh, dv, dk), F32)],
                                temps=[((chunk, nh * dk), F32)] * 8 + [((dv, dk), F32)] * 2),
        name="gla",
    )(main, main, main, main, small, w2, bgk, gn.reshape(1, dv))


def _ffn_up_kernel(h_ref, wg_ref, wv_ref, cwg_ref, cwv_ref, cbg_ref, cbv_ref, o_ref, w_scr, hw_scr,
                   *, tm, tn, bounds):
    @pl.when(pl.program_id(1) == 0)
    def _():
        w_scr[:, :tn] = wg_ref[0].astype(BF16)
        w_scr[:, tn:] = wv_ref[0].astype(BF16)
        hw_scr[0:8, :] = jnp.zeros((8, 2 * tn), F32)

    cw = jnp.concatenate([cwg_ref[0], cwv_ref[0]], axis=1)
    cb = jnp.concatenate([cbg_ref[...], cbv_ref[...]], axis=1)
    w0, w1, w2 = cw[0:1, :], cw[1:2, :], cw[2:3, :]

    for lo, hi in zip(bounds[:-1], bounds[1:]):
        hw = _dot(h_ref[lo:hi, :], w_scr[...])
        hw_scr[8 + lo:8 + hi, :] = hw
        u = w2 * hw + w1 * hw_scr[7 + lo:7 + hi, :] + w0 * hw_scr[6 + lo:6 + hi, :] + cb
        gate, val = u[:, :tn], u[:, tn:]
        o_ref[lo:hi, :] = (gate * jax.nn.sigmoid(gate) * val).astype(o_ref.dtype)
    hw_scr[0:8, :] = hw_scr[tm:tm + 8, :]


def _ffn_up(hb, w_up, conv_w, conv_b, *, d_ff, tm, tn):
    s, d = hb.shape
    nj = d_ff // tn
    return pl.pallas_call(
        functools.partial(_ffn_up_kernel, tm=tm, tn=tn, bounds=(0, tm // 2, tm)),
        grid=(nj, s // tm),
        in_specs=[pl.BlockSpec((tm, d), lambda j, i: (i, 0)),
                  pl.BlockSpec((1, d, tn), lambda j, i: (0, 0, j)),
                  pl.BlockSpec((1, d, tn), lambda j, i: (0, 0, nj + j)),
                  pl.BlockSpec((1, CONV_W, tn), lambda j, i: (0, 0, j)),
                  pl.BlockSpec((1, CONV_W, tn), lambda j, i: (0, 0, nj + j)),
                  pl.BlockSpec((1, tn), lambda j, i: (0, j)),
                  pl.BlockSpec((1, tn), lambda j, i: (0, nj + j))],
        out_specs=pl.BlockSpec((tm, tn), lambda j, i: (i, j)),
        out_shape=jax.ShapeDtypeStruct((s, d_ff), BF16),
        scratch_shapes=[pltpu.VMEM((d, 2 * tn), BF16), pltpu.VMEM((8 + tm, 2 * tn), F32)],
        compiler_params=_params(("parallel", "arbitrary"),
                                blocks=[((tm, d), BF16), ((d, tn), F32), ((d, tn), F32), ((8, tn), F32),
                                        ((8, tn), F32), ((8, tn), F32), ((8, tn), F32), ((tm, tn), BF16)],
                                scratch=[((d, 2 * tn), BF16), ((8 + tm, 2 * tn), F32)],
                                temps=[((tm // 2, 2 * tn), F32)] * 3),
        name="ffn_up",
    )(hb, w_up, w_up, conv_w, conv_w, conv_b, conv_b)


def _layout(d_model):
    a_width = A_HEADS * A_HEAD_DIM
    g_width = d_model - a_width
    g_kwidth = g_width // 2
    names = ("qa", "ka", "va", "qi", "ki", "wi", "qg", "kg", "vg", "glr", "gg")
    sizes = (a_width, a_width, a_width, IDX_HEADS * IDX_DIM, IDX_DIM, IDX_HEADS,
             g_kwidth, g_kwidth, g_width, G_LOWRANK, g_width)
    offs = np.concatenate([[0], np.cumsum(sizes)])
    src = {n: (int(offs[i]), int(offs[i + 1])) for i, n in enumerate(names)}
    order = ("qi", "qa", "ka", "va", "qg", "kg", "vg", "gg", "ki")
    lay, blk = {}, 0
    for n in order:
        width = src[n][1] - src[n][0]
        assert width % LANES == 0
        lay[n] = (blk, width // LANES)
        blk += width // LANES
    return src, order, lay, blk


def kernel(x, w_in, w_gk2, b_gk, attn_out_g, gla_norm_g, w_o, ln1_g, ln1_b, w_up, conv_w, conv_b,
           w_down, ln2_g, ln2_b):
    assert x.shape[0] == 1 and w_in.shape[0] == DEPTH == 1
    _, s, d = x.shape
    x2 = x[0]
    src, order, lay, nb_main = _layout(d)
    g_width = d - A_HEADS * A_HEAD_DIM
    dv = g_width // G_HEADS
    dk = dv // 2
    d_ff = w_down.shape[1]
    topk = min(TOPK_MAX, s // 4)

    w_proj = _relayout_w_in(jnp.swapaxes(w_in, 1, 2), src, order)
    n_small = IDX_HEADS + G_LOWRANK
    w2 = jnp.zeros((LANES, G_HEADS * dk), F32).at[IDX_HEADS:n_small].set(w_gk2[0]).astype(BF16)
    bgk = b_gk[0].reshape(1, G_HEADS * dk)
    n_main = (nb_main + 1) * LANES
    q_lo, q_hi = lay["qa"][0] * LANES, (lay["qa"][0] + lay["qa"][1]) * LANES
    col = np.ones((1, n_main), np.float32)
    col[:, q_lo:q_hi] = A_HEAD_DIM ** -0.5 * LOG2E
    col_scale = jnp.asarray(col)

    tm = _tile(s, 1024)
    main, small, wd = _proj_blocks(x2, w_proj, col_scale, w_down, tm=_tile(s, 512), tn=_tile(n_main, 1280))
    bias = _indexer(main, small, lay, s=s, tq=256, tk=256, topk=topk)
    ta = _tile(s, 512)
    o_a = _attention(main, bias, attn_out_g[0], lay, s=s, tq=ta, tk=ta)
    o_g = _gla(main, small, w2, bgk, gla_norm_g[0], lay, s=s, rows=_tile(s, 512), chunk=256, dk=dk, dv=dv)
    y1 = _w_o(o_a, o_g, w_o, x2, tm=tm, tn=_tile(d, 512), alpha=DN_ALPHA)
    h, hb = _layer_norm(y1, ln1_g[0], ln1_b[0], tr=_tile(s, 256, 8), also_bf16=True)

    act = _ffn_up(hb, w_up, conv_w, conv_b[0].reshape(1, 2 * d_ff), d_ff=d_ff, tm=tm, tn=_tile(d_ff, 256))
    y2 = _matmul(act, wd, tm=_tile(s, 512), tn=_tile(d, 512), tk=d_ff, out_dtype=F32, residual=h,
                 alpha=DN_ALPHA, name="w_down")
    (out,) = _layer_norm(y2, ln2_g[0], ln2_b[0], tr=_tile(s, 256, 8), also_bf16=False)
    return out[None]
```
